```python
import jax
import jax.numpy as jnp
from jax import lax
import numpy as np

D_MODEL = 1024
BATCH = 8
SEQ = 2048
DEPTH = 2

GRID_W = 64
CTX_LEN = 256
N_EVEN = (DEPTH + 1) // 2
N_ODD = DEPTH // 2

MIX_W = D_MODEL
HALF_W = MIX_W // 2
S5_GROUP = 16
S5_GROUPS = HALF_W // S5_GROUP
S5_STATE = 64
RET_HEADS = 4
RET_DV = HALF_W // RET_HEADS
RET_DK = RET_DV // 2
RET_QK = RET_HEADS * RET_DK
RET_CHUNK = 128
ROPE_BASE = 10000.0
GDN_HEADS = 4
GDN_DK = HALF_W // GDN_HEADS
GDN_DV = GDN_DK
GDN_CHUNK = 64
LRU_W = HALF_W
LRU_BLOCKS = 8
LRU_BLOCK = LRU_W // LRU_BLOCKS
LRU_C = 8.0
CONV_K = 4
CONV_PAD_LEFT = 2
N_GROUPS = 4
EXPERTS_PER_GROUP = 8
N_EXPERTS = N_GROUPS * EXPERTS_PER_GROUP
TOP_K = 2
EXPERT_FF = D_MODEL // 2
N_MOD = 6
NORM_EPS = 1e-6
EVEN_SPLITS = [HALF_W, HALF_W + RET_QK, HALF_W + 2 * RET_QK, 2 * HALF_W + 2 * RET_QK]
EVEN_IN = 3 * HALF_W + 2 * RET_QK
ODD_SPLITS = [3 * HALF_W, 4 * HALF_W, 4 * HALF_W + 2 * GDN_HEADS, 4 * HALF_W + 4 * GDN_HEADS,
              4 * HALF_W + 4 * GDN_HEADS + LRU_W]
ODD_IN = 4 * HALF_W + 4 * GDN_HEADS + 2 * LRU_W

kernel_name = "hybrid_prefix_diffusion_block"

F32 = jnp.float32


def rms_norm(t, w):
    tf = t.astype(F32)
    y = tf * lax.rsqrt(jnp.mean(tf * tf, -1, keepdims=True) + NORM_EPS)
    return (y * w.astype(F32)).astype(t.dtype)


def adaln(cond, w, b):
    m = jax.nn.silu(cond) @ w + b
    return m.reshape(cond.shape[0], N_MOD, D_MODEL)


def modulate(t, shift, scale):
    return t * (1 + scale[:, None]) + shift[:, None]


def bwd_order(t, n_ctx):
    return jnp.concatenate([jnp.flip(t[:, :n_ctx], 1), jnp.flip(t[:, n_ctx:], 1)], 1)


def centred_dwconv(t, w, n_ctx):
    rhs = w[:, None, :]

    def conv(s):
        return lax.conv_general_dilated(
            s, rhs, window_strides=(1,), padding=[(CONV_PAD_LEFT, CONV_K - 1 - CONV_PAD_LEFT)],
            dimension_numbers=('NWC', 'WIO', 'NWC'), feature_group_count=s.shape[-1])

    return jnp.concatenate([conv(t[:, :n_ctx]), conv(t[:, n_ctx:])], 1)


def axial_rope(t, rows):
    half = t.shape[-1] // 2
    nf = half // 2
    row = jnp.repeat(jnp.arange(rows, dtype=F32), GRID_W)
    col = jnp.tile(jnp.arange(GRID_W, dtype=F32), rows)
    inv = ROPE_BASE ** (-jnp.arange(nf, dtype=F32) / nf)
    ang = jnp.concatenate([row[:, None] * inv, col[:, None] * inv], -1)[:, None, :]
    cos, sin = jnp.cos(ang), jnp.sin(ang)
    tf = t.astype(F32)
    t1, t2 = tf[..., :half], tf[..., half:]
    return jnp.concatenate([t1 * cos - t2 * sin, t1 * sin + t2 * cos], -1).astype(t.dtype)


def s5_scan(u, lam_re, lam_im, log_step, b_re, b_im, c_re, c_im):
    lr, li = lam_re.astype(F32), lam_im.astype(F32)
    dt = jnp.exp(log_step.astype(F32))[:, None]
    mag = jnp.exp(lr * dt)
    ar, ai = mag * jnp.cos(li * dt), mag * jnp.sin(li * dt)
    pr, pi = ar - 1.0, ai
    den = lr * lr + li * li
    zr, zi = (pr * lr + pi * li) / den, (pi * lr - pr * li) / den
    br, bi = b_re.astype(F32), b_im.astype(F32)
    bbr = zr[..., None] * br - zi[..., None] * bi
    bbi = zr[..., None] * bi + zi[..., None] * br
    xr = jnp.einsum('blgc,gnc->blgn', u, bbr)
    xi = jnp.einsum('blgc,gnc->blgn', u, bbi)
    ar_b = jnp.broadcast_to(ar, xr.shape)
    ai_b = jnp.broadcast_to(ai, xr.shape)

    def combine(e1, e2):
        a1r, a1i, b1r, b1i = e1
        a2r, a2i, b2r, b2i = e2
        return (a1r * a2r - a1i * a2i, a1r * a2i + a1i * a2r,
                a2r * b1r - a2i * b1i + b2r, a2r * b1i + a2i * b1r + b2i)

    _, _, hr, hi = lax.associative_scan(combine, (ar_b, ai_b, xr, xi), axis=1)
    return (jnp.einsum('blgn,gcn->blgc', hr, c_re.astype(F32))
            - jnp.einsum('blgn,gcn->blgc', hi, c_im.astype(F32)))


def s5_mixer(u, n_ctx, lam_re, lam_im, log_step, b_re, b_im, c_re, c_im, d_skip, glu_w, glu_b):
    bsz, L, _ = u.shape
    ug = u.astype(F32).reshape(bsz, L, S5_GROUPS, S5_GROUP)
    y_f = s5_scan(ug, lam_re[0], lam_im[0], log_step[0], b_re[0], b_im[0], c_re[0], c_im[0])
    y_b = bwd_order(s5_scan(bwd_order(ug, n_ctx), lam_re[1], lam_im[1], log_step[1],
                            b_re[1], b_im[1], c_re[1], c_im[1]), n_ctx)
    y = (y_f + y_b).reshape(bsz, L, HALF_W) + d_skip.astype(F32) * u.astype(F32)
    y = jax.nn.gelu(y).astype(u.dtype)
    return y * jax.nn.sigmoid(y @ glu_w + glu_b)


def retention_scan(q, k, v, log_gamma, strict):
    bsz, L, H, dk = q.shape
    dv = v.shape[-1]
    C = RET_CHUNK
    n = L // C
    qc = q.astype(F32).reshape(bsz, n, C, H, dk)
    kc = k.astype(F32).reshape(bsz, n, C, H, dk)
    vc = v.astype(F32).reshape(bsz, n, C, H, dv)
    pos = jnp.arange(C, dtype=F32)
    diff = pos[:, None] - pos[None, :]
    mask = (diff > 0) if strict else (diff >= 0)
    lg = log_gamma.astype(F32)
    dmat = jnp.where(mask[None], jnp.exp(lg[:, None, None] * jnp.where(mask, diff, 0.0)[None]), 0.0)
    scores = jnp.einsum('bnihd,bnjhd->bnhij', qc, kc) * dmat
    o_intra = jnp.einsum('bnhij,bnjhe->bnihe', scores, vc)
    k_dec = jnp.exp(lg[None, :] * (C - 1 - pos)[:, None])
    kv = jnp.einsum('bnjhd,bnjhe->nbhde', kc * k_dec[:, :, None], vc)
    chunk_dec = jnp.exp(lg * C)[None, :, None, None]

    def step(state, kv_n):
        return chunk_dec * state + kv_n, state

    _, states = lax.scan(step, jnp.zeros((bsz, H, dk, dv), F32), kv)
    q_dec = jnp.exp(lg[None, :] * (pos + 1)[:, None])
    o_inter = jnp.einsum('bnihd,nbhde->bnihe', qc * q_dec[:, :, None], states)
    return (o_intra + o_inter).reshape(bsz, L, H, dv)


def retention_mixer(q, k, v, gate, n_ctx, rows, ret_decay):
    bsz, L, _ = q.shape
    q = q.reshape(bsz, L, RET_HEADS, RET_DK)
    k = k.reshape(bsz, L, RET_HEADS, RET_DK)
    v = v.reshape(bsz, L, RET_HEADS, RET_DV)
    q = jnp.concatenate([q[:, :n_ctx], axial_rope(q[:, n_ctx:], rows)], 1)
    k = jnp.concatenate([k[:, :n_ctx], axial_rope(k[:, n_ctx:], rows)], 1) * (RET_DK ** -0.5)
    log_g = jax.nn.log_sigmoid(ret_decay.astype(F32))
    o = retention_scan(q, k, v, log_g[0], False)
    o = o + bwd_order(retention_scan(bwd_order(q, n_ctx), bwd_order(k, n_ctx), bwd_order(v, n_ctx),
                                     log_g[1], True), n_ctx)
    mu = jnp.mean(o, -1, keepdims=True)
    var = jnp.mean(jnp.square(o - mu), -1, keepdims=True)
    o = ((o - mu) * lax.rsqrt(var + NORM_EPS)).reshape(bsz, L, HALF_W)
    return (jax.nn.silu(gate.astype(F32)) * o).astype(gate.dtype)


def l2_normalize(t):
    tf = t.astype(F32)
    return tf * lax.rsqrt(jnp.sum(tf * tf, -1, keepdims=True) + NORM_EPS)


def gated_delta_scan(q, k, v, beta, g):
    bsz, L, H, dk = q.shape
    dv = v.shape[-1]
    C = GDN_CHUNK
    n = L // C
    chunk4 = lambda t: t.reshape(bsz, n, C, H, t.shape[-1]).transpose(0, 3, 1, 2, 4)
    chunk3 = lambda t: t.reshape(bsz, n, C, H).transpose(0, 3, 1, 2)
    qc, kc, vc = chunk4(q), chunk4(k), chunk4(v)
    bc, gc = chunk3(beta), chunk3(g)
    G = jnp.cumsum(gc, -1)
    pos = jnp.arange(C)
    strict = pos[:, None] > pos[None, :]
    incl = pos[:, None] >= pos[None, :]
    diffG = G[..., :, None] - G[..., None, :]
    decay_in = jnp.where(incl, jnp.exp(jnp.where(incl, diffG, 0.0)), 0.0)
    A = jnp.where(strict, jnp.einsum('bhnid,bhnjd->bhnij', kc, kc) * decay_in * bc[..., :, None], 0.0)
    rhs = jnp.concatenate([vc * bc[..., None], kc * (bc * jnp.exp(G))[..., None]], -1)
    sol = lax.linalg.triangular_solve(A + jnp.eye(C, dtype=F32), rhs, left_side=True, lower=True)
    u, w = sol[..., :dv], sol[..., dv:]
    qk = jnp.einsum('bhnid,bhnjd->bhnij', qc, kc) * decay_in
    q_g = qc * jnp.exp(G)[..., None]
    G_last = G[..., -1:]
    k_g = kc * jnp.exp(G_last - G)[..., None]
    d_last = jnp.exp(G_last[..., 0])
    xs = tuple(jnp.moveaxis(t, 2, 0) for t in (w, u, q_g, qk, k_g, d_last))

    def step(S, inp):
        w_n, u_n, qg_n, qk_n, kg_n, dl_n = inp
        v_new = u_n - jnp.einsum('bhcd,bhde->bhce', w_n, S)
        o = jnp.einsum('bhcd,bhde->bhce', qg_n, S) + jnp.einsum('bhij,bhje->bhie', qk_n, v_new)
        S = S * dl_n[..., None, None] + jnp.einsum('bhcd,bhce->bhde', kg_n, v_new)
        return S, o

    _, o = lax.scan(step, jnp.zeros((bsz, H, dk, dv), F32), xs)
    return o.transpose(1, 0, 3, 2, 4).reshape(bsz, L, H, dv)


def gdn_mixer(qkv, zg, beta_raw, a_raw, n_ctx, conv_w, a_log, dt_bias, norm_w):
    bsz, L, _ = qkv.shape
    qkv = jax.nn.silu(centred_dwconv(qkv, conv_w, n_ctx))
    q, k, v = jnp.split(qkv, 3, axis=-1)
    q = l2_normalize(q.reshape(bsz, L, GDN_HEADS, GDN_DK)) * (GDN_DK ** -0.5)
    k = l2_normalize(k.reshape(bsz, L, GDN_HEADS, GDN_DK))
    v = v.reshape(bsz, L, GDN_HEADS, GDN_DV).astype(F32)
    beta = jax.nn.sigmoid(beta_raw.astype(F32)).reshape(bsz, L, 2, GDN_HEADS)
    g = -jnp.exp(a_log.astype(F32)) * jax.nn.softplus(
        a_raw.astype(F32).reshape(bsz, L, 2, GDN_HEADS) + dt_bias.astype(F32))
    o = gated_delta_scan(q, k, v, beta[:, :, 0], g[:, :, 0])
    o = o + bwd_order(gated_delta_scan(bwd_order(q, n_ctx), bwd_order(k, n_ctx), bwd_order(v, n_ctx),
                                       bwd_order(beta[:, :, 1], n_ctx), bwd_order(g[:, :, 1], n_ctx)), n_ctx)
    o = o * lax.rsqrt(jnp.mean(o * o, -1, keepdims=True) + NORM_EPS) * norm_w.astype(F32)
    o = o * jax.nn.silu(zg.astype(F32).reshape(bsz, L, GDN_HEADS, GDN_DV))
    return o.reshape(bsz, L, HALF_W).astype(zg.dtype)


def rglru_scan(xs, w_a, b_a, w_x, b_x, lam):
    bsz, L, _ = xs.shape
    xb = xs.reshape(bsz, L, LRU_BLOCKS, LRU_BLOCK)
    r = jax.nn.sigmoid(jnp.einsum('blki,kij->blkj', xb, w_a.astype(F32)).reshape(bsz, L, LRU_W) + b_a)
    i = jax.nn.sigmoid(jnp.einsum('blki,kij->blkj', xb, w_x.astype(F32)).reshape(bsz, L, LRU_W) + b_x)
    log_a = (-LRU_C * r * jax.nn.softplus(-lam.astype(F32))).astype(F32)
    a = jnp.exp(log_a)
    b = (jnp.sqrt(-jnp.expm1(2.0 * log_a)) * (i * xs)).astype(F32)
    _, h = lax.associative_scan(lambda e1, e2: (e1[0] * e2[0], e2[0] * e1[1] + e2[1]), (a, b), axis=1)
    return h


def lru_mixer(lx, lg, n_ctx, conv_w, conv_b, w_a, b_a, w_x, b_x, lam):
    xs = (centred_dwconv(lx, conv_w, n_ctx) + conv_b).astype(F32)
    h = rglru_scan(xs, w_a[0], b_a[0], w_x[0], b_x[0], lam[0])
    h = h + bwd_order(rglru_scan(bwd_order(xs, n_ctx), w_a[1], b_a[1], w_x[1], b_x[1], lam[1]), n_ctx)
    return (h * jax.nn.gelu(lg.astype(F32))).astype(lx.dtype)


def hier_moe(h, rg_w, rg_b, re_w, re_b, w_gate, w_up, w_down):
    n_tok = h.shape[0]
    g_logits = (h @ rg_w + rg_b).astype(F32)
    g_idx = jnp.argmax(g_logits, -1)
    g_w = jnp.take_along_axis(jax.nn.softmax(g_logits, -1), g_idx[:, None], 1)
    e_logits = (h @ re_w + re_b).astype(F32).reshape(n_tok, N_GROUPS, EXPERTS_PER_GROUP)
    e_in = jnp.take_along_axis(e_logits, g_idx[:, None, None], 1)[:, 0]
    top_v, top_i = lax.top_k(e_in, TOP_K)
    top_w = jax.nn.softmax(top_v, -1) * g_w
    within = jnp.einsum('tk,tke->te', top_w, jax.nn.one_hot(top_i, EXPERTS_PER_GROUP, dtype=F32))
    combine = (jax.nn.one_hot(g_idx, N_GROUPS, dtype=F32)[:, :, None] * within[:, None, :]).astype(h.dtype)
    wg = w_gate.reshape(N_GROUPS, EXPERTS_PER_GROUP, D_MODEL, EXPERT_FF)
    wu = w_up.reshape(N_GROUPS, EXPERTS_PER_GROUP, D_MODEL, EXPERT_FF)
    wd = w_down.reshape(N_GROUPS, EXPERTS_PER_GROUP, EXPERT_FF, D_MODEL)
    out = jnp.zeros_like(h)
    for grp in range(N_GROUPS):
        act = jax.nn.silu(jnp.einsum('td,edf->tef', h, wg[grp])) * jnp.einsum('td,edf->tef', h, wu[grp])
        out = out + jnp.einsum('tef,efd->td', act * combine[:, grp, :, None], wd[grp])
    return out


def setup_inputs(seed: int = 0) -> dict:
    key = jax.random.key(seed)
    ks = iter(jax.random.split(key, 64))
    nrm = lambda shape, s: s * jax.random.normal(next(ks), shape, F32)
    uni = lambda shape, lo, hi: jax.random.uniform(next(ks), shape, F32, lo, hi)
    D = D_MODEL
    gam = 1.0 - 2.0 ** (-5.0 - np.arange(RET_HEADS))
    ret_logit = jnp.asarray(np.log(gam) - np.log1p(-gam), dtype=F32)
    dt = jnp.exp(uni((N_ODD, 2, GDN_HEADS), float(np.log(1e-3)), float(np.log(1e-1))))
    a0 = uni((N_ODD, 2, LRU_W), 0.9, 0.999) ** (1.0 / LRU_C)
    return {
        "x": nrm((BATCH, SEQ, D), 1.0),
        "c": nrm((BATCH, D), 1.0),
        "ctx": nrm((BATCH, CTX_LEN, D), 1.0),
        "c_ctx": nrm((D,), 1.0),
        "mod_w": nrm((DEPTH, D, N_MOD * D), 0.5 * D ** -0.5),
        "mod_b": nrm((DEPTH, N_MOD * D), 0.01),
        "norm_mix": 1.0 + nrm((DEPTH, D), 0.05),
        "norm_ffn": 1.0 + nrm((DEPTH, D), 0.05),
        "router_group_w": nrm((DEPTH, D, N_GROUPS), D ** -0.5),
        "router_group_b": nrm((DEPTH, N_GROUPS), 0.01),
        "router_expert_w": nrm((DEPTH, D, N_EXPERTS), D ** -0.5),
        "router_expert_b": nrm((DEPTH, N_EXPERTS), 0.01),
        "expert_w_gate": nrm((DEPTH, N_EXPERTS, D, EXPERT_FF), D ** -0.5),
        "expert_w_up": nrm((DEPTH, N_EXPERTS, D, EXPERT_FF), D ** -0.5),
        "expert_w_down": nrm((DEPTH, N_EXPERTS, EXPERT_FF, D), EXPERT_FF ** -0.5),
        "even_w_in": nrm((N_EVEN, D, EVEN_IN), D ** -0.5),
        "even_w_out": nrm((N_EVEN, MIX_W, D), MIX_W ** -0.5),
        "s5_lam_re": -0.5 + nrm((N_EVEN, 2, S5_GROUPS, S5_STATE), 0.01),
        "s5_lam_im": jnp.pi * jnp.arange(S5_STATE, dtype=F32) + nrm((N_EVEN, 2, S5_GROUPS, S5_STATE), 0.01),
        "s5_log_step": uni((N_EVEN, 2, S5_GROUPS), float(np.log(1e-3)), float(np.log(1e-1))),
        "s5_b_re": nrm((N_EVEN, 2, S5_GROUPS, S5_STATE, S5_GROUP), (2 * S5_GROUP) ** -0.5),
        "s5_b_im": nrm((N_EVEN, 2, S5_GROUPS, S5_STATE, S5_GROUP), (2 * S5_GROUP) ** -0.5),
        "s5_c_re": nrm((N_EVEN, 2, S5_GROUPS, S5_GROUP, S5_STATE), S5_STATE ** -0.5),
        "s5_c_im": nrm((N_EVEN, 2, S5_GROUPS, S5_GROUP, S5_STATE), S5_STATE ** -0.5),
        "s5_d": nrm((N_EVEN, HALF_W), 1.0),
        "s5_glu_w": nrm((N_EVEN, HALF_W, HALF_W), HALF_W ** -0.5),
        "s5_glu_b": nrm((N_EVEN, HALF_W), 0.01),
        "ret_decay": ret_logit + nrm((N_EVEN, 2, RET_HEADS), 0.1),
        "odd_w_in": nrm((N_ODD, D, ODD_IN), D ** -0.5),
        "odd_w_out": nrm((N_ODD, MIX_W, D), MIX_W ** -0.5),
        "gdn_conv_w": nrm((N_ODD, CONV_K, 3 * HALF_W), CONV_K ** -0.5),
        "gdn_a_log": jnp.log(uni((N_ODD, 2, GDN_HEADS), 1.0, 16.0)),
        "gdn_dt_bias": dt + jnp.log(-jnp.expm1(-dt)),
        "gdn_norm_w": 1.0 + nrm((N_ODD, GDN_DV), 0.05),
        "lru_conv_w": nrm((N_ODD, CONV_K, LRU_W), CONV_K ** -0.5),
        "lru_conv_b": nrm((N_ODD, LRU_W), 0.01),
        "lru_w_a": nrm((N_ODD, 2, LRU_BLOCKS, LRU_BLOCK, LRU_BLOCK), LRU_BLOCK ** -0.5),
        "lru_b_a": nrm((N_ODD, 2, LRU_W), 0.01),
        "lru_w_x": nrm((N_ODD, 2, LRU_BLOCKS, LRU_BLOCK, LRU_BLOCK), LRU_BLOCK ** -0.5),
        "lru_b_x": nrm((N_ODD, 2, LRU_W), 0.01),
        "lru_lam": jnp.log(a0) - jnp.log1p(-a0),
        "final_norm": 1.0 + nrm((D,), 0.05),
    }


def reference(x, c, ctx, c_ctx, mod_w, mod_b, norm_mix, norm_ffn, router_group_w, router_group_b,
              router_expert_w, router_expert_b, expert_w_gate, expert_w_up, expert_w_down,
              even_w_in, even_w_out, s5_lam_re, s5_lam_im, s5_log_step, s5_b_re, s5_b_im, s5_c_re,
              s5_c_im, s5_d, s5_glu_w, s5_glu_b, ret_decay, odd_w_in, odd_w_out, gdn_conv_w, gdn_a_log,
              gdn_dt_bias, gdn_norm_w, lru_conv_w, lru_conv_b, lru_w_a, lru_b_a, lru_w_x, lru_b_x,
              lru_lam, final_norm):
    bsz, n_lat, _ = x.shape
    n_ctx = ctx.shape[1]
    rows = n_lat // GRID_W
    for layer in range(DEPTH):
        last = layer == DEPTH - 1
        m_lat = adaln(c, mod_w[layer], mod_b[layer])
        m_ctx = adaln(c_ctx[None], mod_w[layer], mod_b[layer])
        h = jnp.concatenate([
            modulate(rms_norm(ctx, norm_mix[layer]), m_ctx[:, 0], m_ctx[:, 1]),
            modulate(rms_norm(x, norm_mix[layer]), m_lat[:, 0], m_lat[:, 1])], 1)
        if layer % 2 == 0:
            i = layer // 2
            z = h @ even_w_in[i]
            u, q, k, v, g = jnp.split(z, EVEN_SPLITS, axis=-1)
            y = jnp.concatenate([
                s5_mixer(u, n_ctx, s5_lam_re[i], s5_lam_im[i], s5_log_step[i], s5_b_re[i], s5_b_im[i],
                         s5_c_re[i], s5_c_im[i], s5_d[i], s5_glu_w[i], s5_glu_b[i]),
                retention_mixer(q, k, v, g, n_ctx, rows, ret_decay[i])], -1)
            w_out = even_w_out[i]
        else:
            i = layer // 2
            z = h @ odd_w_in[i]
            qkv, zg, beta_raw, a_raw, lx, lg = jnp.split(z, ODD_SPLITS, axis=-1)
            y = jnp.concatenate([
                gdn_mixer(qkv, zg, beta_raw, a_raw, n_ctx, gdn_conv_w[i], gdn_a_log[i], gdn_dt_bias[i],
                          gdn_norm_w[i]),
                lru_mixer(lx, lg, n_ctx, lru_conv_w[i], lru_conv_b[i], lru_w_a[i], lru_b_a[i], lru_w_x[i],
                          lru_b_x[i], lru_lam[i])], -1)
            w_out = odd_w_out[i]
        x = x + m_lat[:, 2, None] * (y[:, n_ctx:] @ w_out)
        moe_args = (router_group_w[layer], router_group_b[layer], router_expert_w[layer],
                    router_expert_b[layer], expert_w_gate[layer], expert_w_up[layer], expert_w_down[layer])
        if not last:
            ctx = ctx + m_ctx[:, 2, None] * (y[:, :n_ctx] @ w_out)
            hf = jnp.concatenate([
                modulate(rms_norm(ctx, norm_ffn[layer]), m_ctx[:, 3], m_ctx[:, 4]),
                modulate(rms_norm(x, norm_ffn[layer]), m_lat[:, 3], m_lat[:, 4])], 1)
            f = hier_moe(hf.reshape(-1, D_MODEL), *moe_args).reshape(hf.shape)
            ctx = ctx + m_ctx[:, 5, None] * f[:, :n_ctx]
            x = x + m_lat[:, 5, None] * f[:, n_ctx:]
        else:
            hf = modulate(rms_norm(x, norm_ffn[layer]), m_lat[:, 3], m_lat[:, 4])
            f = hier_moe(hf.reshape(-1, D_MODEL), *moe_args).reshape(hf.shape)
            x = x + m_lat[:, 5, None] * f
    return rms_norm(x, final_norm)
```

```python
import functools

import jax
import jax.numpy as jnp
from jax import lax
from jax.experimental import pallas as pl
from jax.experimental.pallas import tpu as pltpu

D_MODEL = 1024
DEPTH = 2
GRID_W = 64
HALF_W = D_MODEL // 2
S5_GROUP = 16
S5_GROUPS = HALF_W // S5_GROUP
S5_STATE = 64
RET_HEADS = 4
RET_DV = HALF_W // RET_HEADS
RET_DK = RET_DV // 2
RET_QK = RET_HEADS * RET_DK
RET_CHUNK = 128
ROPE_BASE = 10000.0
GDN_HEADS = 4
GDN_DK = HALF_W // GDN_HEADS
GDN_DV = GDN_DK
GDN_CHUNK = 64
LRU_W = HALF_W
LRU_BLOCKS = 8
LRU_BLOCK = LRU_W // LRU_BLOCKS
LRU_C = 8.0
CONV_K = 4
CONV_PAD_LEFT = 2
N_GROUPS = 4
EXPERTS_PER_GROUP = 8
N_EXPERTS = N_GROUPS * EXPERTS_PER_GROUP
TOP_K = 2
EXPERT_FF = D_MODEL // 2
N_MOD = 6
NORM_EPS = 1e-6
EVEN_SPLITS = [HALF_W, HALF_W + RET_QK, HALF_W + 2 * RET_QK, 2 * HALF_W + 2 * RET_QK]
ODD_SPLITS = [3 * HALF_W, 4 * HALF_W, 4 * HALF_W + 2 * GDN_HEADS, 4 * HALF_W + 4 * GDN_HEADS,
              4 * HALF_W + 4 * GDN_HEADS + LRU_W]

F32 = jnp.float32


def _rms_norm(t, w):
    y = t * lax.rsqrt(jnp.mean(t * t, -1, keepdims=True) + NORM_EPS)
    return y * w


def _adaln(cond, w, b):
    m = jax.nn.silu(cond) @ w + b
    return m.reshape(cond.shape[0], N_MOD, D_MODEL)


def _modulate(t, shift, scale):
    return t * (1 + scale[:, None]) + shift[:, None]


def _bwd_order(t, n_ctx):
    return jnp.concatenate([jnp.flip(t[:, :n_ctx], 1), jnp.flip(t[:, n_ctx:], 1)], 1)


def _centred_dwconv(t, w, n_ctx):
    rhs = w[:, None, :]

    def conv(s):
        return lax.conv_general_dilated(
            s, rhs, window_strides=(1,), padding=[(CONV_PAD_LEFT, CONV_K - 1 - CONV_PAD_LEFT)],
            dimension_numbers=('NWC', 'WIO', 'NWC'), feature_group_count=s.shape[-1])

    return jnp.concatenate([conv(t[:, :n_ctx]), conv(t[:, n_ctx:])], 1)


def _axial_rope(t, rows):
    half = t.shape[-1] // 2
    nf = half // 2
    row = jnp.repeat(jnp.arange(rows, dtype=F32), GRID_W)
    col = jnp.tile(jnp.arange(GRID_W, dtype=F32), rows)
    inv = ROPE_BASE ** (-jnp.arange(nf, dtype=F32) / nf)
    ang = jnp.concatenate([row[:, None] * inv, col[:, None] * inv], -1)[:, None, :]
    cos, sin = jnp.cos(ang), jnp.sin(ang)
    t1, t2 = t[..., :half], t[..., half:]
    return jnp.concatenate([t1 * cos - t2 * sin, t1 * sin + t2 * cos], -1)


def _s5_scan(u, lam_re, lam_im, log_step, b_re, b_im, c_re, c_im):
    lr, li = lam_re, lam_im
    dt = jnp.exp(log_step)[:, None]
    mag = jnp.exp(lr * dt)
    ar, ai = mag * jnp.cos(li * dt), mag * jnp.sin(li * dt)
    pr, pi = ar - 1.0, ai
    den = lr * lr + li * li
    zr, zi = (pr * lr + pi * li) / den, (pi * lr - pr * li) / den
    bbr = zr[..., None] * b_re - zi[..., None] * b_im
    bbi = zr[..., None] * b_im + zi[..., None] * b_re
    xr = jnp.einsum('blgc,gnc->blgn', u, bbr)
    xi = jnp.einsum('blgc,gnc->blgn', u, bbi)
    ar_b = jnp.broadcast_to(ar, xr.shape)
    ai_b = jnp.broadcast_to(ai, xr.shape)

    def combine(e1, e2):
        a1r, a1i, b1r, b1i = e1
        a2r, a2i, b2r, b2i = e2
        return (a1r * a2r - a1i * a2i, a1r * a2i + a1i * a2r,
                a2r * b1r - a2i * b1i + b2r, a2r * b1i + a2i * b1r + b2i)

    _, _, hr, hi = lax.associative_scan(combine, (ar_b, ai_b, xr, xi), axis=1)
    return (jnp.einsum('blgn,gcn->blgc', hr, c_re) - jnp.einsum('blgn,gcn->blgc', hi, c_im))


def _s5_mixer(u, n_ctx, lam_re, lam_im, log_step, b_re, b_im, c_re, c_im, d_skip, glu_w, glu_b):
    bsz, L, _ = u.shape
    ug = u.reshape(bsz, L, S5_GROUPS, S5_GROUP)
    y_f = _s5_scan(ug, lam_re[0], lam_im[0], log_step[0], b_re[0], b_im[0], c_re[0], c_im[0])
    y_b = _bwd_order(_s5_scan(_bwd_order(ug, n_ctx), lam_re[1], lam_im[1], log_step[1],
                              b_re[1], b_im[1], c_re[1], c_im[1]), n_ctx)
    y = (y_f + y_b).reshape(bsz, L, HALF_W) + d_skip * u
    y = jax.nn.gelu(y)
    return y * jax.nn.sigmoid(y @ glu_w + glu_b)


def _retention_scan(q, k, v, log_gamma, strict):
    bsz, L, H, dk = q.shape
    dv = v.shape[-1]
    C = RET_CHUNK
    n = L // C
    qc = q.reshape(bsz, n, C, H, dk)
    kc = k.reshape(bsz, n, C, H, dk)
    vc = v.reshape(bsz, n, C, H, dv)
    pos = jnp.arange(C, dtype=F32)
    diff = pos[:, None] - pos[None, :]
    mask = (diff > 0) if strict else (diff >= 0)
    lg = log_gamma
    dmat = jnp.where(mask[None], jnp.exp(lg[:, None, None] * jnp.where(mask, diff, 0.0)[None]), 0.0)
    scores = jnp.einsum('bnihd,bnjhd->bnhij', qc, kc) * dmat
    o_intra = jnp.einsum('bnhij,bnjhe->bnihe', scores, vc)
    k_dec = jnp.exp(lg[None, :] * (C - 1 - pos)[:, None])
    kv = jnp.einsum('bnjhd,bnjhe->nbhde', kc * k_dec[:, :, None], vc)
    chunk_dec = jnp.exp(lg * C)[None, :, None, None]

    def step(state, kv_n):
        return chunk_dec * state + kv_n, state

    _, states = lax.scan(step, jnp.zeros((bsz, H, dk, dv), F32), kv)
    q_dec = jnp.exp(lg[None, :] * (pos + 1)[:, None])
    o_inter = jnp.einsum('bnihd,nbhde->bnihe', qc * q_dec[:, :, None], states)
    return (o_intra + o_inter).reshape(bsz, L, H, dv)


def _retention_mixer(q, k, v, gate, n_ctx, rows, ret_decay):
    bsz, L, _ = q.shape
    q = q.reshape(bsz, L, RET_HEADS, RET_DK)
    k = k.reshape(bsz, L, RET_HEADS, RET_DK)
    v = v.reshape(bsz, L, RET_HEADS, RET_DV)
    q = jnp.concatenate([q[:, :n_ctx], _axial_rope(q[:, n_ctx:], rows)], 1)
    k = jnp.concatenate([k[:, :n_ctx], _axial_rope(k[:, n_ctx:], rows)], 1) * (RET_DK ** -0.5)
    log_g = jax.nn.log_sigmoid(ret_decay)
    o = _retention_scan(q, k, v, log_g[0], False)
    o = o + _bwd_order(_retention_scan(_bwd_order(q, n_ctx), _bwd_order(k, n_ctx), _bwd_order(v, n_ctx),
                                       log_g[1], True), n_ctx)
    mu = jnp.mean(o, -1, keepdims=True)
    var = jnp.mean(jnp.square(o - mu), -1, keepdims=True)
    o = ((o - mu) * lax.rsqrt(var + NORM_EPS)).reshape(bsz, L, HALF_W)
    return jax.nn.silu(gate) * o


def _l2_normalize(t):
    return t * lax.rsqrt(jnp.sum(t * t, -1, keepdims=True) + NORM_EPS)


def _gated_delta_scan(q, k, v, beta, g):
    bsz, L, H, dk = q.shape
    dv = v.shape[-1]
    C = GDN_CHUNK
    n = L // C
    chunk4 = lambda t: t.reshape(bsz, n, C, H, t.shape[-1]).transpose(0, 3, 1, 2, 4)
    chunk3 = lambda t: t.reshape(bsz, n, C, H).transpose(0, 3, 1, 2)
    qc, kc, vc = chunk4(q), chunk4(k), chunk4(v)
    bc, gc = chunk3(beta), chunk3(g)
    G = jnp.cumsum(gc, -1)
    pos = jnp.arange(C)
    strict = pos[:, None] > pos[None, :]
    incl = pos[:, None] >= pos[None, :]
    diffG = G[..., :, None] - G[..., None, :]
    decay_in = jnp.where(incl, jnp.exp(jnp.where(incl, diffG, 0.0)), 0.0)
    A = jnp.where(strict, jnp.einsum('bhnid,bhnjd->bhnij', kc, kc) * decay_in * bc[..., :, None], 0.0)
    rhs = jnp.concatenate([vc * bc[..., None], kc * (bc * jnp.exp(G))[..., None]], -1)
    sol = lax.linalg.triangular_solve(A + jnp.eye(C, dtype=F32), rhs, left_side=True, lower=True)
    u, w = sol[..., :dv], sol[..., dv:]
    qk = jnp.einsum('bhnid,bhnjd->bhnij', qc, kc) * decay_in
    q_g = qc * jnp.exp(G)[..., None]
    G_last = G[..., -1:]
    k_g = kc * jnp.exp(G_last - G)[..., None]
    d_last = jnp.exp(G_last[..., 0])
    xs = tuple(jnp.moveaxis(t, 2, 0) for t in (w, u, q_g, qk, k_g, d_last))

    def step(S, inp):
        w_n, u_n, qg_n, qk_n, kg_n, dl_n = inp
        v_new = u_n - jnp.einsum('bhcd,bhde->bhce', w_n, S)
        o = jnp.einsum('bhcd,bhde->bhce', qg_n, S) + jnp.einsum('bhij,bhje->bhie', qk_n, v_new)
        S = S * dl_n[..., None, None] + jnp.einsum('bhcd,bhce->bhde', kg_n, v_new)
        return S, o

    _, o = lax.scan(step, jnp.zeros((bsz, H, dk, dv), F32), xs)
    return o.transpose(1, 0, 3, 2, 4).reshape(bsz, L, H, dv)


def _gdn_mixer(qkv, zg, beta_raw, a_raw, n_ctx, conv_w, a_log, dt_bias, norm_w):
    bsz, L, _ = qkv.shape
    qkv = jax.nn.silu(_centred_dwconv(qkv, conv_w, n_ctx))
    q, k, v = jnp.split(qkv, 3, axis=-1)
    q = _l2_normalize(q.reshape(bsz, L, GDN_HEADS, GDN_DK)) * (GDN_DK ** -0.5)
    k = _l2_normalize(k.reshape(bsz, L, GDN_HEADS, GDN_DK))
    v = v.reshape(bsz, L, GDN_HEADS, GDN_DV)
    beta = jax.nn.sigmoid(beta_raw).reshape(bsz, L, 2, GDN_HEADS)
    g = -jnp.exp(a_log) * jax.nn.softplus(a_raw.reshape(bsz, L, 2, GDN_HEADS) + dt_bias)
    o = _gated_delta_scan(q, k, v, beta[:, :, 0], g[:, :, 0])
    o = o + _bwd_order(_gated_delta_scan(_bwd_order(q, n_ctx), _bwd_order(k, n_ctx), _bwd_order(v, n_ctx),
                                         _bwd_order(beta[:, :, 1], n_ctx), _bwd_order(g[:, :, 1], n_ctx)), n_ctx)
    o = o * lax.rsqrt(jnp.mean(o * o, -1, keepdims=True) + NORM_EPS) * norm_w
    o = o * jax.nn.silu(zg.reshape(bsz, L, GDN_HEADS, GDN_DV))
    return o.reshape(bsz, L, HALF_W)


def _rglru_scan(xs, w_a, b_a, w_x, b_x, lam):
    bsz, L, _ = xs.shape
    xb = xs.reshape(bsz, L, LRU_BLOCKS, LRU_BLOCK)
    r = jax.nn.sigmoid(jnp.einsum('blki,kij->blkj', xb, w_a).reshape(bsz, L, LRU_W) + b_a)
    i = jax.nn.sigmoid(jnp.einsum('blki,kij->blkj', xb, w_x).reshape(bsz, L, LRU_W) + b_x)
    log_a = -LRU_C * r * jax.nn.softplus(-lam)
    a = jnp.exp(log_a)
    b = jnp.sqrt(-jnp.expm1(2.0 * log_a)) * (i * xs)
    _, h = lax.associative_scan(lambda e1, e2: (e1[0] * e2[0], e2[0] * e1[1] + e2[1]), (a, b), axis=1)
    return h


def _lru_mixer(lx, lg, n_ctx, conv_w, conv_b, w_a, b_a, w_x, b_x, lam):
    xs = _centred_dwconv(lx, conv_w, n_ctx) + conv_b
    h = _rglru_scan(xs, w_a[0], b_a[0], w_x[0], b_x[0], lam[0])
    h = h + _bwd_order(_rglru_scan(_bwd_order(xs, n_ctx), w_a[1], b_a[1], w_x[1], b_x[1], lam[1]), n_ctx)
    return h * jax.nn.gelu(lg)


def _hier_moe(h, rg_w, rg_b, re_w, re_b, w_gate, w_up, w_down):
    n_tok = h.shape[0]
    g_logits = h @ rg_w + rg_b
    g_idx = jnp.argmax(g_logits, -1)
    g_w = jnp.take_along_axis(jax.nn.softmax(g_logits, -1), g_idx[:, None], 1)
    e_logits = (h @ re_w + re_b).reshape(n_tok, N_GROUPS, EXPERTS_PER_GROUP)
    e_in = jnp.take_along_axis(e_logits, g_idx[:, None, None], 1)[:, 0]
    top_v, top_i = lax.top_k(e_in, TOP_K)
    top_w = jax.nn.softmax(top_v, -1) * g_w
    within = jnp.einsum('tk,tke->te', top_w, jax.nn.one_hot(top_i, EXPERTS_PER_GROUP, dtype=F32))
    combine = jax.nn.one_hot(g_idx, N_GROUPS, dtype=F32)[:, :, None] * within[:, None, :]
    wg = w_gate.reshape(N_GROUPS, EXPERTS_PER_GROUP, D_MODEL, EXPERT_FF)
    wu = w_up.reshape(N_GROUPS, EXPERTS_PER_GROUP, D_MODEL, EXPERT_FF)
    wd = w_down.reshape(N_GROUPS, EXPERTS_PER_GROUP, EXPERT_FF, D_MODEL)
    out = jnp.zeros_like(h)
    for grp in range(N_GROUPS):
        act = jax.nn.silu(jnp.einsum('td,edf->tef', h, wg[grp])) * jnp.einsum('td,edf->tef', h, wu[grp])
        out = out + jnp.einsum('tef,efd->td', act * combine[:, grp, :, None], wd[grp])
    return out


FINAL_NORM_ROWS = 512


def _final_norm_body(x_ref, w_ref, o_ref):
    x = x_ref[...]
    y = x * lax.rsqrt(jnp.mean(x * x, -1, keepdims=True) + NORM_EPS)
    o_ref[...] = y * w_ref[...]


def _final_norm(x2d, w):
    rows, d = x2d.shape
    return pl.pallas_call(
        _final_norm_body,
        grid=(rows // FINAL_NORM_ROWS,),
        in_specs=[pl.BlockSpec((FINAL_NORM_ROWS, d), lambda i: (i, 0)),
                  pl.BlockSpec((1, d), lambda i: (0, 0))],
        out_specs=pl.BlockSpec((FINAL_NORM_ROWS, d), lambda i: (i, 0)),
        out_shape=jax.ShapeDtypeStruct((rows, d), F32),
        name="final_norm",
    )(x2d, w.reshape(1, d))


def kernel(x, c, ctx, c_ctx, mod_w, mod_b, norm_mix, norm_ffn, router_group_w, router_group_b,
           router_expert_w, router_expert_b, expert_w_gate, expert_w_up, expert_w_down,
           even_w_in, even_w_out, s5_lam_re, s5_lam_im, s5_log_step, s5_b_re, s5_b_im, s5_c_re,
           s5_c_im, s5_d, s5_glu_w, s5_glu_b, ret_decay, odd_w_in, odd_w_out, gdn_conv_w, gdn_a_log,
           gdn_dt_bias, gdn_norm_w, lru_conv_w, lru_conv_b, lru_w_a, lru_b_a, lru_w_x, lru_b_x,
           lru_lam, final_norm):
    bsz, n_lat, _ = x.shape
    n_ctx = ctx.shape[1]
    rows = n_lat // GRID_W
    for layer in range(DEPTH):
        last = layer == DEPTH - 1
        m_lat = _adaln(c, mod_w[layer], mod_b[layer])
        m_ctx = _adaln(c_ctx[None], mod_w[layer], mod_b[layer])
        h = jnp.concatenate([
            _modulate(_rms_norm(ctx, norm_mix[layer]), m_ctx[:, 0], m_ctx[:, 1]),
            _modulate(_rms_norm(x, norm_mix[layer]), m_lat[:, 0], m_lat[:, 1])], 1)
        i = layer // 2
        if layer % 2 == 0:
            z = h @ even_w_in[i]
            u, q, k, v, g = jnp.split(z, EVEN_SPLITS, axis=-1)
            y = jnp.concatenate([
                _s5_mixer(u, n_ctx, s5_lam_re[i], s5_lam_im[i], s5_log_step[i], s5_b_re[i], s5_b_im[i],
                          s5_c_re[i], s5_c_im[i], s5_d[i], s5_glu_w[i], s5_glu_b[i]),
                _retention_mixer(q, k, v, g, n_ctx, rows, ret_decay[i])], -1)
            w_out = even_w_out[i]
        else:
            z = h @ odd_w_in[i]
            qkv, zg, beta_raw, a_raw, lx, lg = jnp.split(z, ODD_SPLITS, axis=-1)
            y = jnp.concatenate([
                _gdn_mixer(qkv, zg, beta_raw, a_raw, n_ctx, gdn_conv_w[i], gdn_a_log[i], gdn_dt_bias[i],
                           gdn_norm_w[i]),
                _lru_mixer(lx, lg, n_ctx, lru_conv_w[i], lru_conv_b[i], lru_w_a[i], lru_b_a[i], lru_w_x[i],
                           lru_b_x[i], lru_lam[i])], -1)
            w_out = odd_w_out[i]
        x = x + m_lat[:, 2, None] * (y[:, n_ctx:] @ w_out)
        moe_args = (router_group_w[layer], router_group_b[layer], router_expert_w[layer],
                    router_expert_b[layer], expert_w_gate[layer], expert_w_up[layer], expert_w_down[layer])
        if not last:
            ctx = ctx + m_ctx[:, 2, None] * (y[:, :n_ctx] @ w_out)
            hf = jnp.concatenate([
                _modulate(_rms_norm(ctx, norm_ffn[layer]), m_ctx[:, 3], m_ctx[:, 4]),
                _modulate(_rms_norm(x, norm_ffn[layer]), m_lat[:, 3], m_lat[:, 4])], 1)
            f = _hier_moe(hf.reshape(-1, D_MODEL), *moe_args).reshape(hf.shape)
            ctx = ctx + m_ctx[:, 5, None] * f[:, :n_ctx]
            x = x + m_lat[:, 5, None] * f[:, n_ctx:]
        else:
            hf = _modulate(_rms_norm(x, norm_ffn[layer]), m_lat[:, 3], m_lat[:, 4])
            f = _hier_moe(hf.reshape(-1, D_MODEL), *moe_args).reshape(hf.shape)
            x = x + m_lat[:, 5, None] * f
    return _final_norm(x.reshape(bsz * n_lat, D_MODEL), final_norm).reshape(bsz, n_lat, D_MODEL)
```

```python
import functools

import jax
import jax.numpy as jnp
from jax import lax
from jax.experimental import pallas as pl
from jax.experimental.pallas import tpu as pltpu

D_MODEL = 1024
DEPTH = 2
GRID_W = 64
HALF_W = D_MODEL // 2
S5_GROUP = 16
S5_GROUPS = HALF_W // S5_GROUP
S5_STATE = 64
RET_HEADS = 4
RET_DV = HALF_W // RET_HEADS
RET_DK = RET_DV // 2
RET_QK = RET_HEADS * RET_DK
RET_CHUNK = 128
ROPE_BASE = 10000.0
GDN_HEADS = 4
GDN_DK = HALF_W // GDN_HEADS
GDN_DV = GDN_DK
GDN_CHUNK = 64
LRU_W = HALF_W
LRU_BLOCKS = 8
LRU_BLOCK = LRU_W // LRU_BLOCKS
LRU_C = 8.0
CONV_K = 4
CONV_PAD_LEFT = 2
N_GROUPS = 4
EXPERTS_PER_GROUP = 8
N_EXPERTS = N_GROUPS * EXPERTS_PER_GROUP
TOP_K = 2
EXPERT_FF = D_MODEL // 2
N_MOD = 6
NORM_EPS = 1e-6
EVEN_SPLITS = [HALF_W, HALF_W + RET_QK, HALF_W + 2 * RET_QK, 2 * HALF_W + 2 * RET_QK]
EVEN_IN = 3 * HALF_W + 2 * RET_QK
ODD_SPLITS = [3 * HALF_W, 4 * HALF_W, 4 * HALF_W + 2 * GDN_HEADS, 4 * HALF_W + 4 * GDN_HEADS,
              4 * HALF_W + 4 * GDN_HEADS + LRU_W]

F32 = jnp.float32
BF16 = jnp.bfloat16

SUBLANES = 8
LANES = 128
BATCH = SUBLANES
VMEM_LIMIT = 48 * 1024 * 1024

ROW_TILE = 512
S5_STEPS = 64
S5_SLABS = 4
S5_SLAB_CH = HALF_W // S5_SLABS
S5_SLAB_STATE = (S5_GROUPS // S5_SLABS) * S5_STATE
MOE_TILE = 256
ODD_PAD = 256
ODD_IN_PADDED = 4 * HALF_W + ODD_PAD + 2 * LRU_W
ROUTER_PAD = LANES


def _dot(a, b):
    return jnp.dot(a, b, preferred_element_type=F32)


def _params(*sem):
    return pltpu.CompilerParams(dimension_semantics=sem, vmem_limit_bytes=VMEM_LIMIT)


def _bwd_order(t, n_ctx):
    return jnp.concatenate([jnp.flip(t[:, :n_ctx], 1), jnp.flip(t[:, n_ctx:], 1)], 1)


def _centred_dwconv(t, w, n_ctx):
    rhs = w[:, None, :]

    def conv(s):
        return lax.conv_general_dilated(
            s, rhs, window_strides=(1,), padding=[(CONV_PAD_LEFT, CONV_K - 1 - CONV_PAD_LEFT)],
            dimension_numbers=('NWC', 'WIO', 'NWC'), feature_group_count=s.shape[-1])

    return jnp.concatenate([conv(t[:, :n_ctx]), conv(t[:, n_ctx:])], 1)


def _axial_rope(t, rows):
    half = t.shape[-1] // 2
    nf = half // 2
    row = jnp.repeat(jnp.arange(rows, dtype=F32), GRID_W)
    col = jnp.tile(jnp.arange(GRID_W, dtype=F32), rows)
    inv = ROPE_BASE ** (-jnp.arange(nf, dtype=F32) / nf)
    ang = jnp.concatenate([row[:, None] * inv, col[:, None] * inv], -1)[:, None, :]
    cos, sin = jnp.cos(ang), jnp.sin(ang)
    t1, t2 = t[..., :half], t[..., half:]
    return jnp.concatenate([t1 * cos - t2 * sin, t1 * sin + t2 * cos], -1)


def _retention_scan(q, k, v, log_gamma, strict):
    bsz, L, H, dk = q.shape
    dv = v.shape[-1]
    C = RET_CHUNK
    n = L // C
    qc = q.reshape(bsz, n, C, H, dk)
    kc = k.reshape(bsz, n, C, H, dk)
    vc = v.reshape(bsz, n, C, H, dv)
    pos = jnp.arange(C, dtype=F32)
    diff = pos[:, None] - pos[None, :]
    mask = (diff > 0) if strict else (diff >= 0)
    lg = log_gamma
    dmat = jnp.where(mask[None], jnp.exp(lg[:, None, None] * jnp.where(mask, diff, 0.0)[None]), 0.0)
    scores = jnp.einsum('bnihd,bnjhd->bnhij', qc, kc) * dmat
    o_intra = jnp.einsum('bnhij,bnjhe->bnihe', scores, vc)
    k_dec = jnp.exp(lg[None, :] * (C - 1 - pos)[:, None])
    kv = jnp.einsum('bnjhd,bnjhe->nbhde', kc * k_dec[:, :, None], vc)
    chunk_dec = jnp.exp(lg * C)[None, :, None, None]

    def step(state, kv_n):
        return chunk_dec * state + kv_n, state

    _, states = lax.scan(step, jnp.zeros((bsz, H, dk, dv), F32), kv)
    q_dec = jnp.exp(lg[None, :] * (pos + 1)[:, None])
    o_inter = jnp.einsum('bnihd,nbhde->bnihe', qc * q_dec[:, :, None], states)
    return (o_intra + o_inter).reshape(bsz, L, H, dv)


def _retention_mixer(q, k, v, gate, n_ctx, rows, ret_decay):
    bsz, L, _ = q.shape
    q = q.reshape(bsz, L, RET_HEADS, RET_DK)
    k = k.reshape(bsz, L, RET_HEADS, RET_DK)
    v = v.reshape(bsz, L, RET_HEADS, RET_DV)
    q = jnp.concatenate([q[:, :n_ctx], _axial_rope(q[:, n_ctx:], rows)], 1)
    k = jnp.concatenate([k[:, :n_ctx], _axial_rope(k[:, n_ctx:], rows)], 1) * (RET_DK ** -0.5)
    log_g = jax.nn.log_sigmoid(ret_decay)
    o = _retention_scan(q, k, v, log_g[0], False)
    o = o + _bwd_order(_retention_scan(_bwd_order(q, n_ctx), _bwd_order(k, n_ctx), _bwd_order(v, n_ctx),
                                       log_g[1], True), n_ctx)
    mu = jnp.mean(o, -1, keepdims=True)
    var = jnp.mean(jnp.square(o - mu), -1, keepdims=True)
    o = ((o - mu) * lax.rsqrt(var + NORM_EPS)).reshape(bsz, L, HALF_W)
    return jax.nn.silu(gate) * o


def _l2_normalize(t):
    return t * lax.rsqrt(jnp.sum(t * t, -1, keepdims=True) + NORM_EPS)


def _gated_delta_scan(q, k, v, beta, g):
    bsz, L, H, dk = q.shape
    dv = v.shape[-1]
    C = GDN_CHUNK
    n = L // C
    chunk4 = lambda t: t.reshape(bsz, n, C, H, t.shape[-1]).transpose(0, 3, 1, 2, 4)
    chunk3 = lambda t: t.reshape(bsz, n, C, H).transpose(0, 3, 1, 2)
    qc, kc, vc = chunk4(q), chunk4(k), chunk4(v)
    bc, gc = chunk3(beta), chunk3(g)
    G = jnp.cumsum(gc, -1)
    pos = jnp.arange(C)
    strict = pos[:, None] > pos[None, :]
    incl = pos[:, None] >= pos[None, :]
    diffG = G[..., :, None] - G[..., None, :]
    decay_in = jnp.where(incl, jnp.exp(jnp.where(incl, diffG, 0.0)), 0.0)
    A = jnp.where(strict, jnp.einsum('bhnid,bhnjd->bhnij', kc, kc) * decay_in * bc[..., :, None], 0.0)
    rhs = jnp.concatenate([vc * bc[..., None], kc * (bc * jnp.exp(G))[..., None]], -1)
    sol = lax.linalg.triangular_solve(A + jnp.eye(C, dtype=F32), rhs, left_side=True, lower=True)
    u, w = sol[..., :dv], sol[..., dv:]
    qk = jnp.einsum('bhnid,bhnjd->bhnij', qc, kc) * decay_in
    q_g = qc * jnp.exp(G)[..., None]
    G_last = G[..., -1:]
    k_g = kc * jnp.exp(G_last - G)[..., None]
    d_last = jnp.exp(G_last[..., 0])
    xs = tuple(jnp.moveaxis(t, 2, 0) for t in (w, u, q_g, qk, k_g, d_last))

    def step(S, inp):
        w_n, u_n, qg_n, qk_n, kg_n, dl_n = inp
        v_new = u_n - jnp.einsum('bhcd,bhde->bhce', w_n, S)
        o = jnp.einsum('bhcd,bhde->bhce', qg_n, S) + jnp.einsum('bhij,bhje->bhie', qk_n, v_new)
        S = S * dl_n[..., None, None] + jnp.einsum('bhcd,bhce->bhde', kg_n, v_new)
        return S, o

    _, o = lax.scan(step, jnp.zeros((bsz, H, dk, dv), F32), xs)
    return o.transpose(1, 0, 3, 2, 4).reshape(bsz, L, H, dv)


def _gdn_mixer(qkv, zg, beta_raw, a_raw, n_ctx, conv_w, a_log, dt_bias, norm_w):
    bsz, L, _ = qkv.shape
    qkv = jax.nn.silu(_centred_dwconv(qkv, conv_w, n_ctx))
    q, k, v = jnp.split(qkv, 3, axis=-1)
    q = _l2_normalize(q.reshape(bsz, L, GDN_HEADS, GDN_DK)) * (GDN_DK ** -0.5)
    k = _l2_normalize(k.reshape(bsz, L, GDN_HEADS, GDN_DK))
    v = v.reshape(bsz, L, GDN_HEADS, GDN_DV)
    beta = jax.nn.sigmoid(beta_raw).reshape(bsz, L, 2, GDN_HEADS)
    g = -jnp.exp(a_log) * jax.nn.softplus(a_raw.reshape(bsz, L, 2, GDN_HEADS) + dt_bias)
    o = _gated_delta_scan(q, k, v, beta[:, :, 0], g[:, :, 0])
    o = o + _bwd_order(_gated_delta_scan(_bwd_order(q, n_ctx), _bwd_order(k, n_ctx), _bwd_order(v, n_ctx),
                                         _bwd_order(beta[:, :, 1], n_ctx), _bwd_order(g[:, :, 1], n_ctx)), n_ctx)
    o = o * lax.rsqrt(jnp.mean(o * o, -1, keepdims=True) + NORM_EPS) * norm_w
    o = o * jax.nn.silu(zg.reshape(bsz, L, GDN_HEADS, GDN_DV))
    return o.reshape(bsz, L, HALF_W)


def _rglru_scan(xs, w_a, b_a, w_x, b_x, lam):
    bsz, L, _ = xs.shape
    xb = xs.reshape(bsz, L, LRU_BLOCKS, LRU_BLOCK)
    r = jax.nn.sigmoid(jnp.einsum('blki,kij->blkj', xb, w_a).reshape(bsz, L, LRU_W) + b_a)
    i = jax.nn.sigmoid(jnp.einsum('blki,kij->blkj', xb, w_x).reshape(bsz, L, LRU_W) + b_x)
    log_a = -LRU_C * r * jax.nn.softplus(-lam)
    a = jnp.exp(log_a)
    b = jnp.sqrt(-jnp.expm1(2.0 * log_a)) * (i * xs)
    _, h = lax.associative_scan(lambda e1, e2: (e1[0] * e2[0], e2[0] * e1[1] + e2[1]), (a, b), axis=1)
    return h


def _lru_mixer(lx, lg, n_ctx, conv_w, conv_b, w_a, b_a, w_x, b_x, lam):
    xs = _centred_dwconv(lx, conv_w, n_ctx) + conv_b
    h = _rglru_scan(xs, w_a[0], b_a[0], w_x[0], b_x[0], lam[0])
    h = h + _bwd_order(_rglru_scan(_bwd_order(xs, n_ctx), w_a[1], b_a[1], w_x[1], b_x[1], lam[1]), n_ctx)
    return h * jax.nn.gelu(lg)


def _to_batch_major(t, n_steps):
    return t.reshape(n_steps, BATCH, t.shape[-1]).transpose(1, 0, 2)


def _to_time_major(t):
    return t.transpose(1, 0, 2).reshape(t.shape[0] * t.shape[1], t.shape[2])


ADALN_COLS = 512
ADALN_ROWS = 2 * BATCH


def _adaln_body(c_ref, w_ref, b_ref, o_ref):
    c = c_ref[...]
    s = (c * jax.nn.sigmoid(c)).astype(BF16)
    o_ref[0] = _dot(s, w_ref[0].astype(BF16)) + b_ref[0]


def _adaln_all(cond16, mod_w, mod_b):
    n_out = mod_w.shape[-1]
    return pl.pallas_call(
        _adaln_body,
        grid=(DEPTH, n_out // ADALN_COLS),
        in_specs=[pl.BlockSpec((ADALN_ROWS, D_MODEL), lambda l, j: (0, 0)),
                  pl.BlockSpec((1, D_MODEL, ADALN_COLS), lambda l, j: (l, 0, j)),
                  pl.BlockSpec((1, 1, ADALN_COLS), lambda l, j: (l, 0, j))],
        out_specs=pl.BlockSpec((1, ADALN_ROWS, ADALN_COLS), lambda l, j: (l, 0, j)),
        out_shape=jax.ShapeDtypeStruct((DEPTH, ADALN_ROWS, n_out), F32),
        compiler_params=_params("arbitrary", "arbitrary"),
        name="adaln",
    )(cond16, mod_w, mod_b.reshape(DEPTH, 1, n_out))


def _col_chunks(n, width=512):
    return [(c0, min(width, n - c0)) for c0 in range(0, n, width)]


def _norm_modulate(x3, nw, shift, scale):
    y = x3 * lax.rsqrt(jnp.mean(x3 * x3, -1, keepdims=True) + NORM_EPS) * nw
    return y * (1.0 + scale) + shift


def _in_proj_body(x_ref, nw_ref, sh_ref, sc_ref, w_ref, o_ref):
    tm = x_ref.shape[0]
    x3 = x_ref[...].reshape(tm // BATCH, BATCH, D_MODEL)
    h = _norm_modulate(x3, nw_ref[...], sh_ref[0], sc_ref[0]).reshape(tm, D_MODEL).astype(BF16)
    for c0, cw in _col_chunks(w_ref.shape[1]):
        o_ref[:, c0:c0 + cw] = _dot(h, w_ref[:, c0:c0 + cw])


def _in_proj(xs, norm_w, shift2, scale2, w_bf16, ctx_tiles):
    rows = xs.shape[0]
    n = w_bf16.shape[1]
    kind = lambda r: (jnp.where(r < ctx_tiles, 0, 1), 0, 0)
    return pl.pallas_call(
        _in_proj_body,
        grid=(rows // ROW_TILE,),
        in_specs=[pl.BlockSpec((ROW_TILE, D_MODEL), lambda r: (r, 0)),
                  pl.BlockSpec((1, D_MODEL), lambda r: (0, 0)),
                  pl.BlockSpec((1, BATCH, D_MODEL), kind),
                  pl.BlockSpec((1, BATCH, D_MODEL), kind),
                  pl.BlockSpec((D_MODEL, n), lambda r: (0, 0))],
        out_specs=pl.BlockSpec((ROW_TILE, n), lambda r: (r, 0)),
        out_shape=jax.ShapeDtypeStruct((rows, n), F32),
        compiler_params=_params("arbitrary"),
        name="in_proj",
    )(xs, norm_w.reshape(1, D_MODEL), shift2, scale2, w_bf16)


def _s5_discretize(lam_re, lam_im, log_step, b_re, b_im, c_re, c_im):
    lr, li = lam_re, lam_im
    dt = jnp.exp(log_step)[:, None]
    mag = jnp.exp(lr * dt)
    ar, ai = mag * jnp.cos(li * dt), mag * jnp.sin(li * dt)
    pr, pi = ar - 1.0, ai
    den = lr * lr + li * li
    zr, zi = (pr * lr + pi * li) / den, (pi * lr - pr * li) / den
    bbr = zr[..., None] * b_re - zi[..., None] * b_im
    bbi = zr[..., None] * b_im + zi[..., None] * b_re
    gps = S5_GROUPS // S5_SLABS
    eye = jnp.eye(gps, dtype=F32)

    def in_slab(m):
        m = m.reshape(S5_SLABS, gps, S5_STATE, S5_GROUP)
        return jnp.einsum('sgnc,gh->sgchn', m, eye).reshape(S5_SLABS, S5_SLAB_CH, S5_SLAB_STATE)

    def out_slab(m):
        m = m.reshape(S5_SLABS, gps, S5_GROUP, S5_STATE)
        return jnp.einsum('sgcn,gh->sgnhc', m, eye).reshape(S5_SLABS, S5_SLAB_STATE, S5_SLAB_CH)

    bb = jnp.concatenate([in_slab(bbr), in_slab(bbi)], -1)
    cc = jnp.concatenate([out_slab(c_re), out_slab(-c_im)], 1)
    a = jnp.concatenate([ar.reshape(S5_SLABS, S5_SLAB_STATE), ai.reshape(S5_SLABS, S5_SLAB_STATE)], -1)
    return bb.astype(BF16), cc.astype(BF16), a.reshape(1, S5_SLABS * 2 * S5_SLAB_STATE)


def _s5_body(u_ref, bb_ref, cc_ref, a_ref, y_ref, x_scr, h_scr):
    d = pl.program_id(0)
    rows = u_ref.shape[0]
    steps = rows // BATCH
    sw = 2 * S5_SLAB_STATE

    @pl.when(pl.program_id(1) == 0)
    def _():
        h_scr[...] = jnp.zeros_like(h_scr)

    u = u_ref[...].astype(BF16)
    for s in range(S5_SLABS):
        x_scr[:, s * sw:(s + 1) * sw] = _dot(u[:, s * S5_SLAB_CH:(s + 1) * S5_SLAB_CH], bb_ref[0, s])

    for s in range(S5_SLABS):
        re0, im0 = s * sw, s * sw + S5_SLAB_STATE
        ar = jnp.broadcast_to(a_ref[0, :, re0:im0], (BATCH, S5_SLAB_STATE))
        ai = jnp.broadcast_to(a_ref[0, :, im0:im0 + S5_SLAB_STATE], (BATCH, S5_SLAB_STATE))

        def step(i, carry):
            hr, hi = carry
            t = jnp.where(d == 0, i, steps - 1 - i)
            r0 = pl.multiple_of(t * BATCH, BATCH)
            xr = x_scr[pl.ds(r0, BATCH), re0:im0]
            xi = x_scr[pl.ds(r0, BATCH), im0:im0 + S5_SLAB_STATE]
            nr = ar * hr - ai * hi + xr
            ni = ar * hi + ai * hr + xi
            x_scr[pl.ds(r0, BATCH), re0:im0] = nr
            x_scr[pl.ds(r0, BATCH), im0:im0 + S5_SLAB_STATE] = ni
            return nr, ni

        hr, hi = lax.fori_loop(0, steps, step,
                               (h_scr[:, re0:im0], h_scr[:, im0:im0 + S5_SLAB_STATE]), unroll=4)
        h_scr[:, re0:im0] = hr
        h_scr[:, im0:im0 + S5_SLAB_STATE] = hi

    for s in range(S5_SLABS):
        y_ref[0, :, s * S5_SLAB_CH:(s + 1) * S5_SLAB_CH] = _dot(
            x_scr[:, s * sw:(s + 1) * sw].astype(BF16), cc_ref[0, s])


def _scan_chunk(d, i, ctx_chunks, all_chunks):
    back = jnp.where(i < ctx_chunks, ctx_chunks - 1 - i, all_chunks + ctx_chunks - 1 - i)
    return jnp.where(d == 0, i, back)


def _s5_scan(z, bb, cc, a, n_ctx, n_all):
    rows = n_all * BATCH
    blk = S5_STEPS * BATCH
    ctx_chunks, all_chunks = n_ctx // S5_STEPS, n_all // S5_STEPS
    chunk = lambda d, i: _scan_chunk(d, i, ctx_chunks, all_chunks)
    state_w = S5_SLABS * 2 * S5_SLAB_STATE
    return pl.pallas_call(
        _s5_body,
        grid=(2, all_chunks),
        in_specs=[pl.BlockSpec((blk, HALF_W), lambda d, i: (chunk(d, i), 0)),
                  pl.BlockSpec((1, S5_SLABS, S5_SLAB_CH, 2 * S5_SLAB_STATE), lambda d, i: (d, 0, 0, 0)),
                  pl.BlockSpec((1, S5_SLABS, 2 * S5_SLAB_STATE, S5_SLAB_CH), lambda d, i: (d, 0, 0, 0)),
                  pl.BlockSpec((1, 1, state_w), lambda d, i: (d, 0, 0))],
        out_specs=pl.BlockSpec((1, blk, HALF_W), lambda d, i: (d, chunk(d, i), 0)),
        out_shape=jax.ShapeDtypeStruct((2, rows, HALF_W), F32),
        scratch_shapes=[pltpu.VMEM((blk, state_w), F32), pltpu.VMEM((BATCH, state_w), F32)],
        compiler_params=_params("arbitrary", "arbitrary"),
        name="s5_scan",
    )(z, bb, cc, a)


def _residual_norm_route(upd, x_ref, g2_ref, nw_ref, sh_ref, sc_ref, wrh_ref, wrl_ref, rb_ref,
                         xo_ref, hf_ref, lg_ref):
    tm = x_ref.shape[0]
    x3 = x_ref[...].reshape(tm // BATCH, BATCH, D_MODEL) + g2_ref[0] * upd.reshape(tm // BATCH, BATCH, D_MODEL)
    xo_ref[...] = x3.reshape(tm, D_MODEL)
    hf = _norm_modulate(x3, nw_ref[...], sh_ref[0], sc_ref[0]).reshape(tm, D_MODEL)
    hf_ref[...] = hf
    hi = hf.astype(BF16)
    lo = (hf - hi.astype(F32)).astype(BF16)
    lg_ref[...] = _dot(hi, wrh_ref[...]) + _dot(hi, wrl_ref[...]) + _dot(lo, wrh_ref[...]) + rb_ref[...]


def _gelu_tanh(x):
    return 0.5 * x * (1.0 + jnp.tanh(0.7978845608028654 * (x + 0.044715 * x * x * x)))


def _out_body_even(yf_ref, yb_ref, u_ref, d_ref, gw_ref, gb_ref, ret_ref, wo_ref, *rest):
    ys = yf_ref[0] + yb_ref[0] + d_ref[...] * u_ref[...]
    y = _gelu_tanh(ys)
    s5o = y * jax.nn.sigmoid(_dot(y.astype(BF16), gw_ref[...]) + gb_ref[...])
    upd = _dot(s5o.astype(BF16), wo_ref[0:HALF_W, :]) + _dot(ret_ref[...].astype(BF16), wo_ref[HALF_W:, :])
    _residual_norm_route(upd, *rest)


def _out_body_odd(y_ref, wo_ref, *rest):
    _residual_norm_route(_dot(y_ref[...].astype(BF16), wo_ref[...]), *rest)


def _out_proj(body, mix_args, mix_specs, xs, w_out, gate2, norm_w, shift2, scale2, wr_hi, wr_lo, rb,
              row0_tiles, n_tiles, ctx_tiles):
    kind = lambda r: (jnp.where(r + row0_tiles < ctx_tiles, 0, 1), 0, 0)
    const2 = lambda r: (0, 0)
    rows = n_tiles * ROW_TILE
    return pl.pallas_call(
        body,
        grid=(n_tiles,),
        in_specs=mix_specs + [
            pl.BlockSpec((D_MODEL, D_MODEL), const2),
            pl.BlockSpec((ROW_TILE, D_MODEL), lambda r: (r + row0_tiles, 0)),
            pl.BlockSpec((1, BATCH, D_MODEL), kind),
            pl.BlockSpec((1, D_MODEL), const2),
            pl.BlockSpec((1, BATCH, D_MODEL), kind),
            pl.BlockSpec((1, BATCH, D_MODEL), kind),
            pl.BlockSpec((D_MODEL, ROUTER_PAD), const2),
            pl.BlockSpec((D_MODEL, ROUTER_PAD), const2),
            pl.BlockSpec((1, ROUTER_PAD), const2)],
        out_specs=[pl.BlockSpec((ROW_TILE, D_MODEL), lambda r: (r, 0)),
                   pl.BlockSpec((ROW_TILE, D_MODEL), lambda r: (r, 0)),
                   pl.BlockSpec((ROW_TILE, ROUTER_PAD), lambda r: (r, 0))],
        out_shape=[jax.ShapeDtypeStruct((rows, D_MODEL), F32),
                   jax.ShapeDtypeStruct((rows, D_MODEL), F32),
                   jax.ShapeDtypeStruct((rows, ROUTER_PAD), F32)],
        compiler_params=_params("arbitrary"),
        name="out_proj",
    )(*mix_args, w_out, xs, gate2, norm_w.reshape(1, D_MODEL), shift2, scale2, wr_hi, wr_lo, rb)


def _route(logits):
    n_tok = logits.shape[0]
    g_logits = logits[:, :N_GROUPS]
    g_idx = jnp.argmax(g_logits, -1)
    g_w = jnp.take_along_axis(jax.nn.softmax(g_logits, -1), g_idx[:, None], 1)
    e_logits = logits[:, N_GROUPS:N_GROUPS + N_EXPERTS].reshape(n_tok, N_GROUPS, EXPERTS_PER_GROUP)
    e_in = jnp.take_along_axis(e_logits, g_idx[:, None, None], 1)[:, 0]
    top_v, top_i = lax.top_k(e_in, TOP_K)
    top_w = jax.nn.softmax(top_v, -1) * g_w
    return (g_idx[:, None] * EXPERTS_PER_GROUP + top_i).astype(jnp.int32), top_w


def _dispatch_plan(expert_ids, n_tiles):
    n_tok = expert_ids.shape[0]
    n_asg = n_tok * TOP_K
    flat_e = expert_ids.T.reshape(n_asg)
    order = jnp.argsort(flat_e, stable=True).astype(jnp.int32)
    counts = jnp.zeros((N_EXPERTS,), jnp.int32).at[flat_e].add(1)
    padded = ((counts + MOE_TILE - 1) // MOE_TILE) * MOE_TILE
    pstart = jnp.cumsum(padded) - padded
    start = jnp.cumsum(counts) - counts
    sorted_e = flat_e[order]
    dest = pstart[sorted_e] + (jnp.arange(n_asg, dtype=jnp.int32) - start[sorted_e])
    n_rows = n_tiles * MOE_TILE
    src_tok = jnp.zeros((n_rows,), jnp.int32).at[dest].set(order % n_tok)
    pos = jnp.zeros((n_asg,), jnp.int32).at[order].set(dest)
    tile_start = jnp.arange(n_tiles, dtype=jnp.int32) * MOE_TILE
    pend = jnp.cumsum(padded)
    tile_e = jnp.minimum(jnp.searchsorted(pend, tile_start, side='right'), N_EXPERTS - 1).astype(jnp.int32)
    tile_valid = (tile_start < pend[-1]).astype(jnp.int32)
    return src_tok, dest, order, pos.reshape(TOP_K, n_tok), tile_e, tile_valid


def _gather_rows(idx_ref, src_hbm, dst_vmem, sem, n_rows):
    def issue(r, c):
        pltpu.make_async_copy(src_hbm.at[pl.ds(idx_ref[0, 0, r], 1)], dst_vmem.at[pl.ds(r, 1)], sem).start()
        return c

    lax.fori_loop(0, n_rows, issue, 0, unroll=8)


def _wait_rows(src_hbm, dst_vmem, sem, n_rows):
    def wait(r, c):
        pltpu.make_async_copy(src_hbm.at[pl.ds(0, 1)], dst_vmem.at[pl.ds(r, 1)], sem).wait()
        return c

    lax.fori_loop(0, n_rows, wait, 0, unroll=8)


def _dispatch_body(idx_ref, src_hbm, o_ref, sem):
    _gather_rows(idx_ref, src_hbm, o_ref, sem.at[0], MOE_TILE)
    _wait_rows(src_hbm, o_ref, sem.at[0], MOE_TILE)


def _dispatch(src_tok, hf, n_tiles):
    return pl.pallas_call(
        _dispatch_body,
        grid=(n_tiles,),
        in_specs=[pl.BlockSpec((1, 1, MOE_TILE), lambda i: (i, 0, 0), memory_space=pltpu.SMEM),
                  pl.BlockSpec(memory_space=pl.ANY)],
        out_specs=pl.BlockSpec((MOE_TILE, D_MODEL), lambda i: (i, 0)),
        out_shape=jax.ShapeDtypeStruct((n_tiles * MOE_TILE, D_MODEL), F32),
        scratch_shapes=[pltpu.SemaphoreType.DMA((1,))],
        compiler_params=_params("arbitrary"),
        name="moe_dispatch",
    )(src_tok.reshape(n_tiles, 1, MOE_TILE), hf)


def _expert_body(te_ref, tv_ref, x_ref, cw_ref, wg_ref, wu_ref, wd_ref, y_ref):
    i = pl.program_id(0)

    @pl.when(tv_ref[i] == 1)
    def _():
        x = x_ref[...].astype(BF16)
        g = _dot(x, wg_ref[0])
        u = _dot(x, wu_ref[0])
        act = (g * jax.nn.sigmoid(g)) * u * cw_ref[:, 0:1]
        y_ref[...] = _dot(act.astype(BF16), wd_ref[0])

    @pl.when(tv_ref[i] == 0)
    def _():
        y_ref[...] = jnp.zeros_like(y_ref)


def _experts(tile_e, tile_valid, xs_sorted, cw_rows, wg, wu, wd):
    n_tiles = tile_e.shape[0]
    grid_spec = pltpu.PrefetchScalarGridSpec(
        num_scalar_prefetch=2,
        grid=(n_tiles,),
        in_specs=[pl.BlockSpec((MOE_TILE, D_MODEL), lambda i, te, tv: (i, 0)),
                  pl.BlockSpec((MOE_TILE, LANES), lambda i, te, tv: (i, 0)),
                  pl.BlockSpec((1, D_MODEL, EXPERT_FF), lambda i, te, tv: (te[i], 0, 0)),
                  pl.BlockSpec((1, D_MODEL, EXPERT_FF), lambda i, te, tv: (te[i], 0, 0)),
                  pl.BlockSpec((1, EXPERT_FF, D_MODEL), lambda i, te, tv: (te[i], 0, 0))],
        out_specs=pl.BlockSpec((MOE_TILE, D_MODEL), lambda i, te, tv: (i, 0)))
    return pl.pallas_call(
        _expert_body,
        grid_spec=grid_spec,
        out_shape=jax.ShapeDtypeStruct((n_tiles * MOE_TILE, D_MODEL), F32),
        compiler_params=_params("arbitrary"),
        name="moe_experts",
    )(tile_e, tile_valid, xs_sorted, cw_rows, wg, wu, wd)


def _combine_body(p0_ref, p1_ref, y_hbm, x_ref, g5_ref, o_ref, buf0, buf1, sem):
    tm = x_ref.shape[0]
    _gather_rows(p0_ref, y_hbm, buf0, sem.at[0], tm)
    _gather_rows(p1_ref, y_hbm, buf1, sem.at[1], tm)
    _wait_rows(y_hbm, buf0, sem.at[0], tm)
    _wait_rows(y_hbm, buf1, sem.at[1], tm)
    f3 = (buf0[...] + buf1[...]).reshape(tm // BATCH, BATCH, D_MODEL)
    o_ref[...] = (x_ref[...].reshape(tm // BATCH, BATCH, D_MODEL) + g5_ref[0] * f3).reshape(tm, D_MODEL)


def _combine(pos, y_sorted, xs, gate2, row0_tiles, ctx_tiles):
    n_tok = pos.shape[1]
    n_tiles = n_tok // MOE_TILE
    kind = lambda r: (jnp.where(r + row0_tiles < ctx_tiles, 0, 1), 0, 0)
    idx_spec = lambda k: pl.BlockSpec((1, 1, MOE_TILE), lambda r: (k * n_tiles + r, 0, 0),
                                      memory_space=pltpu.SMEM)
    p2 = pos.reshape(TOP_K * n_tiles, 1, MOE_TILE)
    return pl.pallas_call(
        _combine_body,
        grid=(n_tiles,),
        in_specs=[idx_spec(0), idx_spec(1),
                  pl.BlockSpec(memory_space=pl.ANY),
                  pl.BlockSpec((MOE_TILE, D_MODEL), lambda r: (r, 0)),
                  pl.BlockSpec((1, BATCH, D_MODEL), kind)],
        out_specs=pl.BlockSpec((MOE_TILE, D_MODEL), lambda r: (r, 0)),
        out_shape=jax.ShapeDtypeStruct((n_tok, D_MODEL), F32),
        scratch_shapes=[pltpu.VMEM((MOE_TILE, D_MODEL), F32), pltpu.VMEM((MOE_TILE, D_MODEL), F32),
                        pltpu.SemaphoreType.DMA((2,))],
        compiler_params=_params("arbitrary"),
        name="moe_combine",
    )(p2, p2, y_sorted, xs, gate2)


def _moe(hf, logits, xs, gate2, wg, wu, wd, row0_tiles, ctx_tiles):
    n_tok = hf.shape[0]
    n_tiles = n_tok * TOP_K // MOE_TILE + N_EXPERTS
    expert_ids, top_w = _route(logits)
    src_tok, dest, order, pos, tile_e, tile_valid = _dispatch_plan(expert_ids, n_tiles)
    flat_w = top_w.T.reshape(n_tok * TOP_K)
    cw = jnp.zeros((n_tiles * MOE_TILE,), F32).at[dest].set(flat_w[order])
    cw_rows = jnp.broadcast_to(cw[:, None], (n_tiles * MOE_TILE, LANES))
    xs_sorted = _dispatch(src_tok, hf, n_tiles)
    y_sorted = _experts(tile_e, tile_valid, xs_sorted, cw_rows, wg, wu, wd)
    return _combine(pos, y_sorted, xs, gate2, row0_tiles, ctx_tiles)


FINAL_STEPS = 256


def _final_norm_body(x_ref, w_ref, o_ref):
    x = x_ref[...]
    o_ref[0] = x * lax.rsqrt(jnp.mean(x * x, -1, keepdims=True) + NORM_EPS) * w_ref[...]


def _final_norm(x_lat, w, n_lat):
    return pl.pallas_call(
        _final_norm_body,
        grid=(BATCH, n_lat // FINAL_STEPS),
        in_specs=[pl.BlockSpec((FINAL_STEPS, D_MODEL), lambda b, j: (j, b)),
                  pl.BlockSpec((1, D_MODEL), lambda b, j: (0, 0))],
        out_specs=pl.BlockSpec((1, FINAL_STEPS, D_MODEL), lambda b, j: (b, j, 0)),
        out_shape=jax.ShapeDtypeStruct((BATCH, n_lat, D_MODEL), F32),
        compiler_params=_params("arbitrary", "arbitrary"),
        name="final_norm",
    )(x_lat.reshape(n_lat, BATCH * D_MODEL), w.reshape(1, D_MODEL))


def _kinds(mod_l, k):
    return jnp.stack([mod_l[BATCH:, k], mod_l[:BATCH, k]])


def kernel(x, c, ctx, c_ctx, mod_w, mod_b, norm_mix, norm_ffn, router_group_w, router_group_b,
           router_expert_w, router_expert_b, expert_w_gate, expert_w_up, expert_w_down,
           even_w_in, even_w_out, s5_lam_re, s5_lam_im, s5_log_step, s5_b_re, s5_b_im, s5_c_re,
           s5_c_im, s5_d, s5_glu_w, s5_glu_b, ret_decay, odd_w_in, odd_w_out, gdn_conv_w, gdn_a_log,
           gdn_dt_bias, gdn_norm_w, lru_conv_w, lru_conv_b, lru_w_a, lru_b_a, lru_w_x, lru_b_x,
           lru_lam, final_norm):
    bsz, n_lat, _ = x.shape
    n_ctx = ctx.shape[1]
    n_all = n_ctx + n_lat
    assert bsz == BATCH and n_ctx % S5_STEPS == 0 and n_lat % FINAL_STEPS == 0
    assert (n_ctx * BATCH) % ROW_TILE == 0 and (n_lat * BATCH) % ROW_TILE == 0
    grid_rows = n_lat // GRID_W
    ctx_tiles = n_ctx * BATCH // ROW_TILE
    all_tiles = n_all * BATCH // ROW_TILE
    ctx_moe_tiles = n_ctx * BATCH // MOE_TILE

    xs = _to_time_major(jnp.concatenate([ctx, x], 1))
    cond16 = jnp.concatenate([c, jnp.broadcast_to(c_ctx[None], (BATCH, D_MODEL))], 0)
    mod = _adaln_all(cond16, mod_w, mod_b).reshape(DEPTH, ADALN_ROWS, N_MOD, D_MODEL)

    for layer in range(DEPTH):
        last = layer == DEPTH - 1
        i = layer // 2
        m = [_kinds(mod[layer], k) for k in range(N_MOD)]
        wr = jnp.zeros((D_MODEL, ROUTER_PAD), F32)
        wr = wr.at[:, :N_GROUPS].set(router_group_w[layer]).at[:, N_GROUPS:N_GROUPS + N_EXPERTS].set(
            router_expert_w[layer])
        wr_hi = wr.astype(BF16)
        wr_lo = (wr - wr_hi.astype(F32)).astype(BF16)
        rb = jnp.zeros((1, ROUTER_PAD), F32)
        rb = rb.at[0, :N_GROUPS].set(router_group_b[layer]).at[0, N_GROUPS:N_GROUPS + N_EXPERTS].set(
            router_expert_b[layer])
        row0 = ctx_tiles if last else 0
        n_tiles = all_tiles - row0
        row_spec = lambda w: pl.BlockSpec((ROW_TILE, w), lambda r: (r + row0, 0))
        const2 = lambda r: (0, 0)

        if layer % 2 == 0:
            z = _in_proj(xs, norm_mix[layer], m[0], m[1], even_w_in[i].astype(BF16), ctx_tiles)
            disc = [_s5_discretize(s5_lam_re[i, d], s5_lam_im[i, d], s5_log_step[i, d], s5_b_re[i, d],
                                   s5_b_im[i, d], s5_c_re[i, d], s5_c_im[i, d]) for d in range(2)]
            bb, cc, a = (jnp.stack(t) for t in zip(*disc))
            ys5 = _s5_scan(z, bb, cc, a, n_ctx, n_all)
            zb = _to_batch_major(z, n_all)
            _, q, k, v, g = jnp.split(zb, EVEN_SPLITS, axis=-1)
            ret = _to_time_major(_retention_mixer(q, k, v, g, n_ctx, grid_rows, ret_decay[i]))
            mix_args = [ys5, ys5, z, s5_d[i].reshape(1, HALF_W), s5_glu_w[i].astype(BF16),
                        s5_glu_b[i].reshape(1, HALF_W), ret]
            mix_specs = [pl.BlockSpec((1, ROW_TILE, HALF_W), lambda r: (0, r + row0, 0)),
                         pl.BlockSpec((1, ROW_TILE, HALF_W), lambda r: (1, r + row0, 0)),
                         row_spec(HALF_W),
                         pl.BlockSpec((1, HALF_W), const2),
                         pl.BlockSpec((HALF_W, HALF_W), const2),
                         pl.BlockSpec((1, HALF_W), const2),
                         row_spec(HALF_W)]
            body, w_out = _out_body_even, even_w_out[i]
        else:
            w_in = odd_w_in[i]
            w_pad = jnp.concatenate([
                w_in[:, :ODD_SPLITS[1]],
                jnp.pad(w_in[:, ODD_SPLITS[1]:ODD_SPLITS[3]], ((0, 0), (0, ODD_PAD - 4 * GDN_HEADS))),
                w_in[:, ODD_SPLITS[3]:]], 1)
            z = _in_proj(xs, norm_mix[layer], m[0], m[1], w_pad.astype(BF16), ctx_tiles)
            zb = _to_batch_major(z, n_all)
            c1 = ODD_SPLITS[1]
            qkv, zg = zb[..., :ODD_SPLITS[0]], zb[..., ODD_SPLITS[0]:c1]
            beta_raw, a_raw = zb[..., c1:c1 + 2 * GDN_HEADS], zb[..., c1 + 2 * GDN_HEADS:c1 + 4 * GDN_HEADS]
            lx, lg = zb[..., c1 + ODD_PAD:c1 + ODD_PAD + LRU_W], zb[..., c1 + ODD_PAD + LRU_W:]
            y = jnp.concatenate([
                _gdn_mixer(qkv, zg, beta_raw, a_raw, n_ctx, gdn_conv_w[i], gdn_a_log[i], gdn_dt_bias[i],
                           gdn_norm_w[i]),
                _lru_mixer(lx, lg, n_ctx, lru_conv_w[i], lru_conv_b[i], lru_w_a[i], lru_b_a[i], lru_w_x[i],
                           lru_b_x[i], lru_lam[i])], -1)
            mix_args = [_to_time_major(y)]
            mix_specs = [row_spec(D_MODEL)]
            body, w_out = _out_body_odd, odd_w_out[i]

        xs_mid, hf, logits = _out_proj(body, mix_args, mix_specs, xs, w_out.astype(BF16), m[2],
                                       norm_ffn[layer], m[3], m[4], wr_hi, wr_lo, rb, row0, n_tiles, ctx_tiles)
        moe_row0 = ctx_moe_tiles if last else 0
        xs = _moe(hf, logits, xs_mid, m[5], expert_w_gate[layer].astype(BF16),
                  expert_w_up[layer].astype(BF16), expert_w_down[layer].astype(BF16), moe_row0, ctx_moe_tiles)

    return _final_norm(xs, final_norm, n_lat)
```

```python
import functools

import jax
import jax.numpy as jnp
from jax import lax
from jax.experimental import pallas as pl
from jax.experimental.pallas import tpu as pltpu

D_MODEL = 1024
DEPTH = 2
GRID_W = 64
HALF_W = D_MODEL // 2
S5_GROUP = 16
S5_GROUPS = HALF_W // S5_GROUP
S5_STATE = 64
RET_HEADS = 4
RET_DV = HALF_W // RET_HEADS
RET_DK = RET_DV // 2
RET_QK = RET_HEADS * RET_DK
RET_CHUNK = 128
ROPE_BASE = 10000.0
GDN_HEADS = 4
GDN_DK = HALF_W // GDN_HEADS
GDN_DV = GDN_DK
GDN_CHUNK = 64
LRU_W = HALF_W
LRU_BLOCKS = 8
LRU_BLOCK = LRU_W // LRU_BLOCKS
LRU_C = 8.0
CONV_K = 4
CONV_PAD_LEFT = 2
N_GROUPS = 4
EXPERTS_PER_GROUP = 8
N_EXPERTS = N_GROUPS * EXPERTS_PER_GROUP
TOP_K = 2
EXPERT_FF = D_MODEL // 2
N_MOD = 6
NORM_EPS = 1e-6
EVEN_SPLITS = [HALF_W, HALF_W + RET_QK, HALF_W + 2 * RET_QK, 2 * HALF_W + 2 * RET_QK]
EVEN_IN = 3 * HALF_W + 2 * RET_QK
ODD_SPLITS = [3 * HALF_W, 4 * HALF_W, 4 * HALF_W + 2 * GDN_HEADS, 4 * HALF_W + 4 * GDN_HEADS,
              4 * HALF_W + 4 * GDN_HEADS + LRU_W]

F32 = jnp.float32
BF16 = jnp.bfloat16

SUBLANES = 8
LANES = 128
BATCH = SUBLANES
ROW_CHUNKS = D_MODEL // LANES
VMEM_LIMIT = 48 * 1024 * 1024

ROW_TILE = 512
S5_STEPS = 64
S5_SLABS = 4
S5_SLAB_CH = HALF_W // S5_SLABS
S5_SLAB_STATE = (S5_GROUPS // S5_SLABS) * S5_STATE
LRU_STEPS = 64
MOE_TILE = 256
ODD_PAD = 256
ODD_LX = 4 * HALF_W
ODD_BD = ODD_LX + 2 * LRU_W
ODD_IN_PADDED = ODD_BD + ODD_PAD
ROUTER_PAD = LANES


def _dot(a, b):
    return jnp.dot(a, b, preferred_element_type=F32)


def _dot_nt(a, b):
    return lax.dot_general(a, b, (((1,), (1,)), ((), ())), preferred_element_type=F32)


def _params(*sem):
    return pltpu.CompilerParams(dimension_semantics=sem, vmem_limit_bytes=VMEM_LIMIT)


def _scan_chunk(d, i, ctx_chunks, all_chunks):
    back = jnp.where(i < ctx_chunks, ctx_chunks - 1 - i, all_chunks + ctx_chunks - 1 - i)
    return jnp.where(d == 0, i, back)


def _bwd_order(t, n_ctx):
    return jnp.concatenate([jnp.flip(t[:, :n_ctx], 1), jnp.flip(t[:, n_ctx:], 1)], 1)


def _centred_dwconv(t, w, n_ctx):
    rhs = w[:, None, :]

    def conv(s):
        return lax.conv_general_dilated(
            s, rhs, window_strides=(1,), padding=[(CONV_PAD_LEFT, CONV_K - 1 - CONV_PAD_LEFT)],
            dimension_numbers=('NWC', 'WIO', 'NWC'), feature_group_count=s.shape[-1])

    return jnp.concatenate([conv(t[:, :n_ctx]), conv(t[:, n_ctx:])], 1)


def _l2_normalize(t):
    return t * lax.rsqrt(jnp.sum(t * t, -1, keepdims=True) + NORM_EPS)


def _gated_delta_scan(q, k, v, beta, g):
    bsz, L, H, dk = q.shape
    dv = v.shape[-1]
    C = GDN_CHUNK
    n = L // C
    chunk4 = lambda t: t.reshape(bsz, n, C, H, t.shape[-1]).transpose(0, 3, 1, 2, 4)
    chunk3 = lambda t: t.reshape(bsz, n, C, H).transpose(0, 3, 1, 2)
    qc, kc, vc = chunk4(q), chunk4(k), chunk4(v)
    bc, gc = chunk3(beta), chunk3(g)
    G = jnp.cumsum(gc, -1)
    pos = jnp.arange(C)
    strict = pos[:, None] > pos[None, :]
    incl = pos[:, None] >= pos[None, :]
    diffG = G[..., :, None] - G[..., None, :]
    decay_in = jnp.where(incl, jnp.exp(jnp.where(incl, diffG, 0.0)), 0.0)
    A = jnp.where(strict, jnp.einsum('bhnid,bhnjd->bhnij', kc, kc) * decay_in * bc[..., :, None], 0.0)
    rhs = jnp.concatenate([vc * bc[..., None], kc * (bc * jnp.exp(G))[..., None]], -1)
    sol = lax.linalg.triangular_solve(A + jnp.eye(C, dtype=F32), rhs, left_side=True, lower=True)
    u, w = sol[..., :dv], sol[..., dv:]
    qk = jnp.einsum('bhnid,bhnjd->bhnij', qc, kc) * decay_in
    q_g = qc * jnp.exp(G)[..., None]
    G_last = G[..., -1:]
    k_g = kc * jnp.exp(G_last - G)[..., None]
    d_last = jnp.exp(G_last[..., 0])
    xs = tuple(jnp.moveaxis(t, 2, 0) for t in (w, u, q_g, qk, k_g, d_last))

    def step(S, inp):
        w_n, u_n, qg_n, qk_n, kg_n, dl_n = inp
        v_new = u_n - jnp.einsum('bhcd,bhde->bhce', w_n, S)
        o = jnp.einsum('bhcd,bhde->bhce', qg_n, S) + jnp.einsum('bhij,bhje->bhie', qk_n, v_new)
        S = S * dl_n[..., None, None] + jnp.einsum('bhcd,bhce->bhde', kg_n, v_new)
        return S, o

    _, o = lax.scan(step, jnp.zeros((bsz, H, dk, dv), F32), xs)
    return o.transpose(1, 0, 3, 2, 4).reshape(bsz, L, H, dv)


def _gdn_mixer(qkv, zg, beta_raw, a_raw, n_ctx, conv_w, a_log, dt_bias, norm_w):
    bsz, L, _ = qkv.shape
    qkv = jax.nn.silu(_centred_dwconv(qkv, conv_w, n_ctx))
    q, k, v = jnp.split(qkv, 3, axis=-1)
    q = _l2_normalize(q.reshape(bsz, L, GDN_HEADS, GDN_DK)) * (GDN_DK ** -0.5)
    k = _l2_normalize(k.reshape(bsz, L, GDN_HEADS, GDN_DK))
    v = v.reshape(bsz, L, GDN_HEADS, GDN_DV)
    beta = jax.nn.sigmoid(beta_raw).reshape(bsz, L, 2, GDN_HEADS)
    g = -jnp.exp(a_log) * jax.nn.softplus(a_raw.reshape(bsz, L, 2, GDN_HEADS) + dt_bias)
    o = _gated_delta_scan(q, k, v, beta[:, :, 0], g[:, :, 0])
    o = o + _bwd_order(_gated_delta_scan(_bwd_order(q, n_ctx), _bwd_order(k, n_ctx), _bwd_order(v, n_ctx),
                                         _bwd_order(beta[:, :, 1], n_ctx), _bwd_order(g[:, :, 1], n_ctx)), n_ctx)
    o = o * lax.rsqrt(jnp.mean(o * o, -1, keepdims=True) + NORM_EPS) * norm_w
    o = o * jax.nn.silu(zg.reshape(bsz, L, GDN_HEADS, GDN_DV))
    return o.reshape(bsz, L, HALF_W)


def _to_batch_major(t, n_steps):
    return t.reshape(n_steps, BATCH, t.shape[-1]).transpose(1, 0, 2)


def _to_time_major(t):
    return t.transpose(1, 0, 2).reshape(t.shape[0] * t.shape[1], t.shape[2])


ADALN_COLS = 512
ADALN_ROWS = 2 * BATCH


def _adaln_body(c_ref, w_ref, b_ref, o_ref):
    c = c_ref[...]
    s = (c * jax.nn.sigmoid(c)).astype(BF16)
    o_ref[0] = _dot(s, w_ref[0].astype(BF16)) + b_ref[0]


def _adaln_all(cond16, mod_w, mod_b):
    n_out = mod_w.shape[-1]
    return pl.pallas_call(
        _adaln_body,
        grid=(DEPTH, n_out // ADALN_COLS),
        in_specs=[pl.BlockSpec((ADALN_ROWS, D_MODEL), lambda l, j: (0, 0)),
                  pl.BlockSpec((1, D_MODEL, ADALN_COLS), lambda l, j: (l, 0, j)),
                  pl.BlockSpec((1, 1, ADALN_COLS), lambda l, j: (l, 0, j))],
        out_specs=pl.BlockSpec((1, ADALN_ROWS, ADALN_COLS), lambda l, j: (l, 0, j)),
        out_shape=jax.ShapeDtypeStruct((DEPTH, ADALN_ROWS, n_out), F32),
        compiler_params=_params("arbitrary", "arbitrary"),
        name="adaln",
    )(cond16, mod_w, mod_b.reshape(DEPTH, 1, n_out))


def _col_chunks(n, width=512):
    return [(c0, min(width, n - c0)) for c0 in range(0, n, width)]


def _norm_modulate(x3, nw, shift, scale):
    y = x3 * lax.rsqrt(jnp.mean(x3 * x3, -1, keepdims=True) + NORM_EPS) * nw
    return y * (1.0 + scale) + shift


def _in_proj_body(x_ref, nw_ref, sh_ref, sc_ref, w_ref, o_ref):
    tm = x_ref.shape[0]
    x3 = x_ref[...].reshape(tm // BATCH, BATCH, D_MODEL)
    h = _norm_modulate(x3, nw_ref[...], sh_ref[0], sc_ref[0]).reshape(tm, D_MODEL).astype(BF16)
    for c0, cw in _col_chunks(w_ref.shape[1]):
        o_ref[:, c0:c0 + cw] = _dot(h, w_ref[:, c0:c0 + cw])


def _in_proj(xs, norm_w, shift2, scale2, w_bf16, ctx_tiles):
    rows = xs.shape[0]
    n = w_bf16.shape[1]
    kind = lambda r: (jnp.where(r < ctx_tiles, 0, 1), 0, 0)
    return pl.pallas_call(
        _in_proj_body,
        grid=(rows // ROW_TILE,),
        in_specs=[pl.BlockSpec((ROW_TILE, D_MODEL), lambda r: (r, 0)),
                  pl.BlockSpec((1, D_MODEL), lambda r: (0, 0)),
                  pl.BlockSpec((1, BATCH, D_MODEL), kind),
                  pl.BlockSpec((1, BATCH, D_MODEL), kind),
                  pl.BlockSpec((D_MODEL, n), lambda r: (0, 0))],
        out_specs=pl.BlockSpec((ROW_TILE, n), lambda r: (r, 0)),
        out_shape=jax.ShapeDtypeStruct((rows, n), F32),
        compiler_params=_params("arbitrary"),
        name="in_proj",
    )(xs, norm_w.reshape(1, D_MODEL), shift2, scale2, w_bf16)


def _s5_discretize(lam_re, lam_im, log_step, b_re, b_im, c_re, c_im):
    lr, li = lam_re, lam_im
    dt = jnp.exp(log_step)[:, None]
    mag = jnp.exp(lr * dt)
    ar, ai = mag * jnp.cos(li * dt), mag * jnp.sin(li * dt)
    pr, pi = ar - 1.0, ai
    den = lr * lr + li * li
    zr, zi = (pr * lr + pi * li) / den, (pi * lr - pr * li) / den
    bbr = zr[..., None] * b_re - zi[..., None] * b_im
    bbi = zr[..., None] * b_im + zi[..., None] * b_re
    gps = S5_GROUPS // S5_SLABS
    eye = jnp.eye(gps, dtype=F32)

    def in_slab(m):
        m = m.reshape(S5_SLABS, gps, S5_STATE, S5_GROUP)
        return jnp.einsum('sgnc,gh->sgchn', m, eye).reshape(S5_SLABS, S5_SLAB_CH, S5_SLAB_STATE)

    def out_slab(m):
        m = m.reshape(S5_SLABS, gps, S5_GROUP, S5_STATE)
        return jnp.einsum('sgcn,gh->sgnhc', m, eye).reshape(S5_SLABS, S5_SLAB_STATE, S5_SLAB_CH)

    bb = jnp.concatenate([in_slab(bbr), in_slab(bbi)], -1)
    cc = jnp.concatenate([out_slab(c_re), out_slab(-c_im)], 1)
    a = jnp.concatenate([ar.reshape(S5_SLABS, S5_SLAB_STATE), ai.reshape(S5_SLABS, S5_SLAB_STATE)], -1)
    return bb.astype(BF16), cc.astype(BF16), a.reshape(1, S5_SLABS * 2 * S5_SLAB_STATE)


def _s5_body(u_ref, bb_ref, cc_ref, a_ref, y_ref, x_scr, h_scr):
    d = pl.program_id(0)
    rows = u_ref.shape[0]
    steps = rows // BATCH
    sw = 2 * S5_SLAB_STATE

    @pl.when(pl.program_id(1) == 0)
    def _():
        h_scr[...] = jnp.zeros_like(h_scr)

    u = u_ref[...].astype(BF16)
    for s in range(S5_SLABS):
        x_scr[:, s * sw:(s + 1) * sw] = _dot(u[:, s * S5_SLAB_CH:(s + 1) * S5_SLAB_CH], bb_ref[0, s])

    for s in range(S5_SLABS):
        re0, im0 = s * sw, s * sw + S5_SLAB_STATE
        ar = jnp.broadcast_to(a_ref[0, :, re0:im0], (BATCH, S5_SLAB_STATE))
        ai = jnp.broadcast_to(a_ref[0, :, im0:im0 + S5_SLAB_STATE], (BATCH, S5_SLAB_STATE))

        def step(i, carry):
            hr, hi = carry
            t = jnp.where(d == 0, i, steps - 1 - i)
            r0 = pl.multiple_of(t * BATCH, BATCH)
            xr = x_scr[pl.ds(r0, BATCH), re0:im0]
            xi = x_scr[pl.ds(r0, BATCH), im0:im0 + S5_SLAB_STATE]
            nr = ar * hr - ai * hi + xr
            ni = ar * hi + ai * hr + xi
            x_scr[pl.ds(r0, BATCH), re0:im0] = nr
            x_scr[pl.ds(r0, BATCH), im0:im0 + S5_SLAB_STATE] = ni
            return nr, ni

        hr, hi = lax.fori_loop(0, steps, step,
                               (h_scr[:, re0:im0], h_scr[:, im0:im0 + S5_SLAB_STATE]), unroll=4)
        h_scr[:, re0:im0] = hr
        h_scr[:, im0:im0 + S5_SLAB_STATE] = hi

    for s in range(S5_SLABS):
        y_ref[0, :, s * S5_SLAB_CH:(s + 1) * S5_SLAB_CH] = _dot(
            x_scr[:, s * sw:(s + 1) * sw].astype(BF16), cc_ref[0, s])


def _s5_scan(z, bb, cc, a, n_ctx, n_all):
    rows = n_all * BATCH
    blk = S5_STEPS * BATCH
    ctx_chunks, all_chunks = n_ctx // S5_STEPS, n_all // S5_STEPS
    chunk = lambda d, i: _scan_chunk(d, i, ctx_chunks, all_chunks)
    state_w = S5_SLABS * 2 * S5_SLAB_STATE
    return pl.pallas_call(
        _s5_body,
        grid=(2, all_chunks),
        in_specs=[pl.BlockSpec((blk, HALF_W), lambda d, i: (chunk(d, i), 0)),
                  pl.BlockSpec((1, S5_SLABS, S5_SLAB_CH, 2 * S5_SLAB_STATE), lambda d, i: (d, 0, 0, 0)),
                  pl.BlockSpec((1, S5_SLABS, 2 * S5_SLAB_STATE, S5_SLAB_CH), lambda d, i: (d, 0, 0, 0)),
                  pl.BlockSpec((1, 1, state_w), lambda d, i: (d, 0, 0))],
        out_specs=pl.BlockSpec((1, blk, HALF_W), lambda d, i: (d, chunk(d, i), 0)),
        out_shape=jax.ShapeDtypeStruct((2, rows, HALF_W), F32),
        scratch_shapes=[pltpu.VMEM((blk, state_w), F32), pltpu.VMEM((BATCH, state_w), F32)],
        compiler_params=_params("arbitrary", "arbitrary"),
        name="s5_scan",
    )(z, bb, cc, a)


def _rope_tables(n_ctx, n_lat):
    half = RET_DK // 2
    nf = half // 2
    grid_rows = n_lat // GRID_W
    row = jnp.repeat(jnp.arange(grid_rows, dtype=F32), GRID_W)
    col = jnp.tile(jnp.arange(GRID_W, dtype=F32), grid_rows)
    inv = ROPE_BASE ** (-jnp.arange(nf, dtype=F32) / nf)
    ang = jnp.concatenate([row[:, None] * inv, col[:, None] * inv], -1)
    cos = jnp.tile(jnp.cos(ang), (1, 2 * RET_HEADS))
    sin = jnp.tile(jnp.concatenate([-jnp.sin(ang), jnp.sin(ang)], -1), (1, RET_HEADS))
    cos = jnp.concatenate([jnp.ones((n_ctx, RET_QK), F32), cos], 0)
    sin = jnp.concatenate([jnp.zeros((n_ctx, RET_QK), F32), sin], 0)
    return cos, sin


def _ret_tables(ret_decay):
    lg = jax.nn.log_sigmoid(ret_decay)
    pos = jnp.arange(RET_CHUNK, dtype=F32)
    diff = pos[:, None] - pos[None, :]
    f_mask, b_mask = diff >= 0, diff < 0
    dm_f = jnp.where(f_mask, jnp.exp(lg[0][:, None, None] * jnp.where(f_mask, diff, 0.0)), 0.0)
    dm_b = jnp.where(b_mask, jnp.exp(lg[1][:, None, None] * jnp.where(b_mask, -diff, 0.0)), 0.0)
    dmat = jnp.stack([dm_f, dm_b])
    heads = lambda t: jnp.repeat(t, RET_DK, axis=-1)
    kdec = jnp.stack([heads(jnp.exp(lg[0][None] * (RET_CHUNK - 1 - pos)[:, None])),
                      heads(jnp.exp(lg[1][None] * pos[:, None]))])
    qdec = jnp.stack([heads(jnp.exp(lg[0][None] * (pos + 1)[:, None])),
                      heads(jnp.exp(lg[1][None] * (RET_CHUNK - pos)[:, None]))])
    blk = jnp.kron(jnp.eye(RET_HEADS, dtype=F32), jnp.ones((RET_DK, RET_DV), F32))
    sdec = jnp.repeat(jnp.exp(lg * RET_CHUNK), RET_DK, axis=-1)[:, :, None] * blk[None]
    return dmat, kdec, qdec, sdec, blk


def _rope(t, cos, sin):
    lane = lax.broadcasted_iota(jnp.int32, t.shape, 1)
    first = (lane % RET_DK) < (RET_DK // 2)
    partner = jnp.where(first, pltpu.roll(t, RET_QK - RET_DK // 2, 1), pltpu.roll(t, RET_DK // 2, 1))
    return t * cos + partner * sin


def _ret_direction(d, q_ref, k_ref, v_ref, cos_ref, sin_ref, dm_ref, kd_ref, qd_ref, sd_ref, blk_ref,
                   o_ref, s_scr):
    cos, sin = cos_ref[...], sin_ref[...]
    q = _rope(q_ref[...], cos, sin)
    k = _rope(k_ref[...], cos, sin) * (RET_DK ** -0.5)
    v = v_ref[...].astype(BF16)
    kb = k.astype(BF16)
    lane = lax.broadcasted_iota(jnp.int32, q.shape, 1)
    state = s_scr[...]
    o_inter = _dot((q * qd_ref[d]).astype(BF16), state.astype(BF16))
    for h in range(RET_HEADS):
        qh = jnp.where(lane // RET_DK == h, q, 0.0).astype(BF16)
        scores = _dot_nt(qh, kb) * dm_ref[d, h]
        o_ref[:, h * RET_DV:(h + 1) * RET_DV] = (
            _dot(scores.astype(BF16), v[:, h * RET_DV:(h + 1) * RET_DV])
            + o_inter[:, h * RET_DV:(h + 1) * RET_DV])
    kv = _dot((k * kd_ref[d]).T.astype(BF16), v)
    s_scr[...] = sd_ref[d] * state + blk_ref[...] * kv


def _ret_body(qf, kf, vf, cf, sf, qb, kb, vb, cb, sb, dm_ref, kd_ref, qd_ref, sd_ref, blk_ref,
              of_ref, ob_ref, sf_scr, sb_scr):
    @pl.when(pl.program_id(1) == 0)
    def _():
        sf_scr[...] = jnp.zeros_like(sf_scr)
        sb_scr[...] = jnp.zeros_like(sb_scr)

    _ret_direction(0, qf, kf, vf, cf, sf, dm_ref, kd_ref, qd_ref, sd_ref, blk_ref, of_ref, sf_scr)
    _ret_direction(1, qb, kb, vb, cb, sb, dm_ref, kd_ref, qd_ref, sd_ref, blk_ref, ob_ref, sb_scr)


def _retention(z, cos, sin, tables, n_ctx, n_all):
    dmat, kdec, qdec, sdec, blk = tables
    zv = z.reshape(n_all, BATCH * EVEN_IN)
    ctx_chunks, all_chunks = n_ctx // RET_CHUNK, n_all // RET_CHUNK
    per_b = EVEN_IN // RET_QK
    per_bv = EVEN_IN // HALF_W
    q0, k0, v0 = HALF_W // RET_QK, HALF_W // RET_QK + 1, (HALF_W + 2 * RET_QK) // HALF_W

    def seq_specs(d):
        ch = lambda b, i: _scan_chunk(d, i, ctx_chunks, all_chunks)
        return [pl.BlockSpec((RET_CHUNK, RET_QK), lambda b, i: (ch(b, i), b * per_b + q0)),
                pl.BlockSpec((RET_CHUNK, RET_QK), lambda b, i: (ch(b, i), b * per_b + k0)),
                pl.BlockSpec((RET_CHUNK, HALF_W), lambda b, i: (ch(b, i), b * per_bv + v0)),
                pl.BlockSpec((RET_CHUNK, RET_QK), lambda b, i: (ch(b, i), 0)),
                pl.BlockSpec((RET_CHUNK, RET_QK), lambda b, i: (ch(b, i), 0))]

    def out_spec(d):
        return pl.BlockSpec((RET_CHUNK, HALF_W), lambda b, i: (_scan_chunk(d, i, ctx_chunks, all_chunks), b))

    const = lambda nd: (lambda b, i: (0,) * nd)
    o_shape = jax.ShapeDtypeStruct((n_all, BATCH * HALF_W), F32)
    of, ob = pl.pallas_call(
        _ret_body,
        grid=(BATCH, all_chunks),
        in_specs=seq_specs(0) + seq_specs(1) + [
            pl.BlockSpec(dmat.shape, const(4)), pl.BlockSpec(kdec.shape, const(3)),
            pl.BlockSpec(qdec.shape, const(3)), pl.BlockSpec(sdec.shape, const(3)),
            pl.BlockSpec(blk.shape, const(2))],
        out_specs=[out_spec(0), out_spec(1)],
        out_shape=[o_shape, o_shape],
        scratch_shapes=[pltpu.VMEM((RET_QK, HALF_W), F32), pltpu.VMEM((RET_QK, HALF_W), F32)],
        compiler_params=_params("arbitrary", "arbitrary"),
        name="retention",
    )(zv, zv, zv, cos, sin, zv, zv, zv, cos, sin, dmat, kdec, qdec, sdec, blk)
    return of.reshape(n_all * BATCH, HALF_W), ob.reshape(n_all * BATCH, HALF_W)


def _lru_body(ctx_chunks, all_chunks, prev_ref, x_ref, next_ref, cw_ref, cb_ref, wg_ref, bg_ref, sp_ref,
              h_ref, a_scr, b_scr, h_scr):
    d, i = pl.program_id(0), pl.program_id(1)
    rows = x_ref.shape[0]
    steps = rows // BATCH
    chunk = _scan_chunk(d, i, ctx_chunks, all_chunks)

    @pl.when(i == 0)
    def _():
        h_scr[...] = jnp.zeros_like(h_scr)

    seg_first = jnp.logical_or(chunk == 0, chunk == ctx_chunks)
    seg_last = jnp.logical_or(chunk == ctx_chunks - 1, chunk == all_chunks - 1)
    prev = jnp.where(seg_first, 0.0, prev_ref[...])
    nxt = jnp.where(seg_last, 0.0, next_ref[...])
    ext = jnp.concatenate([prev, x_ref[...], nxt], 0)
    xs = cb_ref[...] + sum(cw_ref[k:k + 1, :] * ext[k * BATCH:k * BATCH + rows] for k in range(CONV_K))
    gates = _dot(xs.astype(BF16), wg_ref[0]) + bg_ref[0]
    r = jax.nn.sigmoid(gates[:, :LRU_W])
    ig = jax.nn.sigmoid(gates[:, LRU_W:])
    log_a = -r * sp_ref[0]
    a = jnp.exp(log_a)
    a_scr[...] = a
    b_scr[...] = jnp.sqrt(1.0 - jnp.exp(2.0 * log_a)) * (ig * xs)

    def step(j, h):
        t = jnp.where(d == 0, j, steps - 1 - j)
        r0 = pl.multiple_of(t * BATCH, BATCH)
        h = a_scr[pl.ds(r0, BATCH), :] * h + b_scr[pl.ds(r0, BATCH), :]
        h_ref[0, pl.ds(r0, BATCH), :] = h
        return h

    h_scr[...] = lax.fori_loop(0, steps, step, h_scr[...], unroll=8)


def _lru(z, conv_w, conv_b, w_a, b_a, w_x, b_x, lam, n_ctx, n_all):
    rows = n_all * BATCH
    blk = LRU_STEPS * BATCH
    ctx_chunks, all_chunks = n_ctx // LRU_STEPS, n_all // LRU_STEPS
    chunk = lambda d, i: _scan_chunk(d, i, ctx_chunks, all_chunks)
    col = ODD_LX // LRU_W
    eye = jnp.eye(LRU_BLOCKS, dtype=F32)
    dense = lambda w: jnp.einsum('dkij,kl->dkilj', w, eye).reshape(2, LRU_W, LRU_W)
    wg = jnp.concatenate([dense(w_a), dense(w_x)], -1).astype(BF16)
    bg = jnp.concatenate([b_a, b_x], -1).reshape(2, 1, 2 * LRU_W)
    sp = (LRU_C * jax.nn.softplus(-lam)).reshape(2, 1, LRU_W)
    halo_prev, halo_next = 2 * BATCH, BATCH
    return pl.pallas_call(
        functools.partial(_lru_body, ctx_chunks, all_chunks),
        grid=(2, all_chunks),
        in_specs=[pl.BlockSpec((halo_prev, LRU_W),
                               lambda d, i: (jnp.maximum(chunk(d, i) * (blk // halo_prev) - 1, 0), col)),
                  pl.BlockSpec((blk, LRU_W), lambda d, i: (chunk(d, i), col)),
                  pl.BlockSpec((halo_next, LRU_W),
                               lambda d, i: (jnp.minimum((chunk(d, i) + 1) * (blk // halo_next),
                                                         rows // halo_next - 1), col)),
                  pl.BlockSpec((CONV_K, LRU_W), lambda d, i: (0, 0)),
                  pl.BlockSpec((1, LRU_W), lambda d, i: (0, 0)),
                  pl.BlockSpec((1, LRU_W, 2 * LRU_W), lambda d, i: (d, 0, 0)),
                  pl.BlockSpec((1, 1, 2 * LRU_W), lambda d, i: (d, 0, 0)),
                  pl.BlockSpec((1, 1, LRU_W), lambda d, i: (d, 0, 0))],
        out_specs=pl.BlockSpec((1, blk, LRU_W), lambda d, i: (d, chunk(d, i), 0)),
        out_shape=jax.ShapeDtypeStruct((2, rows, LRU_W), F32),
        scratch_shapes=[pltpu.VMEM((blk, LRU_W), F32), pltpu.VMEM((blk, LRU_W), F32),
                        pltpu.VMEM((BATCH, LRU_W), F32)],
        compiler_params=_params("arbitrary", "arbitrary"),
        name="rglru",
    )(z, z, z, conv_w, conv_b.reshape(1, LRU_W), wg, bg, sp)


def _residual_norm_route(upd, x_ref, g2_ref, nw_ref, sh_ref, sc_ref, wrh_ref, wrl_ref, rb_ref,
                         xo_ref, hf_ref, lg_ref):
    tm = x_ref.shape[0]
    x3 = x_ref[...].reshape(tm // BATCH, BATCH, D_MODEL) + g2_ref[0] * upd.reshape(tm // BATCH, BATCH, D_MODEL)
    xo_ref[...] = x3.reshape(tm, D_MODEL)
    hf = _norm_modulate(x3, nw_ref[...], sh_ref[0], sc_ref[0]).reshape(tm, D_MODEL)
    for s in range(ROW_CHUNKS):
        hf_ref[:, s, :] = hf[:, s * LANES:(s + 1) * LANES]
    hi = hf.astype(BF16)
    lo = (hf - hi.astype(F32)).astype(BF16)
    lg_ref[...] = _dot(hi, wrh_ref[...]) + _dot(hi, wrl_ref[...]) + _dot(lo, wrh_ref[...]) + rb_ref[...]


def _gelu_tanh(x):
    return 0.5 * x * (1.0 + jnp.tanh(0.7978845608028654 * (x + 0.044715 * x * x * x)))


def _silu(x):
    return x * jax.nn.sigmoid(x)


def _out_body_even(yf_ref, yb_ref, u_ref, d_ref, gw_ref, gb_ref, of_ref, ob_ref, gate_ref, wo_ref, *rest):
    ys = yf_ref[0] + yb_ref[0] + d_ref[...] * u_ref[...]
    y = _gelu_tanh(ys)
    s5o = y * jax.nn.sigmoid(_dot(y.astype(BF16), gw_ref[...]) + gb_ref[...])
    upd = _dot(s5o.astype(BF16), wo_ref[0:HALF_W, :])
    o = of_ref[...] + ob_ref[...]
    gate = gate_ref[...]
    for h in range(RET_HEADS):
        oh = o[:, h * RET_DV:(h + 1) * RET_DV]
        mu = jnp.mean(oh, -1, keepdims=True)
        var = jnp.mean(jnp.square(oh - mu), -1, keepdims=True)
        rh = _silu(gate[:, h * RET_DV:(h + 1) * RET_DV]) * ((oh - mu) * lax.rsqrt(var + NORM_EPS))
        upd = upd + _dot(rh.astype(BF16), wo_ref[HALF_W + h * RET_DV:HALF_W + (h + 1) * RET_DV, :])
    _residual_norm_route(upd, *rest)


def _out_body_odd(gdn_ref, hf_ref, hb_ref, lg_ref, wo_ref, *rest):
    lru = (hf_ref[0] + hb_ref[0]) * _gelu_tanh(lg_ref[...])
    upd = _dot(gdn_ref[...].astype(BF16), wo_ref[0:HALF_W, :]) + _dot(lru.astype(BF16), wo_ref[HALF_W:, :])
    _residual_norm_route(upd, *rest)


def _out_proj(body, mix_args, mix_specs, xs, w_out, gate2, norm_w, shift2, scale2, wr_hi, wr_lo, rb,
              row0_tiles, n_tiles, ctx_tiles):
    kind = lambda r: (jnp.where(r + row0_tiles < ctx_tiles, 0, 1), 0, 0)
    const2 = lambda r: (0, 0)
    rows = n_tiles * ROW_TILE
    return pl.pallas_call(
        body,
        grid=(n_tiles,),
        in_specs=mix_specs + [
            pl.BlockSpec((D_MODEL, D_MODEL), const2),
            pl.BlockSpec((ROW_TILE, D_MODEL), lambda r: (r + row0_tiles, 0)),
            pl.BlockSpec((1, BATCH, D_MODEL), kind),
            pl.BlockSpec((1, D_MODEL), const2),
            pl.BlockSpec((1, BATCH, D_MODEL), kind),
            pl.BlockSpec((1, BATCH, D_MODEL), kind),
            pl.BlockSpec((D_MODEL, ROUTER_PAD), const2),
            pl.BlockSpec((D_MODEL, ROUTER_PAD), const2),
            pl.BlockSpec((1, ROUTER_PAD), const2)],
        out_specs=[pl.BlockSpec((ROW_TILE, D_MODEL), lambda r: (r, 0)),
                   pl.BlockSpec((ROW_TILE, ROW_CHUNKS, LANES), lambda r: (r, 0, 0)),
                   pl.BlockSpec((ROW_TILE, ROUTER_PAD), lambda r: (r, 0))],
        out_shape=[jax.ShapeDtypeStruct((rows, D_MODEL), F32),
                   jax.ShapeDtypeStruct((rows, ROW_CHUNKS, LANES), F32),
                   jax.ShapeDtypeStruct((rows, ROUTER_PAD), F32)],
        compiler_params=_params("arbitrary"),
        name="out_proj",
    )(*mix_args, w_out, xs, gate2, norm_w.reshape(1, D_MODEL), shift2, scale2, wr_hi, wr_lo, rb)


def _route(logits):
    n_tok = logits.shape[0]
    g_logits = logits[:, :N_GROUPS]
    g_idx = jnp.argmax(g_logits, -1)
    g_w = jnp.take_along_axis(jax.nn.softmax(g_logits, -1), g_idx[:, None], 1)
    e_logits = logits[:, N_GROUPS:N_GROUPS + N_EXPERTS].reshape(n_tok, N_GROUPS, EXPERTS_PER_GROUP)
    e_in = jnp.take_along_axis(e_logits, g_idx[:, None, None], 1)[:, 0]
    top_v, top_i = lax.top_k(e_in, TOP_K)
    top_w = jax.nn.softmax(top_v, -1) * g_w
    return (g_idx[:, None] * EXPERTS_PER_GROUP + top_i).astype(jnp.int32), top_w


def _dispatch_plan(expert_ids, top_w, n_tiles):
    n_tok = expert_ids.shape[0]
    n_asg = n_tok * TOP_K
    flat_e = expert_ids.T.reshape(n_asg)
    flat_w = top_w.T.reshape(n_asg)
    order = jnp.argsort(flat_e, stable=True).astype(jnp.int32)
    inv = jnp.argsort(order).astype(jnp.int32)
    onehot = flat_e[:, None] == jnp.arange(N_EXPERTS, dtype=jnp.int32)[None]
    counts = jnp.sum(onehot, 0, dtype=jnp.int32)
    padded = ((counts + MOE_TILE - 1) // MOE_TILE) * MOE_TILE
    pend = jnp.cumsum(padded)
    pstart = pend - padded
    start = jnp.cumsum(counts) - counts
    shift = pstart - start
    pos = inv + jnp.sum(jnp.where(onehot, shift[None], 0), 1)
    tile_start = jnp.arange(n_tiles, dtype=jnp.int32) * MOE_TILE
    tile_e = jnp.minimum(jnp.sum(tile_start[:, None] >= pend[None], 1), N_EXPERTS - 1).astype(jnp.int32)
    tile_valid = (tile_start < pend[-1]).astype(jnp.int32)
    row = jnp.arange(n_tiles * MOE_TILE, dtype=jnp.int32)
    row_e = jnp.repeat(tile_e, MOE_TILE)
    rank = row - shift[row_e]
    row_ok = jnp.logical_and(rank >= start[row_e], rank < start[row_e] + counts[row_e])
    src_asg = order[jnp.clip(rank, 0, n_asg - 1)]
    src_tok = jnp.where(row_ok, src_asg % n_tok, 0)
    row_w = jnp.where(row_ok, flat_w[src_asg], 0.0)
    return src_tok, row_w, pos.reshape(TOP_K, n_tok), tile_e, tile_valid


def _gather_rows(idx_ref, src_hbm, dst_vmem, sem, n_rows):
    def issue(r, c):
        pltpu.make_async_copy(src_hbm.at[pl.ds(idx_ref[0, 0, r], 1)], dst_vmem.at[pl.ds(r, 1)], sem).start()
        return c

    lax.fori_loop(0, n_rows, issue, 0, unroll=8)


def _wait_rows(src_hbm, dst_vmem, sem, n_rows):
    def wait(r, c):
        pltpu.make_async_copy(src_hbm.at[pl.ds(0, 1)], dst_vmem.at[pl.ds(r, 1)], sem).wait()
        return c

    lax.fori_loop(0, n_rows, wait, 0, unroll=8)


def _rows_2d(ref):
    return jnp.concatenate([ref[:, s, :] for s in range(ROW_CHUNKS)], 1)


def _dispatch_body(idx_ref, src_hbm, o_ref, sem):
    _gather_rows(idx_ref, src_hbm, o_ref, sem.at[0], MOE_TILE)
    _wait_rows(src_hbm, o_ref, sem.at[0], MOE_TILE)


def _dispatch(src_tok, hf, n_tiles):
    return pl.pallas_call(
        _dispatch_body,
        grid=(n_tiles,),
        in_specs=[pl.BlockSpec((1, 1, MOE_TILE), lambda i: (i, 0, 0), memory_space=pltpu.SMEM),
                  pl.BlockSpec(memory_space=pl.ANY)],
        out_specs=pl.BlockSpec((MOE_TILE, ROW_CHUNKS, LANES), lambda i: (i, 0, 0)),
        out_shape=jax.ShapeDtypeStruct((n_tiles * MOE_TILE, ROW_CHUNKS, LANES), F32),
        scratch_shapes=[pltpu.SemaphoreType.DMA((1,))],
        compiler_params=_params("arbitrary"),
        name="moe_dispatch",
    )(src_tok.reshape(n_tiles, 1, MOE_TILE), hf)


def _expert_body(te_ref, tv_ref, x_ref, cw_ref, wg_ref, wu_ref, wd_ref, y_ref):
    i = pl.program_id(0)

    @pl.when(tv_ref[i] == 1)
    def _():
        x = _rows_2d(x_ref).astype(BF16)
        g = _dot(x, wg_ref[0])
        u = _dot(x, wu_ref[0])
        act = _silu(g) * u * cw_ref[:, 0:1]
        y = _dot(act.astype(BF16), wd_ref[0])
        for s in range(ROW_CHUNKS):
            y_ref[:, s, :] = y[:, s * LANES:(s + 1) * LANES]

    @pl.when(tv_ref[i] == 0)
    def _():
        y_ref[...] = jnp.zeros_like(y_ref)


def _experts(tile_e, tile_valid, xs_sorted, cw_rows, wg, wu, wd):
    n_tiles = tile_e.shape[0]
    tile3 = pl.BlockSpec((MOE_TILE, ROW_CHUNKS, LANES), lambda i, te, tv: (i, 0, 0))
    grid_spec = pltpu.PrefetchScalarGridSpec(
        num_scalar_prefetch=2,
        grid=(n_tiles,),
        in_specs=[tile3,
                  pl.BlockSpec((MOE_TILE, LANES), lambda i, te, tv: (i, 0)),
                  pl.BlockSpec((1, D_MODEL, EXPERT_FF), lambda i, te, tv: (te[i], 0, 0)),
                  pl.BlockSpec((1, D_MODEL, EXPERT_FF), lambda i, te, tv: (te[i], 0, 0)),
                  pl.BlockSpec((1, EXPERT_FF, D_MODEL), lambda i, te, tv: (te[i], 0, 0))],
        out_specs=tile3)
    return pl.pallas_call(
        _expert_body,
        grid_spec=grid_spec,
        out_shape=jax.ShapeDtypeStruct((n_tiles * MOE_TILE, ROW_CHUNKS, LANES), F32),
        compiler_params=_params("arbitrary"),
        name="moe_experts",
    )(tile_e, tile_valid, xs_sorted, cw_rows, wg, wu, wd)


def _combine_body(p0_ref, p1_ref, y_hbm, x_ref, g5_ref, o_ref, buf0, buf1, sem):
    tm = x_ref.shape[0]
    _gather_rows(p0_ref, y_hbm, buf0, sem.at[0], tm)
    _gather_rows(p1_ref, y_hbm, buf1, sem.at[1], tm)
    _wait_rows(y_hbm, buf0, sem.at[0], tm)
    _wait_rows(y_hbm, buf1, sem.at[1], tm)
    f3 = (_rows_2d(buf0) + _rows_2d(buf1)).reshape(tm // BATCH, BATCH, D_MODEL)
    o_ref[...] = (x_ref[...].reshape(tm // BATCH, BATCH, D_MODEL) + g5_ref[0] * f3).reshape(tm, D_MODEL)


def _combine(pos, y_sorted, xs, gate2, row0_tiles, ctx_tiles):
    n_tok = pos.shape[1]
    n_tiles = n_tok // MOE_TILE
    kind = lambda r: (jnp.where(r + row0_tiles < ctx_tiles, 0, 1), 0, 0)
    idx_spec = lambda k: pl.BlockSpec((1, 1, MOE_TILE), lambda r: (k * n_tiles + r, 0, 0),
                                      memory_space=pltpu.SMEM)
    p2 = pos.reshape(TOP_K * n_tiles, 1, MOE_TILE)
    return pl.pallas_call(
        _combine_body,
        grid=(n_tiles,),
        in_specs=[idx_spec(0), idx_spec(1),
                  pl.BlockSpec(memory_space=pl.ANY),
                  pl.BlockSpec((MOE_TILE, D_MODEL), lambda r: (r, 0)),
                  pl.BlockSpec((1, BATCH, D_MODEL), kind)],
        out_specs=pl.BlockSpec((MOE_TILE, D_MODEL), lambda r: (r, 0)),
        out_shape=jax.ShapeDtypeStruct((n_tok, D_MODEL), F32),
        scratch_shapes=[pltpu.VMEM((MOE_TILE, ROW_CHUNKS, LANES), F32),
                        pltpu.VMEM((MOE_TILE, ROW_CHUNKS, LANES), F32),
                        pltpu.SemaphoreType.DMA((2,))],
        compiler_params=_params("arbitrary"),
        name="moe_combine",
    )(p2, p2, y_sorted, xs, gate2)


def _moe(hf, logits, xs, gate2, wg, wu, wd, row0_tiles, ctx_tiles):
    n_tok = hf.shape[0]
    n_tiles = n_tok * TOP_K // MOE_TILE + N_EXPERTS
    expert_ids, top_w = _route(logits)
    src_tok, row_w, pos, tile_e, tile_valid = _dispatch_plan(expert_ids, top_w, n_tiles)
    cw_rows = jnp.broadcast_to(row_w[:, None], (n_tiles * MOE_TILE, LANES))
    xs_sorted = _dispatch(src_tok, hf, n_tiles)
    y_sorted = _experts(tile_e, tile_valid, xs_sorted, cw_rows, wg, wu, wd)
    return _combine(pos, y_sorted, xs, gate2, row0_tiles, ctx_tiles)


FINAL_STEPS = 256


def _final_norm_body(x_ref, w_ref, o_ref):
    x = x_ref[...]
    o_ref[0] = x * lax.rsqrt(jnp.mean(x * x, -1, keepdims=True) + NORM_EPS) * w_ref[...]


def _final_norm(x_lat, w, n_lat):
    return pl.pallas_call(
        _final_norm_body,
        grid=(BATCH, n_lat // FINAL_STEPS),
        in_specs=[pl.BlockSpec((FINAL_STEPS, D_MODEL), lambda b, j: (j, b)),
                  pl.BlockSpec((1, D_MODEL), lambda b, j: (0, 0))],
        out_specs=pl.BlockSpec((1, FINAL_STEPS, D_MODEL), lambda b, j: (b, j, 0)),
        out_shape=jax.ShapeDtypeStruct((BATCH, n_lat, D_MODEL), F32),
        compiler_params=_params("arbitrary", "arbitrary"),
        name="final_norm",
    )(x_lat.reshape(n_lat, BATCH * D_MODEL), w.reshape(1, D_MODEL))


def _kinds(mod_l, k):
    return jnp.stack([mod_l[BATCH:, k], mod_l[:BATCH, k]])


def kernel(x, c, ctx, c_ctx, mod_w, mod_b, norm_mix, norm_ffn, router_group_w, router_group_b,
           router_expert_w, router_expert_b, expert_w_gate, expert_w_up, expert_w_down,
           even_w_in, even_w_out, s5_lam_re, s5_lam_im, s5_log_step, s5_b_re, s5_b_im, s5_c_re,
           s5_c_im, s5_d, s5_glu_w, s5_glu_b, ret_decay, odd_w_in, odd_w_out, gdn_conv_w, gdn_a_log,
           gdn_dt_bias, gdn_norm_w, lru_conv_w, lru_conv_b, lru_w_a, lru_b_a, lru_w_x, lru_b_x,
           lru_lam, final_norm):
    bsz, n_lat, _ = x.shape
    n_ctx = ctx.shape[1]
    n_all = n_ctx + n_lat
    assert bsz == BATCH and n_ctx % RET_CHUNK == 0 and n_lat % FINAL_STEPS == 0
    assert (n_ctx * BATCH) % ROW_TILE == 0 and (n_lat * BATCH) % ROW_TILE == 0
    ctx_tiles = n_ctx * BATCH // ROW_TILE
    all_tiles = n_all * BATCH // ROW_TILE
    ctx_moe_tiles = n_ctx * BATCH // MOE_TILE

    xs = _to_time_major(jnp.concatenate([ctx, x], 1))
    cond16 = jnp.concatenate([c, jnp.broadcast_to(c_ctx[None], (BATCH, D_MODEL))], 0)
    mod = _adaln_all(cond16, mod_w, mod_b).reshape(DEPTH, ADALN_ROWS, N_MOD, D_MODEL)

    for layer in range(DEPTH):
        last = layer == DEPTH - 1
        i = layer // 2
        m = [_kinds(mod[layer], k) for k in range(N_MOD)]
        wr = jnp.zeros((D_MODEL, ROUTER_PAD), F32)
        wr = wr.at[:, :N_GROUPS].set(router_group_w[layer]).at[:, N_GROUPS:N_GROUPS + N_EXPERTS].set(
            router_expert_w[layer])
        wr_hi = wr.astype(BF16)
        wr_lo = (wr - wr_hi.astype(F32)).astype(BF16)
        rb = jnp.zeros((1, ROUTER_PAD), F32)
        rb = rb.at[0, :N_GROUPS].set(router_group_b[layer]).at[0, N_GROUPS:N_GROUPS + N_EXPERTS].set(
            router_expert_b[layer])
        row0 = ctx_tiles if last else 0
        n_tiles = all_tiles - row0
        row_spec = lambda w, cb=0: pl.BlockSpec((ROW_TILE, w), lambda r: (r + row0, cb))
        dir_spec = lambda d: pl.BlockSpec((1, ROW_TILE, HALF_W), lambda r: (d, r + row0, 0))
        const2 = lambda r: (0, 0)

        if layer % 2 == 0:
            z = _in_proj(xs, norm_mix[layer], m[0], m[1], even_w_in[i].astype(BF16), ctx_tiles)
            disc = [_s5_discretize(s5_lam_re[i, d], s5_lam_im[i, d], s5_log_step[i, d], s5_b_re[i, d],
                                   s5_b_im[i, d], s5_c_re[i, d], s5_c_im[i, d]) for d in range(2)]
            bb, cc, a = (jnp.stack(t) for t in zip(*disc))
            ys5 = _s5_scan(z, bb, cc, a, n_ctx, n_all)
            cos, sin = _rope_tables(n_ctx, n_lat)
            r_f, r_b = _retention(z, cos, sin, _ret_tables(ret_decay[i]), n_ctx, n_all)
            mix_args = [ys5, ys5, z, s5_d[i].reshape(1, HALF_W), s5_glu_w[i].astype(BF16),
                        s5_glu_b[i].reshape(1, HALF_W), r_f, r_b, z]
            mix_specs = [dir_spec(0), dir_spec(1), row_spec(HALF_W),
                         pl.BlockSpec((1, HALF_W), const2),
                         pl.BlockSpec((HALF_W, HALF_W), const2),
                         pl.BlockSpec((1, HALF_W), const2),
                         row_spec(HALF_W), row_spec(HALF_W), row_spec(HALF_W, EVEN_SPLITS[3] // HALF_W)]
            body, w_out = _out_body_even, even_w_out[i]
        else:
            w_in = odd_w_in[i]
            w_pad = jnp.concatenate([
                w_in[:, :ODD_SPLITS[1]],
                w_in[:, ODD_SPLITS[3]:],
                jnp.pad(w_in[:, ODD_SPLITS[1]:ODD_SPLITS[3]], ((0, 0), (0, ODD_PAD - 4 * GDN_HEADS)))], 1)
            z = _in_proj(xs, norm_mix[layer], m[0], m[1], w_pad.astype(BF16), ctx_tiles)
            zb = _to_batch_major(jnp.concatenate([z[:, :ODD_LX], z[:, ODD_BD:ODD_BD + 4 * GDN_HEADS]], 1), n_all)
            c1 = ODD_SPLITS[1]
            qkv, zg = zb[..., :ODD_SPLITS[0]], zb[..., ODD_SPLITS[0]:c1]
            beta_raw, a_raw = zb[..., c1:c1 + 2 * GDN_HEADS], zb[..., c1 + 2 * GDN_HEADS:c1 + 4 * GDN_HEADS]
            gdn = _to_time_major(_gdn_mixer(qkv, zg, beta_raw, a_raw, n_ctx, gdn_conv_w[i], gdn_a_log[i],
                                            gdn_dt_bias[i], gdn_norm_w[i]))
            hl = _lru(z, lru_conv_w[i], lru_conv_b[i], lru_w_a[i], lru_b_a[i], lru_w_x[i], lru_b_x[i],
                      lru_lam[i], n_ctx, n_all)
            mix_args = [gdn, hl, hl, z]
            mix_specs = [row_spec(HALF_W), dir_spec(0), dir_spec(1),
                         row_spec(LRU_W, (ODD_LX + LRU_W) // LRU_W)]
            body, w_out = _out_body_odd, odd_w_out[i]

        xs_mid, hf, logits = _out_proj(body, mix_args, mix_specs, xs, w_out.astype(BF16), m[2],
                                       norm_ffn[layer], m[3], m[4], wr_hi, wr_lo, rb, row0, n_tiles, ctx_tiles)
        moe_row0 = ctx_moe_tiles if last else 0
        xs = _moe(hf, logits, xs_mid, m[5], expert_w_gate[layer].astype(BF16),
                  expert_w_up[layer].astype(BF16), expert_w_down[layer].astype(BF16), moe_row0, ctx_moe_tiles)

    return _final_norm(xs, final_norm, n_lat)
```

```python
import functools

import jax
import jax.numpy as jnp
from jax import lax
from jax.experimental import pallas as pl
from jax.experimental.pallas import tpu as pltpu

D_MODEL = 1024
DEPTH = 2
GRID_W = 64
HALF_W = D_MODEL // 2
S5_GROUP = 16
S5_GROUPS = HALF_W // S5_GROUP
S5_STATE = 64
RET_HEADS = 4
RET_DV = HALF_W // RET_HEADS
RET_DK = RET_DV // 2
RET_QK = RET_HEADS * RET_DK
RET_CHUNK = 128
ROPE_BASE = 10000.0
GDN_HEADS = 4
GDN_DK = HALF_W // GDN_HEADS
GDN_DV = GDN_DK
GDN_CHUNK = 64
LRU_W = HALF_W
LRU_BLOCKS = 8
LRU_BLOCK = LRU_W // LRU_BLOCKS
LRU_C = 8.0
CONV_K = 4
CONV_PAD_LEFT = 2
N_GROUPS = 4
EXPERTS_PER_GROUP = 8
N_EXPERTS = N_GROUPS * EXPERTS_PER_GROUP
TOP_K = 2
EXPERT_FF = D_MODEL // 2
N_MOD = 6
NORM_EPS = 1e-6
EVEN_SPLITS = [HALF_W, HALF_W + RET_QK, HALF_W + 2 * RET_QK, 2 * HALF_W + 2 * RET_QK]
EVEN_IN = 3 * HALF_W + 2 * RET_QK
ODD_SPLITS = [3 * HALF_W, 4 * HALF_W, 4 * HALF_W + 2 * GDN_HEADS, 4 * HALF_W + 4 * GDN_HEADS,
              4 * HALF_W + 4 * GDN_HEADS + LRU_W]

F32 = jnp.float32
BF16 = jnp.bfloat16

SUBLANES = 8
LANES = 128
BATCH = SUBLANES
ROW_CHUNKS = D_MODEL // LANES
VMEM_LIMIT = 48 * 1024 * 1024

ROW_TILE = 512
S5_STEPS = 64
S5_SLABS = 4
S5_SLAB_CH = HALF_W // S5_SLABS
S5_SLAB_STATE = (S5_GROUPS // S5_SLABS) * S5_STATE
LRU_STEPS = 64
MOE_TILE = 256
ODD_PAD = 512
ODD_LX = 4 * HALF_W
ODD_BD = ODD_LX + 2 * LRU_W
ODD_IN_PADDED = ODD_BD + ODD_PAD
GDN_QKV = 3 * HALF_W
GDN_ROWS = GDN_CHUNK * BATCH
GDN_BG_G = 2 * GDN_HEADS
GDN_NEUMANN_LEVELS = 5
ROUTER_PAD = LANES


def _dot(a, b):
    return jnp.dot(a, b, preferred_element_type=F32)


def _dot_nt(a, b):
    return lax.dot_general(a, b, (((1,), (1,)), ((), ())), preferred_element_type=F32)


def _params(*sem):
    return pltpu.CompilerParams(dimension_semantics=sem, vmem_limit_bytes=VMEM_LIMIT)


def _scan_chunk(d, i, ctx_chunks, all_chunks):
    back = jnp.where(i < ctx_chunks, ctx_chunks - 1 - i, all_chunks + ctx_chunks - 1 - i)
    return jnp.where(d == 0, i, back)


def _to_time_major(t):
    return t.transpose(1, 0, 2).reshape(t.shape[0] * t.shape[1], t.shape[2])


def _silu(x):
    return x * jax.nn.sigmoid(x)


def _softplus(x):
    return jnp.maximum(x, 0.0) + jnp.log(1.0 + jnp.exp(-jnp.abs(x)))


def _segment_edges(chunk, ctx_chunks, all_chunks):
    first = jnp.logical_or(chunk == 0, chunk == ctx_chunks)
    last = jnp.logical_or(chunk == ctx_chunks - 1, chunk == all_chunks - 1)
    return first, last


def _gdn_prep_body(ctx_chunks, all_chunks, prev_ref, x_ref, next_ref, bd_ref, cw_ref, al_ref, dtb_ref,
                   qkv_ref, bg_ref, ext_scr):
    rows = x_ref.shape[0]
    steps = rows // BATCH
    halo = CONV_PAD_LEFT * BATCH
    seg_first, seg_last = _segment_edges(pl.program_id(0), ctx_chunks, all_chunks)
    ext_scr[0:halo, :] = jnp.where(seg_first, 0.0, prev_ref[...])
    ext_scr[halo:halo + rows, :] = x_ref[...]
    ext_scr[halo + rows:, :] = jnp.where(seg_last, 0.0, next_ref[...])
    for j in range(GDN_QKV // LANES):
        sl = slice(j * LANES, (j + 1) * LANES)
        c = sum(cw_ref[k:k + 1, sl] * ext_scr[k * BATCH:k * BATCH + rows, sl] for k in range(CONV_K))
        a = _silu(c)
        if j < 2 * GDN_HEADS:
            a = a * lax.rsqrt(jnp.sum(a * a, -1, keepdims=True) + NORM_EPS)
        if j < GDN_HEADS:
            a = a * (GDN_DK ** -0.5)
        qkv_ref[:, sl] = a

    bd = bd_ref[...]
    lane = lax.broadcasted_iota(jnp.int32, bd.shape, 1)
    val = jnp.where(lane < GDN_BG_G, jax.nn.sigmoid(bd), al_ref[...] * _softplus(bd + dtb_ref[...]))
    prefix = val
    for k in range((steps - 1).bit_length()):
        sh = BATCH << k
        prefix = prefix + jnp.concatenate([jnp.zeros((sh, LANES), F32), prefix[:rows - sh]], 0)
    p3 = prefix.reshape(steps, BATCH, LANES)
    suffix = (p3[steps - 1][None] - p3 + val.reshape(steps, BATCH, LANES)).reshape(rows, LANES)
    fwd_g = jnp.logical_and(lane >= GDN_BG_G, lane < GDN_BG_G + GDN_HEADS)
    bwd_g = jnp.logical_and(lane >= GDN_BG_G + GDN_HEADS, lane < GDN_BG_G + 2 * GDN_HEADS)
    bg_ref[...] = jnp.where(fwd_g, prefix, jnp.where(bwd_g, suffix, val))


def _gdn_prep(z, conv_w, a_log, dt_bias, n_ctx, n_all):
    rows = n_all * BATCH
    ctx_chunks, all_chunks = n_ctx // GDN_CHUNK, n_all // GDN_CHUNK
    halo_prev, halo_next = CONV_PAD_LEFT * BATCH, (CONV_K - 1 - CONV_PAD_LEFT) * BATCH
    lanes16 = lambda t: jnp.zeros((1, LANES), F32).at[0, GDN_BG_G:GDN_BG_G + 2 * GDN_HEADS].set(t.reshape(-1))
    return pl.pallas_call(
        functools.partial(_gdn_prep_body, ctx_chunks, all_chunks),
        grid=(all_chunks,),
        in_specs=[pl.BlockSpec((halo_prev, GDN_QKV), lambda i: (jnp.maximum(i * (GDN_ROWS // halo_prev) - 1, 0), 0)),
                  pl.BlockSpec((GDN_ROWS, GDN_QKV), lambda i: (i, 0)),
                  pl.BlockSpec((halo_next, GDN_QKV),
                               lambda i: (jnp.minimum((i + 1) * (GDN_ROWS // halo_next), rows // halo_next - 1), 0)),
                  pl.BlockSpec((GDN_ROWS, LANES), lambda i: (i, ODD_BD // LANES)),
                  pl.BlockSpec((CONV_K, GDN_QKV), lambda i: (0, 0)),
                  pl.BlockSpec((1, LANES), lambda i: (0, 0)),
                  pl.BlockSpec((1, LANES), lambda i: (0, 0))],
        out_specs=[pl.BlockSpec((GDN_ROWS, GDN_QKV), lambda i: (i, 0)),
                   pl.BlockSpec((GDN_ROWS, LANES), lambda i: (i, 0))],
        out_shape=[jax.ShapeDtypeStruct((rows, GDN_QKV), F32), jax.ShapeDtypeStruct((rows, LANES), F32)],
        scratch_shapes=[pltpu.VMEM((GDN_ROWS + halo_prev + halo_next, GDN_QKV), F32)],
        compiler_params=_params("arbitrary"),
        name="gdn_prep",
    )(z, z, z, z, conv_w, lanes16(-jnp.exp(a_log)), lanes16(dt_bias))


def _dot_split(a, b):
    ah, bh = a.astype(BF16), b.astype(BF16)
    al, bl = (a - ah.astype(F32)).astype(BF16), (b - bh.astype(F32)).astype(BF16)
    return _dot(ah, bh) + _dot(ah, bl) + _dot(al, bh)


def _gdn_unit(d, h, q_ref, k_ref, v_ref, bg, bg_t, o_ref, s_scr):
    sl = slice(h * GDN_DK, (h + 1) * GDN_DK)
    q, k, v = q_ref[:, sl], k_ref[:, sl], v_ref[:, sl]
    cb = d * GDN_HEADS + h
    cg = GDN_BG_G + cb
    last = GDN_CHUNK - 1 if d == 0 else 0
    beta, g_col = bg[:, cb:cb + 1], bg[:, cg:cg + 1]
    g_row, g_last = bg_t[cg:cg + 1, :], bg_t[cg:cg + 1, last:last + 1]
    ii = lax.broadcasted_iota(jnp.int32, (GDN_CHUNK, GDN_CHUNK), 0)
    jj = lax.broadcasted_iota(jnp.int32, (GDN_CHUNK, GDN_CHUNK), 1)
    incl = (jj <= ii) if d == 0 else (jj >= ii)
    strict = (jj < ii) if d == 0 else (jj > ii)
    decay = jnp.where(incl, jnp.exp(jnp.where(incl, g_col - g_row, 0.0)), 0.0)
    kb = k.astype(BF16)
    qk = _dot_nt(q.astype(BF16), kb) * decay
    n = jnp.where(strict, -(_dot_nt(kb, kb) * decay * beta), 0.0)
    t = jnp.where(ii == jj, 1.0, 0.0) + n
    p = n
    for _ in range(GDN_NEUMANN_LEVELS):
        pb = p.astype(BF16)
        p = _dot(pb, pb)
        t = t + _dot(t.astype(BF16), p.astype(BF16))
    eg = jnp.exp(g_col)
    sol = _dot_split(t, jnp.concatenate([v * beta, k * (beta * eg)], 1))
    u, w = sol[:, :GDN_DV], sol[:, GDN_DV:]
    state = s_scr[cb]
    sb = state.astype(BF16)
    v_new = u - _dot(w.astype(BF16), sb)
    vb = v_new.astype(BF16)
    o_ref[:, sl] = _dot((q * eg).astype(BF16), sb) + _dot(qk.astype(BF16), vb)
    kg = k * jnp.exp(g_last - g_col)
    s_scr[cb] = state * jnp.exp(g_last) + _dot(kg.T.astype(BF16), vb)


def _gdn_body(qf, kf, vf, bgf, qb, kb, vb, bgb, of_ref, ob_ref, s_scr):
    @pl.when(pl.program_id(1) == 0)
    def _():
        s_scr[...] = jnp.zeros_like(s_scr)

    for d, (q_ref, k_ref, v_ref, bg_ref, o_ref) in enumerate(((qf, kf, vf, bgf, of_ref),
                                                              (qb, kb, vb, bgb, ob_ref))):
        bg = bg_ref[...]
        bg_t = bg.T
        for h in range(GDN_HEADS):
            _gdn_unit(d, h, q_ref, k_ref, v_ref, bg, bg_t, o_ref, s_scr)


def _gdn(qkv, bg, n_ctx, n_all):
    ctx_chunks, all_chunks = n_ctx // GDN_CHUNK, n_all // GDN_CHUNK
    qv = qkv.reshape(n_all, BATCH * GDN_QKV)
    bv = bg.reshape(n_all, BATCH * LANES)
    per_b = GDN_QKV // HALF_W

    def seq_specs(d):
        ch = lambda i: _scan_chunk(d, i, ctx_chunks, all_chunks)
        return [pl.BlockSpec((GDN_CHUNK, HALF_W), lambda b, i: (ch(i), b * per_b)),
                pl.BlockSpec((GDN_CHUNK, HALF_W), lambda b, i: (ch(i), b * per_b + 1)),
                pl.BlockSpec((GDN_CHUNK, HALF_W), lambda b, i: (ch(i), b * per_b + 2)),
                pl.BlockSpec((GDN_CHUNK, LANES), lambda b, i: (ch(i), b))]

    out_spec = lambda d: pl.BlockSpec((GDN_CHUNK, HALF_W),
                                      lambda b, i: (_scan_chunk(d, i, ctx_chunks, all_chunks), b))
    o_shape = jax.ShapeDtypeStruct((n_all, BATCH * HALF_W), F32)
    of, ob = pl.pallas_call(
        _gdn_body,
        grid=(BATCH, all_chunks),
        in_specs=seq_specs(0) + seq_specs(1),
        out_specs=[out_spec(0), out_spec(1)],
        out_shape=[o_shape, o_shape],
        scratch_shapes=[pltpu.VMEM((2 * GDN_HEADS, GDN_DK, GDN_DV), F32)],
        compiler_params=_params("arbitrary", "arbitrary"),
        name="gdn_chunks",
    )(qv, qv, qv, bv, qv, qv, qv, bv)
    return of.reshape(n_all * BATCH, HALF_W), ob.reshape(n_all * BATCH, HALF_W)


ADALN_COLS = 512
ADALN_ROWS = 2 * BATCH


def _adaln_body(c_ref, w_ref, b_ref, o_ref):
    c = c_ref[...]
    s = (c * jax.nn.sigmoid(c)).astype(BF16)
    o_ref[0] = _dot(s, w_ref[0].astype(BF16)) + b_ref[0]


def _adaln_all(cond16, mod_w, mod_b):
    n_out = mod_w.shape[-1]
    return pl.pallas_call(
        _adaln_body,
        grid=(DEPTH, n_out // ADALN_COLS),
        in_specs=[pl.BlockSpec((ADALN_ROWS, D_MODEL), lambda l, j: (0, 0)),
                  pl.BlockSpec((1, D_MODEL, ADALN_COLS), lambda l, j: (l, 0, j)),
                  pl.BlockSpec((1, 1, ADALN_COLS), lambda l, j: (l, 0, j))],
        out_specs=pl.BlockSpec((1, ADALN_ROWS, ADALN_COLS), lambda l, j: (l, 0, j)),
        out_shape=jax.ShapeDtypeStruct((DEPTH, ADALN_ROWS, n_out), F32),
        compiler_params=_params("arbitrary", "arbitrary"),
        name="adaln",
    )(cond16, mod_w, mod_b.reshape(DEPTH, 1, n_out))


def _col_chunks(n, width=512):
    return [(c0, min(width, n - c0)) for c0 in range(0, n, width)]


def _norm_modulate(x3, nw, shift, scale):
    y = x3 * lax.rsqrt(jnp.mean(x3 * x3, -1, keepdims=True) + NORM_EPS) * nw
    return y * (1.0 + scale) + shift


def _in_proj_body(x_ref, nw_ref, sh_ref, sc_ref, w_ref, o_ref):
    tm = x_ref.shape[0]
    x3 = x_ref[...].reshape(tm // BATCH, BATCH, D_MODEL)
    h = _norm_modulate(x3, nw_ref[...], sh_ref[0], sc_ref[0]).reshape(tm, D_MODEL).astype(BF16)
    for c0, cw in _col_chunks(w_ref.shape[1]):
        o_ref[:, c0:c0 + cw] = _dot(h, w_ref[:, c0:c0 + cw])


def _in_proj(xs, norm_w, shift2, scale2, w_bf16, ctx_tiles):
    rows = xs.shape[0]
    n = w_bf16.shape[1]
    kind = lambda r: (jnp.where(r < ctx_tiles, 0, 1), 0, 0)
    return pl.pallas_call(
        _in_proj_body,
        grid=(rows // ROW_TILE,),
        in_specs=[pl.BlockSpec((ROW_TILE, D_MODEL), lambda r: (r, 0)),
                  pl.BlockSpec((1, D_MODEL), lambda r: (0, 0)),
                  pl.BlockSpec((1, BATCH, D_MODEL), kind),
                  pl.BlockSpec((1, BATCH, D_MODEL), kind),
                  pl.BlockSpec((D_MODEL, n), lambda r: (0, 0))],
        out_specs=pl.BlockSpec((ROW_TILE, n), lambda r: (r, 0)),
        out_shape=jax.ShapeDtypeStruct((rows, n), F32),
        compiler_params=_params("arbitrary"),
        name="in_proj",
    )(xs, norm_w.reshape(1, D_MODEL), shift2, scale2, w_bf16)


def _s5_discretize(lam_re, lam_im, log_step, b_re, b_im, c_re, c_im):
    lr, li = lam_re, lam_im
    dt = jnp.exp(log_step)[:, None]
    mag = jnp.exp(lr * dt)
    ar, ai = mag * jnp.cos(li * dt), mag * jnp.sin(li * dt)
    pr, pi = ar - 1.0, ai
    den = lr * lr + li * li
    zr, zi = (pr * lr + pi * li) / den, (pi * lr - pr * li) / den
    bbr = zr[..., None] * b_re - zi[..., None] * b_im
    bbi = zr[..., None] * b_im + zi[..., None] * b_re
    gps = S5_GROUPS // S5_SLABS
    eye = jnp.eye(gps, dtype=F32)

    def in_slab(m):
        m = m.reshape(S5_SLABS, gps, S5_STATE, S5_GROUP)
        return jnp.einsum('sgnc,gh->sgchn', m, eye).reshape(S5_SLABS, S5_SLAB_CH, S5_SLAB_STATE)

    def out_slab(m):
        m = m.reshape(S5_SLABS, gps, S5_GROUP, S5_STATE)
        return jnp.einsum('sgcn,gh->sgnhc', m, eye).reshape(S5_SLABS, S5_SLAB_STATE, S5_SLAB_CH)

    bb = jnp.concatenate([in_slab(bbr), in_slab(bbi)], -1)
    cc = jnp.concatenate([out_slab(c_re), out_slab(-c_im)], 1)
    a = jnp.concatenate([ar.reshape(S5_SLABS, S5_SLAB_STATE), ai.reshape(S5_SLABS, S5_SLAB_STATE)], -1)
    return bb.astype(BF16), cc.astype(BF16), a.reshape(1, S5_SLABS * 2 * S5_SLAB_STATE)


def _s5_body(u_ref, bb_ref, cc_ref, a_ref, y_ref, x_scr, h_scr):
    d = pl.program_id(0)
    rows = u_ref.shape[0]
    steps = rows // BATCH
    sw = 2 * S5_SLAB_STATE

    @pl.when(pl.program_id(1) == 0)
    def _():
        h_scr[...] = jnp.zeros_like(h_scr)

    u = u_ref[...].astype(BF16)
    for s in range(S5_SLABS):
        x_scr[:, s * sw:(s + 1) * sw] = _dot(u[:, s * S5_SLAB_CH:(s + 1) * S5_SLAB_CH], bb_ref[0, s])

    for s in range(S5_SLABS):
        re0, im0 = s * sw, s * sw + S5_SLAB_STATE
        ar = jnp.broadcast_to(a_ref[0, :, re0:im0], (BATCH, S5_SLAB_STATE))
        ai = jnp.broadcast_to(a_ref[0, :, im0:im0 + S5_SLAB_STATE], (BATCH, S5_SLAB_STATE))

        def step(i, carry):
            hr, hi = carry
            t = jnp.where(d == 0, i, steps - 1 - i)
            r0 = pl.multiple_of(t * BATCH, BATCH)
            xr = x_scr[pl.ds(r0, BATCH), re0:im0]
            xi = x_scr[pl.ds(r0, BATCH), im0:im0 + S5_SLAB_STATE]
            nr = ar * hr - ai * hi + xr
            ni = ar * hi + ai * hr + xi
            x_scr[pl.ds(r0, BATCH), re0:im0] = nr
            x_scr[pl.ds(r0, BATCH), im0:im0 + S5_SLAB_STATE] = ni
            return nr, ni

        hr, hi = lax.fori_loop(0, steps, step,
                               (h_scr[:, re0:im0], h_scr[:, im0:im0 + S5_SLAB_STATE]), unroll=4)
        h_scr[:, re0:im0] = hr
        h_scr[:, im0:im0 + S5_SLAB_STATE] = hi

    for s in range(S5_SLABS):
        y_ref[0, :, s * S5_SLAB_CH:(s + 1) * S5_SLAB_CH] = _dot(
            x_scr[:, s * sw:(s + 1) * sw].astype(BF16), cc_ref[0, s])


def _s5_scan(z, bb, cc, a, n_ctx, n_all):
    rows = n_all * BATCH
    blk = S5_STEPS * BATCH
    ctx_chunks, all_chunks = n_ctx // S5_STEPS, n_all // S5_STEPS
    chunk = lambda d, i: _scan_chunk(d, i, ctx_chunks, all_chunks)
    state_w = S5_SLABS * 2 * S5_SLAB_STATE
    return pl.pallas_call(
        _s5_body,
        grid=(2, all_chunks),
        in_specs=[pl.BlockSpec((blk, HALF_W), lambda d, i: (chunk(d, i), 0)),
                  pl.BlockSpec((1, S5_SLABS, S5_SLAB_CH, 2 * S5_SLAB_STATE), lambda d, i: (d, 0, 0, 0)),
                  pl.BlockSpec((1, S5_SLABS, 2 * S5_SLAB_STATE, S5_SLAB_CH), lambda d, i: (d, 0, 0, 0)),
                  pl.BlockSpec((1, 1, state_w), lambda d, i: (d, 0, 0))],
        out_specs=pl.BlockSpec((1, blk, HALF_W), lambda d, i: (d, chunk(d, i), 0)),
        out_shape=jax.ShapeDtypeStruct((2, rows, HALF_W), F32),
        scratch_shapes=[pltpu.VMEM((blk, state_w), F32), pltpu.VMEM((BATCH, state_w), F32)],
        compiler_params=_params("arbitrary", "arbitrary"),
        name="s5_scan",
    )(z, bb, cc, a)


def _rope_tables(n_ctx, n_lat):
    half = RET_DK // 2
    nf = half // 2
    grid_rows = n_lat // GRID_W
    row = jnp.repeat(jnp.arange(grid_rows, dtype=F32), GRID_W)
    col = jnp.tile(jnp.arange(GRID_W, dtype=F32), grid_rows)
    inv = ROPE_BASE ** (-jnp.arange(nf, dtype=F32) / nf)
    ang = jnp.concatenate([row[:, None] * inv, col[:, None] * inv], -1)
    cos = jnp.tile(jnp.cos(ang), (1, 2 * RET_HEADS))
    sin = jnp.tile(jnp.concatenate([-jnp.sin(ang), jnp.sin(ang)], -1), (1, RET_HEADS))
    cos = jnp.concatenate([jnp.ones((n_ctx, RET_QK), F32), cos], 0)
    sin = jnp.concatenate([jnp.zeros((n_ctx, RET_QK), F32), sin], 0)
    return cos, sin


def _ret_tables(ret_decay):
    lg = jax.nn.log_sigmoid(ret_decay)
    pos = jnp.arange(RET_CHUNK, dtype=F32)
    diff = pos[:, None] - pos[None, :]
    f_mask, b_mask = diff >= 0, diff < 0
    dm_f = jnp.where(f_mask, jnp.exp(lg[0][:, None, None] * jnp.where(f_mask, diff, 0.0)), 0.0)
    dm_b = jnp.where(b_mask, jnp.exp(lg[1][:, None, None] * jnp.where(b_mask, -diff, 0.0)), 0.0)
    dmat = jnp.stack([dm_f, dm_b])
    heads = lambda t: jnp.repeat(t, RET_DK, axis=-1)
    kdec = jnp.stack([heads(jnp.exp(lg[0][None] * (RET_CHUNK - 1 - pos)[:, None])),
                      heads(jnp.exp(lg[1][None] * pos[:, None]))])
    qdec = jnp.stack([heads(jnp.exp(lg[0][None] * (pos + 1)[:, None])),
                      heads(jnp.exp(lg[1][None] * (RET_CHUNK - pos)[:, None]))])
    blk = jnp.kron(jnp.eye(RET_HEADS, dtype=F32), jnp.ones((RET_DK, RET_DV), F32))
    sdec = jnp.repeat(jnp.exp(lg * RET_CHUNK), RET_DK, axis=-1)[:, :, None] * blk[None]
    return dmat, kdec, qdec, sdec, blk


def _rope(t, cos, sin):
    lane = lax.broadcasted_iota(jnp.int32, t.shape, 1)
    first = (lane % RET_DK) < (RET_DK // 2)
    partner = jnp.where(first, pltpu.roll(t, RET_QK - RET_DK // 2, 1), pltpu.roll(t, RET_DK // 2, 1))
    return t * cos + partner * sin


def _ret_direction(d, q_ref, k_ref, v_ref, cos_ref, sin_ref, dm_ref, kd_ref, qd_ref, sd_ref, blk_ref,
                   o_ref, s_scr):
    cos, sin = cos_ref[...], sin_ref[...]
    q = _rope(q_ref[...], cos, sin)
    k = _rope(k_ref[...], cos, sin) * (RET_DK ** -0.5)
    v = v_ref[...].astype(BF16)
    kb = k.astype(BF16)
    lane = lax.broadcasted_iota(jnp.int32, q.shape, 1)
    state = s_scr[...]
    o_inter = _dot((q * qd_ref[d]).astype(BF16), state.astype(BF16))
    for h in range(RET_HEADS):
        qh = jnp.where(lane // RET_DK == h, q, 0.0).astype(BF16)
        scores = _dot_nt(qh, kb) * dm_ref[d, h]
        o_ref[:, h * RET_DV:(h + 1) * RET_DV] = (
            _dot(scores.astype(BF16), v[:, h * RET_DV:(h + 1) * RET_DV])
            + o_inter[:, h * RET_DV:(h + 1) * RET_DV])
    kv = _dot((k * kd_ref[d]).T.astype(BF16), v)
    s_scr[...] = sd_ref[d] * state + blk_ref[...] * kv


def _ret_body(qf, kf, vf, cf, sf, qb, kb, vb, cb, sb, dm_ref, kd_ref, qd_ref, sd_ref, blk_ref,
              of_ref, ob_ref, sf_scr, sb_scr):
    @pl.when(pl.program_id(1) == 0)
    def _():
        sf_scr[...] = jnp.zeros_like(sf_scr)
        sb_scr[...] = jnp.zeros_like(sb_scr)

    _ret_direction(0, qf, kf, vf, cf, sf, dm_ref, kd_ref, qd_ref, sd_ref, blk_ref, of_ref, sf_scr)
    _ret_direction(1, qb, kb, vb, cb, sb, dm_ref, kd_ref, qd_ref, sd_ref, blk_ref, ob_ref, sb_scr)


def _retention(z, cos, sin, tables, n_ctx, n_all):
    dmat, kdec, qdec, sdec, blk = tables
    zv = z.reshape(n_all, BATCH * EVEN_IN)
    ctx_chunks, all_chunks = n_ctx // RET_CHUNK, n_all // RET_CHUNK
    per_b = EVEN_IN // RET_QK
    per_bv = EVEN_IN // HALF_W
    q0, k0, v0 = HALF_W // RET_QK, HALF_W // RET_QK + 1, (HALF_W + 2 * RET_QK) // HALF_W

    def seq_specs(d):
        ch = lambda b, i: _scan_chunk(d, i, ctx_chunks, all_chunks)
        return [pl.BlockSpec((RET_CHUNK, RET_QK), lambda b, i: (ch(b, i), b * per_b + q0)),
                pl.BlockSpec((RET_CHUNK, RET_QK), lambda b, i: (ch(b, i), b * per_b + k0)),
                pl.BlockSpec((RET_CHUNK, HALF_W), lambda b, i: (ch(b, i), b * per_bv + v0)),
                pl.BlockSpec((RET_CHUNK, RET_QK), lambda b, i: (ch(b, i), 0)),
                pl.BlockSpec((RET_CHUNK, RET_QK), lambda b, i: (ch(b, i), 0))]

    def out_spec(d):
        return pl.BlockSpec((RET_CHUNK, HALF_W), lambda b, i: (_scan_chunk(d, i, ctx_chunks, all_chunks), b))

    const = lambda nd: (lambda b, i: (0,) * nd)
    o_shape = jax.ShapeDtypeStruct((n_all, BATCH * HALF_W), F32)
    of, ob = pl.pallas_call(
        _ret_body,
        grid=(BATCH, all_chunks),
        in_specs=seq_specs(0) + seq_specs(1) + [
            pl.BlockSpec(dmat.shape, const(4)), pl.BlockSpec(kdec.shape, const(3)),
            pl.BlockSpec(qdec.shape, const(3)), pl.BlockSpec(sdec.shape, const(3)),
            pl.BlockSpec(blk.shape, const(2))],
        out_specs=[out_spec(0), out_spec(1)],
        out_shape=[o_shape, o_shape],
        scratch_shapes=[pltpu.VMEM((RET_QK, HALF_W), F32), pltpu.VMEM((RET_QK, HALF_W), F32)],
        compiler_params=_params("arbitrary", "arbitrary"),
        name="retention",
    )(zv, zv, zv, cos, sin, zv, zv, zv, cos, sin, dmat, kdec, qdec, sdec, blk)
    return of.reshape(n_all * BATCH, HALF_W), ob.reshape(n_all * BATCH, HALF_W)


def _lru_body(ctx_chunks, all_chunks, prev_ref, x_ref, next_ref, cw_ref, cb_ref, wg_ref, bg_ref, sp_ref,
              h_ref, a_scr, b_scr, h_scr):
    d, i = pl.program_id(0), pl.program_id(1)
    rows = x_ref.shape[0]
    steps = rows // BATCH
    chunk = _scan_chunk(d, i, ctx_chunks, all_chunks)

    @pl.when(i == 0)
    def _():
        h_scr[...] = jnp.zeros_like(h_scr)

    seg_first = jnp.logical_or(chunk == 0, chunk == ctx_chunks)
    seg_last = jnp.logical_or(chunk == ctx_chunks - 1, chunk == all_chunks - 1)
    prev = jnp.where(seg_first, 0.0, prev_ref[...])
    nxt = jnp.where(seg_last, 0.0, next_ref[...])
    ext = jnp.concatenate([prev, x_ref[...], nxt], 0)
    xs = cb_ref[...] + sum(cw_ref[k:k + 1, :] * ext[k * BATCH:k * BATCH + rows] for k in range(CONV_K))
    gates = _dot(xs.astype(BF16), wg_ref[0]) + bg_ref[0]
    r = jax.nn.sigmoid(gates[:, :LRU_W])
    ig = jax.nn.sigmoid(gates[:, LRU_W:])
    log_a = -r * sp_ref[0]
    a = jnp.exp(log_a)
    a_scr[...] = a
    b_scr[...] = jnp.sqrt(1.0 - jnp.exp(2.0 * log_a)) * (ig * xs)

    def step(j, h):
        t = jnp.where(d == 0, j, steps - 1 - j)
        r0 = pl.multiple_of(t * BATCH, BATCH)
        h = a_scr[pl.ds(r0, BATCH), :] * h + b_scr[pl.ds(r0, BATCH), :]
        h_ref[0, pl.ds(r0, BATCH), :] = h
        return h

    h_scr[...] = lax.fori_loop(0, steps, step, h_scr[...], unroll=8)


def _lru(z, conv_w, conv_b, w_a, b_a, w_x, b_x, lam, n_ctx, n_all):
    rows = n_all * BATCH
    blk = LRU_STEPS * BATCH
    ctx_chunks, all_chunks = n_ctx // LRU_STEPS, n_all // LRU_STEPS
    chunk = lambda d, i: _scan_chunk(d, i, ctx_chunks, all_chunks)
    col = ODD_LX // LRU_W
    eye = jnp.eye(LRU_BLOCKS, dtype=F32)
    dense = lambda w: jnp.einsum('dkij,kl->dkilj', w, eye).reshape(2, LRU_W, LRU_W)
    wg = jnp.concatenate([dense(w_a), dense(w_x)], -1).astype(BF16)
    bg = jnp.concatenate([b_a, b_x], -1).reshape(2, 1, 2 * LRU_W)
    sp = (LRU_C * jax.nn.softplus(-lam)).reshape(2, 1, LRU_W)
    halo_prev, halo_next = 2 * BATCH, BATCH
    return pl.pallas_call(
        functools.partial(_lru_body, ctx_chunks, all_chunks),
        grid=(2, all_chunks),
        in_specs=[pl.BlockSpec((halo_prev, LRU_W),
                               lambda d, i: (jnp.maximum(chunk(d, i) * (blk // halo_prev) - 1, 0), col)),
                  pl.BlockSpec((blk, LRU_W), lambda d, i: (chunk(d, i), col)),
                  pl.BlockSpec((halo_next, LRU_W),
                               lambda d, i: (jnp.minimum((chunk(d, i) + 1) * (blk // halo_next),
                                                         rows // halo_next - 1), col)),
                  pl.BlockSpec((CONV_K, LRU_W), lambda d, i: (0, 0)),
                  pl.BlockSpec((1, LRU_W), lambda d, i: (0, 0)),
                  pl.BlockSpec((1, LRU_W, 2 * LRU_W), lambda d, i: (d, 0, 0)),
                  pl.BlockSpec((1, 1, 2 * LRU_W), lambda d, i: (d, 0, 0)),
                  pl.BlockSpec((1, 1, LRU_W), lambda d, i: (d, 0, 0))],
        out_specs=pl.BlockSpec((1, blk, LRU_W), lambda d, i: (d, chunk(d, i), 0)),
        out_shape=jax.ShapeDtypeStruct((2, rows, LRU_W), F32),
        scratch_shapes=[pltpu.VMEM((blk, LRU_W), F32), pltpu.VMEM((blk, LRU_W), F32),
                        pltpu.VMEM((BATCH, LRU_W), F32)],
        compiler_params=_params("arbitrary", "arbitrary"),
        name="rglru",
    )(z, z, z, conv_w, conv_b.reshape(1, LRU_W), wg, bg, sp)


def _residual_norm_route(upd, x_ref, g2_ref, nw_ref, sh_ref, sc_ref, wrh_ref, wrl_ref, rb_ref,
                         xo_ref, hf_ref, lg_ref):
    tm = x_ref.shape[0]
    x3 = x_ref[...].reshape(tm // BATCH, BATCH, D_MODEL) + g2_ref[0] * upd.reshape(tm // BATCH, BATCH, D_MODEL)
    xo_ref[...] = x3.reshape(tm, D_MODEL)
    hf = _norm_modulate(x3, nw_ref[...], sh_ref[0], sc_ref[0]).reshape(tm, D_MODEL)
    for s in range(ROW_CHUNKS):
        hf_ref[:, s, :] = hf[:, s * LANES:(s + 1) * LANES]
    hi = hf.astype(BF16)
    lo = (hf - hi.astype(F32)).astype(BF16)
    lg_ref[...] = _dot(hi, wrh_ref[...]) + _dot(hi, wrl_ref[...]) + _dot(lo, wrh_ref[...]) + rb_ref[...]


def _gelu_tanh(x):
    return 0.5 * x * (1.0 + jnp.tanh(0.7978845608028654 * (x + 0.044715 * x * x * x)))


def _out_body_even(yf_ref, yb_ref, u_ref, d_ref, gw_ref, gb_ref, of_ref, ob_ref, gate_ref, wo_ref, *rest):
    ys = yf_ref[0] + yb_ref[0] + d_ref[...] * u_ref[...]
    y = _gelu_tanh(ys)
    s5o = y * jax.nn.sigmoid(_dot(y.astype(BF16), gw_ref[...]) + gb_ref[...])
    upd = _dot(s5o.astype(BF16), wo_ref[0:HALF_W, :])
    o = of_ref[...] + ob_ref[...]
    gate = gate_ref[...]
    for h in range(RET_HEADS):
        oh = o[:, h * RET_DV:(h + 1) * RET_DV]
        mu = jnp.mean(oh, -1, keepdims=True)
        var = jnp.mean(jnp.square(oh - mu), -1, keepdims=True)
        rh = _silu(gate[:, h * RET_DV:(h + 1) * RET_DV]) * ((oh - mu) * lax.rsqrt(var + NORM_EPS))
        upd = upd + _dot(rh.astype(BF16), wo_ref[HALF_W + h * RET_DV:HALF_W + (h + 1) * RET_DV, :])
    _residual_norm_route(upd, *rest)


def _out_body_odd(gf_ref, gb_ref, zg_ref, gnw_ref, hf_ref, hb_ref, lg_ref, wo_ref, *rest):
    lru = (hf_ref[0] + hb_ref[0]) * _gelu_tanh(lg_ref[...])
    upd = _dot(lru.astype(BF16), wo_ref[HALF_W:, :])
    o = gf_ref[...] + gb_ref[...]
    zg = zg_ref[...]
    for h in range(GDN_HEADS):
        sl = slice(h * GDN_DV, (h + 1) * GDN_DV)
        oh = o[:, sl]
        oh = oh * lax.rsqrt(jnp.mean(oh * oh, -1, keepdims=True) + NORM_EPS) * gnw_ref[...] * _silu(zg[:, sl])
        upd = upd + _dot(oh.astype(BF16), wo_ref[h * GDN_DV:(h + 1) * GDN_DV, :])
    _residual_norm_route(upd, *rest)


def _out_proj(body, mix_args, mix_specs, xs, w_out, gate2, norm_w, shift2, scale2, wr_hi, wr_lo, rb,
              row0_tiles, n_tiles, ctx_tiles):
    kind = lambda r: (jnp.where(r + row0_tiles < ctx_tiles, 0, 1), 0, 0)
    const2 = lambda r: (0, 0)
    rows = n_tiles * ROW_TILE
    return pl.pallas_call(
        body,
        grid=(n_tiles,),
        in_specs=mix_specs + [
            pl.BlockSpec((D_MODEL, D_MODEL), const2),
            pl.BlockSpec((ROW_TILE, D_MODEL), lambda r: (r + row0_tiles, 0)),
            pl.BlockSpec((1, BATCH, D_MODEL), kind),
            pl.BlockSpec((1, D_MODEL), const2),
            pl.BlockSpec((1, BATCH, D_MODEL), kind),
            pl.BlockSpec((1, BATCH, D_MODEL), kind),
            pl.BlockSpec((D_MODEL, ROUTER_PAD), const2),
            pl.BlockSpec((D_MODEL, ROUTER_PAD), const2),
            pl.BlockSpec((1, ROUTER_PAD), const2)],
        out_specs=[pl.BlockSpec((ROW_TILE, D_MODEL), lambda r: (r, 0)),
                   pl.BlockSpec((ROW_TILE, ROW_CHUNKS, LANES), lambda r: (r, 0, 0)),
                   pl.BlockSpec((ROW_TILE, ROUTER_PAD), lambda r: (r, 0))],
        out_shape=[jax.ShapeDtypeStruct((rows, D_MODEL), F32),
                   jax.ShapeDtypeStruct((rows, ROW_CHUNKS, LANES), F32),
                   jax.ShapeDtypeStruct((rows, ROUTER_PAD), F32)],
        compiler_params=_params("arbitrary"),
        name="out_proj",
    )(*mix_args, w_out, xs, gate2, norm_w.reshape(1, D_MODEL), shift2, scale2, wr_hi, wr_lo, rb)


def _route(logits):
    n_tok = logits.shape[0]
    g_logits = logits[:, :N_GROUPS]
    g_idx = jnp.argmax(g_logits, -1)
    g_w = jnp.take_along_axis(jax.nn.softmax(g_logits, -1), g_idx[:, None], 1)
    e_logits = logits[:, N_GROUPS:N_GROUPS + N_EXPERTS].reshape(n_tok, N_GROUPS, EXPERTS_PER_GROUP)
    e_in = jnp.take_along_axis(e_logits, g_idx[:, None, None], 1)[:, 0]
    top_v, top_i = lax.top_k(e_in, TOP_K)
    top_w = jax.nn.softmax(top_v, -1) * g_w
    return (g_idx[:, None] * EXPERTS_PER_GROUP + top_i).astype(jnp.int32), top_w


def _dispatch_plan(expert_ids, top_w, n_tiles):
    n_tok = expert_ids.shape[0]
    n_asg = n_tok * TOP_K
    flat_e = expert_ids.T.reshape(n_asg)
    flat_w = top_w.T.reshape(n_asg)
    order = jnp.argsort(flat_e, stable=True).astype(jnp.int32)
    inv = jnp.argsort(order).astype(jnp.int32)
    onehot = flat_e[:, None] == jnp.arange(N_EXPERTS, dtype=jnp.int32)[None]
    counts = jnp.sum(onehot, 0, dtype=jnp.int32)
    padded = ((counts + MOE_TILE - 1) // MOE_TILE) * MOE_TILE
    pend = jnp.cumsum(padded)
    pstart = pend - padded
    start = jnp.cumsum(counts) - counts
    shift = pstart - start
    pos = inv + jnp.sum(jnp.where(onehot, shift[None], 0), 1)
    tile_start = jnp.arange(n_tiles, dtype=jnp.int32) * MOE_TILE
    tile_e = jnp.minimum(jnp.sum(tile_start[:, None] >= pend[None], 1), N_EXPERTS - 1).astype(jnp.int32)
    tile_valid = (tile_start < pend[-1]).astype(jnp.int32)
    row = jnp.arange(n_tiles * MOE_TILE, dtype=jnp.int32)
    row_e = jnp.repeat(tile_e, MOE_TILE)
    rank = row - shift[row_e]
    row_ok = jnp.logical_and(rank >= start[row_e], rank < start[row_e] + counts[row_e])
    src_asg = order[jnp.clip(rank, 0, n_asg - 1)]
    src_tok = jnp.where(row_ok, src_asg % n_tok, 0)
    row_w = jnp.where(row_ok, flat_w[src_asg], 0.0)
    return src_tok, row_w, pos.reshape(TOP_K, n_tok), tile_e, tile_valid


def _gather_rows(idx_ref, src_hbm, dst_vmem, sem, n_rows):
    def issue(r, c):
        pltpu.make_async_copy(src_hbm.at[pl.ds(idx_ref[0, 0, r], 1)], dst_vmem.at[pl.ds(r, 1)], sem).start()
        return c

    lax.fori_loop(0, n_rows, issue, 0, unroll=8)


def _wait_rows(src_hbm, dst_vmem, sem, n_rows):
    def wait(r, c):
        pltpu.make_async_copy(src_hbm.at[pl.ds(0, 1)], dst_vmem.at[pl.ds(r, 1)], sem).wait()
        return c

    lax.fori_loop(0, n_rows, wait, 0, unroll=8)


def _rows_2d(ref):
    return jnp.concatenate([ref[:, s, :] for s in range(ROW_CHUNKS)], 1)


def _dispatch_body(idx_ref, src_hbm, o_ref, sem):
    _gather_rows(idx_ref, src_hbm, o_ref, sem.at[0], MOE_TILE)
    _wait_rows(src_hbm, o_ref, sem.at[0], MOE_TILE)


def _dispatch(src_tok, hf, n_tiles):
    return pl.pallas_call(
        _dispatch_body,
        grid=(n_tiles,),
        in_specs=[pl.BlockSpec((1, 1, MOE_TILE), lambda i: (i, 0, 0), memory_space=pltpu.SMEM),
                  pl.BlockSpec(memory_space=pl.ANY)],
        out_specs=pl.BlockSpec((MOE_TILE, ROW_CHUNKS, LANES), lambda i: (i, 0, 0)),
        out_shape=jax.ShapeDtypeStruct((n_tiles * MOE_TILE, ROW_CHUNKS, LANES), F32),
        scratch_shapes=[pltpu.SemaphoreType.DMA((1,))],
        compiler_params=_params("arbitrary"),
        name="moe_dispatch",
    )(src_tok.reshape(n_tiles, 1, MOE_TILE), hf)


def _expert_body(te_ref, tv_ref, x_ref, cw_ref, wg_ref, wu_ref, wd_ref, y_ref):
    i = pl.program_id(0)

    @pl.when(tv_ref[i] == 1)
    def _():
        x = _rows_2d(x_ref).astype(BF16)
        g = _dot(x, wg_ref[0])
        u = _dot(x, wu_ref[0])
        act = _silu(g) * u * cw_ref[:, 0:1]
        y = _dot(act.astype(BF16), wd_ref[0])
        for s in range(ROW_CHUNKS):
            y_ref[:, s, :] = y[:, s * LANES:(s + 1) * LANES]

    @pl.when(tv_ref[i] == 0)
    def _():
        y_ref[...] = jnp.zeros_like(y_ref)


def _experts(tile_e, tile_valid, xs_sorted, cw_rows, wg, wu, wd):
    n_tiles = tile_e.shape[0]
    tile3 = pl.BlockSpec((MOE_TILE, ROW_CHUNKS, LANES), lambda i, te, tv: (i, 0, 0))
    grid_spec = pltpu.PrefetchScalarGridSpec(
        num_scalar_prefetch=2,
        grid=(n_tiles,),
        in_specs=[tile3,
                  pl.BlockSpec((MOE_TILE, LANES), lambda i, te, tv: (i, 0)),
                  pl.BlockSpec((1, D_MODEL, EXPERT_FF), lambda i, te, tv: (te[i], 0, 0)),
                  pl.BlockSpec((1, D_MODEL, EXPERT_FF), lambda i, te, tv: (te[i], 0, 0)),
                  pl.BlockSpec((1, EXPERT_FF, D_MODEL), lambda i, te, tv: (te[i], 0, 0))],
        out_specs=tile3)
    return pl.pallas_call(
        _expert_body,
        grid_spec=grid_spec,
        out_shape=jax.ShapeDtypeStruct((n_tiles * MOE_TILE, ROW_CHUNKS, LANES), F32),
        compiler_params=_params("arbitrary"),
        name="moe_experts",
    )(tile_e, tile_valid, xs_sorted, cw_rows, wg, wu, wd)


def _combine_body(p0_ref, p1_ref, y_hbm, x_ref, g5_ref, o_ref, buf0, buf1, sem):
    tm = x_ref.shape[0]
    _gather_rows(p0_ref, y_hbm, buf0, sem.at[0], tm)
    _gather_rows(p1_ref, y_hbm, buf1, sem.at[1], tm)
    _wait_rows(y_hbm, buf0, sem.at[0], tm)
    _wait_rows(y_hbm, buf1, sem.at[1], tm)
    f3 = (_rows_2d(buf0) + _rows_2d(buf1)).reshape(tm // BATCH, BATCH, D_MODEL)
    o_ref[...] = (x_ref[...].reshape(tm // BATCH, BATCH, D_MODEL) + g5_ref[0] * f3).reshape(tm, D_MODEL)


def _combine(pos, y_sorted, xs, gate2, row0_tiles, ctx_tiles):
    n_tok = pos.shape[1]
    n_tiles = n_tok // MOE_TILE
    kind = lambda r: (jnp.where(r + row0_tiles < ctx_tiles, 0, 1), 0, 0)
    idx_spec = lambda k: pl.BlockSpec((1, 1, MOE_TILE), lambda r: (k * n_tiles + r, 0, 0),
                                      memory_space=pltpu.SMEM)
    p2 = pos.reshape(TOP_K * n_tiles, 1, MOE_TILE)
    return pl.pallas_call(
        _combine_body,
        grid=(n_tiles,),
        in_specs=[idx_spec(0), idx_spec(1),
                  pl.BlockSpec(memory_space=pl.ANY),
                  pl.BlockSpec((MOE_TILE, D_MODEL), lambda r: (r, 0)),
                  pl.BlockSpec((1, BATCH, D_MODEL), kind)],
        out_specs=pl.BlockSpec((MOE_TILE, D_MODEL), lambda r: (r, 0)),
        out_shape=jax.ShapeDtypeStruct((n_tok, D_MODEL), F32),
        scratch_shapes=[pltpu.VMEM((MOE_TILE, ROW_CHUNKS, LANES), F32),
                        pltpu.VMEM((MOE_TILE, ROW_CHUNKS, LANES), F32),
                        pltpu.SemaphoreType.DMA((2,))],
        compiler_params=_params("arbitrary"),
        name="moe_combine",
    )(p2, p2, y_sorted, xs, gate2)


def _moe(hf, logits, xs, gate2, wg, wu, wd, row0_tiles, ctx_tiles):
    n_tok = hf.shape[0]
    n_tiles = n_tok * TOP_K // MOE_TILE + N_EXPERTS
    expert_ids, top_w = _route(logits)
    src_tok, row_w, pos, tile_e, tile_valid = _dispatch_plan(expert_ids, top_w, n_tiles)
    cw_rows = jnp.broadcast_to(row_w[:, None], (n_tiles * MOE_TILE, LANES))
    xs_sorted = _dispatch(src_tok, hf, n_tiles)
    y_sorted = _experts(tile_e, tile_valid, xs_sorted, cw_rows, wg, wu, wd)
    return _combine(pos, y_sorted, xs, gate2, row0_tiles, ctx_tiles)


FINAL_STEPS = 256


def _final_norm_body(x_ref, w_ref, o_ref):
    x = x_ref[...]
    o_ref[0] = x * lax.rsqrt(jnp.mean(x * x, -1, keepdims=True) + NORM_EPS) * w_ref[...]


def _final_norm(x_lat, w, n_lat):
    return pl.pallas_call(
        _final_norm_body,
        grid=(BATCH, n_lat // FINAL_STEPS),
        in_specs=[pl.BlockSpec((FINAL_STEPS, D_MODEL), lambda b, j: (j, b)),
                  pl.BlockSpec((1, D_MODEL), lambda b, j: (0, 0))],
        out_specs=pl.BlockSpec((1, FINAL_STEPS, D_MODEL), lambda b, j: (b, j, 0)),
        out_shape=jax.ShapeDtypeStruct((BATCH, n_lat, D_MODEL), F32),
        compiler_params=_params("arbitrary", "arbitrary"),
        name="final_norm",
    )(x_lat.reshape(n_lat, BATCH * D_MODEL), w.reshape(1, D_MODEL))


def _kinds(mod_l, k):
    return jnp.stack([mod_l[BATCH:, k], mod_l[:BATCH, k]])


def kernel(x, c, ctx, c_ctx, mod_w, mod_b, norm_mix, norm_ffn, router_group_w, router_group_b,
           router_expert_w, router_expert_b, expert_w_gate, expert_w_up, expert_w_down,
           even_w_in, even_w_out, s5_lam_re, s5_lam_im, s5_log_step, s5_b_re, s5_b_im, s5_c_re,
           s5_c_im, s5_d, s5_glu_w, s5_glu_b, ret_decay, odd_w_in, odd_w_out, gdn_conv_w, gdn_a_log,
           gdn_dt_bias, gdn_norm_w, lru_conv_w, lru_conv_b, lru_w_a, lru_b_a, lru_w_x, lru_b_x,
           lru_lam, final_norm):
    bsz, n_lat, _ = x.shape
    n_ctx = ctx.shape[1]
    n_all = n_ctx + n_lat
    assert bsz == BATCH and n_ctx % RET_CHUNK == 0 and n_lat % FINAL_STEPS == 0
    assert (n_ctx * BATCH) % ROW_TILE == 0 and (n_lat * BATCH) % ROW_TILE == 0
    ctx_tiles = n_ctx * BATCH // ROW_TILE
    all_tiles = n_all * BATCH // ROW_TILE
    ctx_moe_tiles = n_ctx * BATCH // MOE_TILE

    xs = _to_time_major(jnp.concatenate([ctx, x], 1))
    cond16 = jnp.concatenate([c, jnp.broadcast_to(c_ctx[None], (BATCH, D_MODEL))], 0)
    mod = _adaln_all(cond16, mod_w, mod_b).reshape(DEPTH, ADALN_ROWS, N_MOD, D_MODEL)

    for layer in range(DEPTH):
        last = layer == DEPTH - 1
        i = layer // 2
        m = [_kinds(mod[layer], k) for k in range(N_MOD)]
        wr = jnp.zeros((D_MODEL, ROUTER_PAD), F32)
        wr = wr.at[:, :N_GROUPS].set(router_group_w[layer]).at[:, N_GROUPS:N_GROUPS + N_EXPERTS].set(
            router_expert_w[layer])
        wr_hi = wr.astype(BF16)
        wr_lo = (wr - wr_hi.astype(F32)).astype(BF16)
        rb = jnp.zeros((1, ROUTER_PAD), F32)
        rb = rb.at[0, :N_GROUPS].set(router_group_b[layer]).at[0, N_GROUPS:N_GROUPS + N_EXPERTS].set(
            router_expert_b[layer])
        row0 = ctx_tiles if last else 0
        n_tiles = all_tiles - row0
        row_spec = lambda w, cb=0: pl.BlockSpec((ROW_TILE, w), lambda r: (r + row0, cb))
        dir_spec = lambda d: pl.BlockSpec((1, ROW_TILE, HALF_W), lambda r: (d, r + row0, 0))
        const2 = lambda r: (0, 0)

        if layer % 2 == 0:
            z = _in_proj(xs, norm_mix[layer], m[0], m[1], even_w_in[i].astype(BF16), ctx_tiles)
            disc = [_s5_discretize(s5_lam_re[i, d], s5_lam_im[i, d], s5_log_step[i, d], s5_b_re[i, d],
                                   s5_b_im[i, d], s5_c_re[i, d], s5_c_im[i, d]) for d in range(2)]
            bb, cc, a = (jnp.stack(t) for t in zip(*disc))
            ys5 = _s5_scan(z, bb, cc, a, n_ctx, n_all)
            cos, sin = _rope_tables(n_ctx, n_lat)
            r_f, r_b = _retention(z, cos, sin, _ret_tables(ret_decay[i]), n_ctx, n_all)
            mix_args = [ys5, ys5, z, s5_d[i].reshape(1, HALF_W), s5_glu_w[i].astype(BF16),
                        s5_glu_b[i].reshape(1, HALF_W), r_f, r_b, z]
            mix_specs = [dir_spec(0), dir_spec(1), row_spec(HALF_W),
                         pl.BlockSpec((1, HALF_W), const2),
                         pl.BlockSpec((HALF_W, HALF_W), const2),
                         pl.BlockSpec((1, HALF_W), const2),
                         row_spec(HALF_W), row_spec(HALF_W), row_spec(HALF_W, EVEN_SPLITS[3] // HALF_W)]
            body, w_out = _out_body_even, even_w_out[i]
        else:
            w_in = odd_w_in[i]
            w_pad = jnp.concatenate([
                w_in[:, :ODD_SPLITS[1]],
                w_in[:, ODD_SPLITS[3]:],
                jnp.pad(w_in[:, ODD_SPLITS[1]:ODD_SPLITS[3]], ((0, 0), (0, ODD_PAD - 4 * GDN_HEADS)))], 1)
            z = _in_proj(xs, norm_mix[layer], m[0], m[1], w_pad.astype(BF16), ctx_tiles)
            qkv_act, bg = _gdn_prep(z, gdn_conv_w[i], gdn_a_log[i], gdn_dt_bias[i], n_ctx, n_all)
            g_f, g_b = _gdn(qkv_act, bg, n_ctx, n_all)
            hl = _lru(z, lru_conv_w[i], lru_conv_b[i], lru_w_a[i], lru_b_a[i], lru_w_x[i], lru_b_x[i],
                      lru_lam[i], n_ctx, n_all)
            mix_args = [g_f, g_b, z, gdn_norm_w[i].reshape(1, GDN_DV), hl, hl, z]
            mix_specs = [row_spec(HALF_W), row_spec(HALF_W), row_spec(HALF_W, ODD_SPLITS[0] // HALF_W),
                         pl.BlockSpec((1, GDN_DV), const2), dir_spec(0), dir_spec(1),
                         row_spec(LRU_W, (ODD_LX + LRU_W) // LRU_W)]
            body, w_out = _out_body_odd, odd_w_out[i]

        xs_mid, hf, logits = _out_proj(body, mix_args, mix_specs, xs, w_out.astype(BF16), m[2],
                                       norm_ffn[layer], m[3], m[4], wr_hi, wr_lo, rb, row0, n_tiles, ctx_tiles)
        moe_row0 = ctx_moe_tiles if last else 0
        xs = _moe(hf, logits, xs_mid, m[5], expert_w_gate[layer].astype(BF16),
                  expert_w_up[layer].astype(BF16), expert_w_down[layer].astype(BF16), moe_row0, ctx_moe_tiles)

    return _final_norm(xs, final_norm, n_lat)
```

```python
import functools

import jax
import jax.numpy as jnp
from jax import lax
from jax.experimental import pallas as pl
from jax.experimental.pallas import tpu as pltpu

D_MODEL = 1024
DEPTH = 2
GRID_W = 64
HALF_W = D_MODEL // 2
S5_GROUP = 16
S5_GROUPS = HALF_W // S5_GROUP
S5_STATE = 64
RET_HEADS = 4
RET_DV = HALF_W // RET_HEADS
RET_DK = RET_DV // 2
RET_QK = RET_HEADS * RET_DK
RET_CHUNK = 128
ROPE_BASE = 10000.0
GDN_HEADS = 4
GDN_DK = HALF_W // GDN_HEADS
GDN_DV = GDN_DK
GDN_CHUNK = 64
LRU_W = HALF_W
LRU_BLOCKS = 8
LRU_BLOCK = LRU_W // LRU_BLOCKS
LRU_C = 8.0
CONV_K = 4
CONV_PAD_LEFT = 2
N_GROUPS = 4
EXPERTS_PER_GROUP = 8
N_EXPERTS = N_GROUPS * EXPERTS_PER_GROUP
TOP_K = 2
EXPERT_FF = D_MODEL // 2
N_MOD = 6
NORM_EPS = 1e-6
EVEN_SPLITS = [HALF_W, HALF_W + RET_QK, HALF_W + 2 * RET_QK, 2 * HALF_W + 2 * RET_QK]
EVEN_IN = 3 * HALF_W + 2 * RET_QK
ODD_SPLITS = [3 * HALF_W, 4 * HALF_W, 4 * HALF_W + 2 * GDN_HEADS, 4 * HALF_W + 4 * GDN_HEADS,
              4 * HALF_W + 4 * GDN_HEADS + LRU_W]

F32 = jnp.float32
BF16 = jnp.bfloat16

SUBLANES = 8
LANES = 128
BATCH = SUBLANES
ROW_CHUNKS = D_MODEL // LANES
VMEM_LIMIT = 48 * 1024 * 1024

ROW_TILE = 512
S5_STEPS = 64
S5_SLABS = 4
S5_SLAB_CH = HALF_W // S5_SLABS
S5_SLAB_STATE = (S5_GROUPS // S5_SLABS) * S5_STATE
LRU_STEPS = 64
MOE_TILE = 256
ODD_PAD = 512
ODD_LX = 4 * HALF_W
ODD_BD = ODD_LX + 2 * LRU_W
ODD_IN_PADDED = ODD_BD + ODD_PAD
GDN_QKV = 3 * HALF_W
GDN_ROWS = GDN_CHUNK * BATCH
GDN_BG_G = 2 * GDN_HEADS
GDN_NEUMANN_LEVELS = 5
ROUTER_PAD = LANES


def _dot(a, b):
    return jnp.dot(a, b, preferred_element_type=F32)


def _dot_nt(a, b):
    return lax.dot_general(a, b, (((1,), (1,)), ((), ())), preferred_element_type=F32)


def _params(*sem):
    return pltpu.CompilerParams(dimension_semantics=sem, vmem_limit_bytes=VMEM_LIMIT)


def _scan_chunk(d, i, ctx_chunks, all_chunks):
    back = jnp.where(i < ctx_chunks, ctx_chunks - 1 - i, all_chunks + ctx_chunks - 1 - i)
    return jnp.where(d == 0, i, back)


def _to_time_major(t):
    return t.transpose(1, 0, 2).reshape(t.shape[0] * t.shape[1], t.shape[2])


def _silu(x):
    return x * jax.nn.sigmoid(x)


def _softplus(x):
    return jnp.maximum(x, 0.0) + jnp.log(1.0 + jnp.exp(-jnp.abs(x)))


def _segment_edges(chunk, ctx_chunks, all_chunks):
    first = jnp.logical_or(chunk == 0, chunk == ctx_chunks)
    last = jnp.logical_or(chunk == ctx_chunks - 1, chunk == all_chunks - 1)
    return first, last


def _gdn_prep_body(ctx_chunks, all_chunks, prev_ref, x_ref, next_ref, bd_ref, cw_ref, al_ref, dtb_ref,
                   qkv_ref, bg_ref, ext_scr):
    rows = x_ref.shape[0]
    steps = rows // BATCH
    halo = CONV_PAD_LEFT * BATCH
    seg_first, seg_last = _segment_edges(pl.program_id(0), ctx_chunks, all_chunks)
    ext_scr[0:halo, :] = jnp.where(seg_first, 0.0, prev_ref[...])
    ext_scr[halo:halo + rows, :] = x_ref[...]
    ext_scr[halo + rows:, :] = jnp.where(seg_last, 0.0, next_ref[...])
    for j in range(GDN_QKV // LANES):
        sl = slice(j * LANES, (j + 1) * LANES)
        c = sum(cw_ref[k:k + 1, sl] * ext_scr[k * BATCH:k * BATCH + rows, sl] for k in range(CONV_K))
        a = _silu(c)
        if j < 2 * GDN_HEADS:
            a = a * lax.rsqrt(jnp.sum(a * a, -1, keepdims=True) + NORM_EPS)
        if j < GDN_HEADS:
            a = a * (GDN_DK ** -0.5)
        qkv_ref[:, sl] = a

    bd = bd_ref[...]
    lane = lax.broadcasted_iota(jnp.int32, bd.shape, 1)
    val = jnp.where(lane < GDN_BG_G, jax.nn.sigmoid(bd), al_ref[...] * _softplus(bd + dtb_ref[...]))
    prefix = val
    for k in range((steps - 1).bit_length()):
        sh = BATCH << k
        prefix = prefix + jnp.concatenate([jnp.zeros((sh, LANES), F32), prefix[:rows - sh]], 0)
    p3 = prefix.reshape(steps, BATCH, LANES)
    suffix = (p3[steps - 1][None] - p3 + val.reshape(steps, BATCH, LANES)).reshape(rows, LANES)
    fwd_g = jnp.logical_and(lane >= GDN_BG_G, lane < GDN_BG_G + GDN_HEADS)
    bwd_g = jnp.logical_and(lane >= GDN_BG_G + GDN_HEADS, lane < GDN_BG_G + 2 * GDN_HEADS)
    bg_ref[...] = jnp.where(fwd_g, prefix, jnp.where(bwd_g, suffix, val))


def _gdn_prep(z, conv_w, a_log, dt_bias, n_ctx, n_all):
    rows = n_all * BATCH
    ctx_chunks, all_chunks = n_ctx // GDN_CHUNK, n_all // GDN_CHUNK
    halo_prev, halo_next = CONV_PAD_LEFT * BATCH, (CONV_K - 1 - CONV_PAD_LEFT) * BATCH
    lanes16 = lambda t: jnp.zeros((1, LANES), F32).at[0, GDN_BG_G:GDN_BG_G + 2 * GDN_HEADS].set(t.reshape(-1))
    return pl.pallas_call(
        functools.partial(_gdn_prep_body, ctx_chunks, all_chunks),
        grid=(all_chunks,),
        in_specs=[pl.BlockSpec((halo_prev, GDN_QKV), lambda i: (jnp.maximum(i * (GDN_ROWS // halo_prev) - 1, 0), 0)),
                  pl.BlockSpec((GDN_ROWS, GDN_QKV), lambda i: (i, 0)),
                  pl.BlockSpec((halo_next, GDN_QKV),
                               lambda i: (jnp.minimum((i + 1) * (GDN_ROWS // halo_next), rows // halo_next - 1), 0)),
                  pl.BlockSpec((GDN_ROWS, LANES), lambda i: (i, ODD_BD // LANES)),
                  pl.BlockSpec((CONV_K, GDN_QKV), lambda i: (0, 0)),
                  pl.BlockSpec((1, LANES), lambda i: (0, 0)),
                  pl.BlockSpec((1, LANES), lambda i: (0, 0))],
        out_specs=[pl.BlockSpec((GDN_ROWS, GDN_QKV), lambda i: (i, 0)),
                   pl.BlockSpec((GDN_ROWS, LANES), lambda i: (i, 0))],
        out_shape=[jax.ShapeDtypeStruct((rows, GDN_QKV), F32), jax.ShapeDtypeStruct((rows, LANES), F32)],
        scratch_shapes=[pltpu.VMEM((GDN_ROWS + halo_prev + halo_next, GDN_QKV), F32)],
        compiler_params=_params("arbitrary"),
        name="gdn_prep",
    )(z, z, z, z, conv_w, lanes16(-jnp.exp(a_log)), lanes16(dt_bias))


def _dot_split(a, b):
    ah, bh = a.astype(BF16), b.astype(BF16)
    al, bl = (a - ah.astype(F32)).astype(BF16), (b - bh.astype(F32)).astype(BF16)
    return _dot(ah, bh) + _dot(ah, bl) + _dot(al, bh)


def _gdn_body(qf, kf, vf, bgf, qb, kb, vb, bgb, of_ref, ob_ref, s_scr):
    @pl.when(pl.program_id(1) == 0)
    def _():
        s_scr[...] = jnp.zeros_like(s_scr)

    ii = lax.broadcasted_iota(jnp.int32, (GDN_CHUNK, GDN_CHUNK), 0)
    jj = lax.broadcasted_iota(jnp.int32, (GDN_CHUNK, GDN_CHUNK), 1)
    eye = jnp.where(ii == jj, 1.0, 0.0)
    units = []
    for d, (q_ref, k_ref, v_ref, bg_ref, o_ref) in enumerate(((qf, kf, vf, bgf, of_ref),
                                                              (qb, kb, vb, bgb, ob_ref))):
        bg = bg_ref[...]
        bg_t = bg.T
        incl = (jj <= ii) if d == 0 else (jj >= ii)
        strict = (jj < ii) if d == 0 else (jj > ii)
        last = GDN_CHUNK - 1 if d == 0 else 0
        for h in range(GDN_HEADS):
            sl = slice(h * GDN_DK, (h + 1) * GDN_DK)
            q, k, v = q_ref[:, sl], k_ref[:, sl], v_ref[:, sl]
            cb = d * GDN_HEADS + h
            cg = GDN_BG_G + cb
            beta, g_col = bg[:, cb:cb + 1], bg[:, cg:cg + 1]
            g_row, g_last = bg_t[cg:cg + 1, :], bg_t[cg:cg + 1, last:last + 1]
            decay = jnp.where(incl, jnp.exp(jnp.where(incl, g_col - g_row, 0.0)), 0.0)
            kb16 = k.astype(BF16)
            eg = jnp.exp(g_col)
            units.append(dict(
                cb=cb, sl=sl, o_ref=o_ref,
                qk=(_dot_nt(q.astype(BF16), kb16) * decay).astype(BF16),
                n=jnp.where(strict, -(_dot_nt(kb16, kb16) * decay * beta), 0.0),
                rhs=jnp.concatenate([v * beta, k * (beta * eg)], 1),
                qg=(q * eg).astype(BF16),
                kg_t=(k * jnp.exp(g_last - g_col)).T.astype(BF16),
                d_last=jnp.exp(g_last)))

    ts = [eye + u['n'] for u in units]
    ps = [u['n'] for u in units]
    for _ in range(GDN_NEUMANN_LEVELS):
        ps = [_dot(p.astype(BF16), p.astype(BF16)) for p in ps]
        ts = [t + _dot(t.astype(BF16), p.astype(BF16)) for t, p in zip(ts, ps)]
    sols = [_dot_split(t, u['rhs']) for t, u in zip(ts, units)]
    states = [s_scr[u['cb']] for u in units]
    sbs = [s.astype(BF16) for s in states]
    v_news = [(sol[:, :GDN_DV] - _dot(sol[:, GDN_DV:].astype(BF16), sb)).astype(BF16)
              for sol, sb in zip(sols, sbs)]
    for u, s, sb, vn in zip(units, states, sbs, v_news):
        u['o_ref'][:, u['sl']] = _dot(u['qg'], sb) + _dot(u['qk'], vn)
        s_scr[u['cb']] = s * u['d_last'] + _dot(u['kg_t'], vn)


def _gdn(qkv, bg, n_ctx, n_all):
    ctx_chunks, all_chunks = n_ctx // GDN_CHUNK, n_all // GDN_CHUNK
    qv = qkv.reshape(n_all, BATCH * GDN_QKV)
    bv = bg.reshape(n_all, BATCH * LANES)
    per_b = GDN_QKV // HALF_W

    def seq_specs(d):
        ch = lambda i: _scan_chunk(d, i, ctx_chunks, all_chunks)
        return [pl.BlockSpec((GDN_CHUNK, HALF_W), lambda b, i: (ch(i), b * per_b)),
                pl.BlockSpec((GDN_CHUNK, HALF_W), lambda b, i: (ch(i), b * per_b + 1)),
                pl.BlockSpec((GDN_CHUNK, HALF_W), lambda b, i: (ch(i), b * per_b + 2)),
                pl.BlockSpec((GDN_CHUNK, LANES), lambda b, i: (ch(i), b))]

    out_spec = lambda d: pl.BlockSpec((GDN_CHUNK, HALF_W),
                                      lambda b, i: (_scan_chunk(d, i, ctx_chunks, all_chunks), b))
    o_shape = jax.ShapeDtypeStruct((n_all, BATCH * HALF_W), F32)
    of, ob = pl.pallas_call(
        _gdn_body,
        grid=(BATCH, all_chunks),
        in_specs=seq_specs(0) + seq_specs(1),
        out_specs=[out_spec(0), out_spec(1)],
        out_shape=[o_shape, o_shape],
        scratch_shapes=[pltpu.VMEM((2 * GDN_HEADS, GDN_DK, GDN_DV), F32)],
        compiler_params=_params("arbitrary", "arbitrary"),
        name="gdn_chunks",
    )(qv, qv, qv, bv, qv, qv, qv, bv)
    return of.reshape(n_all * BATCH, HALF_W), ob.reshape(n_all * BATCH, HALF_W)


ADALN_COLS = 512
ADALN_ROWS = 2 * BATCH


def _adaln_body(c_ref, w_ref, b_ref, o_ref):
    c = c_ref[...]
    s = (c * jax.nn.sigmoid(c)).astype(BF16)
    o_ref[0] = _dot(s, w_ref[0].astype(BF16)) + b_ref[0]


def _adaln_all(cond16, mod_w, mod_b):
    n_out = mod_w.shape[-1]
    return pl.pallas_call(
        _adaln_body,
        grid=(DEPTH, n_out // ADALN_COLS),
        in_specs=[pl.BlockSpec((ADALN_ROWS, D_MODEL), lambda l, j: (0, 0)),
                  pl.BlockSpec((1, D_MODEL, ADALN_COLS), lambda l, j: (l, 0, j)),
                  pl.BlockSpec((1, 1, ADALN_COLS), lambda l, j: (l, 0, j))],
        out_specs=pl.BlockSpec((1, ADALN_ROWS, ADALN_COLS), lambda l, j: (l, 0, j)),
        out_shape=jax.ShapeDtypeStruct((DEPTH, ADALN_ROWS, n_out), F32),
        compiler_params=_params("arbitrary", "arbitrary"),
        name="adaln",
    )(cond16, mod_w, mod_b.reshape(DEPTH, 1, n_out))


def _col_chunks(n, width=512):
    return [(c0, min(width, n - c0)) for c0 in range(0, n, width)]


def _norm_modulate(x3, nw, shift, scale):
    y = x3 * lax.rsqrt(jnp.mean(x3 * x3, -1, keepdims=True) + NORM_EPS) * nw
    return y * (1.0 + scale) + shift


def _in_proj_body(x_ref, nw_ref, sh_ref, sc_ref, w_ref, o_ref):
    tm = x_ref.shape[0]
    x3 = x_ref[...].reshape(tm // BATCH, BATCH, D_MODEL)
    h = _norm_modulate(x3, nw_ref[...], sh_ref[0], sc_ref[0]).reshape(tm, D_MODEL).astype(BF16)
    for c0, cw in _col_chunks(w_ref.shape[1]):
        o_ref[:, c0:c0 + cw] = _dot(h, w_ref[:, c0:c0 + cw])


def _in_proj(xs, norm_w, shift2, scale2, w_bf16, ctx_tiles):
    rows = xs.shape[0]
    n = w_bf16.shape[1]
    kind = lambda r: (jnp.where(r < ctx_tiles, 0, 1), 0, 0)
    return pl.pallas_call(
        _in_proj_body,
        grid=(rows // ROW_TILE,),
        in_specs=[pl.BlockSpec((ROW_TILE, D_MODEL), lambda r: (r, 0)),
                  pl.BlockSpec((1, D_MODEL), lambda r: (0, 0)),
                  pl.BlockSpec((1, BATCH, D_MODEL), kind),
                  pl.BlockSpec((1, BATCH, D_MODEL), kind),
                  pl.BlockSpec((D_MODEL, n), lambda r: (0, 0))],
        out_specs=pl.BlockSpec((ROW_TILE, n), lambda r: (r, 0)),
        out_shape=jax.ShapeDtypeStruct((rows, n), F32),
        compiler_params=_params("arbitrary"),
        name="in_proj",
    )(xs, norm_w.reshape(1, D_MODEL), shift2, scale2, w_bf16)


def _s5_discretize(lam_re, lam_im, log_step, b_re, b_im, c_re, c_im):
    lr, li = lam_re, lam_im
    dt = jnp.exp(log_step)[:, None]
    mag = jnp.exp(lr * dt)
    ar, ai = mag * jnp.cos(li * dt), mag * jnp.sin(li * dt)
    pr, pi = ar - 1.0, ai
    den = lr * lr + li * li
    zr, zi = (pr * lr + pi * li) / den, (pi * lr - pr * li) / den
    bbr = zr[..., None] * b_re - zi[..., None] * b_im
    bbi = zr[..., None] * b_im + zi[..., None] * b_re
    gps = S5_GROUPS // S5_SLABS
    eye = jnp.eye(gps, dtype=F32)

    def in_slab(m):
        m = m.reshape(S5_SLABS, gps, S5_STATE, S5_GROUP)
        return jnp.einsum('sgnc,gh->sgchn', m, eye).reshape(S5_SLABS, S5_SLAB_CH, S5_SLAB_STATE)

    def out_slab(m):
        m = m.reshape(S5_SLABS, gps, S5_GROUP, S5_STATE)
        return jnp.einsum('sgcn,gh->sgnhc', m, eye).reshape(S5_SLABS, S5_SLAB_STATE, S5_SLAB_CH)

    bb = jnp.concatenate([in_slab(bbr), in_slab(bbi)], -1)
    cc = jnp.concatenate([out_slab(c_re), out_slab(-c_im)], 1)
    a = jnp.concatenate([ar.reshape(S5_SLABS, S5_SLAB_STATE), ai.reshape(S5_SLABS, S5_SLAB_STATE)], -1)
    return bb.astype(BF16), cc.astype(BF16), a.reshape(1, S5_SLABS * 2 * S5_SLAB_STATE)


def _s5_body(u_ref, bb_ref, cc_ref, a_ref, y_ref, x_scr, h_scr):
    d = pl.program_id(0)
    rows = u_ref.shape[0]
    steps = rows // BATCH
    sw = 2 * S5_SLAB_STATE

    @pl.when(pl.program_id(1) == 0)
    def _():
        h_scr[...] = jnp.zeros_like(h_scr)

    u = u_ref[...].astype(BF16)
    for s in range(S5_SLABS):
        x_scr[:, s * sw:(s + 1) * sw] = _dot(u[:, s * S5_SLAB_CH:(s + 1) * S5_SLAB_CH], bb_ref[0, s])

    for s in range(S5_SLABS):
        re0, im0 = s * sw, s * sw + S5_SLAB_STATE
        ar = jnp.broadcast_to(a_ref[0, :, re0:im0], (BATCH, S5_SLAB_STATE))
        ai = jnp.broadcast_to(a_ref[0, :, im0:im0 + S5_SLAB_STATE], (BATCH, S5_SLAB_STATE))

        def step(i, carry):
            hr, hi = carry
            t = jnp.where(d == 0, i, steps - 1 - i)
            r0 = pl.multiple_of(t * BATCH, BATCH)
            xr = x_scr[pl.ds(r0, BATCH), re0:im0]
            xi = x_scr[pl.ds(r0, BATCH), im0:im0 + S5_SLAB_STATE]
            nr = ar * hr - ai * hi + xr
            ni = ar * hi + ai * hr + xi
            x_scr[pl.ds(r0, BATCH), re0:im0] = nr
            x_scr[pl.ds(r0, BATCH), im0:im0 + S5_SLAB_STATE] = ni
            return nr, ni

        hr, hi = lax.fori_loop(0, steps, step,
                               (h_scr[:, re0:im0], h_scr[:, im0:im0 + S5_SLAB_STATE]), unroll=4)
        h_scr[:, re0:im0] = hr
        h_scr[:, im0:im0 + S5_SLAB_STATE] = hi

    for s in range(S5_SLABS):
        y_ref[0, :, s * S5_SLAB_CH:(s + 1) * S5_SLAB_CH] = _dot(
            x_scr[:, s * sw:(s + 1) * sw].astype(BF16), cc_ref[0, s])


def _s5_scan(z, bb, cc, a, n_ctx, n_all):
    rows = n_all * BATCH
    blk = S5_STEPS * BATCH
    ctx_chunks, all_chunks = n_ctx // S5_STEPS, n_all // S5_STEPS
    chunk = lambda d, i: _scan_chunk(d, i, ctx_chunks, all_chunks)
    state_w = S5_SLABS * 2 * S5_SLAB_STATE
    return pl.pallas_call(
        _s5_body,
        grid=(2, all_chunks),
        in_specs=[pl.BlockSpec((blk, HALF_W), lambda d, i: (chunk(d, i), 0)),
                  pl.BlockSpec((1, S5_SLABS, S5_SLAB_CH, 2 * S5_SLAB_STATE), lambda d, i: (d, 0, 0, 0)),
                  pl.BlockSpec((1, S5_SLABS, 2 * S5_SLAB_STATE, S5_SLAB_CH), lambda d, i: (d, 0, 0, 0)),
                  pl.BlockSpec((1, 1, state_w), lambda d, i: (d, 0, 0))],
        out_specs=pl.BlockSpec((1, blk, HALF_W), lambda d, i: (d, chunk(d, i), 0)),
        out_shape=jax.ShapeDtypeStruct((2, rows, HALF_W), F32),
        scratch_shapes=[pltpu.VMEM((blk, state_w), F32), pltpu.VMEM((BATCH, state_w), F32)],
        compiler_params=_params("arbitrary", "arbitrary"),
        name="s5_scan",
    )(z, bb, cc, a)


def _rope_tables(n_ctx, n_lat):
    half = RET_DK // 2
    nf = half // 2
    grid_rows = n_lat // GRID_W
    row = jnp.repeat(jnp.arange(grid_rows, dtype=F32), GRID_W)
    col = jnp.tile(jnp.arange(GRID_W, dtype=F32), grid_rows)
    inv = ROPE_BASE ** (-jnp.arange(nf, dtype=F32) / nf)
    ang = jnp.concatenate([row[:, None] * inv, col[:, None] * inv], -1)
    cos = jnp.tile(jnp.cos(ang), (1, 2 * RET_HEADS))
    sin = jnp.tile(jnp.concatenate([-jnp.sin(ang), jnp.sin(ang)], -1), (1, RET_HEADS))
    cos = jnp.concatenate([jnp.ones((n_ctx, RET_QK), F32), cos], 0)
    sin = jnp.concatenate([jnp.zeros((n_ctx, RET_QK), F32), sin], 0)
    return cos, sin


def _ret_tables(ret_decay):
    lg = jax.nn.log_sigmoid(ret_decay)
    pos = jnp.arange(RET_CHUNK, dtype=F32)
    diff = pos[:, None] - pos[None, :]
    f_mask, b_mask = diff >= 0, diff < 0
    dm_f = jnp.where(f_mask, jnp.exp(lg[0][:, None, None] * jnp.where(f_mask, diff, 0.0)), 0.0)
    dm_b = jnp.where(b_mask, jnp.exp(lg[1][:, None, None] * jnp.where(b_mask, -diff, 0.0)), 0.0)
    dmat = jnp.stack([dm_f, dm_b])
    heads = lambda t: jnp.repeat(t, RET_DK, axis=-1)
    kdec = jnp.stack([heads(jnp.exp(lg[0][None] * (RET_CHUNK - 1 - pos)[:, None])),
                      heads(jnp.exp(lg[1][None] * pos[:, None]))])
    qdec = jnp.stack([heads(jnp.exp(lg[0][None] * (pos + 1)[:, None])),
                      heads(jnp.exp(lg[1][None] * (RET_CHUNK - pos)[:, None]))])
    blk = jnp.kron(jnp.eye(RET_HEADS, dtype=F32), jnp.ones((RET_DK, RET_DV), F32))
    sdec = jnp.repeat(jnp.exp(lg * RET_CHUNK), RET_DK, axis=-1)[:, :, None] * blk[None]
    return dmat, kdec, qdec, sdec, blk


def _rope(t, cos, sin):
    lane = lax.broadcasted_iota(jnp.int32, t.shape, 1)
    first = (lane % RET_DK) < (RET_DK // 2)
    partner = jnp.where(first, pltpu.roll(t, RET_QK - RET_DK // 2, 1), pltpu.roll(t, RET_DK // 2, 1))
    return t * cos + partner * sin


def _ret_direction(d, q_ref, k_ref, v_ref, cos_ref, sin_ref, dm_ref, kd_ref, qd_ref, sd_ref, blk_ref,
                   o_ref, s_scr):
    cos, sin = cos_ref[...], sin_ref[...]
    q = _rope(q_ref[...], cos, sin)
    k = _rope(k_ref[...], cos, sin) * (RET_DK ** -0.5)
    v = v_ref[...].astype(BF16)
    kb = k.astype(BF16)
    lane = lax.broadcasted_iota(jnp.int32, q.shape, 1)
    state = s_scr[...]
    o_inter = _dot((q * qd_ref[d]).astype(BF16), state.astype(BF16))
    for h in range(RET_HEADS):
        qh = jnp.where(lane // RET_DK == h, q, 0.0).astype(BF16)
        scores = _dot_nt(qh, kb) * dm_ref[d, h]
        o_ref[:, h * RET_DV:(h + 1) * RET_DV] = (
            _dot(scores.astype(BF16), v[:, h * RET_DV:(h + 1) * RET_DV])
            + o_inter[:, h * RET_DV:(h + 1) * RET_DV])
    kv = _dot((k * kd_ref[d]).T.astype(BF16), v)
    s_scr[...] = sd_ref[d] * state + blk_ref[...] * kv


def _ret_body(qf, kf, vf, cf, sf, qb, kb, vb, cb, sb, dm_ref, kd_ref, qd_ref, sd_ref, blk_ref,
              of_ref, ob_ref, sf_scr, sb_scr):
    @pl.when(pl.program_id(1) == 0)
    def _():
        sf_scr[...] = jnp.zeros_like(sf_scr)
        sb_scr[...] = jnp.zeros_like(sb_scr)

    _ret_direction(0, qf, kf, vf, cf, sf, dm_ref, kd_ref, qd_ref, sd_ref, blk_ref, of_ref, sf_scr)
    _ret_direction(1, qb, kb, vb, cb, sb, dm_ref, kd_ref, qd_ref, sd_ref, blk_ref, ob_ref, sb_scr)


def _retention(z, cos, sin, tables, n_ctx, n_all):
    dmat, kdec, qdec, sdec, blk = tables
    zv = z.reshape(n_all, BATCH * EVEN_IN)
    ctx_chunks, all_chunks = n_ctx // RET_CHUNK, n_all // RET_CHUNK
    per_b = EVEN_IN // RET_QK
    per_bv = EVEN_IN // HALF_W
    q0, k0, v0 = HALF_W // RET_QK, HALF_W // RET_QK + 1, (HALF_W + 2 * RET_QK) // HALF_W

    def seq_specs(d):
        ch = lambda b, i: _scan_chunk(d, i, ctx_chunks, all_chunks)
        return [pl.BlockSpec((RET_CHUNK, RET_QK), lambda b, i: (ch(b, i), b * per_b + q0)),
                pl.BlockSpec((RET_CHUNK, RET_QK), lambda b, i: (ch(b, i), b * per_b + k0)),
                pl.BlockSpec((RET_CHUNK, HALF_W), lambda b, i: (ch(b, i), b * per_bv + v0)),
                pl.BlockSpec((RET_CHUNK, RET_QK), lambda b, i: (ch(b, i), 0)),
                pl.BlockSpec((RET_CHUNK, RET_QK), lambda b, i: (ch(b, i), 0))]

    def out_spec(d):
        return pl.BlockSpec((RET_CHUNK, HALF_W), lambda b, i: (_scan_chunk(d, i, ctx_chunks, all_chunks), b))

    const = lambda nd: (lambda b, i: (0,) * nd)
    o_shape = jax.ShapeDtypeStruct((n_all, BATCH * HALF_W), F32)
    of, ob = pl.pallas_call(
        _ret_body,
        grid=(BATCH, all_chunks),
        in_specs=seq_specs(0) + seq_specs(1) + [
            pl.BlockSpec(dmat.shape, const(4)), pl.BlockSpec(kdec.shape, const(3)),
            pl.BlockSpec(qdec.shape, const(3)), pl.BlockSpec(sdec.shape, const(3)),
            pl.BlockSpec(blk.shape, const(2))],
        out_specs=[out_spec(0), out_spec(1)],
        out_shape=[o_shape, o_shape],
        scratch_shapes=[pltpu.VMEM((RET_QK, HALF_W), F32), pltpu.VMEM((RET_QK, HALF_W), F32)],
        compiler_params=_params("arbitrary", "arbitrary"),
        name="retention",
    )(zv, zv, zv, cos, sin, zv, zv, zv, cos, sin, dmat, kdec, qdec, sdec, blk)
    return of.reshape(n_all * BATCH, HALF_W), ob.reshape(n_all * BATCH, HALF_W)


def _lru_body(ctx_chunks, all_chunks, prev_ref, x_ref, next_ref, cw_ref, cb_ref, wg_ref, bg_ref, sp_ref,
              h_ref, a_scr, b_scr, h_scr):
    d, i = pl.program_id(0), pl.program_id(1)
    rows = x_ref.shape[0]
    steps = rows // BATCH
    chunk = _scan_chunk(d, i, ctx_chunks, all_chunks)

    @pl.when(i == 0)
    def _():
        h_scr[...] = jnp.zeros_like(h_scr)

    seg_first = jnp.logical_or(chunk == 0, chunk == ctx_chunks)
    seg_last = jnp.logical_or(chunk == ctx_chunks - 1, chunk == all_chunks - 1)
    prev = jnp.where(seg_first, 0.0, prev_ref[...])
    nxt = jnp.where(seg_last, 0.0, next_ref[...])
    ext = jnp.concatenate([prev, x_ref[...], nxt], 0)
    xs = cb_ref[...] + sum(cw_ref[k:k + 1, :] * ext[k * BATCH:k * BATCH + rows] for k in range(CONV_K))
    gates = _dot(xs.astype(BF16), wg_ref[0]) + bg_ref[0]
    r = jax.nn.sigmoid(gates[:, :LRU_W])
    ig = jax.nn.sigmoid(gates[:, LRU_W:])
    log_a = -r * sp_ref[0]
    a = jnp.exp(log_a)
    a_scr[...] = a
    b_scr[...] = jnp.sqrt(1.0 - jnp.exp(2.0 * log_a)) * (ig * xs)

    def step(j, h):
        t = jnp.where(d == 0, j, steps - 1 - j)
        r0 = pl.multiple_of(t * BATCH, BATCH)
        h = a_scr[pl.ds(r0, BATCH), :] * h + b_scr[pl.ds(r0, BATCH), :]
        h_ref[0, pl.ds(r0, BATCH), :] = h
        return h

    h_scr[...] = lax.fori_loop(0, steps, step, h_scr[...], unroll=8)


def _lru(z, conv_w, conv_b, w_a, b_a, w_x, b_x, lam, n_ctx, n_all):
    rows = n_all * BATCH
    blk = LRU_STEPS * BATCH
    ctx_chunks, all_chunks = n_ctx // LRU_STEPS, n_all // LRU_STEPS
    chunk = lambda d, i: _scan_chunk(d, i, ctx_chunks, all_chunks)
    col = ODD_LX // LRU_W
    eye = jnp.eye(LRU_BLOCKS, dtype=F32)
    dense = lambda w: jnp.einsum('dkij,kl->dkilj', w, eye).reshape(2, LRU_W, LRU_W)
    wg = jnp.concatenate([dense(w_a), dense(w_x)], -1).astype(BF16)
    bg = jnp.concatenate([b_a, b_x], -1).reshape(2, 1, 2 * LRU_W)
    sp = (LRU_C * jax.nn.softplus(-lam)).reshape(2, 1, LRU_W)
    halo_prev, halo_next = 2 * BATCH, BATCH
    return pl.pallas_call(
        functools.partial(_lru_body, ctx_chunks, all_chunks),
        grid=(2, all_chunks),
        in_specs=[pl.BlockSpec((halo_prev, LRU_W),
                               lambda d, i: (jnp.maximum(chunk(d, i) * (blk // halo_prev) - 1, 0), col)),
                  pl.BlockSpec((blk, LRU_W), lambda d, i: (chunk(d, i), col)),
                  pl.BlockSpec((halo_next, LRU_W),
                               lambda d, i: (jnp.minimum((chunk(d, i) + 1) * (blk // halo_next),
                                                         rows // halo_next - 1), col)),
                  pl.BlockSpec((CONV_K, LRU_W), lambda d, i: (0, 0)),
                  pl.BlockSpec((1, LRU_W), lambda d, i: (0, 0)),
                  pl.BlockSpec((1, LRU_W, 2 * LRU_W), lambda d, i: (d, 0, 0)),
                  pl.BlockSpec((1, 1, 2 * LRU_W), lambda d, i: (d, 0, 0)),
                  pl.BlockSpec((1, 1, LRU_W), lambda d, i: (d, 0, 0))],
        out_specs=pl.BlockSpec((1, blk, LRU_W), lambda d, i: (d, chunk(d, i), 0)),
        out_shape=jax.ShapeDtypeStruct((2, rows, LRU_W), F32),
        scratch_shapes=[pltpu.VMEM((blk, LRU_W), F32), pltpu.VMEM((blk, LRU_W), F32),
                        pltpu.VMEM((BATCH, LRU_W), F32)],
        compiler_params=_params("arbitrary", "arbitrary"),
        name="rglru",
    )(z, z, z, conv_w, conv_b.reshape(1, LRU_W), wg, bg, sp)


def _residual_norm_route(upd, x_ref, g2_ref, nw_ref, sh_ref, sc_ref, wrh_ref, wrl_ref, rb_ref,
                         xo_ref, hf_ref, lg_ref):
    tm = x_ref.shape[0]
    x3 = x_ref[...].reshape(tm // BATCH, BATCH, D_MODEL) + g2_ref[0] * upd.reshape(tm // BATCH, BATCH, D_MODEL)
    xo_ref[...] = x3.reshape(tm, D_MODEL)
    hf = _norm_modulate(x3, nw_ref[...], sh_ref[0], sc_ref[0]).reshape(tm, D_MODEL)
    for s in range(ROW_CHUNKS):
        hf_ref[:, s, :] = hf[:, s * LANES:(s + 1) * LANES]
    hi = hf.astype(BF16)
    lo = (hf - hi.astype(F32)).astype(BF16)
    lg_ref[...] = _dot(hi, wrh_ref[...]) + _dot(hi, wrl_ref[...]) + _dot(lo, wrh_ref[...]) + rb_ref[...]


def _gelu_tanh(x):
    return 0.5 * x * (1.0 + jnp.tanh(0.7978845608028654 * (x + 0.044715 * x * x * x)))


def _out_body_even(yf_ref, yb_ref, u_ref, d_ref, gw_ref, gb_ref, of_ref, ob_ref, gate_ref, wo_ref, *rest):
    ys = yf_ref[0] + yb_ref[0] + d_ref[...] * u_ref[...]
    y = _gelu_tanh(ys)
    s5o = y * jax.nn.sigmoid(_dot(y.astype(BF16), gw_ref[...]) + gb_ref[...])
    upd = _dot(s5o.astype(BF16), wo_ref[0:HALF_W, :])
    o = of_ref[...] + ob_ref[...]
    gate = gate_ref[...]
    for h in range(RET_HEADS):
        oh = o[:, h * RET_DV:(h + 1) * RET_DV]
        mu = jnp.mean(oh, -1, keepdims=True)
        var = jnp.mean(jnp.square(oh - mu), -1, keepdims=True)
        rh = _silu(gate[:, h * RET_DV:(h + 1) * RET_DV]) * ((oh - mu) * lax.rsqrt(var + NORM_EPS))
        upd = upd + _dot(rh.astype(BF16), wo_ref[HALF_W + h * RET_DV:HALF_W + (h + 1) * RET_DV, :])
    _residual_norm_route(upd, *rest)


def _out_body_odd(gf_ref, gb_ref, zg_ref, gnw_ref, hf_ref, hb_ref, lg_ref, wo_ref, *rest):
    lru = (hf_ref[0] + hb_ref[0]) * _gelu_tanh(lg_ref[...])
    upd = _dot(lru.astype(BF16), wo_ref[HALF_W:, :])
    o = gf_ref[...] + gb_ref[...]
    zg = zg_ref[...]
    for h in range(GDN_HEADS):
        sl = slice(h * GDN_DV, (h + 1) * GDN_DV)
        oh = o[:, sl]
        oh = oh * lax.rsqrt(jnp.mean(oh * oh, -1, keepdims=True) + NORM_EPS) * gnw_ref[...] * _silu(zg[:, sl])
        upd = upd + _dot(oh.astype(BF16), wo_ref[h * GDN_DV:(h + 1) * GDN_DV, :])
    _residual_norm_route(upd, *rest)


def _out_proj(body, mix_args, mix_specs, xs, w_out, gate2, norm_w, shift2, scale2, wr_hi, wr_lo, rb,
              row0_tiles, n_tiles, ctx_tiles):
    kind = lambda r: (jnp.where(r + row0_tiles < ctx_tiles, 0, 1), 0, 0)
    const2 = lambda r: (0, 0)
    rows = n_tiles * ROW_TILE
    return pl.pallas_call(
        body,
        grid=(n_tiles,),
        in_specs=mix_specs + [
            pl.BlockSpec((D_MODEL, D_MODEL), const2),
            pl.BlockSpec((ROW_TILE, D_MODEL), lambda r: (r + row0_tiles, 0)),
            pl.BlockSpec((1, BATCH, D_MODEL), kind),
            pl.BlockSpec((1, D_MODEL), const2),
            pl.BlockSpec((1, BATCH, D_MODEL), kind),
            pl.BlockSpec((1, BATCH, D_MODEL), kind),
            pl.BlockSpec((D_MODEL, ROUTER_PAD), const2),
            pl.BlockSpec((D_MODEL, ROUTER_PAD), const2),
            pl.BlockSpec((1, ROUTER_PAD), const2)],
        out_specs=[pl.BlockSpec((ROW_TILE, D_MODEL), lambda r: (r, 0)),
                   pl.BlockSpec((ROW_TILE, ROW_CHUNKS, LANES), lambda r: (r, 0, 0)),
                   pl.BlockSpec((ROW_TILE, ROUTER_PAD), lambda r: (r, 0))],
        out_shape=[jax.ShapeDtypeStruct((rows, D_MODEL), F32),
                   jax.ShapeDtypeStruct((rows, ROW_CHUNKS, LANES), F32),
                   jax.ShapeDtypeStruct((rows, ROUTER_PAD), F32)],
        compiler_params=_params("arbitrary"),
        name="out_proj",
    )(*mix_args, w_out, xs, gate2, norm_w.reshape(1, D_MODEL), shift2, scale2, wr_hi, wr_lo, rb)


def _route(logits):
    n_tok = logits.shape[0]
    g_logits = logits[:, :N_GROUPS]
    g_idx = jnp.argmax(g_logits, -1)
    g_w = jnp.take_along_axis(jax.nn.softmax(g_logits, -1), g_idx[:, None], 1)
    e_logits = logits[:, N_GROUPS:N_GROUPS + N_EXPERTS].reshape(n_tok, N_GROUPS, EXPERTS_PER_GROUP)
    e_in = jnp.take_along_axis(e_logits, g_idx[:, None, None], 1)[:, 0]
    top_v, top_i = lax.top_k(e_in, TOP_K)
    top_w = jax.nn.softmax(top_v, -1) * g_w
    return (g_idx[:, None] * EXPERTS_PER_GROUP + top_i).astype(jnp.int32), top_w


def _dispatch_plan(expert_ids, top_w, n_tiles):
    n_tok = expert_ids.shape[0]
    n_asg = n_tok * TOP_K
    flat_e = expert_ids.T.reshape(n_asg)
    flat_w = top_w.T.reshape(n_asg)
    order = jnp.argsort(flat_e, stable=True).astype(jnp.int32)
    inv = jnp.argsort(order).astype(jnp.int32)
    onehot = flat_e[:, None] == jnp.arange(N_EXPERTS, dtype=jnp.int32)[None]
    counts = jnp.sum(onehot, 0, dtype=jnp.int32)
    padded = ((counts + MOE_TILE - 1) // MOE_TILE) * MOE_TILE
    pend = jnp.cumsum(padded)
    pstart = pend - padded
    start = jnp.cumsum(counts) - counts
    shift = pstart - start
    pos = inv + jnp.sum(jnp.where(onehot, shift[None], 0), 1)
    tile_start = jnp.arange(n_tiles, dtype=jnp.int32) * MOE_TILE
    tile_e = jnp.minimum(jnp.sum(tile_start[:, None] >= pend[None], 1), N_EXPERTS - 1).astype(jnp.int32)
    tile_valid = (tile_start < pend[-1]).astype(jnp.int32)
    row = jnp.arange(n_tiles * MOE_TILE, dtype=jnp.int32)
    row_e = jnp.repeat(tile_e, MOE_TILE)
    rank = row - shift[row_e]
    row_ok = jnp.logical_and(rank >= start[row_e], rank < start[row_e] + counts[row_e])
    src_asg = order[jnp.clip(rank, 0, n_asg - 1)]
    src_tok = jnp.where(row_ok, src_asg % n_tok, 0)
    row_w = jnp.where(row_ok, flat_w[src_asg], 0.0)
    return src_tok, row_w, pos.reshape(TOP_K, n_tok), tile_e, tile_valid


def _gather_rows(idx_ref, src_hbm, dst_vmem, sem, n_rows):
    def issue(r, c):
        pltpu.make_async_copy(src_hbm.at[pl.ds(idx_ref[0, 0, r], 1)], dst_vmem.at[pl.ds(r, 1)], sem).start()
        return c

    lax.fori_loop(0, n_rows, issue, 0, unroll=16)


def _wait_rows(src_hbm, dst_vmem, sem, n_rows):
    def wait(r, c):
        pltpu.make_async_copy(src_hbm.at[pl.ds(0, 1)], dst_vmem.at[pl.ds(r, 1)], sem).wait()
        return c

    lax.fori_loop(0, n_rows, wait, 0, unroll=16)


def _rows_2d(ref):
    return jnp.concatenate([ref[:, s, :] for s in range(ROW_CHUNKS)], 1)


def _expert_body(te_ref, tv_ref, idx_ref, idx_next_ref, hf_hbm, cw_ref, wg_ref, wu_ref, wd_ref, y_ref,
                 xbuf, sem):
    i = pl.program_id(0)
    n = pl.num_programs(0)
    slot = i % 2
    nxt = jnp.minimum(i + 1, n - 1)

    @pl.when(jnp.logical_and(i == 0, tv_ref[0] == 1))
    def _():
        _gather_rows(idx_ref, hf_hbm, xbuf.at[0], sem.at[0], MOE_TILE)

    @pl.when(jnp.logical_and(i + 1 < n, tv_ref[nxt] == 1))
    def _():
        _gather_rows(idx_next_ref, hf_hbm, xbuf.at[1 - slot], sem.at[1 - slot], MOE_TILE)

    @pl.when(tv_ref[i] == 1)
    def _():
        _wait_rows(hf_hbm, xbuf.at[slot], sem.at[slot], MOE_TILE)
        x = _rows_2d(xbuf.at[slot]).astype(BF16)
        g = _dot(x, wg_ref[0])
        u = _dot(x, wu_ref[0])
        act = _silu(g) * u * cw_ref[:, 0:1]
        y = _dot(act.astype(BF16), wd_ref[0])
        for s in range(ROW_CHUNKS):
            y_ref[:, s, :] = y[:, s * LANES:(s + 1) * LANES]

    @pl.when(tv_ref[i] == 0)
    def _():
        y_ref[...] = jnp.zeros_like(y_ref)


def _experts(tile_e, tile_valid, src_tok, hf, cw_rows, wg, wu, wd):
    n_tiles = tile_e.shape[0]
    tile3 = pl.BlockSpec((MOE_TILE, ROW_CHUNKS, LANES), lambda i, te, tv: (i, 0, 0))
    idx3 = src_tok.reshape(n_tiles, 1, MOE_TILE)
    grid_spec = pltpu.PrefetchScalarGridSpec(
        num_scalar_prefetch=2,
        grid=(n_tiles,),
        in_specs=[pl.BlockSpec((1, 1, MOE_TILE), lambda i, te, tv: (i, 0, 0), memory_space=pltpu.SMEM),
                  pl.BlockSpec((1, 1, MOE_TILE), lambda i, te, tv: (jnp.minimum(i + 1, n_tiles - 1), 0, 0),
                               memory_space=pltpu.SMEM),
                  pl.BlockSpec(memory_space=pl.ANY),
                  pl.BlockSpec((MOE_TILE, LANES), lambda i, te, tv: (i, 0)),
                  pl.BlockSpec((1, D_MODEL, EXPERT_FF), lambda i, te, tv: (te[i], 0, 0)),
                  pl.BlockSpec((1, D_MODEL, EXPERT_FF), lambda i, te, tv: (te[i], 0, 0)),
                  pl.BlockSpec((1, EXPERT_FF, D_MODEL), lambda i, te, tv: (te[i], 0, 0))],
        out_specs=tile3,
        scratch_shapes=[pltpu.VMEM((2, MOE_TILE, ROW_CHUNKS, LANES), F32), pltpu.SemaphoreType.DMA((2,))])
    return pl.pallas_call(
        _expert_body,
        grid_spec=grid_spec,
        out_shape=jax.ShapeDtypeStruct((n_tiles * MOE_TILE, ROW_CHUNKS, LANES), F32),
        compiler_params=_params("arbitrary"),
        name="moe_experts",
    )(tile_e, tile_valid, idx3, idx3, hf, cw_rows, wg, wu, wd)


def _combine_body(p0_ref, p1_ref, p0n_ref, p1n_ref, y_hbm, x_ref, g5_ref, o_ref, buf, sem):
    r = pl.program_id(0)
    n = pl.num_programs(0)
    tm = x_ref.shape[0]
    slot = r % 2

    def fetch(p0, p1, s):
        _gather_rows(p0, y_hbm, buf.at[s, 0], sem.at[s, 0], tm)
        _gather_rows(p1, y_hbm, buf.at[s, 1], sem.at[s, 1], tm)

    @pl.when(r == 0)
    def _():
        fetch(p0_ref, p1_ref, 0)

    @pl.when(r + 1 < n)
    def _():
        fetch(p0n_ref, p1n_ref, 1 - slot)

    _wait_rows(y_hbm, buf.at[slot, 0], sem.at[slot, 0], tm)
    _wait_rows(y_hbm, buf.at[slot, 1], sem.at[slot, 1], tm)
    f3 = (_rows_2d(buf.at[slot, 0]) + _rows_2d(buf.at[slot, 1])).reshape(tm // BATCH, BATCH, D_MODEL)
    o_ref[...] = (x_ref[...].reshape(tm // BATCH, BATCH, D_MODEL) + g5_ref[0] * f3).reshape(tm, D_MODEL)


def _combine(pos, y_sorted, xs, gate2, row0_tiles, ctx_tiles):
    n_tok = pos.shape[1]
    n_tiles = n_tok // MOE_TILE
    kind = lambda r: (jnp.where(r + row0_tiles < ctx_tiles, 0, 1), 0, 0)
    idx_spec = lambda k, ahead: pl.BlockSpec(
        (1, 1, MOE_TILE), lambda r: (k * n_tiles + jnp.minimum(r + ahead, n_tiles - 1), 0, 0),
        memory_space=pltpu.SMEM)
    p2 = pos.reshape(TOP_K * n_tiles, 1, MOE_TILE)
    return pl.pallas_call(
        _combine_body,
        grid=(n_tiles,),
        in_specs=[idx_spec(0, 0), idx_spec(1, 0), idx_spec(0, 1), idx_spec(1, 1),
                  pl.BlockSpec(memory_space=pl.ANY),
                  pl.BlockSpec((MOE_TILE, D_MODEL), lambda r: (r, 0)),
                  pl.BlockSpec((1, BATCH, D_MODEL), kind)],
        out_specs=pl.BlockSpec((MOE_TILE, D_MODEL), lambda r: (r, 0)),
        out_shape=jax.ShapeDtypeStruct((n_tok, D_MODEL), F32),
        scratch_shapes=[pltpu.VMEM((2, TOP_K, MOE_TILE, ROW_CHUNKS, LANES), F32),
                        pltpu.SemaphoreType.DMA((2, TOP_K))],
        compiler_params=_params("arbitrary"),
        name="moe_combine",
    )(p2, p2, p2, p2, y_sorted, xs, gate2)


def _moe(hf, logits, xs, gate2, wg, wu, wd, row0_tiles, ctx_tiles):
    n_tok = hf.shape[0]
    n_tiles = n_tok * TOP_K // MOE_TILE + N_EXPERTS
    expert_ids, top_w = _route(logits)
    src_tok, row_w, pos, tile_e, tile_valid = _dispatch_plan(expert_ids, top_w, n_tiles)
    cw_rows = jnp.broadcast_to(row_w[:, None], (n_tiles * MOE_TILE, LANES))
    y_sorted = _experts(tile_e, tile_valid, src_tok, hf, cw_rows, wg, wu, wd)
    return _combine(pos, y_sorted, xs, gate2, row0_tiles, ctx_tiles)


FINAL_STEPS = 256


def _final_norm_body(x_ref, w_ref, o_ref):
    x = x_ref[...]
    o_ref[0] = x * lax.rsqrt(jnp.mean(x * x, -1, keepdims=True) + NORM_EPS) * w_ref[...]


def _final_norm(x_lat, w, n_lat):
    return pl.pallas_call(
        _final_norm_body,
        grid=(BATCH, n_lat // FINAL_STEPS),
        in_specs=[pl.BlockSpec((FINAL_STEPS, D_MODEL), lambda b, j: (j, b)),
                  pl.BlockSpec((1, D_MODEL), lambda b, j: (0, 0))],
        out_specs=pl.BlockSpec((1, FINAL_STEPS, D_MODEL), lambda b, j: (b, j, 0)),
        out_shape=jax.ShapeDtypeStruct((BATCH, n_lat, D_MODEL), F32),
        compiler_params=_params("arbitrary", "arbitrary"),
        name="final_norm",
    )(x_lat.reshape(n_lat, BATCH * D_MODEL), w.reshape(1, D_MODEL))


def _kinds(mod_l, k):
    return jnp.stack([mod_l[BATCH:, k], mod_l[:BATCH, k]])


def kernel(x, c, ctx, c_ctx, mod_w, mod_b, norm_mix, norm_ffn, router_group_w, router_group_b,
           router_expert_w, router_expert_b, expert_w_gate, expert_w_up, expert_w_down,
           even_w_in, even_w_out, s5_lam_re, s5_lam_im, s5_log_step, s5_b_re, s5_b_im, s5_c_re,
           s5_c_im, s5_d, s5_glu_w, s5_glu_b, ret_decay, odd_w_in, odd_w_out, gdn_conv_w, gdn_a_log,
           gdn_dt_bias, gdn_norm_w, lru_conv_w, lru_conv_b, lru_w_a, lru_b_a, lru_w_x, lru_b_x,
           lru_lam, final_norm):
    bsz, n_lat, _ = x.shape
    n_ctx = ctx.shape[1]
    n_all = n_ctx + n_lat
    assert bsz == BATCH and n_ctx % RET_CHUNK == 0 and n_lat % FINAL_STEPS == 0
    assert (n_ctx * BATCH) % ROW_TILE == 0 and (n_lat * BATCH) % ROW_TILE == 0
    ctx_tiles = n_ctx * BATCH // ROW_TILE
    all_tiles = n_all * BATCH // ROW_TILE
    ctx_moe_tiles = n_ctx * BATCH // MOE_TILE

    xs = _to_time_major(jnp.concatenate([ctx, x], 1))
    cond16 = jnp.concatenate([c, jnp.broadcast_to(c_ctx[None], (BATCH, D_MODEL))], 0)
    mod = _adaln_all(cond16, mod_w, mod_b).reshape(DEPTH, ADALN_ROWS, N_MOD, D_MODEL)

    for layer in range(DEPTH):
        last = layer == DEPTH - 1
        i = layer // 2
        m = [_kinds(mod[layer], k) for k in range(N_MOD)]
        wr = jnp.zeros((D_MODEL, ROUTER_PAD), F32)
        wr = wr.at[:, :N_GROUPS].set(router_group_w[layer]).at[:, N_GROUPS:N_GROUPS + N_EXPERTS].set(
            router_expert_w[layer])
        wr_hi = wr.astype(BF16)
        wr_lo = (wr - wr_hi.astype(F32)).astype(BF16)
        rb = jnp.zeros((1, ROUTER_PAD), F32)
        rb = rb.at[0, :N_GROUPS].set(router_group_b[layer]).at[0, N_GROUPS:N_GROUPS + N_EXPERTS].set(
            router_expert_b[layer])
        row0 = ctx_tiles if last else 0
        n_tiles = all_tiles - row0
        row_spec = lambda w, cb=0: pl.BlockSpec((ROW_TILE, w), lambda r: (r + row0, cb))
        dir_spec = lambda d: pl.BlockSpec((1, ROW_TILE, HALF_W), lambda r: (d, r + row0, 0))
        const2 = lambda r: (0, 0)

        if layer % 2 == 0:
            z = _in_proj(xs, norm_mix[layer], m[0], m[1], even_w_in[i].astype(BF16), ctx_tiles)
            disc = [_s5_discretize(s5_lam_re[i, d], s5_lam_im[i, d], s5_log_step[i, d], s5_b_re[i, d],
                                   s5_b_im[i, d], s5_c_re[i, d], s5_c_im[i, d]) for d in range(2)]
            bb, cc, a = (jnp.stack(t) for t in zip(*disc))
            ys5 = _s5_scan(z, bb, cc, a, n_ctx, n_all)
            cos, sin = _rope_tables(n_ctx, n_lat)
            r_f, r_b = _retention(z, cos, sin, _ret_tables(ret_decay[i]), n_ctx, n_all)
            mix_args = [ys5, ys5, z, s5_d[i].reshape(1, HALF_W), s5_glu_w[i].astype(BF16),
                        s5_glu_b[i].reshape(1, HALF_W), r_f, r_b, z]
            mix_specs = [dir_spec(0), dir_spec(1), row_spec(HALF_W),
                         pl.BlockSpec((1, HALF_W), const2),
                         pl.BlockSpec((HALF_W, HALF_W), const2),
                         pl.BlockSpec((1, HALF_W), const2),
                         row_spec(HALF_W), row_spec(HALF_W), row_spec(HALF_W, EVEN_SPLITS[3] // HALF_W)]
            body, w_out = _out_body_even, even_w_out[i]
        else:
            w_in = odd_w_in[i]
            w_pad = jnp.concatenate([
                w_in[:, :ODD_SPLITS[1]],
                w_in[:, ODD_SPLITS[3]:],
                jnp.pad(w_in[:, ODD_SPLITS[1]:ODD_SPLITS[3]], ((0, 0), (0, ODD_PAD - 4 * GDN_HEADS)))], 1)
            z = _in_proj(xs, norm_mix[layer], m[0], m[1], w_pad.astype(BF16), ctx_tiles)
            qkv_act, bg = _gdn_prep(z, gdn_conv_w[i], gdn_a_log[i], gdn_dt_bias[i], n_ctx, n_all)
            g_f, g_b = _gdn(qkv_act, bg, n_ctx, n_all)
            hl = _lru(z, lru_conv_w[i], lru_conv_b[i], lru_w_a[i], lru_b_a[i], lru_w_x[i], lru_b_x[i],
                      lru_lam[i], n_ctx, n_all)
            mix_args = [g_f, g_b, z, gdn_norm_w[i].reshape(1, GDN_DV), hl, hl, z]
            mix_specs = [row_spec(HALF_W), row_spec(HALF_W), row_spec(HALF_W, ODD_SPLITS[0] // HALF_W),
                         pl.BlockSpec((1, GDN_DV), const2), dir_spec(0), dir_spec(1),
                         row_spec(LRU_W, (ODD_LX + LRU_W) // LRU_W)]
            body, w_out = _out_body_odd, odd_w_out[i]

        xs_mid, hf, logits = _out_proj(body, mix_args, mix_specs, xs, w_out.astype(BF16), m[2],
                                       norm_ffn[layer], m[3], m[4], wr_hi, wr_lo, rb, row0, n_tiles, ctx_tiles)
        moe_row0 = ctx_moe_tiles if last else 0
        xs = _moe(hf, logits, xs_mid, m[5], expert_w_gate[layer].astype(BF16),
                  expert_w_up[layer].astype(BF16), expert_w_down[layer].astype(BF16), moe_row0, ctx_moe_tiles)

    return _final_norm(xs, final_norm, n_lat)
```

```python
import functools

import jax
import jax.numpy as jnp
from jax import lax
from jax.experimental import pallas as pl
from jax.experimental.pallas import tpu as pltpu

D_MODEL = 1024
DEPTH = 2
GRID_W = 64
HALF_W = D_MODEL // 2
S5_GROUP = 16
S5_GROUPS = HALF_W // S5_GROUP
S5_STATE = 64
RET_HEADS = 4
RET_DV = HALF_W // RET_HEADS
RET_DK = RET_DV // 2
RET_QK = RET_HEADS * RET_DK
RET_CHUNK = 128
ROPE_BASE = 10000.0
GDN_HEADS = 4
GDN_DK = HALF_W // GDN_HEADS
GDN_DV = GDN_DK
GDN_CHUNK = 64
LRU_W = HALF_W
LRU_BLOCKS = 8
LRU_BLOCK = LRU_W // LRU_BLOCKS
LRU_C = 8.0
CONV_K = 4
CONV_PAD_LEFT = 2
N_GROUPS = 4
EXPERTS_PER_GROUP = 8
N_EXPERTS = N_GROUPS * EXPERTS_PER_GROUP
TOP_K = 2
EXPERT_FF = D_MODEL // 2
N_MOD = 6
NORM_EPS = 1e-6
EVEN_SPLITS = [HALF_W, HALF_W + RET_QK, HALF_W + 2 * RET_QK, 2 * HALF_W + 2 * RET_QK]
EVEN_IN = 3 * HALF_W + 2 * RET_QK
ODD_SPLITS = [3 * HALF_W, 4 * HALF_W, 4 * HALF_W + 2 * GDN_HEADS, 4 * HALF_W + 4 * GDN_HEADS,
              4 * HALF_W + 4 * GDN_HEADS + LRU_W]

F32 = jnp.float32
BF16 = jnp.bfloat16

SUBLANES = 8
LANES = 128
BATCH = SUBLANES
ROW_CHUNKS = D_MODEL // LANES
VMEM_LIMIT = 48 * 1024 * 1024

ROW_TILE = 512
S5_STEPS = 64
S5_SLABS = 4
S5_SLAB_CH = HALF_W // S5_SLABS
S5_SLAB_STATE = (S5_GROUPS // S5_SLABS) * S5_STATE
LRU_STEPS = 64
MOE_TILE = 256
DMA_PRIORITIES = 2
ODD_PAD = 512
ODD_LX = 4 * HALF_W
ODD_BD = ODD_LX + 2 * LRU_W
ODD_IN_PADDED = ODD_BD + ODD_PAD
GDN_QKV = 3 * HALF_W
GDN_ROWS = GDN_CHUNK * BATCH
GDN_BG_G = 2 * GDN_HEADS
GDN_NEUMANN_LEVELS = 5
ROUTER_PAD = LANES


def _dot(a, b):
    return jnp.dot(a, b, preferred_element_type=F32)


def _dot_nt(a, b):
    return lax.dot_general(a, b, (((1,), (1,)), ((), ())), preferred_element_type=F32)


def _params(*sem):
    return pltpu.CompilerParams(dimension_semantics=sem, vmem_limit_bytes=VMEM_LIMIT)


def _scan_chunk(d, i, ctx_chunks, all_chunks):
    back = jnp.where(i < ctx_chunks, ctx_chunks - 1 - i, all_chunks + ctx_chunks - 1 - i)
    return jnp.where(d == 0, i, back)


def _to_time_major(t):
    return t.transpose(1, 0, 2).reshape(t.shape[0] * t.shape[1], t.shape[2])


def _silu(x):
    return x * jax.nn.sigmoid(x)


def _softplus(x):
    return jnp.maximum(x, 0.0) + jnp.log(1.0 + jnp.exp(-jnp.abs(x)))


def _segment_edges(chunk, ctx_chunks, all_chunks):
    first = jnp.logical_or(chunk == 0, chunk == ctx_chunks)
    last = jnp.logical_or(chunk == ctx_chunks - 1, chunk == all_chunks - 1)
    return first, last


def _gdn_prep_body(ctx_chunks, all_chunks, prev_ref, x_ref, next_ref, bd_ref, cw_ref, al_ref, dtb_ref,
                   qkv_ref, bg_ref, ext_scr):
    rows = x_ref.shape[1]
    steps = rows // BATCH
    halo = CONV_PAD_LEFT * BATCH
    seg_first, seg_last = _segment_edges(pl.program_id(0), ctx_chunks, all_chunks)
    ext_scr[:, 0:halo, :] = jnp.where(seg_first, 0.0, prev_ref[...])
    ext_scr[:, halo:halo + rows, :] = x_ref[...]
    ext_scr[:, halo + rows:, :] = jnp.where(seg_last, 0.0, next_ref[...])
    for j in range(GDN_QKV // LANES):
        sl = slice(j * LANES, (j + 1) * LANES)
        c = sum(cw_ref[k:k + 1, sl] * ext_scr[j, k * BATCH:k * BATCH + rows, :] for k in range(CONV_K))
        a = _silu(c)
        if j < 2 * GDN_HEADS:
            a = a * lax.rsqrt(jnp.sum(a * a, -1, keepdims=True) + NORM_EPS)
        if j < GDN_HEADS:
            a = a * (GDN_DK ** -0.5)
        qkv_ref[j] = a

    bd = bd_ref[0]
    lane = lax.broadcasted_iota(jnp.int32, bd.shape, 1)
    val = jnp.where(lane < GDN_BG_G, jax.nn.sigmoid(bd), al_ref[...] * _softplus(bd + dtb_ref[...]))
    prefix = val
    for k in range((steps - 1).bit_length()):
        sh = BATCH << k
        prefix = prefix + jnp.concatenate([jnp.zeros((sh, LANES), F32), prefix[:rows - sh]], 0)
    p3 = prefix.reshape(steps, BATCH, LANES)
    suffix = (p3[steps - 1][None] - p3 + val.reshape(steps, BATCH, LANES)).reshape(rows, LANES)
    fwd_g = jnp.logical_and(lane >= GDN_BG_G, lane < GDN_BG_G + GDN_HEADS)
    bwd_g = jnp.logical_and(lane >= GDN_BG_G + GDN_HEADS, lane < GDN_BG_G + 2 * GDN_HEADS)
    bg_ref[...] = jnp.where(fwd_g, prefix, jnp.where(bwd_g, suffix, val))


def _gdn_prep(z, conv_w, a_log, dt_bias, n_ctx, n_all):
    rows = n_all * BATCH
    ctx_chunks, all_chunks = n_ctx // GDN_CHUNK, n_all // GDN_CHUNK
    halo_prev, halo_next = CONV_PAD_LEFT * BATCH, (CONV_K - 1 - CONV_PAD_LEFT) * BATCH
    slabs = GDN_QKV // LANES
    lanes16 = lambda t: jnp.zeros((1, LANES), F32).at[0, GDN_BG_G:GDN_BG_G + 2 * GDN_HEADS].set(t.reshape(-1))
    return pl.pallas_call(
        functools.partial(_gdn_prep_body, ctx_chunks, all_chunks),
        grid=(all_chunks,),
        in_specs=[pl.BlockSpec((slabs, halo_prev, LANES),
                               lambda i: (0, jnp.maximum(i * (GDN_ROWS // halo_prev) - 1, 0), 0)),
                  pl.BlockSpec((slabs, GDN_ROWS, LANES), lambda i: (0, i, 0)),
                  pl.BlockSpec((slabs, halo_next, LANES),
                               lambda i: (0, jnp.minimum((i + 1) * (GDN_ROWS // halo_next), rows // halo_next - 1), 0)),
                  pl.BlockSpec((1, GDN_ROWS, LANES), lambda i: (ODD_BD // LANES, i, 0)),
                  pl.BlockSpec((CONV_K, GDN_QKV), lambda i: (0, 0)),
                  pl.BlockSpec((1, LANES), lambda i: (0, 0)),
                  pl.BlockSpec((1, LANES), lambda i: (0, 0))],
        out_specs=[pl.BlockSpec((slabs, GDN_ROWS, LANES), lambda i: (0, i, 0)),
                   pl.BlockSpec((GDN_ROWS, LANES), lambda i: (i, 0))],
        out_shape=[jax.ShapeDtypeStruct((slabs, rows, LANES), F32), jax.ShapeDtypeStruct((rows, LANES), F32)],
        scratch_shapes=[pltpu.VMEM((slabs, GDN_ROWS + halo_prev + halo_next, LANES), F32)],
        compiler_params=_params("arbitrary"),
        name="gdn_prep",
    )(z, z, z, z, conv_w, lanes16(-jnp.exp(a_log)), lanes16(dt_bias))


def _dot_split(a, b):
    ah, bh = a.astype(BF16), b.astype(BF16)
    al, bl = (a - ah.astype(F32)).astype(BF16), (b - bh.astype(F32)).astype(BF16)
    return _dot(ah, bh) + _dot(ah, bl) + _dot(al, bh)


def _gdn_batch(b, xf_ref, bgf_ref, xb_ref, bgb_ref, of_ref, ob_ref, s_scr):
    rows = _batch_rows(b, GDN_CHUNK)
    ii = lax.broadcasted_iota(jnp.int32, (GDN_CHUNK, GDN_CHUNK), 0)
    jj = lax.broadcasted_iota(jnp.int32, (GDN_CHUNK, GDN_CHUNK), 1)
    eye = jnp.where(ii == jj, 1.0, 0.0)
    units = []
    for d, (x_ref, bg_ref, o_ref) in enumerate(((xf_ref, bgf_ref, of_ref), (xb_ref, bgb_ref, ob_ref))):
        bg = bg_ref[rows, :]
        bg_t = bg.T
        incl = (jj <= ii) if d == 0 else (jj >= ii)
        strict = (jj < ii) if d == 0 else (jj > ii)
        last = GDN_CHUNK - 1 if d == 0 else 0
        for h in range(GDN_HEADS):
            q, k, v = x_ref[h, rows, :], x_ref[GDN_HEADS + h, rows, :], x_ref[2 * GDN_HEADS + h, rows, :]
            cb = d * GDN_HEADS + h
            cg = GDN_BG_G + cb
            beta, g_col = bg[:, cb:cb + 1], bg[:, cg:cg + 1]
            g_row, g_last = bg_t[cg:cg + 1, :], bg_t[cg:cg + 1, last:last + 1]
            decay = jnp.where(incl, jnp.exp(jnp.where(incl, g_col - g_row, 0.0)), 0.0)
            kb16 = k.astype(BF16)
            eg = jnp.exp(g_col)
            units.append(dict(
                si=b * (2 * GDN_HEADS) + cb, h=h, o_ref=o_ref,
                qk=(_dot_nt(q.astype(BF16), kb16) * decay).astype(BF16),
                n=jnp.where(strict, -(_dot_nt(kb16, kb16) * decay * beta), 0.0),
                rhs=jnp.concatenate([v * beta, k * (beta * eg)], 1),
                qg=(q * eg).astype(BF16),
                kg_t=(k * jnp.exp(g_last - g_col)).T.astype(BF16),
                d_last=jnp.exp(g_last)))

    ts = [eye + u['n'] for u in units]
    ps = [u['n'] for u in units]
    for _ in range(GDN_NEUMANN_LEVELS):
        ps = [_dot(p.astype(BF16), p.astype(BF16)) for p in ps]
        ts = [t + _dot(t.astype(BF16), p.astype(BF16)) for t, p in zip(ts, ps)]
    sols = [_dot_split(t, u['rhs']) for t, u in zip(ts, units)]
    states = [s_scr[u['si']] for u in units]
    sbs = [s.astype(BF16) for s in states]
    v_news = [(sol[:, :GDN_DV] - _dot(sol[:, GDN_DV:].astype(BF16), sb)).astype(BF16)
              for sol, sb in zip(sols, sbs)]
    for u, s, sb, vn in zip(units, states, sbs, v_news):
        u['o_ref'][u['h'], rows, :] = _dot(u['qg'], sb) + _dot(u['qk'], vn)
        s_scr[u['si']] = s * u['d_last'] + _dot(u['kg_t'], vn)


def _gdn_body(xf_ref, bgf_ref, xb_ref, bgb_ref, of_ref, ob_ref, s_scr):
    @pl.when(pl.program_id(0) == 0)
    def _():
        s_scr[...] = jnp.zeros_like(s_scr)

    def per_batch(b, carry):
        _gdn_batch(b, xf_ref, bgf_ref, xb_ref, bgb_ref, of_ref, ob_ref, s_scr)
        return carry

    lax.fori_loop(0, BATCH, per_batch, 0)


def _gdn(qkv, bg, n_ctx, n_all):
    rows = n_all * BATCH
    ctx_chunks, all_chunks = n_ctx // GDN_CHUNK, n_all // GDN_CHUNK
    slabs = GDN_QKV // LANES

    def seq_specs(d):
        ch = lambda i: _scan_chunk(d, i, ctx_chunks, all_chunks)
        return [pl.BlockSpec((slabs, GDN_ROWS, LANES), lambda i: (0, ch(i), 0)),
                pl.BlockSpec((GDN_ROWS, LANES), lambda i: (ch(i), 0))]

    out_spec = lambda d: pl.BlockSpec((GDN_HEADS, GDN_ROWS, LANES),
                                      lambda i: (0, _scan_chunk(d, i, ctx_chunks, all_chunks), 0))
    o_shape = jax.ShapeDtypeStruct((GDN_HEADS, rows, LANES), F32)
    return pl.pallas_call(
        _gdn_body,
        grid=(all_chunks,),
        in_specs=seq_specs(0) + seq_specs(1),
        out_specs=[out_spec(0), out_spec(1)],
        out_shape=[o_shape, o_shape],
        scratch_shapes=[pltpu.VMEM((BATCH * 2 * GDN_HEADS, GDN_DK, GDN_DV), F32)],
        compiler_params=_params("arbitrary"),
        name="gdn_chunks",
    )(qkv, bg, qkv, bg)


ADALN_COLS = 512
ADALN_ROWS = 2 * BATCH


def _adaln_body(c_ref, w_ref, b_ref, o_ref):
    c = c_ref[...]
    s = (c * jax.nn.sigmoid(c)).astype(BF16)
    o_ref[0] = _dot(s, w_ref[0].astype(BF16)) + b_ref[0]


def _adaln_all(cond16, mod_w, mod_b):
    n_out = mod_w.shape[-1]
    return pl.pallas_call(
        _adaln_body,
        grid=(DEPTH, n_out // ADALN_COLS),
        in_specs=[pl.BlockSpec((ADALN_ROWS, D_MODEL), lambda l, j: (0, 0)),
                  pl.BlockSpec((1, D_MODEL, ADALN_COLS), lambda l, j: (l, 0, j)),
                  pl.BlockSpec((1, 1, ADALN_COLS), lambda l, j: (l, 0, j))],
        out_specs=pl.BlockSpec((1, ADALN_ROWS, ADALN_COLS), lambda l, j: (l, 0, j)),
        out_shape=jax.ShapeDtypeStruct((DEPTH, ADALN_ROWS, n_out), F32),
        compiler_params=_params("arbitrary", "arbitrary"),
        name="adaln",
    )(cond16, mod_w, mod_b.reshape(DEPTH, 1, n_out))


def _col_chunks(n, width=512):
    return [(c0, min(width, n - c0)) for c0 in range(0, n, width)]


def _norm_modulate(x3, nw, shift, scale):
    y = x3 * lax.rsqrt(jnp.mean(x3 * x3, -1, keepdims=True) + NORM_EPS) * nw
    return y * (1.0 + scale) + shift


def _in_proj_body(x_ref, nw_ref, sh_ref, sc_ref, w_ref, o_ref):
    tm = x_ref.shape[0]
    x3 = x_ref[...].reshape(tm // BATCH, BATCH, D_MODEL)
    h = _norm_modulate(x3, nw_ref[...], sh_ref[0], sc_ref[0]).reshape(tm, D_MODEL).astype(BF16)
    for c0, cw in _col_chunks(w_ref.shape[1]):
        res = _dot(h, w_ref[:, c0:c0 + cw])
        for j in range(cw // LANES):
            o_ref[c0 // LANES + j] = res[:, j * LANES:(j + 1) * LANES]


def _in_proj(xs, norm_w, shift2, scale2, w_bf16, ctx_tiles):
    rows = xs.shape[0]
    n = w_bf16.shape[1]
    kind = lambda r: (jnp.where(r < ctx_tiles, 0, 1), 0, 0)
    return pl.pallas_call(
        _in_proj_body,
        grid=(rows // ROW_TILE,),
        in_specs=[pl.BlockSpec((ROW_TILE, D_MODEL), lambda r: (r, 0)),
                  pl.BlockSpec((1, D_MODEL), lambda r: (0, 0)),
                  pl.BlockSpec((1, BATCH, D_MODEL), kind),
                  pl.BlockSpec((1, BATCH, D_MODEL), kind),
                  pl.BlockSpec((D_MODEL, n), lambda r: (0, 0))],
        out_specs=pl.BlockSpec((n // LANES, ROW_TILE, LANES), lambda r: (0, r, 0)),
        out_shape=jax.ShapeDtypeStruct((n // LANES, rows, LANES), F32),
        compiler_params=_params("arbitrary"),
        name="in_proj",
    )(xs, norm_w.reshape(1, D_MODEL), shift2, scale2, w_bf16)


def _s5_discretize(lam_re, lam_im, log_step, b_re, b_im, c_re, c_im):
    lr, li = lam_re, lam_im
    dt = jnp.exp(log_step)[:, None]
    mag = jnp.exp(lr * dt)
    ar, ai = mag * jnp.cos(li * dt), mag * jnp.sin(li * dt)
    pr, pi = ar - 1.0, ai
    den = lr * lr + li * li
    zr, zi = (pr * lr + pi * li) / den, (pi * lr - pr * li) / den
    bbr = zr[..., None] * b_re - zi[..., None] * b_im
    bbi = zr[..., None] * b_im + zi[..., None] * b_re
    gps = S5_GROUPS // S5_SLABS
    eye = jnp.eye(gps, dtype=F32)

    def in_slab(m):
        m = m.reshape(S5_SLABS, gps, S5_STATE, S5_GROUP)
        return jnp.einsum('sgnc,gh->sgchn', m, eye).reshape(S5_SLABS, S5_SLAB_CH, S5_SLAB_STATE)

    def out_slab(m):
        m = m.reshape(S5_SLABS, gps, S5_GROUP, S5_STATE)
        return jnp.einsum('sgcn,gh->sgnhc', m, eye).reshape(S5_SLABS, S5_SLAB_STATE, S5_SLAB_CH)

    bb = jnp.concatenate([in_slab(bbr), in_slab(bbi)], -1)
    cc = jnp.concatenate([out_slab(c_re), out_slab(-c_im)], 1)
    a = jnp.concatenate([ar.reshape(S5_SLABS, S5_SLAB_STATE), ai.reshape(S5_SLABS, S5_SLAB_STATE)], -1)
    return bb.astype(BF16), cc.astype(BF16), a.reshape(1, S5_SLABS * 2 * S5_SLAB_STATE)


def _s5_body(u_ref, bb_ref, cc_ref, a_ref, y_ref, x_scr, h_scr):
    d = pl.program_id(0)
    rows = u_ref.shape[1]
    steps = rows // BATCH
    sw = 2 * S5_SLAB_STATE

    @pl.when(pl.program_id(1) == 0)
    def _():
        h_scr[...] = jnp.zeros_like(h_scr)

    for s in range(S5_SLABS):
        x_scr[:, s * sw:(s + 1) * sw] = _dot(u_ref[s].astype(BF16), bb_ref[0, s])

    for s in range(S5_SLABS):
        re0, im0 = s * sw, s * sw + S5_SLAB_STATE
        ar = jnp.broadcast_to(a_ref[0, :, re0:im0], (BATCH, S5_SLAB_STATE))
        ai = jnp.broadcast_to(a_ref[0, :, im0:im0 + S5_SLAB_STATE], (BATCH, S5_SLAB_STATE))

        def step(i, carry):
            hr, hi = carry
            t = jnp.where(d == 0, i, steps - 1 - i)
            r0 = pl.multiple_of(t * BATCH, BATCH)
            xr = x_scr[pl.ds(r0, BATCH), re0:im0]
            xi = x_scr[pl.ds(r0, BATCH), im0:im0 + S5_SLAB_STATE]
            nr = ar * hr - ai * hi + xr
            ni = ar * hi + ai * hr + xi
            x_scr[pl.ds(r0, BATCH), re0:im0] = nr
            x_scr[pl.ds(r0, BATCH), im0:im0 + S5_SLAB_STATE] = ni
            return nr, ni

        hr, hi = lax.fori_loop(0, steps, step,
                               (h_scr[:, re0:im0], h_scr[:, im0:im0 + S5_SLAB_STATE]), unroll=4)
        h_scr[:, re0:im0] = hr
        h_scr[:, im0:im0 + S5_SLAB_STATE] = hi

    for s in range(S5_SLABS):
        y_ref[0, :, s * S5_SLAB_CH:(s + 1) * S5_SLAB_CH] = _dot(
            x_scr[:, s * sw:(s + 1) * sw].astype(BF16), cc_ref[0, s])


def _s5_scan(z, bb, cc, a, n_ctx, n_all):
    rows = n_all * BATCH
    blk = S5_STEPS * BATCH
    ctx_chunks, all_chunks = n_ctx // S5_STEPS, n_all // S5_STEPS
    chunk = lambda d, i: _scan_chunk(d, i, ctx_chunks, all_chunks)
    state_w = S5_SLABS * 2 * S5_SLAB_STATE
    return pl.pallas_call(
        _s5_body,
        grid=(2, all_chunks),
        in_specs=[pl.BlockSpec((S5_SLABS, blk, LANES), lambda d, i: (0, chunk(d, i), 0)),
                  pl.BlockSpec((1, S5_SLABS, S5_SLAB_CH, 2 * S5_SLAB_STATE), lambda d, i: (d, 0, 0, 0)),
                  pl.BlockSpec((1, S5_SLABS, 2 * S5_SLAB_STATE, S5_SLAB_CH), lambda d, i: (d, 0, 0, 0)),
                  pl.BlockSpec((1, 1, state_w), lambda d, i: (d, 0, 0))],
        out_specs=pl.BlockSpec((1, blk, HALF_W), lambda d, i: (d, chunk(d, i), 0)),
        out_shape=jax.ShapeDtypeStruct((2, rows, HALF_W), F32),
        scratch_shapes=[pltpu.VMEM((blk, state_w), F32), pltpu.VMEM((BATCH, state_w), F32)],
        compiler_params=_params("arbitrary", "arbitrary"),
        name="s5_scan",
    )(z, bb, cc, a)


def _rope_tables(n_ctx, n_lat):
    half = RET_DK // 2
    nf = half // 2
    grid_rows = n_lat // GRID_W
    row = jnp.repeat(jnp.arange(grid_rows, dtype=F32), GRID_W)
    col = jnp.tile(jnp.arange(GRID_W, dtype=F32), grid_rows)
    inv = ROPE_BASE ** (-jnp.arange(nf, dtype=F32) / nf)
    ang = jnp.concatenate([row[:, None] * inv, col[:, None] * inv], -1)
    cos = jnp.tile(jnp.cos(ang), (1, 2 * RET_HEADS))
    sin = jnp.tile(jnp.concatenate([-jnp.sin(ang), jnp.sin(ang)], -1), (1, RET_HEADS))
    cos = jnp.concatenate([jnp.ones((n_ctx, RET_QK), F32), cos], 0)
    sin = jnp.concatenate([jnp.zeros((n_ctx, RET_QK), F32), sin], 0)
    return cos, sin


def _ret_tables(ret_decay):
    lg = jax.nn.log_sigmoid(ret_decay)
    pos = jnp.arange(RET_CHUNK, dtype=F32)
    diff = pos[:, None] - pos[None, :]
    f_mask, b_mask = diff >= 0, diff < 0
    dm_f = jnp.where(f_mask, jnp.exp(lg[0][:, None, None] * jnp.where(f_mask, diff, 0.0)), 0.0)
    dm_b = jnp.where(b_mask, jnp.exp(lg[1][:, None, None] * jnp.where(b_mask, -diff, 0.0)), 0.0)
    dmat = jnp.stack([dm_f, dm_b])
    heads = lambda t: jnp.repeat(t, RET_DK, axis=-1)
    kdec = jnp.stack([heads(jnp.exp(lg[0][None] * (RET_CHUNK - 1 - pos)[:, None])),
                      heads(jnp.exp(lg[1][None] * pos[:, None]))])
    qdec = jnp.stack([heads(jnp.exp(lg[0][None] * (pos + 1)[:, None])),
                      heads(jnp.exp(lg[1][None] * (RET_CHUNK - pos)[:, None]))])
    blk = jnp.kron(jnp.eye(RET_HEADS, dtype=F32), jnp.ones((RET_DK, RET_DV), F32))
    sdec = jnp.repeat(jnp.exp(lg * RET_CHUNK), RET_DK, axis=-1)[:, :, None] * blk[None]
    return dmat, kdec, qdec, sdec, blk


def _rope(t, cos, sin):
    lane = lax.broadcasted_iota(jnp.int32, t.shape, 1)
    first = (lane % RET_DK) < (RET_DK // 2)
    partner = jnp.where(first, pltpu.roll(t, RET_QK - RET_DK // 2, 1), pltpu.roll(t, RET_DK // 2, 1))
    return t * cos + partner * sin


def _batch_rows(b, steps):
    return pl.ds(b, steps, stride=BATCH)


def _ret_direction(d, b, q_ref, k_ref, v_ref, cos_ref, sin_ref, dm_ref, kd_ref, qd_ref, sd_ref, blk_ref,
                   o_ref, s_scr):
    rows = _batch_rows(b, RET_CHUNK)
    cos, sin = cos_ref[...], sin_ref[...]
    q = _rope(jnp.concatenate([q_ref[0, rows, :], q_ref[1, rows, :]], 1), cos, sin)
    k = _rope(jnp.concatenate([k_ref[0, rows, :], k_ref[1, rows, :]], 1), cos, sin) * (RET_DK ** -0.5)
    v = [v_ref[h, rows, :].astype(BF16) for h in range(RET_HEADS)]
    kb = k.astype(BF16)
    lane = lax.broadcasted_iota(jnp.int32, q.shape, 1)
    si = d * BATCH + b
    state = s_scr[si]
    o_inter = _dot((q * qd_ref[d]).astype(BF16), state.astype(BF16))
    for h in range(RET_HEADS):
        qh = jnp.where(lane // RET_DK == h, q, 0.0).astype(BF16)
        scores = _dot_nt(qh, kb) * dm_ref[d, h]
        o_ref[h, rows, :] = _dot(scores.astype(BF16), v[h]) + o_inter[:, h * RET_DV:(h + 1) * RET_DV]
    kv = _dot((k * kd_ref[d]).T.astype(BF16), jnp.concatenate(v, 1))
    s_scr[si] = sd_ref[d] * state + blk_ref[...] * kv


def _ret_body(qf, kf, vf, cf, sf, qb, kb, vb, cb, sb, dm_ref, kd_ref, qd_ref, sd_ref, blk_ref,
              of_ref, ob_ref, s_scr):
    @pl.when(pl.program_id(0) == 0)
    def _():
        s_scr[...] = jnp.zeros_like(s_scr)

    def per_batch(b, carry):
        _ret_direction(0, b, qf, kf, vf, cf, sf, dm_ref, kd_ref, qd_ref, sd_ref, blk_ref, of_ref, s_scr)
        _ret_direction(1, b, qb, kb, vb, cb, sb, dm_ref, kd_ref, qd_ref, sd_ref, blk_ref, ob_ref, s_scr)
        return carry

    lax.fori_loop(0, BATCH, per_batch, 0)


def _retention(z, cos, sin, tables, n_ctx, n_all):
    dmat, kdec, qdec, sdec, blk = tables
    rows = n_all * BATCH
    blk_rows = RET_CHUNK * BATCH
    ctx_chunks, all_chunks = n_ctx // RET_CHUNK, n_all // RET_CHUNK
    qk_slabs, v_slabs = RET_QK // LANES, HALF_W // LANES
    q0, k0, v0 = EVEN_SPLITS[0] // RET_QK, EVEN_SPLITS[1] // RET_QK, EVEN_SPLITS[2] // HALF_W

    def seq_specs(d):
        ch = lambda i: _scan_chunk(d, i, ctx_chunks, all_chunks)
        return [pl.BlockSpec((qk_slabs, blk_rows, LANES), lambda i: (q0, ch(i), 0)),
                pl.BlockSpec((qk_slabs, blk_rows, LANES), lambda i: (k0, ch(i), 0)),
                pl.BlockSpec((v_slabs, blk_rows, LANES), lambda i: (v0, ch(i), 0)),
                pl.BlockSpec((RET_CHUNK, RET_QK), lambda i: (ch(i), 0)),
                pl.BlockSpec((RET_CHUNK, RET_QK), lambda i: (ch(i), 0))]

    def out_spec(d):
        return pl.BlockSpec((RET_HEADS, blk_rows, LANES),
                            lambda i: (0, _scan_chunk(d, i, ctx_chunks, all_chunks), 0))

    const = lambda nd: (lambda i: (0,) * nd)
    o_shape = jax.ShapeDtypeStruct((RET_HEADS, rows, LANES), F32)
    return pl.pallas_call(
        _ret_body,
        grid=(all_chunks,),
        in_specs=seq_specs(0) + seq_specs(1) + [
            pl.BlockSpec(dmat.shape, const(4)), pl.BlockSpec(kdec.shape, const(3)),
            pl.BlockSpec(qdec.shape, const(3)), pl.BlockSpec(sdec.shape, const(3)),
            pl.BlockSpec(blk.shape, const(2))],
        out_specs=[out_spec(0), out_spec(1)],
        out_shape=[o_shape, o_shape],
        scratch_shapes=[pltpu.VMEM((2 * BATCH, RET_QK, HALF_W), F32)],
        compiler_params=_params("arbitrary"),
        name="retention",
    )(z, z, z, cos, sin, z, z, z, cos, sin, dmat, kdec, qdec, sdec, blk)


def _lru_body(ctx_chunks, all_chunks, prev_ref, x_ref, next_ref, cw_ref, cb_ref, wg_ref, bg_ref, sp_ref,
              h_ref, a_scr, b_scr, h_scr):
    d, i = pl.program_id(0), pl.program_id(1)
    rows = x_ref.shape[1]
    steps = rows // BATCH
    chunk = _scan_chunk(d, i, ctx_chunks, all_chunks)

    @pl.when(i == 0)
    def _():
        h_scr[...] = jnp.zeros_like(h_scr)

    seg_first, seg_last = _segment_edges(chunk, ctx_chunks, all_chunks)
    prev = jnp.where(seg_first, 0.0, prev_ref[...])
    nxt = jnp.where(seg_last, 0.0, next_ref[...])
    slabs = []
    for s in range(LRU_W // LANES):
        sl = slice(s * LANES, (s + 1) * LANES)
        ext = jnp.concatenate([prev[s], x_ref[s], nxt[s]], 0)
        slabs.append(cb_ref[:, sl] + sum(cw_ref[k:k + 1, sl] * ext[k * BATCH:k * BATCH + rows]
                                         for k in range(CONV_K)))
    xs = jnp.concatenate(slabs, 1)
    gates = _dot(xs.astype(BF16), wg_ref[0]) + bg_ref[0]
    r = jax.nn.sigmoid(gates[:, :LRU_W])
    ig = jax.nn.sigmoid(gates[:, LRU_W:])
    log_a = -r * sp_ref[0]
    a = jnp.exp(log_a)
    a_scr[...] = a
    b_scr[...] = jnp.sqrt(1.0 - jnp.exp(2.0 * log_a)) * (ig * xs)

    def step(j, h):
        t = jnp.where(d == 0, j, steps - 1 - j)
        r0 = pl.multiple_of(t * BATCH, BATCH)
        h = a_scr[pl.ds(r0, BATCH), :] * h + b_scr[pl.ds(r0, BATCH), :]
        h_ref[0, pl.ds(r0, BATCH), :] = h
        return h

    h_scr[...] = lax.fori_loop(0, steps, step, h_scr[...], unroll=8)


def _lru(z, conv_w, conv_b, w_a, b_a, w_x, b_x, lam, n_ctx, n_all):
    rows = n_all * BATCH
    blk = LRU_STEPS * BATCH
    ctx_chunks, all_chunks = n_ctx // LRU_STEPS, n_all // LRU_STEPS
    chunk = lambda d, i: _scan_chunk(d, i, ctx_chunks, all_chunks)
    col = ODD_LX // LRU_W
    slabs = LRU_W // LANES
    eye = jnp.eye(LRU_BLOCKS, dtype=F32)
    dense = lambda w: jnp.einsum('dkij,kl->dkilj', w, eye).reshape(2, LRU_W, LRU_W)
    wg = jnp.concatenate([dense(w_a), dense(w_x)], -1).astype(BF16)
    bg = jnp.concatenate([b_a, b_x], -1).reshape(2, 1, 2 * LRU_W)
    sp = (LRU_C * jax.nn.softplus(-lam)).reshape(2, 1, LRU_W)
    halo_prev, halo_next = 2 * BATCH, BATCH
    return pl.pallas_call(
        functools.partial(_lru_body, ctx_chunks, all_chunks),
        grid=(2, all_chunks),
        in_specs=[pl.BlockSpec((slabs, halo_prev, LANES),
                               lambda d, i: (col, jnp.maximum(chunk(d, i) * (blk // halo_prev) - 1, 0), 0)),
                  pl.BlockSpec((slabs, blk, LANES), lambda d, i: (col, chunk(d, i), 0)),
                  pl.BlockSpec((slabs, halo_next, LANES),
                               lambda d, i: (col, jnp.minimum((chunk(d, i) + 1) * (blk // halo_next),
                                                              rows // halo_next - 1), 0)),
                  pl.BlockSpec((CONV_K, LRU_W), lambda d, i: (0, 0)),
                  pl.BlockSpec((1, LRU_W), lambda d, i: (0, 0)),
                  pl.BlockSpec((1, LRU_W, 2 * LRU_W), lambda d, i: (d, 0, 0)),
                  pl.BlockSpec((1, 1, 2 * LRU_W), lambda d, i: (d, 0, 0)),
                  pl.BlockSpec((1, 1, LRU_W), lambda d, i: (d, 0, 0))],
        out_specs=pl.BlockSpec((1, blk, LRU_W), lambda d, i: (d, chunk(d, i), 0)),
        out_shape=jax.ShapeDtypeStruct((2, rows, LRU_W), F32),
        scratch_shapes=[pltpu.VMEM((blk, LRU_W), F32), pltpu.VMEM((blk, LRU_W), F32),
                        pltpu.VMEM((BATCH, LRU_W), F32)],
        compiler_params=_params("arbitrary", "arbitrary"),
        name="rglru",
    )(z, z, z, conv_w, conv_b.reshape(1, LRU_W), wg, bg, sp)


def _residual_norm_route(upd, x_ref, g2_ref, nw_ref, sh_ref, sc_ref, wrh_ref, wrl_ref, rb_ref,
                         xo_ref, hf_ref, lg_ref):
    tm = x_ref.shape[0]
    x3 = x_ref[...].reshape(tm // BATCH, BATCH, D_MODEL) + g2_ref[0] * upd.reshape(tm // BATCH, BATCH, D_MODEL)
    xo_ref[...] = x3.reshape(tm, D_MODEL)
    hf = _norm_modulate(x3, nw_ref[...], sh_ref[0], sc_ref[0]).reshape(tm, D_MODEL)
    for s in range(ROW_CHUNKS):
        hf_ref[:, s, :] = hf[:, s * LANES:(s + 1) * LANES]
    hi = hf.astype(BF16)
    lo = (hf - hi.astype(F32)).astype(BF16)
    lg_ref[...] = _dot(hi, wrh_ref[...]) + _dot(hi, wrl_ref[...]) + _dot(lo, wrh_ref[...]) + rb_ref[...]


def _gelu_tanh(x):
    return 0.5 * x * (1.0 + jnp.tanh(0.7978845608028654 * (x + 0.044715 * x * x * x)))


def _out_body_even(yf_ref, yb_ref, u_ref, d_ref, gw_ref, gb_ref, of_ref, ob_ref, gate_ref, wo_ref, *rest):
    u = jnp.concatenate([u_ref[s] for s in range(HALF_W // LANES)], 1)
    ys = yf_ref[0] + yb_ref[0] + d_ref[...] * u
    y = _gelu_tanh(ys)
    s5o = y * jax.nn.sigmoid(_dot(y.astype(BF16), gw_ref[...]) + gb_ref[...])
    upd = _dot(s5o.astype(BF16), wo_ref[0:HALF_W, :])
    for h in range(RET_HEADS):
        oh = of_ref[h] + ob_ref[h]
        mu = jnp.mean(oh, -1, keepdims=True)
        var = jnp.mean(jnp.square(oh - mu), -1, keepdims=True)
        rh = _silu(gate_ref[h]) * ((oh - mu) * lax.rsqrt(var + NORM_EPS))
        upd = upd + _dot(rh.astype(BF16), wo_ref[HALF_W + h * RET_DV:HALF_W + (h + 1) * RET_DV, :])
    _residual_norm_route(upd, *rest)


def _out_body_odd(gf_ref, gb_ref, zg_ref, gnw_ref, hf_ref, hb_ref, lg_ref, wo_ref, *rest):
    lg = jnp.concatenate([lg_ref[s] for s in range(LRU_W // LANES)], 1)
    lru = (hf_ref[0] + hb_ref[0]) * _gelu_tanh(lg)
    upd = _dot(lru.astype(BF16), wo_ref[HALF_W:, :])
    for h in range(GDN_HEADS):
        oh = gf_ref[h] + gb_ref[h]
        oh = oh * lax.rsqrt(jnp.mean(oh * oh, -1, keepdims=True) + NORM_EPS) * gnw_ref[...] * _silu(zg_ref[h])
        upd = upd + _dot(oh.astype(BF16), wo_ref[h * GDN_DV:(h + 1) * GDN_DV, :])
    _residual_norm_route(upd, *rest)


def _out_proj(body, mix_args, mix_specs, xs, w_out, gate2, norm_w, shift2, scale2, wr_hi, wr_lo, rb,
              row0_tiles, n_tiles, ctx_tiles):
    kind = lambda r: (jnp.where(r + row0_tiles < ctx_tiles, 0, 1), 0, 0)
    const2 = lambda r: (0, 0)
    rows = n_tiles * ROW_TILE
    return pl.pallas_call(
        body,
        grid=(n_tiles,),
        in_specs=mix_specs + [
            pl.BlockSpec((D_MODEL, D_MODEL), const2),
            pl.BlockSpec((ROW_TILE, D_MODEL), lambda r: (r + row0_tiles, 0)),
            pl.BlockSpec((1, BATCH, D_MODEL), kind),
            pl.BlockSpec((1, D_MODEL), const2),
            pl.BlockSpec((1, BATCH, D_MODEL), kind),
            pl.BlockSpec((1, BATCH, D_MODEL), kind),
            pl.BlockSpec((D_MODEL, ROUTER_PAD), const2),
            pl.BlockSpec((D_MODEL, ROUTER_PAD), const2),
            pl.BlockSpec((1, ROUTER_PAD), const2)],
        out_specs=[pl.BlockSpec((ROW_TILE, D_MODEL), lambda r: (r, 0)),
                   pl.BlockSpec((ROW_TILE, ROW_CHUNKS, LANES), lambda r: (r, 0, 0)),
                   pl.BlockSpec((ROW_TILE, ROUTER_PAD), lambda r: (r, 0))],
        out_shape=[jax.ShapeDtypeStruct((rows, D_MODEL), F32),
                   jax.ShapeDtypeStruct((rows, ROW_CHUNKS, LANES), F32),
                   jax.ShapeDtypeStruct((rows, ROUTER_PAD), F32)],
        compiler_params=_params("arbitrary"),
        name="out_proj",
    )(*mix_args, w_out, xs, gate2, norm_w.reshape(1, D_MODEL), shift2, scale2, wr_hi, wr_lo, rb)


def _route(logits):
    n_tok = logits.shape[0]
    g_logits = logits[:, :N_GROUPS]
    g_idx = jnp.argmax(g_logits, -1)
    g_w = jnp.take_along_axis(jax.nn.softmax(g_logits, -1), g_idx[:, None], 1)
    e_logits = logits[:, N_GROUPS:N_GROUPS + N_EXPERTS].reshape(n_tok, N_GROUPS, EXPERTS_PER_GROUP)
    e_in = jnp.take_along_axis(e_logits, g_idx[:, None, None], 1)[:, 0]
    top_v, top_i = lax.top_k(e_in, TOP_K)
    top_w = jax.nn.softmax(top_v, -1) * g_w
    return (g_idx[:, None] * EXPERTS_PER_GROUP + top_i).astype(jnp.int32), top_w


def _dispatch_plan(expert_ids, top_w, n_tiles):
    n_tok = expert_ids.shape[0]
    n_asg = n_tok * TOP_K
    flat_e = expert_ids.T.reshape(n_asg)
    flat_w = top_w.T.reshape(n_asg)
    order = jnp.argsort(flat_e, stable=True).astype(jnp.int32)
    inv = jnp.argsort(order).astype(jnp.int32)
    onehot = flat_e[:, None] == jnp.arange(N_EXPERTS, dtype=jnp.int32)[None]
    counts = jnp.sum(onehot, 0, dtype=jnp.int32)
    padded = ((counts + MOE_TILE - 1) // MOE_TILE) * MOE_TILE
    pend = jnp.cumsum(padded)
    pstart = pend - padded
    start = jnp.cumsum(counts) - counts
    shift = pstart - start
    pos = inv + jnp.sum(jnp.where(onehot, shift[None], 0), 1)
    tile_start = jnp.arange(n_tiles, dtype=jnp.int32) * MOE_TILE
    tile_e = jnp.minimum(jnp.sum(tile_start[:, None] >= pend[None], 1), N_EXPERTS - 1).astype(jnp.int32)
    tile_valid = (tile_start < pend[-1]).astype(jnp.int32)
    row = jnp.arange(n_tiles * MOE_TILE, dtype=jnp.int32)
    row_e = jnp.repeat(tile_e, MOE_TILE)
    rank = row - shift[row_e]
    row_ok = jnp.logical_and(rank >= start[row_e], rank < start[row_e] + counts[row_e])
    src_asg = order[jnp.clip(rank, 0, n_asg - 1)]
    src_tok = jnp.where(row_ok, src_asg % n_tok, 0)
    row_w = jnp.where(row_ok, flat_w[src_asg], 0.0)
    return src_tok, row_w, pos.reshape(TOP_K, n_tok), tile_e, tile_valid


def _gather_rows(idx_ref, src_hbm, dst_vmem, sem, n_rows):
    def issue(j, c):
        for p in range(DMA_PRIORITIES):
            r = j * DMA_PRIORITIES + p
            pltpu.make_async_copy(src_hbm.at[pl.ds(idx_ref[0, 0, r], 1)], dst_vmem.at[pl.ds(r, 1)],
                                  sem).start(priority=p)
        return c

    lax.fori_loop(0, n_rows // DMA_PRIORITIES, issue, 0, unroll=8)


def _wait_rows(src_hbm, dst_vmem, sem, n_rows):
    def wait(r, c):
        pltpu.make_async_copy(src_hbm.at[pl.ds(0, 1)], dst_vmem.at[pl.ds(r, 1)], sem).wait()
        return c

    lax.fori_loop(0, n_rows, wait, 0, unroll=16)


def _rows_2d(ref):
    return jnp.concatenate([ref[:, s, :] for s in range(ROW_CHUNKS)], 1)


def _expert_body(te_ref, tv_ref, idx_ref, idx_next_ref, hf_hbm, cw_ref, wg32_ref, wu32_ref, wd32_ref, y_ref,
                 xbuf, wg_ref, wu_ref, wd_ref, sem):
    i = pl.program_id(0)
    n = pl.num_programs(0)
    slot = i % 2
    nxt = jnp.minimum(i + 1, n - 1)

    @pl.when(jnp.logical_or(i == 0, te_ref[i] != te_ref[jnp.maximum(i - 1, 0)]))
    def _():
        wg_ref[0] = wg32_ref[0, 0].astype(BF16)
        wu_ref[0] = wu32_ref[0, 0].astype(BF16)
        wd_ref[0] = wd32_ref[0, 0].astype(BF16)

    @pl.when(jnp.logical_and(i == 0, tv_ref[0] == 1))
    def _():
        _gather_rows(idx_ref, hf_hbm, xbuf.at[0], sem.at[0], MOE_TILE)

    @pl.when(jnp.logical_and(i + 1 < n, tv_ref[nxt] == 1))
    def _():
        _gather_rows(idx_next_ref, hf_hbm, xbuf.at[1 - slot], sem.at[1 - slot], MOE_TILE)

    @pl.when(tv_ref[i] == 1)
    def _():
        _wait_rows(hf_hbm, xbuf.at[slot], sem.at[slot], MOE_TILE)
        x = _rows_2d(xbuf.at[slot]).astype(BF16)
        g = _dot(x, wg_ref[0])
        u = _dot(x, wu_ref[0])
        act = _silu(g) * u * cw_ref[:, 0:1]
        y = _dot(act.astype(BF16), wd_ref[0])
        for s in range(ROW_CHUNKS):
            y_ref[:, s, :] = y[:, s * LANES:(s + 1) * LANES]

    @pl.when(tv_ref[i] == 0)
    def _():
        y_ref[...] = jnp.zeros_like(y_ref)


def _experts(layer, tile_e, tile_valid, src_tok, hf, cw_rows, wg, wu, wd):
    n_tiles = tile_e.shape[0]
    tile3 = pl.BlockSpec((MOE_TILE, ROW_CHUNKS, LANES), lambda i, te, tv: (i, 0, 0))
    idx3 = src_tok.reshape(n_tiles, 1, MOE_TILE)
    grid_spec = pltpu.PrefetchScalarGridSpec(
        num_scalar_prefetch=2,
        grid=(n_tiles,),
        in_specs=[pl.BlockSpec((1, 1, MOE_TILE), lambda i, te, tv: (i, 0, 0), memory_space=pltpu.SMEM),
                  pl.BlockSpec((1, 1, MOE_TILE), lambda i, te, tv: (jnp.minimum(i + 1, n_tiles - 1), 0, 0),
                               memory_space=pltpu.SMEM),
                  pl.BlockSpec(memory_space=pl.ANY),
                  pl.BlockSpec((MOE_TILE, LANES), lambda i, te, tv: (i, 0)),
                  pl.BlockSpec((1, 1, D_MODEL, EXPERT_FF), lambda i, te, tv: (layer, te[i], 0, 0)),
                  pl.BlockSpec((1, 1, D_MODEL, EXPERT_FF), lambda i, te, tv: (layer, te[i], 0, 0)),
                  pl.BlockSpec((1, 1, EXPERT_FF, D_MODEL), lambda i, te, tv: (layer, te[i], 0, 0))],
        out_specs=tile3,
        scratch_shapes=[pltpu.VMEM((2, MOE_TILE, ROW_CHUNKS, LANES), F32),
                        pltpu.VMEM((1, D_MODEL, EXPERT_FF), BF16), pltpu.VMEM((1, D_MODEL, EXPERT_FF), BF16),
                        pltpu.VMEM((1, EXPERT_FF, D_MODEL), BF16), pltpu.SemaphoreType.DMA((2,))])
    return pl.pallas_call(
        _expert_body,
        grid_spec=grid_spec,
        out_shape=jax.ShapeDtypeStruct((n_tiles * MOE_TILE, ROW_CHUNKS, LANES), F32),
        compiler_params=_params("arbitrary"),
        name="moe_experts",
    )(tile_e, tile_valid, idx3, idx3, hf, cw_rows, wg, wu, wd)


def _combine_rows(p0_ref, p1_ref, p0n_ref, p1n_ref, y_hbm, x_ref, g5_ref, buf, sem):
    r = pl.program_id(0)
    n = pl.num_programs(0)
    tm = x_ref.shape[0]
    slot = r % 2

    def fetch(p0, p1, s):
        _gather_rows(p0, y_hbm, buf.at[s, 0], sem.at[s, 0], tm)
        _gather_rows(p1, y_hbm, buf.at[s, 1], sem.at[s, 1], tm)

    @pl.when(r == 0)
    def _():
        fetch(p0_ref, p1_ref, 0)

    @pl.when(r + 1 < n)
    def _():
        fetch(p0n_ref, p1n_ref, 1 - slot)

    _wait_rows(y_hbm, buf.at[slot, 0], sem.at[slot, 0], tm)
    _wait_rows(y_hbm, buf.at[slot, 1], sem.at[slot, 1], tm)
    f3 = (_rows_2d(buf.at[slot, 0]) + _rows_2d(buf.at[slot, 1])).reshape(tm // BATCH, BATCH, D_MODEL)
    return (x_ref[...].reshape(tm // BATCH, BATCH, D_MODEL) + g5_ref[0] * f3).reshape(tm, D_MODEL)


def _combine_body(p0_ref, p1_ref, p0n_ref, p1n_ref, y_hbm, x_ref, g5_ref, o_ref, buf, sem):
    o_ref[...] = _combine_rows(p0_ref, p1_ref, p0n_ref, p1n_ref, y_hbm, x_ref, g5_ref, buf, sem)


def _combine_final_body(p0_ref, p1_ref, p0n_ref, p1n_ref, y_hbm, x_ref, g5_ref, fw_ref, o_ref, buf, sem, slab):
    x = _combine_rows(p0_ref, p1_ref, p0n_ref, p1n_ref, y_hbm, x_ref, g5_ref, buf, sem)
    steps = x.shape[0] // BATCH
    xn = x * lax.rsqrt(jnp.mean(x * x, -1, keepdims=True) + NORM_EPS) * fw_ref[...]
    for s in range(ROW_CHUNKS):
        slab[s] = xn[:, s * LANES:(s + 1) * LANES]
    for b in range(BATCH):
        for s in range(ROW_CHUNKS):
            o_ref[b, :, s * LANES:(s + 1) * LANES] = slab[s, _batch_rows(b, steps), :]


def _combine(pos, y_sorted, xs, gate2, row0_tiles, ctx_tiles, final_w=None):
    n_tok = pos.shape[1]
    n_tiles = n_tok // MOE_TILE
    kind = lambda r: (jnp.where(r + row0_tiles < ctx_tiles, 0, 1), 0, 0)
    idx_spec = lambda k, ahead: pl.BlockSpec(
        (1, 1, MOE_TILE), lambda r: (k * n_tiles + jnp.minimum(r + ahead, n_tiles - 1), 0, 0),
        memory_space=pltpu.SMEM)
    p2 = pos.reshape(TOP_K * n_tiles, 1, MOE_TILE)
    in_specs = [idx_spec(0, 0), idx_spec(1, 0), idx_spec(0, 1), idx_spec(1, 1),
                pl.BlockSpec(memory_space=pl.ANY),
                pl.BlockSpec((MOE_TILE, D_MODEL), lambda r: (r, 0)),
                pl.BlockSpec((1, BATCH, D_MODEL), kind)]
    scratch = [pltpu.VMEM((2, TOP_K, MOE_TILE, ROW_CHUNKS, LANES), F32), pltpu.SemaphoreType.DMA((2, TOP_K))]
    if final_w is None:
        return pl.pallas_call(
            _combine_body,
            grid=(n_tiles,),
            in_specs=in_specs,
            out_specs=pl.BlockSpec((MOE_TILE, D_MODEL), lambda r: (r, 0)),
            out_shape=jax.ShapeDtypeStruct((n_tok, D_MODEL), F32),
            scratch_shapes=scratch,
            compiler_params=_params("arbitrary"),
            name="moe_combine",
        )(p2, p2, p2, p2, y_sorted, xs, gate2)
    steps = MOE_TILE // BATCH
    return pl.pallas_call(
        _combine_final_body,
        grid=(n_tiles,),
        in_specs=in_specs + [pl.BlockSpec((1, D_MODEL), lambda r: (0, 0))],
        out_specs=pl.BlockSpec((BATCH, steps, D_MODEL), lambda r: (0, r, 0)),
        out_shape=jax.ShapeDtypeStruct((BATCH, n_tok // BATCH, D_MODEL), F32),
        scratch_shapes=scratch + [pltpu.VMEM((ROW_CHUNKS, MOE_TILE, LANES), F32)],
        compiler_params=_params("arbitrary"),
        name="moe_combine_final",
    )(p2, p2, p2, p2, y_sorted, xs, gate2, final_w.reshape(1, D_MODEL))


def _moe(layer, hf, logits, xs, gate2, wg, wu, wd, row0_tiles, ctx_tiles, final_w=None):
    n_tok = hf.shape[0]
    n_tiles = n_tok * TOP_K // MOE_TILE + N_EXPERTS
    expert_ids, top_w = _route(logits)
    src_tok, row_w, pos, tile_e, tile_valid = _dispatch_plan(expert_ids, top_w, n_tiles)
    cw_rows = jnp.broadcast_to(row_w[:, None], (n_tiles * MOE_TILE, LANES))
    y_sorted = _experts(layer, tile_e, tile_valid, src_tok, hf, cw_rows, wg, wu, wd)
    return _combine(pos, y_sorted, xs, gate2, row0_tiles, ctx_tiles, final_w)


def _kinds(mod_l, k):
    return jnp.stack([mod_l[BATCH:, k], mod_l[:BATCH, k]])


def kernel(x, c, ctx, c_ctx, mod_w, mod_b, norm_mix, norm_ffn, router_group_w, router_group_b,
           router_expert_w, router_expert_b, expert_w_gate, expert_w_up, expert_w_down,
           even_w_in, even_w_out, s5_lam_re, s5_lam_im, s5_log_step, s5_b_re, s5_b_im, s5_c_re,
           s5_c_im, s5_d, s5_glu_w, s5_glu_b, ret_decay, odd_w_in, odd_w_out, gdn_conv_w, gdn_a_log,
           gdn_dt_bias, gdn_norm_w, lru_conv_w, lru_conv_b, lru_w_a, lru_b_a, lru_w_x, lru_b_x,
           lru_lam, final_norm):
    bsz, n_lat, _ = x.shape
    n_ctx = ctx.shape[1]
    n_all = n_ctx + n_lat
    assert bsz == BATCH and n_ctx % RET_CHUNK == 0 and n_lat % RET_CHUNK == 0
    assert (n_ctx * BATCH) % ROW_TILE == 0 and (n_lat * BATCH) % ROW_TILE == 0
    ctx_tiles = n_ctx * BATCH // ROW_TILE
    all_tiles = n_all * BATCH // ROW_TILE
    ctx_moe_tiles = n_ctx * BATCH // MOE_TILE

    xs = _to_time_major(jnp.concatenate([ctx, x], 1))
    cond16 = jnp.concatenate([c, jnp.broadcast_to(c_ctx[None], (BATCH, D_MODEL))], 0)
    mod = _adaln_all(cond16, mod_w, mod_b).reshape(DEPTH, ADALN_ROWS, N_MOD, D_MODEL)

    for layer in range(DEPTH):
        last = layer == DEPTH - 1
        i = layer // 2
        m = [_kinds(mod[layer], k) for k in range(N_MOD)]
        wr = jnp.zeros((D_MODEL, ROUTER_PAD), F32)
        wr = wr.at[:, :N_GROUPS].set(router_group_w[layer]).at[:, N_GROUPS:N_GROUPS + N_EXPERTS].set(
            router_expert_w[layer])
        wr_hi = wr.astype(BF16)
        wr_lo = (wr - wr_hi.astype(F32)).astype(BF16)
        rb = jnp.zeros((1, ROUTER_PAD), F32)
        rb = rb.at[0, :N_GROUPS].set(router_group_b[layer]).at[0, N_GROUPS:N_GROUPS + N_EXPERTS].set(
            router_expert_b[layer])
        row0 = ctx_tiles if last else 0
        n_tiles = all_tiles - row0
        slab_spec = lambda sb: pl.BlockSpec((HALF_W // LANES, ROW_TILE, LANES), lambda r: (sb, r + row0, 0))
        dir_spec = lambda d: pl.BlockSpec((1, ROW_TILE, HALF_W), lambda r: (d, r + row0, 0))
        const2 = lambda r: (0, 0)

        if layer % 2 == 0:
            z = _in_proj(xs, norm_mix[layer], m[0], m[1], even_w_in[i].astype(BF16), ctx_tiles)
            disc = [_s5_discretize(s5_lam_re[i, d], s5_lam_im[i, d], s5_log_step[i, d], s5_b_re[i, d],
                                   s5_b_im[i, d], s5_c_re[i, d], s5_c_im[i, d]) for d in range(2)]
            bb, cc, a = (jnp.stack(t) for t in zip(*disc))
            ys5 = _s5_scan(z, bb, cc, a, n_ctx, n_all)
            cos, sin = _rope_tables(n_ctx, n_lat)
            r_f, r_b = _retention(z, cos, sin, _ret_tables(ret_decay[i]), n_ctx, n_all)
            mix_args = [ys5, ys5, z, s5_d[i].reshape(1, HALF_W), s5_glu_w[i].astype(BF16),
                        s5_glu_b[i].reshape(1, HALF_W), r_f, r_b, z]
            mix_specs = [dir_spec(0), dir_spec(1), slab_spec(0),
                         pl.BlockSpec((1, HALF_W), const2),
                         pl.BlockSpec((HALF_W, HALF_W), const2),
                         pl.BlockSpec((1, HALF_W), const2),
                         slab_spec(0), slab_spec(0), slab_spec(EVEN_SPLITS[3] // HALF_W)]
            body, w_out = _out_body_even, even_w_out[i]
        else:
            w_in = odd_w_in[i]
            w_pad = jnp.concatenate([
                w_in[:, :ODD_SPLITS[1]],
                w_in[:, ODD_SPLITS[3]:],
                jnp.pad(w_in[:, ODD_SPLITS[1]:ODD_SPLITS[3]], ((0, 0), (0, ODD_PAD - 4 * GDN_HEADS)))], 1)
            z = _in_proj(xs, norm_mix[layer], m[0], m[1], w_pad.astype(BF16), ctx_tiles)
            qkv_act, bg = _gdn_prep(z, gdn_conv_w[i], gdn_a_log[i], gdn_dt_bias[i], n_ctx, n_all)
            g_f, g_b = _gdn(qkv_act, bg, n_ctx, n_all)
            hl = _lru(z, lru_conv_w[i], lru_conv_b[i], lru_w_a[i], lru_b_a[i], lru_w_x[i], lru_b_x[i],
                      lru_lam[i], n_ctx, n_all)
            mix_args = [g_f, g_b, z, gdn_norm_w[i].reshape(1, GDN_DV), hl, hl, z]
            mix_specs = [slab_spec(0), slab_spec(0), slab_spec(ODD_SPLITS[0] // HALF_W),
                         pl.BlockSpec((1, GDN_DV), const2), dir_spec(0), dir_spec(1),
                         slab_spec((ODD_LX + LRU_W) // LRU_W)]
            body, w_out = _out_body_odd, odd_w_out[i]

        xs_mid, hf, logits = _out_proj(body, mix_args, mix_specs, xs, w_out.astype(BF16), m[2],
                                       norm_ffn[layer], m[3], m[4], wr_hi, wr_lo, rb, row0, n_tiles, ctx_tiles)
        moe_row0 = ctx_moe_tiles if last else 0
        xs = _moe(layer, hf, logits, xs_mid, m[5], expert_w_gate, expert_w_up, expert_w_down,
                  moe_row0, ctx_moe_tiles, final_norm if last else None)

    return xs
```

```python
import functools

import jax
import jax.numpy as jnp
from jax import lax
from jax.experimental import pallas as pl
from jax.experimental.pallas import tpu as pltpu

D_MODEL = 1024
DEPTH = 2
GRID_W = 64
HALF_W = D_MODEL // 2
S5_GROUP = 16
S5_GROUPS = HALF_W // S5_GROUP
S5_STATE = 64
RET_HEADS = 4
RET_DV = HALF_W // RET_HEADS
RET_DK = RET_DV // 2
RET_QK = RET_HEADS * RET_DK
RET_CHUNK = 128
ROPE_BASE = 10000.0
GDN_HEADS = 4
GDN_DK = HALF_W // GDN_HEADS
GDN_DV = GDN_DK
GDN_CHUNK = 64
LRU_W = HALF_W
LRU_BLOCKS = 8
LRU_BLOCK = LRU_W // LRU_BLOCKS
LRU_C = 8.0
CONV_K = 4
CONV_PAD_LEFT = 2
N_GROUPS = 4
EXPERTS_PER_GROUP = 8
N_EXPERTS = N_GROUPS * EXPERTS_PER_GROUP
TOP_K = 2
EXPERT_FF = D_MODEL // 2
N_MOD = 6
NORM_EPS = 1e-6
EVEN_SPLITS = [HALF_W, HALF_W + RET_QK, HALF_W + 2 * RET_QK, 2 * HALF_W + 2 * RET_QK]
EVEN_IN = 3 * HALF_W + 2 * RET_QK
ODD_SPLITS = [3 * HALF_W, 4 * HALF_W, 4 * HALF_W + 2 * GDN_HEADS, 4 * HALF_W + 4 * GDN_HEADS,
              4 * HALF_W + 4 * GDN_HEADS + LRU_W]

F32 = jnp.float32
BF16 = jnp.bfloat16

SUBLANES = 8
LANES = 128
BATCH = SUBLANES
ROW_CHUNKS = D_MODEL // LANES
VMEM_LIMIT = 48 * 1024 * 1024

ROW_TILE = 512
S5_STEPS = 64
S5_SLABS = 4
S5_SLAB_CH = HALF_W // S5_SLABS
S5_SLAB_STATE = (S5_GROUPS // S5_SLABS) * S5_STATE
LRU_STEPS = 64
MOE_TILE = 256
DMA_PRIORITIES = 2
ODD_PAD = 512
ODD_LX = 4 * HALF_W
ODD_BD = ODD_LX + 2 * LRU_W
ODD_IN_PADDED = ODD_BD + ODD_PAD
GDN_QKV = 3 * HALF_W
GDN_ROWS = GDN_CHUNK * BATCH
GDN_BG_G = 2 * GDN_HEADS
GDN_NEUMANN_LEVELS = 5
GDN_BATCHES_PER_ITER = 4
ROUTER_PAD = LANES


def _dot(a, b):
    return jnp.dot(a, b, preferred_element_type=F32)


def _dot_nt(a, b):
    return lax.dot_general(a, b, (((1,), (1,)), ((), ())), preferred_element_type=F32)


def _params(*sem):
    return pltpu.CompilerParams(dimension_semantics=sem, vmem_limit_bytes=VMEM_LIMIT)


def _scan_chunk(d, i, ctx_chunks, all_chunks):
    back = jnp.where(i < ctx_chunks, ctx_chunks - 1 - i, all_chunks + ctx_chunks - 1 - i)
    return jnp.where(d == 0, i, back)


def _silu(x):
    return x * jax.nn.sigmoid(x)


def _softplus(x):
    return jnp.maximum(x, 0.0) + jnp.log(1.0 + jnp.exp(-jnp.abs(x)))


def _segment_edges(chunk, ctx_chunks, all_chunks):
    first = jnp.logical_or(chunk == 0, chunk == ctx_chunks)
    last = jnp.logical_or(chunk == ctx_chunks - 1, chunk == all_chunks - 1)
    return first, last


def _gdn_prep_body(ctx_chunks, all_chunks, prev_ref, x_ref, next_ref, bd_ref, cw_ref, al_ref, dtb_ref,
                   qkv_ref, bg_ref, ext_scr):
    rows = x_ref.shape[1]
    steps = rows // BATCH
    halo = CONV_PAD_LEFT * BATCH
    seg_first, seg_last = _segment_edges(pl.program_id(0), ctx_chunks, all_chunks)
    ext_scr[:, 0:halo, :] = jnp.where(seg_first, 0.0, prev_ref[...])
    ext_scr[:, halo:halo + rows, :] = x_ref[...]
    ext_scr[:, halo + rows:, :] = jnp.where(seg_last, 0.0, next_ref[...])
    for j in range(GDN_QKV // LANES):
        sl = slice(j * LANES, (j + 1) * LANES)
        c = sum(cw_ref[k:k + 1, sl] * ext_scr[j, k * BATCH:k * BATCH + rows, :] for k in range(CONV_K))
        a = _silu(c)
        if j < 2 * GDN_HEADS:
            a = a * lax.rsqrt(jnp.sum(a * a, -1, keepdims=True) + NORM_EPS)
        if j < GDN_HEADS:
            a = a * (GDN_DK ** -0.5)
        qkv_ref[j] = a

    bd = bd_ref[0]
    lane = lax.broadcasted_iota(jnp.int32, bd.shape, 1)
    val = jnp.where(lane < GDN_BG_G, jax.nn.sigmoid(bd), al_ref[...] * _softplus(bd + dtb_ref[...]))
    prefix = val
    for k in range((steps - 1).bit_length()):
        sh = BATCH << k
        prefix = prefix + jnp.concatenate([jnp.zeros((sh, LANES), F32), prefix[:rows - sh]], 0)
    p3 = prefix.reshape(steps, BATCH, LANES)
    suffix = (p3[steps - 1][None] - p3 + val.reshape(steps, BATCH, LANES)).reshape(rows, LANES)
    fwd_g = jnp.logical_and(lane >= GDN_BG_G, lane < GDN_BG_G + GDN_HEADS)
    bwd_g = jnp.logical_and(lane >= GDN_BG_G + GDN_HEADS, lane < GDN_BG_G + 2 * GDN_HEADS)
    bg_ref[...] = jnp.where(fwd_g, prefix, jnp.where(bwd_g, suffix, val))


def _gdn_prep(z, conv_w, a_log, dt_bias, n_ctx, n_all):
    rows = n_all * BATCH
    ctx_chunks, all_chunks = n_ctx // GDN_CHUNK, n_all // GDN_CHUNK
    halo_prev, halo_next = CONV_PAD_LEFT * BATCH, (CONV_K - 1 - CONV_PAD_LEFT) * BATCH
    slabs = GDN_QKV // LANES
    lanes16 = lambda t: jnp.zeros((1, LANES), F32).at[0, GDN_BG_G:GDN_BG_G + 2 * GDN_HEADS].set(t.reshape(-1))
    return pl.pallas_call(
        functools.partial(_gdn_prep_body, ctx_chunks, all_chunks),
        grid=(all_chunks,),
        in_specs=[pl.BlockSpec((slabs, halo_prev, LANES),
                               lambda i: (0, jnp.maximum(i * (GDN_ROWS // halo_prev) - 1, 0), 0)),
                  pl.BlockSpec((slabs, GDN_ROWS, LANES), lambda i: (0, i, 0)),
                  pl.BlockSpec((slabs, halo_next, LANES),
                               lambda i: (0, jnp.minimum((i + 1) * (GDN_ROWS // halo_next), rows // halo_next - 1), 0)),
                  pl.BlockSpec((1, GDN_ROWS, LANES), lambda i: (ODD_BD // LANES, i, 0)),
                  pl.BlockSpec((CONV_K, GDN_QKV), lambda i: (0, 0)),
                  pl.BlockSpec((1, LANES), lambda i: (0, 0)),
                  pl.BlockSpec((1, LANES), lambda i: (0, 0))],
        out_specs=[pl.BlockSpec((slabs, GDN_ROWS, LANES), lambda i: (0, i, 0)),
                   pl.BlockSpec((GDN_ROWS, LANES), lambda i: (i, 0))],
        out_shape=[jax.ShapeDtypeStruct((slabs, rows, LANES), F32), jax.ShapeDtypeStruct((rows, LANES), F32)],
        scratch_shapes=[pltpu.VMEM((slabs, GDN_ROWS + halo_prev + halo_next, LANES), F32)],
        compiler_params=_params("arbitrary"),
        name="gdn_prep",
    )(z, z, z, z, conv_w, lanes16(-jnp.exp(a_log)), lanes16(dt_bias))


def _dot_split(a, b):
    ah, bh = a.astype(BF16), b.astype(BF16)
    al, bl = (a - ah.astype(F32)).astype(BF16), (b - bh.astype(F32)).astype(BF16)
    return _dot(ah, bh) + _dot(ah, bl) + _dot(al, bh)


def _gdn_batches(batches, xf_ref, bgf_ref, xb_ref, bgb_ref, of_ref, ob_ref, s_scr):
    ii = lax.broadcasted_iota(jnp.int32, (GDN_CHUNK, GDN_CHUNK), 0)
    jj = lax.broadcasted_iota(jnp.int32, (GDN_CHUNK, GDN_CHUNK), 1)
    eye = jnp.where(ii == jj, 1.0, 0.0)
    units = []
    for b in batches:
        rows = _batch_rows(b, GDN_CHUNK)
        for d, (x_ref, bg_ref, o_ref) in enumerate(((xf_ref, bgf_ref, of_ref), (xb_ref, bgb_ref, ob_ref))):
            bg = bg_ref[rows, :]
            bg_t = bg.T
            incl = (jj <= ii) if d == 0 else (jj >= ii)
            strict = (jj < ii) if d == 0 else (jj > ii)
            last = GDN_CHUNK - 1 if d == 0 else 0
            for h in range(GDN_HEADS):
                q, k, v = x_ref[h, rows, :], x_ref[GDN_HEADS + h, rows, :], x_ref[2 * GDN_HEADS + h, rows, :]
                cb = d * GDN_HEADS + h
                cg = GDN_BG_G + cb
                beta, g_col = bg[:, cb:cb + 1], bg[:, cg:cg + 1]
                g_row, g_last = bg_t[cg:cg + 1, :], bg_t[cg:cg + 1, last:last + 1]
                decay = jnp.where(incl, jnp.exp(jnp.where(incl, g_col - g_row, 0.0)), 0.0)
                kb16 = k.astype(BF16)
                eg = jnp.exp(g_col)
                units.append(dict(
                    si=b * (2 * GDN_HEADS) + cb, h=h, o_ref=o_ref, rows=rows,
                    qk=(_dot_nt(q.astype(BF16), kb16) * decay).astype(BF16),
                    n=jnp.where(strict, -(_dot_nt(kb16, kb16) * decay * beta), 0.0),
                    rhs=jnp.concatenate([v * beta, k * (beta * eg)], 1),
                    qg=(q * eg).astype(BF16),
                    kg_t=(k * jnp.exp(g_last - g_col)).T.astype(BF16),
                    d_last=jnp.exp(g_last)))

    ts = [eye + u['n'] for u in units]
    ps = [u['n'] for u in units]
    for _ in range(GDN_NEUMANN_LEVELS):
        ps = [_dot(p.astype(BF16), p.astype(BF16)) for p in ps]
        ts = [t + _dot(t.astype(BF16), p.astype(BF16)) for t, p in zip(ts, ps)]
    sols = [_dot_split(t, u['rhs']) for t, u in zip(ts, units)]
    states = [s_scr[u['si']] for u in units]
    sbs = [s.astype(BF16) for s in states]
    v_news = [(sol[:, :GDN_DV] - _dot(sol[:, GDN_DV:].astype(BF16), sb)).astype(BF16)
              for sol, sb in zip(sols, sbs)]
    for u, s, sb, vn in zip(units, states, sbs, v_news):
        u['o_ref'][u['h'], u['rows'], :] = _dot(u['qg'], sb) + _dot(u['qk'], vn)
        s_scr[u['si']] = s * u['d_last'] + _dot(u['kg_t'], vn)


def _gdn_body(xf_ref, bgf_ref, xb_ref, bgb_ref, of_ref, ob_ref, s_scr):
    @pl.when(pl.program_id(0) == 0)
    def _():
        s_scr[...] = jnp.zeros_like(s_scr)

    def per_pair(j, carry):
        batches = [j * GDN_BATCHES_PER_ITER + p for p in range(GDN_BATCHES_PER_ITER)]
        _gdn_batches(batches, xf_ref, bgf_ref, xb_ref, bgb_ref, of_ref, ob_ref, s_scr)
        return carry

    lax.fori_loop(0, BATCH // GDN_BATCHES_PER_ITER, per_pair, 0)


def _gdn(qkv, bg, n_ctx, n_all):
    rows = n_all * BATCH
    ctx_chunks, all_chunks = n_ctx // GDN_CHUNK, n_all // GDN_CHUNK
    slabs = GDN_QKV // LANES

    def seq_specs(d):
        ch = lambda i: _scan_chunk(d, i, ctx_chunks, all_chunks)
        return [pl.BlockSpec((slabs, GDN_ROWS, LANES), lambda i: (0, ch(i), 0)),
                pl.BlockSpec((GDN_ROWS, LANES), lambda i: (ch(i), 0))]

    out_spec = lambda d: pl.BlockSpec((GDN_HEADS, GDN_ROWS, LANES),
                                      lambda i: (0, _scan_chunk(d, i, ctx_chunks, all_chunks), 0))
    o_shape = jax.ShapeDtypeStruct((GDN_HEADS, rows, LANES), F32)
    return pl.pallas_call(
        _gdn_body,
        grid=(all_chunks,),
        in_specs=seq_specs(0) + seq_specs(1),
        out_specs=[out_spec(0), out_spec(1)],
        out_shape=[o_shape, o_shape],
        scratch_shapes=[pltpu.VMEM((BATCH * 2 * GDN_HEADS, GDN_DK, GDN_DV), F32)],
        compiler_params=_params("arbitrary"),
        name="gdn_chunks",
    )(qkv, bg, qkv, bg)


ADALN_COLS = 512
ADALN_ROWS = 2 * BATCH


def _adaln_body(c_ref, w_ref, b_ref, o_ref):
    c = c_ref[...]
    s = (c * jax.nn.sigmoid(c)).astype(BF16)
    o_ref[0] = _dot(s, w_ref[0].astype(BF16)) + b_ref[0]


def _adaln_all(cond16, mod_w, mod_b):
    n_out = mod_w.shape[-1]
    return pl.pallas_call(
        _adaln_body,
        grid=(DEPTH, n_out // ADALN_COLS),
        in_specs=[pl.BlockSpec((ADALN_ROWS, D_MODEL), lambda l, j: (0, 0)),
                  pl.BlockSpec((1, D_MODEL, ADALN_COLS), lambda l, j: (l, 0, j)),
                  pl.BlockSpec((1, 1, ADALN_COLS), lambda l, j: (l, 0, j))],
        out_specs=pl.BlockSpec((1, ADALN_ROWS, ADALN_COLS), lambda l, j: (l, 0, j)),
        out_shape=jax.ShapeDtypeStruct((DEPTH, ADALN_ROWS, n_out), F32),
        compiler_params=_params("arbitrary", "arbitrary"),
        name="adaln",
    )(cond16, mod_w, mod_b.reshape(DEPTH, 1, n_out))


def _col_chunks(n, width=512):
    return [(c0, min(width, n - c0)) for c0 in range(0, n, width)]


def _norm_modulate(x3, nw, shift, scale):
    y = x3 * lax.rsqrt(jnp.mean(x3 * x3, -1, keepdims=True) + NORM_EPS) * nw
    return y * (1.0 + scale) + shift


def _project(x, nw_ref, sh_ref, sc_ref, w_ref, o_ref):
    tm = x.shape[0]
    x3 = x.reshape(tm // BATCH, BATCH, D_MODEL)
    h = _norm_modulate(x3, nw_ref[...], sh_ref[0], sc_ref[0]).reshape(tm, D_MODEL).astype(BF16)
    for c0, cw in _col_chunks(w_ref.shape[1]):
        res = _dot(h, w_ref[:, c0:c0 + cw])
        for j in range(cw // LANES):
            o_ref[c0 // LANES + j] = res[:, j * LANES:(j + 1) * LANES]


def _in_proj_body(x_ref, nw_ref, sh_ref, sc_ref, w_ref, o_ref):
    _project(x_ref[...], nw_ref, sh_ref, sc_ref, w_ref, o_ref)


def _in_proj_first_body(ctx_tiles, ctx_ref, lat_ref, nw_ref, sh_ref, sc_ref, w_ref, o_ref, xs_ref, slab):
    steps = ROW_TILE // BATCH

    def load(src_ref):
        for b in range(BATCH):
            for s in range(ROW_CHUNKS):
                slab[s, _batch_rows(b, steps), :] = src_ref[b, :, s * LANES:(s + 1) * LANES]

    @pl.when(pl.program_id(0) < ctx_tiles)
    def _():
        load(ctx_ref)

    @pl.when(pl.program_id(0) >= ctx_tiles)
    def _():
        load(lat_ref)

    x = jnp.concatenate([slab[s] for s in range(ROW_CHUNKS)], 1)
    xs_ref[...] = x
    _project(x, nw_ref, sh_ref, sc_ref, w_ref, o_ref)


def _in_proj_first(ctx, x, norm_w, shift2, scale2, w_bf16, ctx_tiles):
    rows = (ctx.shape[1] + x.shape[1]) * BATCH
    n = w_bf16.shape[1]
    steps = ROW_TILE // BATCH
    kind = lambda r: (jnp.where(r < ctx_tiles, 0, 1), 0, 0)
    return pl.pallas_call(
        functools.partial(_in_proj_first_body, ctx_tiles),
        grid=(rows // ROW_TILE,),
        in_specs=[pl.BlockSpec((BATCH, steps, D_MODEL), lambda r: (0, jnp.minimum(r, ctx_tiles - 1), 0)),
                  pl.BlockSpec((BATCH, steps, D_MODEL), lambda r: (0, jnp.maximum(r - ctx_tiles, 0), 0)),
                  pl.BlockSpec((1, D_MODEL), lambda r: (0, 0)),
                  pl.BlockSpec((1, BATCH, D_MODEL), kind),
                  pl.BlockSpec((1, BATCH, D_MODEL), kind),
                  pl.BlockSpec((D_MODEL, n), lambda r: (0, 0))],
        out_specs=[pl.BlockSpec((n // LANES, ROW_TILE, LANES), lambda r: (0, r, 0)),
                   pl.BlockSpec((ROW_TILE, D_MODEL), lambda r: (r, 0))],
        out_shape=[jax.ShapeDtypeStruct((n // LANES, rows, LANES), F32),
                   jax.ShapeDtypeStruct((rows, D_MODEL), F32)],
        scratch_shapes=[pltpu.VMEM((ROW_CHUNKS, ROW_TILE, LANES), F32)],
        compiler_params=_params("arbitrary"),
        name="in_proj_first",
    )(ctx, x, norm_w.reshape(1, D_MODEL), shift2, scale2, w_bf16)


def _in_proj(xs, norm_w, shift2, scale2, w_bf16, ctx_tiles):
    rows = xs.shape[0]
    n = w_bf16.shape[1]
    kind = lambda r: (jnp.where(r < ctx_tiles, 0, 1), 0, 0)
    return pl.pallas_call(
        _in_proj_body,
        grid=(rows // ROW_TILE,),
        in_specs=[pl.BlockSpec((ROW_TILE, D_MODEL), lambda r: (r, 0)),
                  pl.BlockSpec((1, D_MODEL), lambda r: (0, 0)),
                  pl.BlockSpec((1, BATCH, D_MODEL), kind),
                  pl.BlockSpec((1, BATCH, D_MODEL), kind),
                  pl.BlockSpec((D_MODEL, n), lambda r: (0, 0))],
        out_specs=pl.BlockSpec((n // LANES, ROW_TILE, LANES), lambda r: (0, r, 0)),
        out_shape=jax.ShapeDtypeStruct((n // LANES, rows, LANES), F32),
        compiler_params=_params("arbitrary"),
        name="in_proj",
    )(xs, norm_w.reshape(1, D_MODEL), shift2, scale2, w_bf16)


def _s5_discretize(lam_re, lam_im, log_step, b_re, b_im, c_re, c_im):
    lr, li = lam_re, lam_im
    dt = jnp.exp(log_step)[:, None]
    mag = jnp.exp(lr * dt)
    ar, ai = mag * jnp.cos(li * dt), mag * jnp.sin(li * dt)
    pr, pi = ar - 1.0, ai
    den = lr * lr + li * li
    zr, zi = (pr * lr + pi * li) / den, (pi * lr - pr * li) / den
    bbr = zr[..., None] * b_re - zi[..., None] * b_im
    bbi = zr[..., None] * b_im + zi[..., None] * b_re
    gps = S5_GROUPS // S5_SLABS
    eye = jnp.eye(gps, dtype=F32)

    def in_slab(m):
        m = m.reshape(S5_SLABS, gps, S5_STATE, S5_GROUP)
        return jnp.einsum('sgnc,gh->sgchn', m, eye).reshape(S5_SLABS, S5_SLAB_CH, S5_SLAB_STATE)

    def out_slab(m):
        m = m.reshape(S5_SLABS, gps, S5_GROUP, S5_STATE)
        return jnp.einsum('sgcn,gh->sgnhc', m, eye).reshape(S5_SLABS, S5_SLAB_STATE, S5_SLAB_CH)

    bb = jnp.concatenate([in_slab(bbr), in_slab(bbi)], -1)
    cc = jnp.concatenate([out_slab(c_re), out_slab(-c_im)], 1)
    a = jnp.concatenate([ar.reshape(S5_SLABS, S5_SLAB_STATE), ai.reshape(S5_SLABS, S5_SLAB_STATE)], -1)
    return bb.astype(BF16), cc.astype(BF16), a.reshape(1, S5_SLABS * 2 * S5_SLAB_STATE)


def _s5_body(u_ref, bb_ref, cc_ref, a_ref, y_ref, x_scr, h_scr):
    d = pl.program_id(0)
    rows = u_ref.shape[1]
    steps = rows // BATCH
    sw = 2 * S5_SLAB_STATE

    @pl.when(pl.program_id(1) == 0)
    def _():
        h_scr[...] = jnp.zeros_like(h_scr)

    for s in range(S5_SLABS):
        x_scr[:, s * sw:(s + 1) * sw] = _dot(u_ref[s].astype(BF16), bb_ref[0, s])

    for s in range(S5_SLABS):
        re0, im0 = s * sw, s * sw + S5_SLAB_STATE
        ar = jnp.broadcast_to(a_ref[0, :, re0:im0], (BATCH, S5_SLAB_STATE))
        ai = jnp.broadcast_to(a_ref[0, :, im0:im0 + S5_SLAB_STATE], (BATCH, S5_SLAB_STATE))

        def step(i, carry):
            hr, hi = carry
            t = jnp.where(d == 0, i, steps - 1 - i)
            r0 = pl.multiple_of(t * BATCH, BATCH)
            xr = x_scr[pl.ds(r0, BATCH), re0:im0]
            xi = x_scr[pl.ds(r0, BATCH), im0:im0 + S5_SLAB_STATE]
            nr = ar * hr - ai * hi + xr
            ni = ar * hi + ai * hr + xi
            x_scr[pl.ds(r0, BATCH), re0:im0] = nr
            x_scr[pl.ds(r0, BATCH), im0:im0 + S5_SLAB_STATE] = ni
            return nr, ni

        hr, hi = lax.fori_loop(0, steps, step,
                               (h_scr[:, re0:im0], h_scr[:, im0:im0 + S5_SLAB_STATE]), unroll=4)
        h_scr[:, re0:im0] = hr
        h_scr[:, im0:im0 + S5_SLAB_STATE] = hi

    for s in range(S5_SLABS):
        y_ref[0, :, s * S5_SLAB_CH:(s + 1) * S5_SLAB_CH] = _dot(
            x_scr[:, s * sw:(s + 1) * sw].astype(BF16), cc_ref[0, s])


def _s5_scan(z, bb, cc, a, n_ctx, n_all):
    rows = n_all * BATCH
    blk = S5_STEPS * BATCH
    ctx_chunks, all_chunks = n_ctx // S5_STEPS, n_all // S5_STEPS
    chunk = lambda d, i: _scan_chunk(d, i, ctx_chunks, all_chunks)
    state_w = S5_SLABS * 2 * S5_SLAB_STATE
    return pl.pallas_call(
        _s5_body,
        grid=(2, all_chunks),
        in_specs=[pl.BlockSpec((S5_SLABS, blk, LANES), lambda d, i: (0, chunk(d, i), 0)),
                  pl.BlockSpec((1, S5_SLABS, S5_SLAB_CH, 2 * S5_SLAB_STATE), lambda d, i: (d, 0, 0, 0)),
                  pl.BlockSpec((1, S5_SLABS, 2 * S5_SLAB_STATE, S5_SLAB_CH), lambda d, i: (d, 0, 0, 0)),
                  pl.BlockSpec((1, 1, state_w), lambda d, i: (d, 0, 0))],
        out_specs=pl.BlockSpec((1, blk, HALF_W), lambda d, i: (d, chunk(d, i), 0)),
        out_shape=jax.ShapeDtypeStruct((2, rows, HALF_W), F32),
        scratch_shapes=[pltpu.VMEM((blk, state_w), F32), pltpu.VMEM((BATCH, state_w), F32)],
        compiler_params=_params("arbitrary", "arbitrary"),
        name="s5_scan",
    )(z, bb, cc, a)


def _rope_tables(n_ctx, n_lat):
    half = RET_DK // 2
    nf = half // 2
    grid_rows = n_lat // GRID_W
    row = jnp.repeat(jnp.arange(grid_rows, dtype=F32), GRID_W)
    col = jnp.tile(jnp.arange(GRID_W, dtype=F32), grid_rows)
    inv = ROPE_BASE ** (-jnp.arange(nf, dtype=F32) / nf)
    ang = jnp.concatenate([row[:, None] * inv, col[:, None] * inv], -1)
    cos = jnp.tile(jnp.cos(ang), (1, 2 * RET_HEADS))
    sin = jnp.tile(jnp.concatenate([-jnp.sin(ang), jnp.sin(ang)], -1), (1, RET_HEADS))
    cos = jnp.concatenate([jnp.ones((n_ctx, RET_QK), F32), cos], 0)
    sin = jnp.concatenate([jnp.zeros((n_ctx, RET_QK), F32), sin], 0)
    return cos, sin


def _ret_tables(ret_decay):
    lg = jax.nn.log_sigmoid(ret_decay)
    pos = jnp.arange(RET_CHUNK, dtype=F32)
    diff = pos[:, None] - pos[None, :]
    f_mask, b_mask = diff >= 0, diff < 0
    dm_f = jnp.where(f_mask, jnp.exp(lg[0][:, None, None] * jnp.where(f_mask, diff, 0.0)), 0.0)
    dm_b = jnp.where(b_mask, jnp.exp(lg[1][:, None, None] * jnp.where(b_mask, -diff, 0.0)), 0.0)
    dmat = jnp.stack([dm_f, dm_b])
    heads = lambda t: jnp.repeat(t, RET_DK, axis=-1)
    kdec = jnp.stack([heads(jnp.exp(lg[0][None] * (RET_CHUNK - 1 - pos)[:, None])),
                      heads(jnp.exp(lg[1][None] * pos[:, None]))])
    qdec = jnp.stack([heads(jnp.exp(lg[0][None] * (pos + 1)[:, None])),
                      heads(jnp.exp(lg[1][None] * (RET_CHUNK - pos)[:, None]))])
    blk = jnp.kron(jnp.eye(RET_HEADS, dtype=F32), jnp.ones((RET_DK, RET_DV), F32))
    sdec = jnp.repeat(jnp.exp(lg * RET_CHUNK), RET_DK, axis=-1)[:, :, None] * blk[None]
    return dmat, kdec, qdec, sdec, blk


def _rope(t, cos, sin):
    lane = lax.broadcasted_iota(jnp.int32, t.shape, 1)
    first = (lane % RET_DK) < (RET_DK // 2)
    partner = jnp.where(first, pltpu.roll(t, RET_QK - RET_DK // 2, 1), pltpu.roll(t, RET_DK // 2, 1))
    return t * cos + partner * sin


def _batch_rows(b, steps):
    return pl.ds(b, steps, stride=BATCH)


def _ret_direction(d, b, q_ref, k_ref, v_ref, cos_ref, sin_ref, dm_ref, kd_ref, qd_ref, sd_ref, blk_ref,
                   o_ref, s_scr):
    rows = _batch_rows(b, RET_CHUNK)
    cos, sin = cos_ref[...], sin_ref[...]
    q = _rope(jnp.concatenate([q_ref[0, rows, :], q_ref[1, rows, :]], 1), cos, sin)
    k = _rope(jnp.concatenate([k_ref[0, rows, :], k_ref[1, rows, :]], 1), cos, sin) * (RET_DK ** -0.5)
    v = [v_ref[h, rows, :].astype(BF16) for h in range(RET_HEADS)]
    kb = k.astype(BF16)
    lane = lax.broadcasted_iota(jnp.int32, q.shape, 1)
    si = d * BATCH + b
    state = s_scr[si]
    o_inter = _dot((q * qd_ref[d]).astype(BF16), state.astype(BF16))
    for h in range(RET_HEADS):
        qh = jnp.where(lane // RET_DK == h, q, 0.0).astype(BF16)
        scores = _dot_nt(qh, kb) * dm_ref[d, h]
        o_ref[h, rows, :] = _dot(scores.astype(BF16), v[h]) + o_inter[:, h * RET_DV:(h + 1) * RET_DV]
    kv = _dot((k * kd_ref[d]).T.astype(BF16), jnp.concatenate(v, 1))
    s_scr[si] = sd_ref[d] * state + blk_ref[...] * kv


def _ret_body(qf, kf, vf, cf, sf, qb, kb, vb, cb, sb, dm_ref, kd_ref, qd_ref, sd_ref, blk_ref,
              of_ref, ob_ref, s_scr):
    @pl.when(pl.program_id(0) == 0)
    def _():
        s_scr[...] = jnp.zeros_like(s_scr)

    def per_batch(b, carry):
        _ret_direction(0, b, qf, kf, vf, cf, sf, dm_ref, kd_ref, qd_ref, sd_ref, blk_ref, of_ref, s_scr)
        _ret_direction(1, b, qb, kb, vb, cb, sb, dm_ref, kd_ref, qd_ref, sd_ref, blk_ref, ob_ref, s_scr)
        return carry

    lax.fori_loop(0, BATCH, per_batch, 0)


def _retention(z, cos, sin, tables, n_ctx, n_all):
    dmat, kdec, qdec, sdec, blk = tables
    rows = n_all * BATCH
    blk_rows = RET_CHUNK * BATCH
    ctx_chunks, all_chunks = n_ctx // RET_CHUNK, n_all // RET_CHUNK
    qk_slabs, v_slabs = RET_QK // LANES, HALF_W // LANES
    q0, k0, v0 = EVEN_SPLITS[0] // RET_QK, EVEN_SPLITS[1] // RET_QK, EVEN_SPLITS[2] // HALF_W

    def seq_specs(d):
        ch = lambda i: _scan_chunk(d, i, ctx_chunks, all_chunks)
        return [pl.BlockSpec((qk_slabs, blk_rows, LANES), lambda i: (q0, ch(i), 0)),
                pl.BlockSpec((qk_slabs, blk_rows, LANES), lambda i: (k0, ch(i), 0)),
                pl.BlockSpec((v_slabs, blk_rows, LANES), lambda i: (v0, ch(i), 0)),
                pl.BlockSpec((RET_CHUNK, RET_QK), lambda i: (ch(i), 0)),
                pl.BlockSpec((RET_CHUNK, RET_QK), lambda i: (ch(i), 0))]

    def out_spec(d):
        return pl.BlockSpec((RET_HEADS, blk_rows, LANES),
                            lambda i: (0, _scan_chunk(d, i, ctx_chunks, all_chunks), 0))

    const = lambda nd: (lambda i: (0,) * nd)
    o_shape = jax.ShapeDtypeStruct((RET_HEADS, rows, LANES), F32)
    return pl.pallas_call(
        _ret_body,
        grid=(all_chunks,),
        in_specs=seq_specs(0) + seq_specs(1) + [
            pl.BlockSpec(dmat.shape, const(4)), pl.BlockSpec(kdec.shape, const(3)),
            pl.BlockSpec(qdec.shape, const(3)), pl.BlockSpec(sdec.shape, const(3)),
            pl.BlockSpec(blk.shape, const(2))],
        out_specs=[out_spec(0), out_spec(1)],
        out_shape=[o_shape, o_shape],
        scratch_shapes=[pltpu.VMEM((2 * BATCH, RET_QK, HALF_W), F32)],
        compiler_params=_params("arbitrary"),
        name="retention",
    )(z, z, z, cos, sin, z, z, z, cos, sin, dmat, kdec, qdec, sdec, blk)


def _lru_body(ctx_chunks, all_chunks, prev_ref, x_ref, next_ref, cw_ref, cb_ref, wg_ref, bg_ref, sp_ref,
              h_ref, a_scr, b_scr, h_scr):
    d, i = pl.program_id(0), pl.program_id(1)
    rows = x_ref.shape[1]
    steps = rows // BATCH
    chunk = _scan_chunk(d, i, ctx_chunks, all_chunks)

    @pl.when(i == 0)
    def _():
        h_scr[...] = jnp.zeros_like(h_scr)

    seg_first, seg_last = _segment_edges(chunk, ctx_chunks, all_chunks)
    prev = jnp.where(seg_first, 0.0, prev_ref[...])
    nxt = jnp.where(seg_last, 0.0, next_ref[...])
    slabs = []
    for s in range(LRU_W // LANES):
        sl = slice(s * LANES, (s + 1) * LANES)
        ext = jnp.concatenate([prev[s], x_ref[s], nxt[s]], 0)
        slabs.append(cb_ref[:, sl] + sum(cw_ref[k:k + 1, sl] * ext[k * BATCH:k * BATCH + rows]
                                         for k in range(CONV_K)))
    xs = jnp.concatenate(slabs, 1)
    gates = _dot(xs.astype(BF16), wg_ref[0]) + bg_ref[0]
    r = jax.nn.sigmoid(gates[:, :LRU_W])
    ig = jax.nn.sigmoid(gates[:, LRU_W:])
    log_a = -r * sp_ref[0]
    a = jnp.exp(log_a)
    a_scr[...] = a
    b_scr[...] = jnp.sqrt(1.0 - jnp.exp(2.0 * log_a)) * (ig * xs)

    def step(j, h):
        t = jnp.where(d == 0, j, steps - 1 - j)
        r0 = pl.multiple_of(t * BATCH, BATCH)
        h = a_scr[pl.ds(r0, BATCH), :] * h + b_scr[pl.ds(r0, BATCH), :]
        h_ref[0, pl.ds(r0, BATCH), :] = h
        return h

    h_scr[...] = lax.fori_loop(0, steps, step, h_scr[...], unroll=8)


def _lru(z, conv_w, conv_b, w_a, b_a, w_x, b_x, lam, n_ctx, n_all):
    rows = n_all * BATCH
    blk = LRU_STEPS * BATCH
    ctx_chunks, all_chunks = n_ctx // LRU_STEPS, n_all // LRU_STEPS
    chunk = lambda d, i: _scan_chunk(d, i, ctx_chunks, all_chunks)
    col = ODD_LX // LRU_W
    slabs = LRU_W // LANES
    eye = jnp.eye(LRU_BLOCKS, dtype=F32)
    dense = lambda w: jnp.einsum('dkij,kl->dkilj', w, eye).reshape(2, LRU_W, LRU_W)
    wg = jnp.concatenate([dense(w_a), dense(w_x)], -1).astype(BF16)
    bg = jnp.concatenate([b_a, b_x], -1).reshape(2, 1, 2 * LRU_W)
    sp = (LRU_C * jax.nn.softplus(-lam)).reshape(2, 1, LRU_W)
    halo_prev, halo_next = 2 * BATCH, BATCH
    return pl.pallas_call(
        functools.partial(_lru_body, ctx_chunks, all_chunks),
        grid=(2, all_chunks),
        in_specs=[pl.BlockSpec((slabs, halo_prev, LANES),
                               lambda d, i: (col, jnp.maximum(chunk(d, i) * (blk // halo_prev) - 1, 0), 0)),
                  pl.BlockSpec((slabs, blk, LANES), lambda d, i: (col, chunk(d, i), 0)),
                  pl.BlockSpec((slabs, halo_next, LANES),
                               lambda d, i: (col, jnp.minimum((chunk(d, i) + 1) * (blk // halo_next),
                                                              rows // halo_next - 1), 0)),
                  pl.BlockSpec((CONV_K, LRU_W), lambda d, i: (0, 0)),
                  pl.BlockSpec((1, LRU_W), lambda d, i: (0, 0)),
                  pl.BlockSpec((1, LRU_W, 2 * LRU_W), lambda d, i: (d, 0, 0)),
                  pl.BlockSpec((1, 1, 2 * LRU_W), lambda d, i: (d, 0, 0)),
                  pl.BlockSpec((1, 1, LRU_W), lambda d, i: (d, 0, 0))],
        out_specs=pl.BlockSpec((1, blk, LRU_W), lambda d, i: (d, chunk(d, i), 0)),
        out_shape=jax.ShapeDtypeStruct((2, rows, LRU_W), F32),
        scratch_shapes=[pltpu.VMEM((blk, LRU_W), F32), pltpu.VMEM((blk, LRU_W), F32),
                        pltpu.VMEM((BATCH, LRU_W), F32)],
        compiler_params=_params("arbitrary", "arbitrary"),
        name="rglru",
    )(z, z, z, conv_w, conv_b.reshape(1, LRU_W), wg, bg, sp)


def _residual_norm_route(upd, x_ref, g2_ref, nw_ref, sh_ref, sc_ref, wrh_ref, wrl_ref, rb_ref,
                         xo_ref, hf_ref, lg_ref):
    tm = x_ref.shape[0]
    x3 = x_ref[...].reshape(tm // BATCH, BATCH, D_MODEL) + g2_ref[0] * upd.reshape(tm // BATCH, BATCH, D_MODEL)
    xo_ref[...] = x3.reshape(tm, D_MODEL)
    hf = _norm_modulate(x3, nw_ref[...], sh_ref[0], sc_ref[0]).reshape(tm, D_MODEL)
    for s in range(ROW_CHUNKS):
        hf_ref[:, s, :] = hf[:, s * LANES:(s + 1) * LANES]
    hi = hf.astype(BF16)
    lo = (hf - hi.astype(F32)).astype(BF16)
    lg_ref[...] = _dot(hi, wrh_ref[...]) + _dot(hi, wrl_ref[...]) + _dot(lo, wrh_ref[...]) + rb_ref[...]


def _gelu_tanh(x):
    return 0.5 * x * (1.0 + jnp.tanh(0.7978845608028654 * (x + 0.044715 * x * x * x)))


def _out_body_even(yf_ref, yb_ref, u_ref, d_ref, gw_ref, gb_ref, of_ref, ob_ref, gate_ref, wo_ref, *rest):
    u = jnp.concatenate([u_ref[s] for s in range(HALF_W // LANES)], 1)
    ys = yf_ref[0] + yb_ref[0] + d_ref[...] * u
    y = _gelu_tanh(ys)
    s5o = y * jax.nn.sigmoid(_dot(y.astype(BF16), gw_ref[...]) + gb_ref[...])
    upd = _dot(s5o.astype(BF16), wo_ref[0:HALF_W, :])
    for h in range(RET_HEADS):
        oh = of_ref[h] + ob_ref[h]
        mu = jnp.mean(oh, -1, keepdims=True)
        var = jnp.mean(jnp.square(oh - mu), -1, keepdims=True)
        rh = _silu(gate_ref[h]) * ((oh - mu) * lax.rsqrt(var + NORM_EPS))
        upd = upd + _dot(rh.astype(BF16), wo_ref[HALF_W + h * RET_DV:HALF_W + (h + 1) * RET_DV, :])
    _residual_norm_route(upd, *rest)


def _out_body_odd(gf_ref, gb_ref, zg_ref, gnw_ref, hf_ref, hb_ref, lg_ref, wo_ref, *rest):
    lg = jnp.concatenate([lg_ref[s] for s in range(LRU_W // LANES)], 1)
    lru = (hf_ref[0] + hb_ref[0]) * _gelu_tanh(lg)
    upd = _dot(lru.astype(BF16), wo_ref[HALF_W:, :])
    for h in range(GDN_HEADS):
        oh = gf_ref[h] + gb_ref[h]
        oh = oh * lax.rsqrt(jnp.mean(oh * oh, -1, keepdims=True) + NORM_EPS) * gnw_ref[...] * _silu(zg_ref[h])
        upd = upd + _dot(oh.astype(BF16), wo_ref[h * GDN_DV:(h + 1) * GDN_DV, :])
    _residual_norm_route(upd, *rest)


def _out_proj(body, mix_args, mix_specs, xs, w_out, gate2, norm_w, shift2, scale2, wr_hi, wr_lo, rb,
              row0_tiles, n_tiles, ctx_tiles):
    kind = lambda r: (jnp.where(r + row0_tiles < ctx_tiles, 0, 1), 0, 0)
    const2 = lambda r: (0, 0)
    rows = n_tiles * ROW_TILE
    return pl.pallas_call(
        body,
        grid=(n_tiles,),
        in_specs=mix_specs + [
            pl.BlockSpec((D_MODEL, D_MODEL), const2),
            pl.BlockSpec((ROW_TILE, D_MODEL), lambda r: (r + row0_tiles, 0)),
            pl.BlockSpec((1, BATCH, D_MODEL), kind),
            pl.BlockSpec((1, D_MODEL), const2),
            pl.BlockSpec((1, BATCH, D_MODEL), kind),
            pl.BlockSpec((1, BATCH, D_MODEL), kind),
            pl.BlockSpec((D_MODEL, ROUTER_PAD), const2),
            pl.BlockSpec((D_MODEL, ROUTER_PAD), const2),
            pl.BlockSpec((1, ROUTER_PAD), const2)],
        out_specs=[pl.BlockSpec((ROW_TILE, D_MODEL), lambda r: (r, 0)),
                   pl.BlockSpec((ROW_TILE, ROW_CHUNKS, LANES), lambda r: (r, 0, 0)),
                   pl.BlockSpec((ROW_TILE, ROUTER_PAD), lambda r: (r, 0))],
        out_shape=[jax.ShapeDtypeStruct((rows, D_MODEL), F32),
                   jax.ShapeDtypeStruct((rows, ROW_CHUNKS, LANES), F32),
                   jax.ShapeDtypeStruct((rows, ROUTER_PAD), F32)],
        compiler_params=_params("arbitrary"),
        name="out_proj",
    )(*mix_args, w_out, xs, gate2, norm_w.reshape(1, D_MODEL), shift2, scale2, wr_hi, wr_lo, rb)


def _route(logits):
    n_tok = logits.shape[0]
    g_logits = logits[:, :N_GROUPS]
    g_idx = jnp.argmax(g_logits, -1)
    g_w = jnp.take_along_axis(jax.nn.softmax(g_logits, -1), g_idx[:, None], 1)
    e_logits = logits[:, N_GROUPS:N_GROUPS + N_EXPERTS].reshape(n_tok, N_GROUPS, EXPERTS_PER_GROUP)
    e_in = jnp.take_along_axis(e_logits, g_idx[:, None, None], 1)[:, 0]
    top_v, top_i = lax.top_k(e_in, TOP_K)
    top_w = jax.nn.softmax(top_v, -1) * g_w
    return (g_idx[:, None] * EXPERTS_PER_GROUP + top_i).astype(jnp.int32), top_w


def _dispatch_plan(expert_ids, top_w, n_tiles):
    n_tok = expert_ids.shape[0]
    n_asg = n_tok * TOP_K
    flat_e = expert_ids.T.reshape(n_asg)
    flat_w = top_w.T.reshape(n_asg)
    order = jnp.argsort(flat_e, stable=True).astype(jnp.int32)
    inv = jnp.argsort(order).astype(jnp.int32)
    onehot = flat_e[:, None] == jnp.arange(N_EXPERTS, dtype=jnp.int32)[None]
    counts = jnp.sum(onehot, 0, dtype=jnp.int32)
    padded = ((counts + MOE_TILE - 1) // MOE_TILE) * MOE_TILE
    pend = jnp.cumsum(padded)
    pstart = pend - padded
    start = jnp.cumsum(counts) - counts
    shift = pstart - start
    pos = inv + jnp.sum(jnp.where(onehot, shift[None], 0), 1)
    tile_start = jnp.arange(n_tiles, dtype=jnp.int32) * MOE_TILE
    tile_e = jnp.minimum(jnp.sum(tile_start[:, None] >= pend[None], 1), N_EXPERTS - 1).astype(jnp.int32)
    tile_valid = (tile_start < pend[-1]).astype(jnp.int32)
    row = jnp.arange(n_tiles * MOE_TILE, dtype=jnp.int32)
    row_e = jnp.repeat(tile_e, MOE_TILE)
    rank = row - shift[row_e]
    row_ok = jnp.logical_and(rank >= start[row_e], rank < start[row_e] + counts[row_e])
    src_asg = order[jnp.clip(rank, 0, n_asg - 1)]
    src_tok = jnp.where(row_ok, src_asg % n_tok, 0)
    row_w = jnp.where(row_ok, flat_w[src_asg], 0.0)
    return src_tok, row_w, pos.reshape(TOP_K, n_tok), tile_e, tile_valid


def _gather_rows(idx_ref, src_hbm, dst_vmem, sem, n_rows):
    def issue(j, c):
        for p in range(DMA_PRIORITIES):
            r = j * DMA_PRIORITIES + p
            pltpu.make_async_copy(src_hbm.at[pl.ds(idx_ref[0, 0, r], 1)], dst_vmem.at[pl.ds(r, 1)],
                                  sem).start(priority=p)
        return c

    lax.fori_loop(0, n_rows // DMA_PRIORITIES, issue, 0, unroll=8)


def _wait_rows(src_hbm, dst_vmem, sem, n_rows):
    pltpu.make_async_copy(src_hbm.at[pl.ds(0, n_rows)], dst_vmem, sem).wait()


def _rows_2d(ref):
    return jnp.concatenate([ref[:, s, :] for s in range(ROW_CHUNKS)], 1)


def _expert_body(te_ref, tv_ref, idx_ref, idx_next_ref, hf_hbm, cw_ref, wg32_ref, wu32_ref, wd32_ref, y_ref,
                 xbuf, wg_ref, wu_ref, wd_ref, sem):
    i = pl.program_id(0)
    n = pl.num_programs(0)
    slot = i % 2
    nxt = jnp.minimum(i + 1, n - 1)

    @pl.when(jnp.logical_or(i == 0, te_ref[i] != te_ref[jnp.maximum(i - 1, 0)]))
    def _():
        wg_ref[0] = wg32_ref[0, 0].astype(BF16)
        wu_ref[0] = wu32_ref[0, 0].astype(BF16)
        wd_ref[0] = wd32_ref[0, 0].astype(BF16)

    @pl.when(jnp.logical_and(i == 0, tv_ref[0] == 1))
    def _():
        _gather_rows(idx_ref, hf_hbm, xbuf.at[0], sem.at[0], MOE_TILE)

    @pl.when(jnp.logical_and(i + 1 < n, tv_ref[nxt] == 1))
    def _():
        _gather_rows(idx_next_ref, hf_hbm, xbuf.at[1 - slot], sem.at[1 - slot], MOE_TILE)

    @pl.when(tv_ref[i] == 1)
    def _():
        _wait_rows(hf_hbm, xbuf.at[slot], sem.at[slot], MOE_TILE)
        x = _rows_2d(xbuf.at[slot]).astype(BF16)
        g = _dot(x, wg_ref[0])
        u = _dot(x, wu_ref[0])
        act = _silu(g) * u * cw_ref[:, 0:1]
        y = _dot(act.astype(BF16), wd_ref[0])
        for s in range(ROW_CHUNKS):
            y_ref[:, s, :] = y[:, s * LANES:(s + 1) * LANES]

    @pl.when(tv_ref[i] == 0)
    def _():
        y_ref[...] = jnp.zeros_like(y_ref)


def _experts(layer, tile_e, tile_valid, src_tok, hf, cw_rows, wg, wu, wd):
    n_tiles = tile_e.shape[0]
    tile_spec = pl.BlockSpec((MOE_TILE, ROW_CHUNKS, LANES), lambda i, te, tv: (i, 0, 0))
    idx3 = src_tok.reshape(n_tiles, 1, MOE_TILE)
    grid_spec = pltpu.PrefetchScalarGridSpec(
        num_scalar_prefetch=2,
        grid=(n_tiles,),
        in_specs=[pl.BlockSpec((1, 1, MOE_TILE), lambda i, te, tv: (i, 0, 0), memory_space=pltpu.SMEM),
                  pl.BlockSpec((1, 1, MOE_TILE), lambda i, te, tv: (jnp.minimum(i + 1, n_tiles - 1), 0, 0),
                               memory_space=pltpu.SMEM),
                  pl.BlockSpec(memory_space=pl.ANY),
                  pl.BlockSpec((MOE_TILE, LANES), lambda i, te, tv: (i, 0)),
                  pl.BlockSpec((1, 1, D_MODEL, EXPERT_FF), lambda i, te, tv: (layer, te[i], 0, 0)),
                  pl.BlockSpec((1, 1, D_MODEL, EXPERT_FF), lambda i, te, tv: (layer, te[i], 0, 0)),
                  pl.BlockSpec((1, 1, EXPERT_FF, D_MODEL), lambda i, te, tv: (layer, te[i], 0, 0))],
        out_specs=tile_spec,
        scratch_shapes=[pltpu.VMEM((2, MOE_TILE, ROW_CHUNKS, LANES), F32),
                        pltpu.VMEM((1, D_MODEL, EXPERT_FF), BF16), pltpu.VMEM((1, D_MODEL, EXPERT_FF), BF16),
                        pltpu.VMEM((1, EXPERT_FF, D_MODEL), BF16), pltpu.SemaphoreType.DMA((2,))])
    return pl.pallas_call(
        _expert_body,
        grid_spec=grid_spec,
        out_shape=jax.ShapeDtypeStruct((n_tiles * MOE_TILE, ROW_CHUNKS, LANES), F32),
        compiler_params=_params("arbitrary"),
        name="moe_experts",
    )(tile_e, tile_valid, idx3, idx3, hf, cw_rows, wg, wu, wd)


def _combine_rows(p0_ref, p1_ref, p0n_ref, p1n_ref, y_hbm, x_ref, g5_ref, buf, sem):
    r = pl.program_id(0)
    n = pl.num_programs(0)
    tm = x_ref.shape[0]
    slot = r % 2

    def fetch(p0, p1, s):
        _gather_rows(p0, y_hbm, buf.at[s, 0], sem.at[s, 0], tm)
        _gather_rows(p1, y_hbm, buf.at[s, 1], sem.at[s, 1], tm)

    @pl.when(r == 0)
    def _():
        fetch(p0_ref, p1_ref, 0)

    @pl.when(r + 1 < n)
    def _():
        fetch(p0n_ref, p1n_ref, 1 - slot)

    _wait_rows(y_hbm, buf.at[slot, 0], sem.at[slot, 0], tm)
    _wait_rows(y_hbm, buf.at[slot, 1], sem.at[slot, 1], tm)
    f3 = (_rows_2d(buf.at[slot, 0]) + _rows_2d(buf.at[slot, 1])).reshape(tm // BATCH, BATCH, D_MODEL)
    return (x_ref[...].reshape(tm // BATCH, BATCH, D_MODEL) + g5_ref[0] * f3).reshape(tm, D_MODEL)


def _combine_body(p0_ref, p1_ref, p0n_ref, p1n_ref, y_hbm, x_ref, g5_ref, o_ref, buf, sem):
    o_ref[...] = _combine_rows(p0_ref, p1_ref, p0n_ref, p1n_ref, y_hbm, x_ref, g5_ref, buf, sem)


def _combine_final_body(p0_ref, p1_ref, p0n_ref, p1n_ref, y_hbm, x_ref, g5_ref, fw_ref, o_ref, buf, sem, slab):
    x = _combine_rows(p0_ref, p1_ref, p0n_ref, p1n_ref, y_hbm, x_ref, g5_ref, buf, sem)
    steps = x.shape[0] // BATCH
    xn = x * lax.rsqrt(jnp.mean(x * x, -1, keepdims=True) + NORM_EPS) * fw_ref[...]
    for s in range(ROW_CHUNKS):
        slab[s] = xn[:, s * LANES:(s + 1) * LANES]
    for b in range(BATCH):
        for s in range(ROW_CHUNKS):
            o_ref[b, :, s * LANES:(s + 1) * LANES] = slab[s, _batch_rows(b, steps), :]


def _combine(pos, y_sorted, xs, gate2, row0_tiles, ctx_tiles, final_w=None):
    n_tok = pos.shape[1]
    n_tiles = n_tok // MOE_TILE
    kind = lambda r: (jnp.where(r + row0_tiles < ctx_tiles, 0, 1), 0, 0)
    idx_spec = lambda k, ahead: pl.BlockSpec(
        (1, 1, MOE_TILE), lambda r: (k * n_tiles + jnp.minimum(r + ahead, n_tiles - 1), 0, 0),
        memory_space=pltpu.SMEM)
    p2 = pos.reshape(TOP_K * n_tiles, 1, MOE_TILE)
    in_specs = [idx_spec(0, 0), idx_spec(1, 0), idx_spec(0, 1), idx_spec(1, 1),
                pl.BlockSpec(memory_space=pl.ANY),
                pl.BlockSpec((MOE_TILE, D_MODEL), lambda r: (r, 0)),
                pl.BlockSpec((1, BATCH, D_MODEL), kind)]
    scratch = [pltpu.VMEM((2, TOP_K, MOE_TILE, ROW_CHUNKS, LANES), F32), pltpu.SemaphoreType.DMA((2, TOP_K))]
    if final_w is None:
        return pl.pallas_call(
            _combine_body,
            grid=(n_tiles,),
            in_specs=in_specs,
            out_specs=pl.BlockSpec((MOE_TILE, D_MODEL), lambda r: (r, 0)),
            out_shape=jax.ShapeDtypeStruct((n_tok, D_MODEL), F32),
            scratch_shapes=scratch,
            compiler_params=_params("arbitrary"),
            name="moe_combine",
        )(p2, p2, p2, p2, y_sorted, xs, gate2)
    steps = MOE_TILE // BATCH
    return pl.pallas_call(
        _combine_final_body,
        grid=(n_tiles,),
        in_specs=in_specs + [pl.BlockSpec((1, D_MODEL), lambda r: (0, 0))],
        out_specs=pl.BlockSpec((BATCH, steps, D_MODEL), lambda r: (0, r, 0)),
        out_shape=jax.ShapeDtypeStruct((BATCH, n_tok // BATCH, D_MODEL), F32),
        scratch_shapes=scratch + [pltpu.VMEM((ROW_CHUNKS, MOE_TILE, LANES), F32)],
        compiler_params=_params("arbitrary"),
        name="moe_combine_final",
    )(p2, p2, p2, p2, y_sorted, xs, gate2, final_w.reshape(1, D_MODEL))


def _moe(layer, hf, logits, xs, gate2, wg, wu, wd, row0_tiles, ctx_tiles, final_w=None):
    n_tok = hf.shape[0]
    n_tiles = n_tok * TOP_K // MOE_TILE + N_EXPERTS
    expert_ids, top_w = _route(logits)
    src_tok, row_w, pos, tile_e, tile_valid = _dispatch_plan(expert_ids, top_w, n_tiles)
    cw_rows = jnp.broadcast_to(row_w[:, None], (n_tiles * MOE_TILE, LANES))
    y_sorted = _experts(layer, tile_e, tile_valid, src_tok, hf, cw_rows, wg, wu, wd)
    return _combine(pos, y_sorted, xs, gate2, row0_tiles, ctx_tiles, final_w)


def _kinds(mod_l, k):
    return jnp.stack([mod_l[BATCH:, k], mod_l[:BATCH, k]])


def kernel(x, c, ctx, c_ctx, mod_w, mod_b, norm_mix, norm_ffn, router_group_w, router_group_b,
           router_expert_w, router_expert_b, expert_w_gate, expert_w_up, expert_w_down,
           even_w_in, even_w_out, s5_lam_re, s5_lam_im, s5_log_step, s5_b_re, s5_b_im, s5_c_re,
           s5_c_im, s5_d, s5_glu_w, s5_glu_b, ret_decay, odd_w_in, odd_w_out, gdn_conv_w, gdn_a_log,
           gdn_dt_bias, gdn_norm_w, lru_conv_w, lru_conv_b, lru_w_a, lru_b_a, lru_w_x, lru_b_x,
           lru_lam, final_norm):
    bsz, n_lat, _ = x.shape
    n_ctx = ctx.shape[1]
    n_all = n_ctx + n_lat
    assert bsz == BATCH and n_ctx % RET_CHUNK == 0 and n_lat % RET_CHUNK == 0
    assert (n_ctx * BATCH) % ROW_TILE == 0 and (n_lat * BATCH) % ROW_TILE == 0
    ctx_tiles = n_ctx * BATCH // ROW_TILE
    all_tiles = n_all * BATCH // ROW_TILE
    ctx_moe_tiles = n_ctx * BATCH // MOE_TILE

    xs = None
    cond16 = jnp.concatenate([c, jnp.broadcast_to(c_ctx[None], (BATCH, D_MODEL))], 0)
    mod = _adaln_all(cond16, mod_w, mod_b).reshape(DEPTH, ADALN_ROWS, N_MOD, D_MODEL)

    for layer in range(DEPTH):
        last = layer == DEPTH - 1
        i = layer // 2
        m = [_kinds(mod[layer], k) for k in range(N_MOD)]
        wr = jnp.zeros((D_MODEL, ROUTER_PAD), F32)
        wr = wr.at[:, :N_GROUPS].set(router_group_w[layer]).at[:, N_GROUPS:N_GROUPS + N_EXPERTS].set(
            router_expert_w[layer])
        wr_hi = wr.astype(BF16)
        wr_lo = (wr - wr_hi.astype(F32)).astype(BF16)
        rb = jnp.zeros((1, ROUTER_PAD), F32)
        rb = rb.at[0, :N_GROUPS].set(router_group_b[layer]).at[0, N_GROUPS:N_GROUPS + N_EXPERTS].set(
            router_expert_b[layer])
        row0 = ctx_tiles if last else 0
        n_tiles = all_tiles - row0
        slab_spec = lambda sb: pl.BlockSpec((HALF_W // LANES, ROW_TILE, LANES), lambda r: (sb, r + row0, 0))
        dir_spec = lambda d: pl.BlockSpec((1, ROW_TILE, HALF_W), lambda r: (d, r + row0, 0))
        const2 = lambda r: (0, 0)

        if layer % 2 == 0:
            w_in = even_w_in[i].astype(BF16)
            if layer == 0:
                z, xs = _in_proj_first(ctx, x, norm_mix[layer], m[0], m[1], w_in, ctx_tiles)
            else:
                z = _in_proj(xs, norm_mix[layer], m[0], m[1], w_in, ctx_tiles)
            disc = [_s5_discretize(s5_lam_re[i, d], s5_lam_im[i, d], s5_log_step[i, d], s5_b_re[i, d],
                                   s5_b_im[i, d], s5_c_re[i, d], s5_c_im[i, d]) for d in range(2)]
            bb, cc, a = (jnp.stack(t) for t in zip(*disc))
            ys5 = _s5_scan(z, bb, cc, a, n_ctx, n_all)
            cos, sin = _rope_tables(n_ctx, n_lat)
            r_f, r_b = _retention(z, cos, sin, _ret_tables(ret_decay[i]), n_ctx, n_all)
            mix_args = [ys5, ys5, z, s5_d[i].reshape(1, HALF_W), s5_glu_w[i].astype(BF16),
                        s5_glu_b[i].reshape(1, HALF_W), r_f, r_b, z]
            mix_specs = [dir_spec(0), dir_spec(1), slab_spec(0),
                         pl.BlockSpec((1, HALF_W), const2),
                         pl.BlockSpec((HALF_W, HALF_W), const2),
                         pl.BlockSpec((1, HALF_W), const2),
                         slab_spec(0), slab_spec(0), slab_spec(EVEN_SPLITS[3] // HALF_W)]
            body, w_out = _out_body_even, even_w_out[i]
        else:
            w_in = odd_w_in[i]
            w_pad = jnp.concatenate([
                w_in[:, :ODD_SPLITS[1]],
                w_in[:, ODD_SPLITS[3]:],
                jnp.pad(w_in[:, ODD_SPLITS[1]:ODD_SPLITS[3]], ((0, 0), (0, ODD_PAD - 4 * GDN_HEADS)))], 1)
            z = _in_proj(xs, norm_mix[layer], m[0], m[1], w_pad.astype(BF16), ctx_tiles)
            qkv_act, bg = _gdn_prep(z, gdn_conv_w[i], gdn_a_log[i], gdn_dt_bias[i], n_ctx, n_all)
            g_f, g_b = _gdn(qkv_act, bg, n_ctx, n_all)
            hl = _lru(z, lru_conv_w[i], lru_conv_b[i], lru_w_a[i], lru_b_a[i], lru_w_x[i], lru_b_x[i],
                      lru_lam[i], n_ctx, n_all)
            mix_args = [g_f, g_b, z, gdn_norm_w[i].reshape(1, GDN_DV), hl, hl, z]
            mix_specs = [slab_spec(0), slab_spec(0), slab_spec(ODD_SPLITS[0] // HALF_W),
                         pl.BlockSpec((1, GDN_DV), const2), dir_spec(0), dir_spec(1),
                         slab_spec((ODD_LX + LRU_W) // LRU_W)]
            body, w_out = _out_body_odd, odd_w_out[i]

        xs_mid, hf, logits = _out_proj(body, mix_args, mix_specs, xs, w_out.astype(BF16), m[2],
                                       norm_ffn[layer], m[3], m[4], wr_hi, wr_lo, rb, row0, n_tiles, ctx_tiles)
        moe_row0 = ctx_moe_tiles if last else 0
        xs = _moe(layer, hf, logits, xs_mid, m[5], expert_w_gate, expert_w_up, expert_w_down,
                  moe_row0, ctx_moe_tiles, final_norm if last else None)

    return xs
```

```python
import functools

import jax
import jax.numpy as jnp
from jax import lax
from jax.experimental import pallas as pl
from jax.experimental.pallas import tpu as pltpu

D_MODEL = 1024
DEPTH = 2
GRID_W = 64
HALF_W = D_MODEL // 2
S5_GROUP = 16
S5_GROUPS = HALF_W // S5_GROUP
S5_STATE = 64
RET_HEADS = 4
RET_DV = HALF_W // RET_HEADS
RET_DK = RET_DV // 2
RET_QK = RET_HEADS * RET_DK
RET_CHUNK = 128
ROPE_BASE = 10000.0
GDN_HEADS = 4
GDN_DK = HALF_W // GDN_HEADS
GDN_DV = GDN_DK
GDN_CHUNK = 64
LRU_W = HALF_W
LRU_BLOCKS = 8
LRU_BLOCK = LRU_W // LRU_BLOCKS
LRU_C = 8.0
CONV_K = 4
CONV_PAD_LEFT = 2
N_GROUPS = 4
EXPERTS_PER_GROUP = 8
N_EXPERTS = N_GROUPS * EXPERTS_PER_GROUP
TOP_K = 2
EXPERT_FF = D_MODEL // 2
N_MOD = 6
NORM_EPS = 1e-6
EVEN_SPLITS = [HALF_W, HALF_W + RET_QK, HALF_W + 2 * RET_QK, 2 * HALF_W + 2 * RET_QK]
EVEN_IN = 3 * HALF_W + 2 * RET_QK
ODD_SPLITS = [3 * HALF_W, 4 * HALF_W, 4 * HALF_W + 2 * GDN_HEADS, 4 * HALF_W + 4 * GDN_HEADS,
              4 * HALF_W + 4 * GDN_HEADS + LRU_W]

F32 = jnp.float32
BF16 = jnp.bfloat16

SUBLANES = 8
LANES = 128
BATCH = SUBLANES
ROW_CHUNKS = D_MODEL // LANES
VMEM_LIMIT = 48 * 1024 * 1024

ROW_TILE = 512
S5_STEPS = 64
S5_SLABS = 4
S5_SLAB_CH = HALF_W // S5_SLABS
S5_SLAB_STATE = (S5_GROUPS // S5_SLABS) * S5_STATE
LRU_STEPS = 64
MOE_TILE = 256
DMA_PRIORITIES = 2
ODD_PAD = 512
ODD_LX = 4 * HALF_W
ODD_BD = ODD_LX + 2 * LRU_W
ODD_IN_PADDED = ODD_BD + ODD_PAD
GDN_QKV = 3 * HALF_W
GDN_ROWS = GDN_CHUNK * BATCH
GDN_BG_G = 2 * GDN_HEADS
GDN_NEUMANN_LEVELS = 5
GDN_BATCHES_PER_ITER = 4
ROUTER_PAD = LANES


def _dot(a, b):
    return jnp.dot(a, b, preferred_element_type=F32)


def _dot_nt(a, b):
    return lax.dot_general(a, b, (((1,), (1,)), ((), ())), preferred_element_type=F32)


def _params(*sem):
    return pltpu.CompilerParams(dimension_semantics=sem, vmem_limit_bytes=VMEM_LIMIT)


def _scan_chunk(d, i, ctx_chunks, all_chunks):
    back = jnp.where(i < ctx_chunks, ctx_chunks - 1 - i, all_chunks + ctx_chunks - 1 - i)
    return jnp.where(d == 0, i, back)


def _silu(x):
    return x * jax.nn.sigmoid(x)


def _softplus(x):
    return jnp.maximum(x, 0.0) + jnp.log(1.0 + jnp.exp(-jnp.abs(x)))


def _segment_edges(chunk, ctx_chunks, all_chunks):
    first = jnp.logical_or(chunk == 0, chunk == ctx_chunks)
    last = jnp.logical_or(chunk == ctx_chunks - 1, chunk == all_chunks - 1)
    return first, last


def _gdn_prep_body(ctx_chunks, all_chunks, prev_ref, x_ref, next_ref, bd_ref, cw_ref, al_ref, dtb_ref,
                   qkv_ref, bg_ref, ext_scr):
    rows = x_ref.shape[1]
    steps = rows // BATCH
    halo = CONV_PAD_LEFT * BATCH
    seg_first, seg_last = _segment_edges(pl.program_id(0), ctx_chunks, all_chunks)
    ext_scr[:, 0:halo, :] = jnp.where(seg_first, 0.0, prev_ref[...])
    ext_scr[:, halo:halo + rows, :] = x_ref[...]
    ext_scr[:, halo + rows:, :] = jnp.where(seg_last, 0.0, next_ref[...])
    for j in range(GDN_QKV // LANES):
        sl = slice(j * LANES, (j + 1) * LANES)
        c = sum(cw_ref[k:k + 1, sl] * ext_scr[j, k * BATCH:k * BATCH + rows, :] for k in range(CONV_K))
        a = _silu(c)
        if j < 2 * GDN_HEADS:
            a = a * lax.rsqrt(jnp.sum(a * a, -1, keepdims=True) + NORM_EPS)
        if j < GDN_HEADS:
            a = a * (GDN_DK ** -0.5)
        qkv_ref[j] = a

    bd = bd_ref[0]
    lane = lax.broadcasted_iota(jnp.int32, bd.shape, 1)
    val = jnp.where(lane < GDN_BG_G, jax.nn.sigmoid(bd), al_ref[...] * _softplus(bd + dtb_ref[...]))
    prefix = val
    for k in range((steps - 1).bit_length()):
        sh = BATCH << k
        prefix = prefix + jnp.concatenate([jnp.zeros((sh, LANES), F32), prefix[:rows - sh]], 0)
    p3 = prefix.reshape(steps, BATCH, LANES)
    suffix = (p3[steps - 1][None] - p3 + val.reshape(steps, BATCH, LANES)).reshape(rows, LANES)
    fwd_g = jnp.logical_and(lane >= GDN_BG_G, lane < GDN_BG_G + GDN_HEADS)
    bwd_g = jnp.logical_and(lane >= GDN_BG_G + GDN_HEADS, lane < GDN_BG_G + 2 * GDN_HEADS)
    bg_ref[...] = jnp.where(fwd_g, prefix, jnp.where(bwd_g, suffix, val))


def _gdn_prep(z, conv_w, a_log, dt_bias, n_ctx, n_all):
    rows = n_all * BATCH
    ctx_chunks, all_chunks = n_ctx // GDN_CHUNK, n_all // GDN_CHUNK
    halo_prev, halo_next = CONV_PAD_LEFT * BATCH, (CONV_K - 1 - CONV_PAD_LEFT) * BATCH
    slabs = GDN_QKV // LANES
    lanes16 = lambda t: jnp.zeros((1, LANES), F32).at[0, GDN_BG_G:GDN_BG_G + 2 * GDN_HEADS].set(t.reshape(-1))
    return pl.pallas_call(
        functools.partial(_gdn_prep_body, ctx_chunks, all_chunks),
        grid=(all_chunks,),
        in_specs=[pl.BlockSpec((slabs, halo_prev, LANES),
                               lambda i: (0, jnp.maximum(i * (GDN_ROWS // halo_prev) - 1, 0), 0)),
                  pl.BlockSpec((slabs, GDN_ROWS, LANES), lambda i: (0, i, 0)),
                  pl.BlockSpec((slabs, halo_next, LANES),
                               lambda i: (0, jnp.minimum((i + 1) * (GDN_ROWS // halo_next), rows // halo_next - 1), 0)),
                  pl.BlockSpec((1, GDN_ROWS, LANES), lambda i: (ODD_BD // LANES, i, 0)),
                  pl.BlockSpec((CONV_K, GDN_QKV), lambda i: (0, 0)),
                  pl.BlockSpec((1, LANES), lambda i: (0, 0)),
                  pl.BlockSpec((1, LANES), lambda i: (0, 0))],
        out_specs=[pl.BlockSpec((slabs, GDN_ROWS, LANES), lambda i: (0, i, 0)),
                   pl.BlockSpec((GDN_ROWS, LANES), lambda i: (i, 0))],
        out_shape=[jax.ShapeDtypeStruct((slabs, rows, LANES), F32), jax.ShapeDtypeStruct((rows, LANES), F32)],
        scratch_shapes=[pltpu.VMEM((slabs, GDN_ROWS + halo_prev + halo_next, LANES), F32)],
        compiler_params=_params("arbitrary"),
        name="gdn_prep",
    )(z, z, z, z, conv_w, lanes16(-jnp.exp(a_log)), lanes16(dt_bias))


def _dot_split(a, b):
    ah, bh = a.astype(BF16), b.astype(BF16)
    al, bl = (a - ah.astype(F32)).astype(BF16), (b - bh.astype(F32)).astype(BF16)
    return _dot(ah, bh) + _dot(ah, bl) + _dot(al, bh)


def _gdn_batches(batches, xf_ref, bgf_ref, xb_ref, bgb_ref, of_ref, ob_ref, s_scr):
    ii = lax.broadcasted_iota(jnp.int32, (GDN_CHUNK, GDN_CHUNK), 0)
    jj = lax.broadcasted_iota(jnp.int32, (GDN_CHUNK, GDN_CHUNK), 1)
    eye = jnp.where(ii == jj, 1.0, 0.0)
    units = []
    for b in batches:
        rows = _batch_rows(b, GDN_CHUNK)
        for d, (x_ref, bg_ref, o_ref) in enumerate(((xf_ref, bgf_ref, of_ref), (xb_ref, bgb_ref, ob_ref))):
            bg = bg_ref[rows, :]
            bg_t = bg.T
            incl = (jj <= ii) if d == 0 else (jj >= ii)
            strict = (jj < ii) if d == 0 else (jj > ii)
            last = GDN_CHUNK - 1 if d == 0 else 0
            for h in range(GDN_HEADS):
                q, k, v = x_ref[h, rows, :], x_ref[GDN_HEADS + h, rows, :], x_ref[2 * GDN_HEADS + h, rows, :]
                cb = d * GDN_HEADS + h
                cg = GDN_BG_G + cb
                beta, g_col = bg[:, cb:cb + 1], bg[:, cg:cg + 1]
                g_row, g_last = bg_t[cg:cg + 1, :], bg_t[cg:cg + 1, last:last + 1]
                decay = jnp.where(incl, jnp.exp(jnp.where(incl, g_col - g_row, 0.0)), 0.0)
                kb16 = k.astype(BF16)
                eg = jnp.exp(g_col)
                units.append(dict(
                    si=b * (2 * GDN_HEADS) + cb, h=h, o_ref=o_ref, rows=rows,
                    qk=(_dot_nt(q.astype(BF16), kb16) * decay).astype(BF16),
                    n=jnp.where(strict, -(_dot_nt(kb16, kb16) * decay * beta), 0.0),
                    rhs=jnp.concatenate([v * beta, k * (beta * eg)], 1),
                    qg=(q * eg).astype(BF16),
                    kg_t=(k * jnp.exp(g_last - g_col)).T.astype(BF16),
                    d_last=jnp.exp(g_last)))

    ts = [eye + u['n'] for u in units]
    ps = [u['n'] for u in units]
    for _ in range(GDN_NEUMANN_LEVELS):
        ps = [_dot(p.astype(BF16), p.astype(BF16)) for p in ps]
        ts = [t + _dot(t.astype(BF16), p.astype(BF16)) for t, p in zip(ts, ps)]
    sols = [_dot_split(t, u['rhs']) for t, u in zip(ts, units)]
    states = [s_scr[u['si']] for u in units]
    sbs = [s.astype(BF16) for s in states]
    v_news = [(sol[:, :GDN_DV] - _dot(sol[:, GDN_DV:].astype(BF16), sb)).astype(BF16)
              for sol, sb in zip(sols, sbs)]
    for u, s, sb, vn in zip(units, states, sbs, v_news):
        u['o_ref'][u['h'], u['rows'], :] = _dot(u['qg'], sb) + _dot(u['qk'], vn)
        s_scr[u['si']] = s * u['d_last'] + _dot(u['kg_t'], vn)


def _gdn_body(xf_ref, bgf_ref, xb_ref, bgb_ref, of_ref, ob_ref, s_scr):
    @pl.when(pl.program_id(0) == 0)
    def _():
        s_scr[...] = jnp.zeros_like(s_scr)

    def per_pair(j, carry):
        batches = [j * GDN_BATCHES_PER_ITER + p for p in range(GDN_BATCHES_PER_ITER)]
        _gdn_batches(batches, xf_ref, bgf_ref, xb_ref, bgb_ref, of_ref, ob_ref, s_scr)
        return carry

    lax.fori_loop(0, BATCH // GDN_BATCHES_PER_ITER, per_pair, 0)


def _gdn(qkv, bg, n_ctx, n_all):
    rows = n_all * BATCH
    ctx_chunks, all_chunks = n_ctx // GDN_CHUNK, n_all // GDN_CHUNK
    slabs = GDN_QKV // LANES

    def seq_specs(d):
        ch = lambda i: _scan_chunk(d, i, ctx_chunks, all_chunks)
        return [pl.BlockSpec((slabs, GDN_ROWS, LANES), lambda i: (0, ch(i), 0)),
                pl.BlockSpec((GDN_ROWS, LANES), lambda i: (ch(i), 0))]

    out_spec = lambda d: pl.BlockSpec((GDN_HEADS, GDN_ROWS, LANES),
                                      lambda i: (0, _scan_chunk(d, i, ctx_chunks, all_chunks), 0))
    o_shape = jax.ShapeDtypeStruct((GDN_HEADS, rows, LANES), F32)
    return pl.pallas_call(
        _gdn_body,
        grid=(all_chunks,),
        in_specs=seq_specs(0) + seq_specs(1),
        out_specs=[out_spec(0), out_spec(1)],
        out_shape=[o_shape, o_shape],
        scratch_shapes=[pltpu.VMEM((BATCH * 2 * GDN_HEADS, GDN_DK, GDN_DV), F32)],
        compiler_params=_params("arbitrary"),
        name="gdn_chunks",
    )(qkv, bg, qkv, bg)


ADALN_COLS = 512
ADALN_ROWS = 2 * BATCH


def _adaln_body(c_ref, w_ref, b_ref, o_ref):
    c = c_ref[...]
    s = (c * jax.nn.sigmoid(c)).astype(BF16)
    o_ref[0] = _dot(s, w_ref[0].astype(BF16)) + b_ref[0]


def _adaln_all(cond16, mod_w, mod_b):
    n_out = mod_w.shape[-1]
    return pl.pallas_call(
        _adaln_body,
        grid=(DEPTH, n_out // ADALN_COLS),
        in_specs=[pl.BlockSpec((ADALN_ROWS, D_MODEL), lambda l, j: (0, 0)),
                  pl.BlockSpec((1, D_MODEL, ADALN_COLS), lambda l, j: (l, 0, j)),
                  pl.BlockSpec((1, 1, ADALN_COLS), lambda l, j: (l, 0, j))],
        out_specs=pl.BlockSpec((1, ADALN_ROWS, ADALN_COLS), lambda l, j: (l, 0, j)),
        out_shape=jax.ShapeDtypeStruct((DEPTH, ADALN_ROWS, n_out), F32),
        compiler_params=_params("arbitrary", "arbitrary"),
        name="adaln",
    )(cond16, mod_w, mod_b.reshape(DEPTH, 1, n_out))


def _col_chunks(n, width=512):
    return [(c0, min(width, n - c0)) for c0 in range(0, n, width)]


def _norm_modulate(x3, nw, shift, scale):
    y = x3 * lax.rsqrt(jnp.mean(x3 * x3, -1, keepdims=True) + NORM_EPS) * nw
    return y * (1.0 + scale) + shift


def _project(x, nw_ref, sh_ref, sc_ref, w_ref, o_ref):
    tm = x.shape[0]
    x3 = x.reshape(tm // BATCH, BATCH, D_MODEL)
    h = _norm_modulate(x3, nw_ref[...], sh_ref[0], sc_ref[0]).reshape(tm, D_MODEL).astype(BF16)
    for c0, cw in _col_chunks(w_ref.shape[1]):
        res = _dot(h, w_ref[:, c0:c0 + cw])
        for j in range(cw // LANES):
            o_ref[c0 // LANES + j] = res[:, j * LANES:(j + 1) * LANES]


def _in_proj_body(x_ref, nw_ref, sh_ref, sc_ref, w_ref, o_ref):
    _project(x_ref[...], nw_ref, sh_ref, sc_ref, w_ref, o_ref)


def _in_proj_first_body(ctx_tiles, ctx_ref, lat_ref, nw_ref, sh_ref, sc_ref, w_ref, o_ref, xs_ref, slab):
    steps = ROW_TILE // BATCH

    def load(src_ref):
        for b in range(BATCH):
            for s in range(ROW_CHUNKS):
                slab[s, _batch_rows(b, steps), :] = src_ref[b, :, s * LANES:(s + 1) * LANES]

    @pl.when(pl.program_id(0) < ctx_tiles)
    def _():
        load(ctx_ref)

    @pl.when(pl.program_id(0) >= ctx_tiles)
    def _():
        load(lat_ref)

    x = jnp.concatenate([slab[s] for s in range(ROW_CHUNKS)], 1)
    xs_ref[...] = x
    _project(x, nw_ref, sh_ref, sc_ref, w_ref, o_ref)


def _in_proj_first(ctx, x, norm_w, shift2, scale2, w_bf16, ctx_tiles):
    rows = (ctx.shape[1] + x.shape[1]) * BATCH
    n = w_bf16.shape[1]
    steps = ROW_TILE // BATCH
    kind = lambda r: (jnp.where(r < ctx_tiles, 0, 1), 0, 0)
    return pl.pallas_call(
        functools.partial(_in_proj_first_body, ctx_tiles),
        grid=(rows // ROW_TILE,),
        in_specs=[pl.BlockSpec((BATCH, steps, D_MODEL), lambda r: (0, jnp.minimum(r, ctx_tiles - 1), 0)),
                  pl.BlockSpec((BATCH, steps, D_MODEL), lambda r: (0, jnp.maximum(r - ctx_tiles, 0), 0)),
                  pl.BlockSpec((1, D_MODEL), lambda r: (0, 0)),
                  pl.BlockSpec((1, BATCH, D_MODEL), kind),
                  pl.BlockSpec((1, BATCH, D_MODEL), kind),
                  pl.BlockSpec((D_MODEL, n), lambda r: (0, 0))],
        out_specs=[pl.BlockSpec((n // LANES, ROW_TILE, LANES), lambda r: (0, r, 0)),
                   pl.BlockSpec((ROW_TILE, D_MODEL), lambda r: (r, 0))],
        out_shape=[jax.ShapeDtypeStruct((n // LANES, rows, LANES), F32),
                   jax.ShapeDtypeStruct((rows, D_MODEL), F32)],
        scratch_shapes=[pltpu.VMEM((ROW_CHUNKS, ROW_TILE, LANES), F32)],
        compiler_params=_params("arbitrary"),
        name="in_proj_first",
    )(ctx, x, norm_w.reshape(1, D_MODEL), shift2, scale2, w_bf16)


def _in_proj(xs, norm_w, shift2, scale2, w_bf16, ctx_tiles):
    rows = xs.shape[0]
    n = w_bf16.shape[1]
    kind = lambda r: (jnp.where(r < ctx_tiles, 0, 1), 0, 0)
    return pl.pallas_call(
        _in_proj_body,
        grid=(rows // ROW_TILE,),
        in_specs=[pl.BlockSpec((ROW_TILE, D_MODEL), lambda r: (r, 0)),
                  pl.BlockSpec((1, D_MODEL), lambda r: (0, 0)),
                  pl.BlockSpec((1, BATCH, D_MODEL), kind),
                  pl.BlockSpec((1, BATCH, D_MODEL), kind),
                  pl.BlockSpec((D_MODEL, n), lambda r: (0, 0))],
        out_specs=pl.BlockSpec((n // LANES, ROW_TILE, LANES), lambda r: (0, r, 0)),
        out_shape=jax.ShapeDtypeStruct((n // LANES, rows, LANES), F32),
        compiler_params=_params("arbitrary"),
        name="in_proj",
    )(xs, norm_w.reshape(1, D_MODEL), shift2, scale2, w_bf16)


def _s5_discretize(lam_re, lam_im, log_step, b_re, b_im, c_re, c_im):
    lr, li = lam_re, lam_im
    dt = jnp.exp(log_step)[:, None]
    mag = jnp.exp(lr * dt)
    ar, ai = mag * jnp.cos(li * dt), mag * jnp.sin(li * dt)
    pr, pi = ar - 1.0, ai
    den = lr * lr + li * li
    zr, zi = (pr * lr + pi * li) / den, (pi * lr - pr * li) / den
    bbr = zr[..., None] * b_re - zi[..., None] * b_im
    bbi = zr[..., None] * b_im + zi[..., None] * b_re
    gps = S5_GROUPS // S5_SLABS
    eye = jnp.eye(gps, dtype=F32)

    def in_slab(m):
        m = m.reshape(S5_SLABS, gps, S5_STATE, S5_GROUP)
        return jnp.einsum('sgnc,gh->sgchn', m, eye).reshape(S5_SLABS, S5_SLAB_CH, S5_SLAB_STATE)

    def out_slab(m):
        m = m.reshape(S5_SLABS, gps, S5_GROUP, S5_STATE)
        return jnp.einsum('sgcn,gh->sgnhc', m, eye).reshape(S5_SLABS, S5_SLAB_STATE, S5_SLAB_CH)

    bb = jnp.concatenate([in_slab(bbr), in_slab(bbi)], -1)
    cc = jnp.concatenate([out_slab(c_re), out_slab(-c_im)], 1)
    a = jnp.concatenate([ar.reshape(S5_SLABS, S5_SLAB_STATE), ai.reshape(S5_SLABS, S5_SLAB_STATE)], -1)
    return bb.astype(BF16), cc.astype(BF16), a.reshape(1, S5_SLABS * 2 * S5_SLAB_STATE)


def _s5_body(u_ref, bb_ref, cc_ref, a_ref, y_ref, x_scr, h_scr):
    d = pl.program_id(0)
    rows = u_ref.shape[1]
    steps = rows // BATCH
    sw = 2 * S5_SLAB_STATE

    @pl.when(pl.program_id(1) == 0)
    def _():
        h_scr[...] = jnp.zeros_like(h_scr)

    for s in range(S5_SLABS):
        x_scr[:, s * sw:(s + 1) * sw] = _dot(u_ref[s].astype(BF16), bb_ref[0, s])

    for s in range(S5_SLABS):
        re0, im0 = s * sw, s * sw + S5_SLAB_STATE
        ar = jnp.broadcast_to(a_ref[0, :, re0:im0], (BATCH, S5_SLAB_STATE))
        ai = jnp.broadcast_to(a_ref[0, :, im0:im0 + S5_SLAB_STATE], (BATCH, S5_SLAB_STATE))

        def step(i, carry):
            hr, hi = carry
            t = jnp.where(d == 0, i, steps - 1 - i)
            r0 = pl.multiple_of(t * BATCH, BATCH)
            xr = x_scr[pl.ds(r0, BATCH), re0:im0]
            xi = x_scr[pl.ds(r0, BATCH), im0:im0 + S5_SLAB_STATE]
            nr = ar * hr - ai * hi + xr
            ni = ar * hi + ai * hr + xi
            x_scr[pl.ds(r0, BATCH), re0:im0] = nr
            x_scr[pl.ds(r0, BATCH), im0:im0 + S5_SLAB_STATE] = ni
            return nr, ni

        hr, hi = lax.fori_loop(0, steps, step,
                               (h_scr[:, re0:im0], h_scr[:, im0:im0 + S5_SLAB_STATE]), unroll=4)
        h_scr[:, re0:im0] = hr
        h_scr[:, im0:im0 + S5_SLAB_STATE] = hi

    for s in range(S5_SLABS):
        y_ref[0, :, s * S5_SLAB_CH:(s + 1) * S5_SLAB_CH] = _dot(
            x_scr[:, s * sw:(s + 1) * sw].astype(BF16), cc_ref[0, s])


def _s5_scan(z, bb, cc, a, n_ctx, n_all):
    rows = n_all * BATCH
    blk = S5_STEPS * BATCH
    ctx_chunks, all_chunks = n_ctx // S5_STEPS, n_all // S5_STEPS
    chunk = lambda d, i: _scan_chunk(d, i, ctx_chunks, all_chunks)
    state_w = S5_SLABS * 2 * S5_SLAB_STATE
    return pl.pallas_call(
        _s5_body,
        grid=(2, all_chunks),
        in_specs=[pl.BlockSpec((S5_SLABS, blk, LANES), lambda d, i: (0, chunk(d, i), 0)),
                  pl.BlockSpec((1, S5_SLABS, S5_SLAB_CH, 2 * S5_SLAB_STATE), lambda d, i: (d, 0, 0, 0)),
                  pl.BlockSpec((1, S5_SLABS, 2 * S5_SLAB_STATE, S5_SLAB_CH), lambda d, i: (d, 0, 0, 0)),
                  pl.BlockSpec((1, 1, state_w), lambda d, i: (d, 0, 0))],
        out_specs=pl.BlockSpec((1, blk, HALF_W), lambda d, i: (d, chunk(d, i), 0)),
        out_shape=jax.ShapeDtypeStruct((2, rows, HALF_W), F32),
        scratch_shapes=[pltpu.VMEM((blk, state_w), F32), pltpu.VMEM((BATCH, state_w), F32)],
        compiler_params=_params("arbitrary", "arbitrary"),
        name="s5_scan",
    )(z, bb, cc, a)


def _rope_tables(n_ctx, n_lat):
    half = RET_DK // 2
    nf = half // 2
    grid_rows = n_lat // GRID_W
    row = jnp.repeat(jnp.arange(grid_rows, dtype=F32), GRID_W)
    col = jnp.tile(jnp.arange(GRID_W, dtype=F32), grid_rows)
    inv = ROPE_BASE ** (-jnp.arange(nf, dtype=F32) / nf)
    ang = jnp.concatenate([row[:, None] * inv, col[:, None] * inv], -1)
    cos = jnp.tile(jnp.cos(ang), (1, 2 * RET_HEADS))
    sin = jnp.tile(jnp.concatenate([-jnp.sin(ang), jnp.sin(ang)], -1), (1, RET_HEADS))
    cos = jnp.concatenate([jnp.ones((n_ctx, RET_QK), F32), cos], 0)
    sin = jnp.concatenate([jnp.zeros((n_ctx, RET_QK), F32), sin], 0)
    return cos, sin


def _ret_tables(ret_decay):
    lg = jax.nn.log_sigmoid(ret_decay)
    pos = jnp.arange(RET_CHUNK, dtype=F32)
    diff = pos[:, None] - pos[None, :]
    f_mask, b_mask = diff >= 0, diff < 0
    dm_f = jnp.where(f_mask, jnp.exp(lg[0][:, None, None] * jnp.where(f_mask, diff, 0.0)), 0.0)
    dm_b = jnp.where(b_mask, jnp.exp(lg[1][:, None, None] * jnp.where(b_mask, -diff, 0.0)), 0.0)
    dmat = jnp.stack([dm_f, dm_b])
    heads = lambda t: jnp.repeat(t, RET_DK, axis=-1)
    kdec = jnp.stack([heads(jnp.exp(lg[0][None] * (RET_CHUNK - 1 - pos)[:, None])),
                      heads(jnp.exp(lg[1][None] * pos[:, None]))])
    qdec = jnp.stack([heads(jnp.exp(lg[0][None] * (pos + 1)[:, None])),
                      heads(jnp.exp(lg[1][None] * (RET_CHUNK - pos)[:, None]))])
    blk = jnp.kron(jnp.eye(RET_HEADS, dtype=F32), jnp.ones((RET_DK, RET_DV), F32))
    sdec = jnp.repeat(jnp.exp(lg * RET_CHUNK), RET_DK, axis=-1)[:, :, None] * blk[None]
    return dmat, kdec, qdec, sdec, blk


def _rope(t, cos, sin):
    lane = lax.broadcasted_iota(jnp.int32, t.shape, 1)
    first = (lane % RET_DK) < (RET_DK // 2)
    partner = jnp.where(first, pltpu.roll(t, RET_QK - RET_DK // 2, 1), pltpu.roll(t, RET_DK // 2, 1))
    return t * cos + partner * sin


def _batch_rows(b, steps):
    return pl.ds(b, steps, stride=BATCH)


def _ret_direction(d, b, q_ref, k_ref, v_ref, cos_ref, sin_ref, dm_ref, kd_ref, qd_ref, sd_ref, blk_ref,
                   o_ref, s_scr):
    rows = _batch_rows(b, RET_CHUNK)
    cos, sin = cos_ref[...], sin_ref[...]
    q = _rope(jnp.concatenate([q_ref[0, rows, :], q_ref[1, rows, :]], 1), cos, sin)
    k = _rope(jnp.concatenate([k_ref[0, rows, :], k_ref[1, rows, :]], 1), cos, sin) * (RET_DK ** -0.5)
    v = [v_ref[h, rows, :].astype(BF16) for h in range(RET_HEADS)]
    kb = k.astype(BF16)
    lane = lax.broadcasted_iota(jnp.int32, q.shape, 1)
    si = d * BATCH + b
    state = s_scr[si]
    o_inter = _dot((q * qd_ref[d]).astype(BF16), state.astype(BF16))
    for h in range(RET_HEADS):
        qh = jnp.where(lane // RET_DK == h, q, 0.0).astype(BF16)
        scores = _dot_nt(qh, kb) * dm_ref[d, h]
        o_ref[h, rows, :] = _dot(scores.astype(BF16), v[h]) + o_inter[:, h * RET_DV:(h + 1) * RET_DV]
    kv = _dot((k * kd_ref[d]).T.astype(BF16), jnp.concatenate(v, 1))
    s_scr[si] = sd_ref[d] * state + blk_ref[...] * kv


def _ret_body(qf, kf, vf, cf, sf, qb, kb, vb, cb, sb, dm_ref, kd_ref, qd_ref, sd_ref, blk_ref,
              of_ref, ob_ref, s_scr):
    @pl.when(pl.program_id(0) == 0)
    def _():
        s_scr[...] = jnp.zeros_like(s_scr)

    def per_batch(b, carry):
        _ret_direction(0, b, qf, kf, vf, cf, sf, dm_ref, kd_ref, qd_ref, sd_ref, blk_ref, of_ref, s_scr)
        _ret_direction(1, b, qb, kb, vb, cb, sb, dm_ref, kd_ref, qd_ref, sd_ref, blk_ref, ob_ref, s_scr)
        return carry

    lax.fori_loop(0, BATCH, per_batch, 0)


def _retention(z, cos, sin, tables, n_ctx, n_all):
    dmat, kdec, qdec, sdec, blk = tables
    rows = n_all * BATCH
    blk_rows = RET_CHUNK * BATCH
    ctx_chunks, all_chunks = n_ctx // RET_CHUNK, n_all // RET_CHUNK
    qk_slabs, v_slabs = RET_QK // LANES, HALF_W // LANES
    q0, k0, v0 = EVEN_SPLITS[0] // RET_QK, EVEN_SPLITS[1] // RET_QK, EVEN_SPLITS[2] // HALF_W

    def seq_specs(d):
        ch = lambda i: _scan_chunk(d, i, ctx_chunks, all_chunks)
        return [pl.BlockSpec((qk_slabs, blk_rows, LANES), lambda i: (q0, ch(i), 0)),
                pl.BlockSpec((qk_slabs, blk_rows, LANES), lambda i: (k0, ch(i), 0)),
                pl.BlockSpec((v_slabs, blk_rows, LANES), lambda i: (v0, ch(i), 0)),
                pl.BlockSpec((RET_CHUNK, RET_QK), lambda i: (ch(i), 0)),
                pl.BlockSpec((RET_CHUNK, RET_QK), lambda i: (ch(i), 0))]

    def out_spec(d):
        return pl.BlockSpec((RET_HEADS, blk_rows, LANES),
                            lambda i: (0, _scan_chunk(d, i, ctx_chunks, all_chunks), 0))

    const = lambda nd: (lambda i: (0,) * nd)
    o_shape = jax.ShapeDtypeStruct((RET_HEADS, rows, LANES), F32)
    return pl.pallas_call(
        _ret_body,
        grid=(all_chunks,),
        in_specs=seq_specs(0) + seq_specs(1) + [
            pl.BlockSpec(dmat.shape, const(4)), pl.BlockSpec(kdec.shape, const(3)),
            pl.BlockSpec(qdec.shape, const(3)), pl.BlockSpec(sdec.shape, const(3)),
            pl.BlockSpec(blk.shape, const(2))],
        out_specs=[out_spec(0), out_spec(1)],
        out_shape=[o_shape, o_shape],
        scratch_shapes=[pltpu.VMEM((2 * BATCH, RET_QK, HALF_W), F32)],
        compiler_params=_params("arbitrary"),
        name="retention",
    )(z, z, z, cos, sin, z, z, z, cos, sin, dmat, kdec, qdec, sdec, blk)


def _lru_body(ctx_chunks, all_chunks, prev_ref, x_ref, next_ref, cw_ref, cb_ref, wg_ref, bg_ref, sp_ref,
              h_ref, a_scr, b_scr, h_scr):
    d, i = pl.program_id(0), pl.program_id(1)
    rows = x_ref.shape[1]
    steps = rows // BATCH
    chunk = _scan_chunk(d, i, ctx_chunks, all_chunks)

    @pl.when(i == 0)
    def _():
        h_scr[...] = jnp.zeros_like(h_scr)

    seg_first, seg_last = _segment_edges(chunk, ctx_chunks, all_chunks)
    prev = jnp.where(seg_first, 0.0, prev_ref[...])
    nxt = jnp.where(seg_last, 0.0, next_ref[...])
    slabs = []
    for s in range(LRU_W // LANES):
        sl = slice(s * LANES, (s + 1) * LANES)
        ext = jnp.concatenate([prev[s], x_ref[s], nxt[s]], 0)
        slabs.append(cb_ref[:, sl] + sum(cw_ref[k:k + 1, sl] * ext[k * BATCH:k * BATCH + rows]
                                         for k in range(CONV_K)))
    xs = jnp.concatenate(slabs, 1)
    gates = _dot(xs.astype(BF16), wg_ref[0]) + bg_ref[0]
    r = jax.nn.sigmoid(gates[:, :LRU_W])
    ig = jax.nn.sigmoid(gates[:, LRU_W:])
    log_a = -r * sp_ref[0]
    a = jnp.exp(log_a)
    a_scr[...] = a
    b_scr[...] = jnp.sqrt(1.0 - jnp.exp(2.0 * log_a)) * (ig * xs)

    def step(j, h):
        t = jnp.where(d == 0, j, steps - 1 - j)
        r0 = pl.multiple_of(t * BATCH, BATCH)
        h = a_scr[pl.ds(r0, BATCH), :] * h + b_scr[pl.ds(r0, BATCH), :]
        h_ref[0, pl.ds(r0, BATCH), :] = h
        return h

    h_scr[...] = lax.fori_loop(0, steps, step, h_scr[...], unroll=8)


def _lru(z, conv_w, conv_b, w_a, b_a, w_x, b_x, lam, n_ctx, n_all):
    rows = n_all * BATCH
    blk = LRU_STEPS * BATCH
    ctx_chunks, all_chunks = n_ctx // LRU_STEPS, n_all // LRU_STEPS
    chunk = lambda d, i: _scan_chunk(d, i, ctx_chunks, all_chunks)
    col = ODD_LX // LRU_W
    slabs = LRU_W // LANES
    eye = jnp.eye(LRU_BLOCKS, dtype=F32)
    dense = lambda w: jnp.einsum('dkij,kl->dkilj', w, eye).reshape(2, LRU_W, LRU_W)
    wg = jnp.concatenate([dense(w_a), dense(w_x)], -1).astype(BF16)
    bg = jnp.concatenate([b_a, b_x], -1).reshape(2, 1, 2 * LRU_W)
    sp = (LRU_C * jax.nn.softplus(-lam)).reshape(2, 1, LRU_W)
    halo_prev, halo_next = 2 * BATCH, BATCH
    return pl.pallas_call(
        functools.partial(_lru_body, ctx_chunks, all_chunks),
        grid=(2, all_chunks),
        in_specs=[pl.BlockSpec((slabs, halo_prev, LANES),
                               lambda d, i: (col, jnp.maximum(chunk(d, i) * (blk // halo_prev) - 1, 0), 0)),
                  pl.BlockSpec((slabs, blk, LANES), lambda d, i: (col, chunk(d, i), 0)),
                  pl.BlockSpec((slabs, halo_next, LANES),
                               lambda d, i: (col, jnp.minimum((chunk(d, i) + 1) * (blk // halo_next),
                                                              rows // halo_next - 1), 0)),
                  pl.BlockSpec((CONV_K, LRU_W), lambda d, i: (0, 0)),
                  pl.BlockSpec((1, LRU_W), lambda d, i: (0, 0)),
                  pl.BlockSpec((1, LRU_W, 2 * LRU_W), lambda d, i: (d, 0, 0)),
                  pl.BlockSpec((1, 1, 2 * LRU_W), lambda d, i: (d, 0, 0)),
                  pl.BlockSpec((1, 1, LRU_W), lambda d, i: (d, 0, 0))],
        out_specs=pl.BlockSpec((1, blk, LRU_W), lambda d, i: (d, chunk(d, i), 0)),
        out_shape=jax.ShapeDtypeStruct((2, rows, LRU_W), F32),
        scratch_shapes=[pltpu.VMEM((blk, LRU_W), F32), pltpu.VMEM((blk, LRU_W), F32),
                        pltpu.VMEM((BATCH, LRU_W), F32)],
        compiler_params=_params("arbitrary", "arbitrary"),
        name="rglru",
    )(z, z, z, conv_w, conv_b.reshape(1, LRU_W), wg, bg, sp)


def _residual_norm_route(upd, x_ref, g2_ref, nw_ref, sh_ref, sc_ref, wrh_ref, wrl_ref, rb_ref,
                         xo_ref, hf_ref, lg_ref):
    tm = x_ref.shape[0]
    x3 = x_ref[...].reshape(tm // BATCH, BATCH, D_MODEL) + g2_ref[0] * upd.reshape(tm // BATCH, BATCH, D_MODEL)
    xo_ref[...] = x3.reshape(tm, D_MODEL)
    hf = _norm_modulate(x3, nw_ref[...], sh_ref[0], sc_ref[0]).reshape(tm, D_MODEL)
    _store_rows(hf_ref, hf)
    hi = hf.astype(BF16)
    lo = (hf - hi.astype(F32)).astype(BF16)
    lg_ref[...] = _dot(hi, wrh_ref[...]) + _dot(hi, wrl_ref[...]) + _dot(lo, wrh_ref[...]) + rb_ref[...]


def _gelu_tanh(x):
    return 0.5 * x * (1.0 + jnp.tanh(0.7978845608028654 * (x + 0.044715 * x * x * x)))


def _out_body_even(yf_ref, yb_ref, u_ref, d_ref, gw_ref, gb_ref, of_ref, ob_ref, gate_ref, wo_ref, *rest):
    u = jnp.concatenate([u_ref[s] for s in range(HALF_W // LANES)], 1)
    ys = yf_ref[0] + yb_ref[0] + d_ref[...] * u
    y = _gelu_tanh(ys)
    s5o = y * jax.nn.sigmoid(_dot(y.astype(BF16), gw_ref[...]) + gb_ref[...])
    upd = _dot(s5o.astype(BF16), wo_ref[0:HALF_W, :])
    for h in range(RET_HEADS):
        oh = of_ref[h] + ob_ref[h]
        mu = jnp.mean(oh, -1, keepdims=True)
        var = jnp.mean(jnp.square(oh - mu), -1, keepdims=True)
        rh = _silu(gate_ref[h]) * ((oh - mu) * lax.rsqrt(var + NORM_EPS))
        upd = upd + _dot(rh.astype(BF16), wo_ref[HALF_W + h * RET_DV:HALF_W + (h + 1) * RET_DV, :])
    _residual_norm_route(upd, *rest)


def _out_body_odd(gf_ref, gb_ref, zg_ref, gnw_ref, hf_ref, hb_ref, lg_ref, wo_ref, *rest):
    lg = jnp.concatenate([lg_ref[s] for s in range(LRU_W // LANES)], 1)
    lru = (hf_ref[0] + hb_ref[0]) * _gelu_tanh(lg)
    upd = _dot(lru.astype(BF16), wo_ref[HALF_W:, :])
    for h in range(GDN_HEADS):
        oh = gf_ref[h] + gb_ref[h]
        oh = oh * lax.rsqrt(jnp.mean(oh * oh, -1, keepdims=True) + NORM_EPS) * gnw_ref[...] * _silu(zg_ref[h])
        upd = upd + _dot(oh.astype(BF16), wo_ref[h * GDN_DV:(h + 1) * GDN_DV, :])
    _residual_norm_route(upd, *rest)


def _out_proj(body, mix_args, mix_specs, xs, w_out, gate2, norm_w, shift2, scale2, wr_hi, wr_lo, rb,
              row0_tiles, n_tiles, ctx_tiles):
    kind = lambda r: (jnp.where(r + row0_tiles < ctx_tiles, 0, 1), 0, 0)
    const2 = lambda r: (0, 0)
    rows = n_tiles * ROW_TILE
    return pl.pallas_call(
        body,
        grid=(n_tiles,),
        in_specs=mix_specs + [
            pl.BlockSpec((D_MODEL, D_MODEL), const2),
            pl.BlockSpec((ROW_TILE, D_MODEL), lambda r: (r + row0_tiles, 0)),
            pl.BlockSpec((1, BATCH, D_MODEL), kind),
            pl.BlockSpec((1, D_MODEL), const2),
            pl.BlockSpec((1, BATCH, D_MODEL), kind),
            pl.BlockSpec((1, BATCH, D_MODEL), kind),
            pl.BlockSpec((D_MODEL, ROUTER_PAD), const2),
            pl.BlockSpec((D_MODEL, ROUTER_PAD), const2),
            pl.BlockSpec((1, ROUTER_PAD), const2)],
        out_specs=[pl.BlockSpec((ROW_TILE, D_MODEL), lambda r: (r, 0)),
                   pl.BlockSpec((ROW_TILE * ROW_CHUNKS, LANES), lambda r: (r, 0)),
                   pl.BlockSpec((ROW_TILE, ROUTER_PAD), lambda r: (r, 0))],
        out_shape=[jax.ShapeDtypeStruct((rows, D_MODEL), F32),
                   jax.ShapeDtypeStruct((rows * ROW_CHUNKS, LANES), F32),
                   jax.ShapeDtypeStruct((rows, ROUTER_PAD), F32)],
        compiler_params=_params("arbitrary"),
        name="out_proj",
    )(*mix_args, w_out, xs, gate2, norm_w.reshape(1, D_MODEL), shift2, scale2, wr_hi, wr_lo, rb)


def _route(logits):
    n_tok = logits.shape[0]
    g_logits = logits[:, :N_GROUPS]
    g_idx = jnp.argmax(g_logits, -1)
    g_w = jnp.take_along_axis(jax.nn.softmax(g_logits, -1), g_idx[:, None], 1)
    e_logits = logits[:, N_GROUPS:N_GROUPS + N_EXPERTS].reshape(n_tok, N_GROUPS, EXPERTS_PER_GROUP)
    e_in = jnp.take_along_axis(e_logits, g_idx[:, None, None], 1)[:, 0]
    top_v, top_i = lax.top_k(e_in, TOP_K)
    top_w = jax.nn.softmax(top_v, -1) * g_w
    return (g_idx[:, None] * EXPERTS_PER_GROUP + top_i).astype(jnp.int32), top_w


def _dispatch_plan(expert_ids, top_w, n_tiles):
    n_tok = expert_ids.shape[0]
    n_asg = n_tok * TOP_K
    flat_e = expert_ids.T.reshape(n_asg)
    flat_w = top_w.T.reshape(n_asg)
    order = jnp.argsort(flat_e, stable=True).astype(jnp.int32)
    inv = jnp.argsort(order).astype(jnp.int32)
    onehot = flat_e[:, None] == jnp.arange(N_EXPERTS, dtype=jnp.int32)[None]
    counts = jnp.sum(onehot, 0, dtype=jnp.int32)
    padded = ((counts + MOE_TILE - 1) // MOE_TILE) * MOE_TILE
    pend = jnp.cumsum(padded)
    pstart = pend - padded
    start = jnp.cumsum(counts) - counts
    shift = pstart - start
    pos = inv + jnp.sum(jnp.where(onehot, shift[None], 0), 1)
    tile_start = jnp.arange(n_tiles, dtype=jnp.int32) * MOE_TILE
    tile_e = jnp.minimum(jnp.sum(tile_start[:, None] >= pend[None], 1), N_EXPERTS - 1).astype(jnp.int32)
    tile_valid = (tile_start < pend[-1]).astype(jnp.int32)
    row = jnp.arange(n_tiles * MOE_TILE, dtype=jnp.int32)
    row_e = jnp.repeat(tile_e, MOE_TILE)
    rank = row - shift[row_e]
    row_ok = jnp.logical_and(rank >= start[row_e], rank < start[row_e] + counts[row_e])
    src_asg = order[jnp.clip(rank, 0, n_asg - 1)]
    src_tok = jnp.where(row_ok, src_asg % n_tok, 0)
    row_w = jnp.where(row_ok, flat_w[src_asg], 0.0)
    return src_tok, row_w, pos.reshape(TOP_K, n_tok), tile_e, tile_valid


def _gather_rows(idx_ref, src_hbm, dst_vmem, sem, n_rows):
    def issue(j, c):
        for p in range(DMA_PRIORITIES):
            r = j * DMA_PRIORITIES + p
            src = pl.multiple_of(idx_ref[0, 0, r] * ROW_CHUNKS, ROW_CHUNKS)
            dst = pl.multiple_of(r * ROW_CHUNKS, ROW_CHUNKS)
            pltpu.make_async_copy(src_hbm.at[pl.ds(src, ROW_CHUNKS)], dst_vmem.at[pl.ds(dst, ROW_CHUNKS)],
                                  sem).start(priority=p)
        return c

    lax.fori_loop(0, n_rows // DMA_PRIORITIES, issue, 0, unroll=8)


def _wait_rows(src_hbm, dst_vmem, sem, n_rows):
    pltpu.make_async_copy(src_hbm.at[pl.ds(0, n_rows * ROW_CHUNKS)], dst_vmem, sem).wait()


def _rows_2d(ref, n_rows):
    return jnp.concatenate([ref[pl.ds(s, n_rows, stride=ROW_CHUNKS), :] for s in range(ROW_CHUNKS)], 1)


def _store_rows(ref, val):
    for s in range(ROW_CHUNKS):
        ref[pl.ds(s, val.shape[0], stride=ROW_CHUNKS), :] = val[:, s * LANES:(s + 1) * LANES]


def _expert_body(te_ref, tv_ref, idx_ref, idx_next_ref, hf_hbm, cw_ref, wg32_ref, wu32_ref, wd32_ref, y_ref,
                 xbuf, wg_ref, wu_ref, wd_ref, sem):
    i = pl.program_id(0)
    n = pl.num_programs(0)
    slot = i % 2
    nxt = jnp.minimum(i + 1, n - 1)

    @pl.when(jnp.logical_or(i == 0, te_ref[i] != te_ref[jnp.maximum(i - 1, 0)]))
    def _():
        wg_ref[0] = wg32_ref[0, 0].astype(BF16)
        wu_ref[0] = wu32_ref[0, 0].astype(BF16)
        wd_ref[0] = wd32_ref[0, 0].astype(BF16)

    @pl.when(jnp.logical_and(i == 0, tv_ref[0] == 1))
    def _():
        _gather_rows(idx_ref, hf_hbm, xbuf.at[0], sem.at[0], MOE_TILE)

    @pl.when(jnp.logical_and(i + 1 < n, tv_ref[nxt] == 1))
    def _():
        _gather_rows(idx_next_ref, hf_hbm, xbuf.at[1 - slot], sem.at[1 - slot], MOE_TILE)

    @pl.when(tv_ref[i] == 1)
    def _():
        _wait_rows(hf_hbm, xbuf.at[slot], sem.at[slot], MOE_TILE)
        x = _rows_2d(xbuf.at[slot], MOE_TILE).astype(BF16)
        g = _dot(x, wg_ref[0])
        u = _dot(x, wu_ref[0])
        act = _silu(g) * u * cw_ref[:, 0:1]
        _store_rows(y_ref, _dot(act.astype(BF16), wd_ref[0]))

    @pl.when(tv_ref[i] == 0)
    def _():
        y_ref[...] = jnp.zeros_like(y_ref)


def _experts(layer, tile_e, tile_valid, src_tok, hf, cw_rows, wg, wu, wd):
    n_tiles = tile_e.shape[0]
    tile_spec = pl.BlockSpec((MOE_TILE * ROW_CHUNKS, LANES), lambda i, te, tv: (i, 0))
    idx3 = src_tok.reshape(n_tiles, 1, MOE_TILE)
    grid_spec = pltpu.PrefetchScalarGridSpec(
        num_scalar_prefetch=2,
        grid=(n_tiles,),
        in_specs=[pl.BlockSpec((1, 1, MOE_TILE), lambda i, te, tv: (i, 0, 0), memory_space=pltpu.SMEM),
                  pl.BlockSpec((1, 1, MOE_TILE), lambda i, te, tv: (jnp.minimum(i + 1, n_tiles - 1), 0, 0),
                               memory_space=pltpu.SMEM),
                  pl.BlockSpec(memory_space=pl.ANY),
                  pl.BlockSpec((MOE_TILE, LANES), lambda i, te, tv: (i, 0)),
                  pl.BlockSpec((1, 1, D_MODEL, EXPERT_FF), lambda i, te, tv: (layer, te[i], 0, 0)),
                  pl.BlockSpec((1, 1, D_MODEL, EXPERT_FF), lambda i, te, tv: (layer, te[i], 0, 0)),
                  pl.BlockSpec((1, 1, EXPERT_FF, D_MODEL), lambda i, te, tv: (layer, te[i], 0, 0))],
        out_specs=tile_spec,
        scratch_shapes=[pltpu.VMEM((2, MOE_TILE * ROW_CHUNKS, LANES), F32),
                        pltpu.VMEM((1, D_MODEL, EXPERT_FF), BF16), pltpu.VMEM((1, D_MODEL, EXPERT_FF), BF16),
                        pltpu.VMEM((1, EXPERT_FF, D_MODEL), BF16), pltpu.SemaphoreType.DMA((2,))])
    return pl.pallas_call(
        _expert_body,
        grid_spec=grid_spec,
        out_shape=jax.ShapeDtypeStruct((n_tiles * MOE_TILE * ROW_CHUNKS, LANES), F32),
        compiler_params=_params("arbitrary"),
        name="moe_experts",
    )(tile_e, tile_valid, idx3, idx3, hf, cw_rows, wg, wu, wd)


def _combine_rows(p0_ref, p1_ref, p0n_ref, p1n_ref, y_hbm, x_ref, g5_ref, buf, sem):
    r = pl.program_id(0)
    n = pl.num_programs(0)
    tm = x_ref.shape[0]
    slot = r % 2

    def fetch(p0, p1, s):
        _gather_rows(p0, y_hbm, buf.at[s, 0], sem.at[s, 0], tm)
        _gather_rows(p1, y_hbm, buf.at[s, 1], sem.at[s, 1], tm)

    @pl.when(r == 0)
    def _():
        fetch(p0_ref, p1_ref, 0)

    @pl.when(r + 1 < n)
    def _():
        fetch(p0n_ref, p1n_ref, 1 - slot)

    _wait_rows(y_hbm, buf.at[slot, 0], sem.at[slot, 0], tm)
    _wait_rows(y_hbm, buf.at[slot, 1], sem.at[slot, 1], tm)
    f3 = (_rows_2d(buf.at[slot, 0], tm) + _rows_2d(buf.at[slot, 1], tm)).reshape(tm // BATCH, BATCH, D_MODEL)
    return (x_ref[...].reshape(tm // BATCH, BATCH, D_MODEL) + g5_ref[0] * f3).reshape(tm, D_MODEL)


def _combine_body(p0_ref, p1_ref, p0n_ref, p1n_ref, y_hbm, x_ref, g5_ref, o_ref, buf, sem):
    o_ref[...] = _combine_rows(p0_ref, p1_ref, p0n_ref, p1n_ref, y_hbm, x_ref, g5_ref, buf, sem)


def _combine_final_body(p0_ref, p1_ref, p0n_ref, p1n_ref, y_hbm, x_ref, g5_ref, fw_ref, o_ref, buf, sem, slab):
    x = _combine_rows(p0_ref, p1_ref, p0n_ref, p1n_ref, y_hbm, x_ref, g5_ref, buf, sem)
    steps = x.shape[0] // BATCH
    xn = x * lax.rsqrt(jnp.mean(x * x, -1, keepdims=True) + NORM_EPS) * fw_ref[...]
    for s in range(ROW_CHUNKS):
        slab[s] = xn[:, s * LANES:(s + 1) * LANES]
    for b in range(BATCH):
        for s in range(ROW_CHUNKS):
            o_ref[b, :, s * LANES:(s + 1) * LANES] = slab[s, _batch_rows(b, steps), :]


def _combine(pos, y_sorted, xs, gate2, row0_tiles, ctx_tiles, final_w=None):
    n_tok = pos.shape[1]
    n_tiles = n_tok // MOE_TILE
    kind = lambda r: (jnp.where(r + row0_tiles < ctx_tiles, 0, 1), 0, 0)
    idx_spec = lambda k, ahead: pl.BlockSpec(
        (1, 1, MOE_TILE), lambda r: (k * n_tiles + jnp.minimum(r + ahead, n_tiles - 1), 0, 0),
        memory_space=pltpu.SMEM)
    p2 = pos.reshape(TOP_K * n_tiles, 1, MOE_TILE)
    in_specs = [idx_spec(0, 0), idx_spec(1, 0), idx_spec(0, 1), idx_spec(1, 1),
                pl.BlockSpec(memory_space=pl.ANY),
                pl.BlockSpec((MOE_TILE, D_MODEL), lambda r: (r, 0)),
                pl.BlockSpec((1, BATCH, D_MODEL), kind)]
    scratch = [pltpu.VMEM((2, TOP_K, MOE_TILE * ROW_CHUNKS, LANES), F32), pltpu.SemaphoreType.DMA((2, TOP_K))]
    if final_w is None:
        return pl.pallas_call(
            _combine_body,
            grid=(n_tiles,),
            in_specs=in_specs,
            out_specs=pl.BlockSpec((MOE_TILE, D_MODEL), lambda r: (r, 0)),
            out_shape=jax.ShapeDtypeStruct((n_tok, D_MODEL), F32),
            scratch_shapes=scratch,
            compiler_params=_params("arbitrary"),
            name="moe_combine",
        )(p2, p2, p2, p2, y_sorted, xs, gate2)
    steps = MOE_TILE // BATCH
    return pl.pallas_call(
        _combine_final_body,
        grid=(n_tiles,),
        in_specs=in_specs + [pl.BlockSpec((1, D_MODEL), lambda r: (0, 0))],
        out_specs=pl.BlockSpec((BATCH, steps, D_MODEL), lambda r: (0, r, 0)),
        out_shape=jax.ShapeDtypeStruct((BATCH, n_tok // BATCH, D_MODEL), F32),
        scratch_shapes=scratch + [pltpu.VMEM((ROW_CHUNKS, MOE_TILE, LANES), F32)],
        compiler_params=_params("arbitrary"),
        name="moe_combine_final",
    )(p2, p2, p2, p2, y_sorted, xs, gate2, final_w.reshape(1, D_MODEL))


def _moe(layer, hf, logits, xs, gate2, wg, wu, wd, row0_tiles, ctx_tiles, final_w=None):
    n_tok = hf.shape[0] // ROW_CHUNKS
    n_tiles = n_tok * TOP_K // MOE_TILE + N_EXPERTS
    expert_ids, top_w = _route(logits)
    src_tok, row_w, pos, tile_e, tile_valid = _dispatch_plan(expert_ids, top_w, n_tiles)
    cw_rows = jnp.broadcast_to(row_w[:, None], (n_tiles * MOE_TILE, LANES))
    y_sorted = _experts(layer, tile_e, tile_valid, src_tok, hf, cw_rows, wg, wu, wd)
    return _combine(pos, y_sorted, xs, gate2, row0_tiles, ctx_tiles, final_w)


def _kinds(mod_l, k):
    return jnp.stack([mod_l[BATCH:, k], mod_l[:BATCH, k]])


def kernel(x, c, ctx, c_ctx, mod_w, mod_b, norm_mix, norm_ffn, router_group_w, router_group_b,
           router_expert_w, router_expert_b, expert_w_gate, expert_w_up, expert_w_down,
           even_w_in, even_w_out, s5_lam_re, s5_lam_im, s5_log_step, s5_b_re, s5_b_im, s5_c_re,
           s5_c_im, s5_d, s5_glu_w, s5_glu_b, ret_decay, odd_w_in, odd_w_out, gdn_conv_w, gdn_a_log,
           gdn_dt_bias, gdn_norm_w, lru_conv_w, lru_conv_b, lru_w_a, lru_b_a, lru_w_x, lru_b_x,
           lru_lam, final_norm):
    bsz, n_lat, _ = x.shape
    n_ctx = ctx.shape[1]
    n_all = n_ctx + n_lat
    assert bsz == BATCH and n_ctx % RET_CHUNK == 0 and n_lat % RET_CHUNK == 0
    assert (n_ctx * BATCH) % ROW_TILE == 0 and (n_lat * BATCH) % ROW_TILE == 0
    ctx_tiles = n_ctx * BATCH // ROW_TILE
    all_tiles = n_all * BATCH // ROW_TILE
    ctx_moe_tiles = n_ctx * BATCH // MOE_TILE

    xs = None
    cond16 = jnp.concatenate([c, jnp.broadcast_to(c_ctx[None], (BATCH, D_MODEL))], 0)
    mod = _adaln_all(cond16, mod_w, mod_b).reshape(DEPTH, ADALN_ROWS, N_MOD, D_MODEL)

    for layer in range(DEPTH):
        last = layer == DEPTH - 1
        i = layer // 2
        m = [_kinds(mod[layer], k) for k in range(N_MOD)]
        wr = jnp.zeros((D_MODEL, ROUTER_PAD), F32)
        wr = wr.at[:, :N_GROUPS].set(router_group_w[layer]).at[:, N_GROUPS:N_GROUPS + N_EXPERTS].set(
            router_expert_w[layer])
        wr_hi = wr.astype(BF16)
        wr_lo = (wr - wr_hi.astype(F32)).astype(BF16)
        rb = jnp.zeros((1, ROUTER_PAD), F32)
        rb = rb.at[0, :N_GROUPS].set(router_group_b[layer]).at[0, N_GROUPS:N_GROUPS + N_EXPERTS].set(
            router_expert_b[layer])
        row0 = ctx_tiles if last else 0
        n_tiles = all_tiles - row0
        slab_spec = lambda sb: pl.BlockSpec((HALF_W // LANES, ROW_TILE, LANES), lambda r: (sb, r + row0, 0))
        dir_spec = lambda d: pl.BlockSpec((1, ROW_TILE, HALF_W), lambda r: (d, r + row0, 0))
        const2 = lambda r: (0, 0)

        if layer % 2 == 0:
            w_in = even_w_in[i].astype(BF16)
            if layer == 0:
                z, xs = _in_proj_first(ctx, x, norm_mix[layer], m[0], m[1], w_in, ctx_tiles)
            else:
                z = _in_proj(xs, norm_mix[layer], m[0], m[1], w_in, ctx_tiles)
            disc = [_s5_discretize(s5_lam_re[i, d], s5_lam_im[i, d], s5_log_step[i, d], s5_b_re[i, d],
                                   s5_b_im[i, d], s5_c_re[i, d], s5_c_im[i, d]) for d in range(2)]
            bb, cc, a = (jnp.stack(t) for t in zip(*disc))
            ys5 = _s5_scan(z, bb, cc, a, n_ctx, n_all)
            cos, sin = _rope_tables(n_ctx, n_lat)
            r_f, r_b = _retention(z, cos, sin, _ret_tables(ret_decay[i]), n_ctx, n_all)
            mix_args = [ys5, ys5, z, s5_d[i].reshape(1, HALF_W), s5_glu_w[i].astype(BF16),
                        s5_glu_b[i].reshape(1, HALF_W), r_f, r_b, z]
            mix_specs = [dir_spec(0), dir_spec(1), slab_spec(0),
                         pl.BlockSpec((1, HALF_W), const2),
                         pl.BlockSpec((HALF_W, HALF_W), const2),
                         pl.BlockSpec((1, HALF_W), const2),
                         slab_spec(0), slab_spec(0), slab_spec(EVEN_SPLITS[3] // HALF_W)]
            body, w_out = _out_body_even, even_w_out[i]
        else:
            w_in = odd_w_in[i]
            w_pad = jnp.concatenate([
                w_in[:, :ODD_SPLITS[1]],
                w_in[:, ODD_SPLITS[3]:],
                jnp.pad(w_in[:, ODD_SPLITS[1]:ODD_SPLITS[3]], ((0, 0), (0, ODD_PAD - 4 * GDN_HEADS)))], 1)
            z = _in_proj(xs, norm_mix[layer], m[0], m[1], w_pad.astype(BF16), ctx_tiles)
            qkv_act, bg = _gdn_prep(z, gdn_conv_w[i], gdn_a_log[i], gdn_dt_bias[i], n_ctx, n_all)
            g_f, g_b = _gdn(qkv_act, bg, n_ctx, n_all)
            hl = _lru(z, lru_conv_w[i], lru_conv_b[i], lru_w_a[i], lru_b_a[i], lru_w_x[i], lru_b_x[i],
                      lru_lam[i], n_ctx, n_all)
            mix_args = [g_f, g_b, z, gdn_norm_w[i].reshape(1, GDN_DV), hl, hl, z]
            mix_specs = [slab_spec(0), slab_spec(0), slab_spec(ODD_SPLITS[0] // HALF_W),
                         pl.BlockSpec((1, GDN_DV), const2), dir_spec(0), dir_spec(1),
                         slab_spec((ODD_LX + LRU_W) // LRU_W)]
            body, w_out = _out_body_odd, odd_w_out[i]

        xs_mid, hf, logits = _out_proj(body, mix_args, mix_specs, xs, w_out.astype(BF16), m[2],
                                       norm_ffn[layer], m[3], m[4], wr_hi, wr_lo, rb, row0, n_tiles, ctx_tiles)
        moe_row0 = ctx_moe_tiles if last else 0
        xs = _moe(layer, hf, logits, xs_mid, m[5], expert_w_gate, expert_w_up, expert_w_down,
                  moe_row0, ctx_moe_tiles, final_norm if last else None)

    return xs
```

```python
import functools

import jax
import jax.numpy as jnp
from jax import lax
from jax.experimental import pallas as pl
from jax.experimental.pallas import tpu as pltpu

D_MODEL = 1024
DEPTH = 2
GRID_W = 64
HALF_W = D_MODEL // 2
S5_GROUP = 16
S5_GROUPS = HALF_W // S5_GROUP
S5_STATE = 64
RET_HEADS = 4
RET_DV = HALF_W // RET_HEADS
RET_DK = RET_DV // 2
RET_QK = RET_HEADS * RET_DK
RET_CHUNK = 128
ROPE_BASE = 10000.0
GDN_HEADS = 4
GDN_DK = HALF_W // GDN_HEADS
GDN_DV = GDN_DK
GDN_CHUNK = 64
LRU_W = HALF_W
LRU_BLOCKS = 8
LRU_BLOCK = LRU_W // LRU_BLOCKS
LRU_C = 8.0
CONV_K = 4
CONV_PAD_LEFT = 2
N_GROUPS = 4
EXPERTS_PER_GROUP = 8
N_EXPERTS = N_GROUPS * EXPERTS_PER_GROUP
TOP_K = 2
EXPERT_FF = D_MODEL // 2
N_MOD = 6
NORM_EPS = 1e-6
EVEN_SPLITS = [HALF_W, HALF_W + RET_QK, HALF_W + 2 * RET_QK, 2 * HALF_W + 2 * RET_QK]
EVEN_IN = 3 * HALF_W + 2 * RET_QK
ODD_SPLITS = [3 * HALF_W, 4 * HALF_W, 4 * HALF_W + 2 * GDN_HEADS, 4 * HALF_W + 4 * GDN_HEADS,
              4 * HALF_W + 4 * GDN_HEADS + LRU_W]

F32 = jnp.float32
BF16 = jnp.bfloat16

SUBLANES = 8
LANES = 128
BATCH = SUBLANES
ROW_CHUNKS = D_MODEL // LANES
VMEM_LIMIT = 48 * 1024 * 1024

ROW_TILE = 512
S5_STEPS = 64
S5_SLABS = 4
S5_SLAB_CH = HALF_W // S5_SLABS
S5_SLAB_STATE = (S5_GROUPS // S5_SLABS) * S5_STATE
LRU_STEPS = 64
MOE_TILE = 256
DMA_PRIORITIES = 2
ODD_PAD = 512
ODD_LX = 4 * HALF_W
ODD_BD = ODD_LX + 2 * LRU_W
ODD_IN_PADDED = ODD_BD + ODD_PAD
GDN_QKV = 3 * HALF_W
GDN_ROWS = GDN_CHUNK * BATCH
GDN_BG_G = 2 * GDN_HEADS
GDN_NEUMANN_LEVELS = 5
GDN_BATCHES_PER_ITER = 4
ROUTER_PAD = LANES


def _dot(a, b):
    return jnp.dot(a, b, preferred_element_type=F32)


def _dot_nt(a, b):
    return lax.dot_general(a, b, (((1,), (1,)), ((), ())), preferred_element_type=F32)


def _params(*sem):
    return pltpu.CompilerParams(dimension_semantics=sem, vmem_limit_bytes=VMEM_LIMIT)


def _scan_chunk(d, i, ctx_chunks, all_chunks):
    back = jnp.where(i < ctx_chunks, ctx_chunks - 1 - i, all_chunks + ctx_chunks - 1 - i)
    return jnp.where(d == 0, i, back)


def _silu(x):
    return x * jax.nn.sigmoid(x)


def _softplus(x):
    return jnp.maximum(x, 0.0) + jnp.log(1.0 + jnp.exp(-jnp.abs(x)))


def _segment_edges(chunk, ctx_chunks, all_chunks):
    first = jnp.logical_or(chunk == 0, chunk == ctx_chunks)
    last = jnp.logical_or(chunk == ctx_chunks - 1, chunk == all_chunks - 1)
    return first, last


def _gdn_prep_body(ctx_chunks, all_chunks, prev_ref, x_ref, next_ref, bd_ref, cw_ref, al_ref, dtb_ref,
                   qkv_ref, bg_ref, ext_scr):
    rows = x_ref.shape[1]
    steps = rows // BATCH
    halo = CONV_PAD_LEFT * BATCH
    seg_first, seg_last = _segment_edges(pl.program_id(0), ctx_chunks, all_chunks)
    ext_scr[:, 0:halo, :] = jnp.where(seg_first, 0.0, prev_ref[...])
    ext_scr[:, halo:halo + rows, :] = x_ref[...]
    ext_scr[:, halo + rows:, :] = jnp.where(seg_last, 0.0, next_ref[...])
    for j in range(GDN_QKV // LANES):
        sl = slice(j * LANES, (j + 1) * LANES)
        c = sum(cw_ref[k:k + 1, sl] * ext_scr[j, k * BATCH:k * BATCH + rows, :] for k in range(CONV_K))
        a = _silu(c)
        if j < 2 * GDN_HEADS:
            a = a * lax.rsqrt(jnp.sum(a * a, -1, keepdims=True) + NORM_EPS)
        if j < GDN_HEADS:
            a = a * (GDN_DK ** -0.5)
        qkv_ref[j] = a

    bd = bd_ref[0]
    lane = lax.broadcasted_iota(jnp.int32, bd.shape, 1)
    val = jnp.where(lane < GDN_BG_G, jax.nn.sigmoid(bd), al_ref[...] * _softplus(bd + dtb_ref[...]))
    prefix = val
    for k in range((steps - 1).bit_length()):
        sh = BATCH << k
        prefix = prefix + jnp.concatenate([jnp.zeros((sh, LANES), F32), prefix[:rows - sh]], 0)
    p3 = prefix.reshape(steps, BATCH, LANES)
    suffix = (p3[steps - 1][None] - p3 + val.reshape(steps, BATCH, LANES)).reshape(rows, LANES)
    fwd_g = jnp.logical_and(lane >= GDN_BG_G, lane < GDN_BG_G + GDN_HEADS)
    bwd_g = jnp.logical_and(lane >= GDN_BG_G + GDN_HEADS, lane < GDN_BG_G + 2 * GDN_HEADS)
    bg_ref[...] = jnp.where(fwd_g, prefix, jnp.where(bwd_g, suffix, val))


def _gdn_prep(z, conv_w, a_log, dt_bias, n_ctx, n_all):
    rows = n_all * BATCH
    ctx_chunks, all_chunks = n_ctx // GDN_CHUNK, n_all // GDN_CHUNK
    halo_prev, halo_next = CONV_PAD_LEFT * BATCH, (CONV_K - 1 - CONV_PAD_LEFT) * BATCH
    slabs = GDN_QKV // LANES
    lanes16 = lambda t: jnp.zeros((1, LANES), F32).at[0, GDN_BG_G:GDN_BG_G + 2 * GDN_HEADS].set(t.reshape(-1))
    return pl.pallas_call(
        functools.partial(_gdn_prep_body, ctx_chunks, all_chunks),
        grid=(all_chunks,),
        in_specs=[pl.BlockSpec((slabs, halo_prev, LANES),
                               lambda i: (0, jnp.maximum(i * (GDN_ROWS // halo_prev) - 1, 0), 0)),
                  pl.BlockSpec((slabs, GDN_ROWS, LANES), lambda i: (0, i, 0)),
                  pl.BlockSpec((slabs, halo_next, LANES),
                               lambda i: (0, jnp.minimum((i + 1) * (GDN_ROWS // halo_next), rows // halo_next - 1), 0)),
                  pl.BlockSpec((1, GDN_ROWS, LANES), lambda i: (ODD_BD // LANES, i, 0)),
                  pl.BlockSpec((CONV_K, GDN_QKV), lambda i: (0, 0)),
                  pl.BlockSpec((1, LANES), lambda i: (0, 0)),
                  pl.BlockSpec((1, LANES), lambda i: (0, 0))],
        out_specs=[pl.BlockSpec((slabs, GDN_ROWS, LANES), lambda i: (0, i, 0)),
                   pl.BlockSpec((GDN_ROWS, LANES), lambda i: (i, 0))],
        out_shape=[jax.ShapeDtypeStruct((slabs, rows, LANES), F32), jax.ShapeDtypeStruct((rows, LANES), F32)],
        scratch_shapes=[pltpu.VMEM((slabs, GDN_ROWS + halo_prev + halo_next, LANES), F32)],
        compiler_params=_params("arbitrary"),
        name="gdn_prep",
    )(z, z, z, z, conv_w, lanes16(-jnp.exp(a_log)), lanes16(dt_bias))


def _dot_split(a, b):
    ah, bh = a.astype(BF16), b.astype(BF16)
    al, bl = (a - ah.astype(F32)).astype(BF16), (b - bh.astype(F32)).astype(BF16)
    return _dot(ah, bh) + _dot(ah, bl) + _dot(al, bh)


def _gdn_batches(batches, xf_ref, bgf_ref, xb_ref, bgb_ref, of_ref, ob_ref, s_scr):
    ii = lax.broadcasted_iota(jnp.int32, (GDN_CHUNK, GDN_CHUNK), 0)
    jj = lax.broadcasted_iota(jnp.int32, (GDN_CHUNK, GDN_CHUNK), 1)
    eye = jnp.where(ii == jj, 1.0, 0.0)
    units = []
    for b in batches:
        rows = _batch_rows(b, GDN_CHUNK)
        for d, (x_ref, bg_ref, o_ref) in enumerate(((xf_ref, bgf_ref, of_ref), (xb_ref, bgb_ref, ob_ref))):
            bg = bg_ref[rows, :]
            bg_t = bg.T
            incl = (jj <= ii) if d == 0 else (jj >= ii)
            strict = (jj < ii) if d == 0 else (jj > ii)
            last = GDN_CHUNK - 1 if d == 0 else 0
            for h in range(GDN_HEADS):
                q, k, v = x_ref[h, rows, :], x_ref[GDN_HEADS + h, rows, :], x_ref[2 * GDN_HEADS + h, rows, :]
                cb = d * GDN_HEADS + h
                cg = GDN_BG_G + cb
                beta, g_col = bg[:, cb:cb + 1], bg[:, cg:cg + 1]
                g_row, g_last = bg_t[cg:cg + 1, :], bg_t[cg:cg + 1, last:last + 1]
                decay = jnp.where(incl, jnp.exp(jnp.where(incl, g_col - g_row, 0.0)), 0.0)
                kb16 = k.astype(BF16)
                eg = jnp.exp(g_col)
                units.append(dict(
                    si=b * (2 * GDN_HEADS) + cb, h=h, o_ref=o_ref, rows=rows,
                    qk=(_dot_nt(q.astype(BF16), kb16) * decay).astype(BF16),
                    n=jnp.where(strict, -(_dot_nt(kb16, kb16) * decay * beta), 0.0),
                    rhs=jnp.concatenate([v * beta, k * (beta * eg)], 1),
                    qg=(q * eg).astype(BF16),
                    kg_t=(k * jnp.exp(g_last - g_col)).T.astype(BF16),
                    d_last=jnp.exp(g_last)))

    ts = [eye + u['n'] for u in units]
    ps = [u['n'] for u in units]
    for _ in range(GDN_NEUMANN_LEVELS):
        ps = [_dot(p.astype(BF16), p.astype(BF16)) for p in ps]
        ts = [t + _dot(t.astype(BF16), p.astype(BF16)) for t, p in zip(ts, ps)]
    sols = [_dot_split(t, u['rhs']) for t, u in zip(ts, units)]
    states = [s_scr[u['si']] for u in units]
    sbs = [s.astype(BF16) for s in states]
    v_news = [(sol[:, :GDN_DV] - _dot(sol[:, GDN_DV:].astype(BF16), sb)).astype(BF16)
              for sol, sb in zip(sols, sbs)]
    for u, s, sb, vn in zip(units, states, sbs, v_news):
        u['o_ref'][u['h'], u['rows'], :] = _dot(u['qg'], sb) + _dot(u['qk'], vn)
        s_scr[u['si']] = s * u['d_last'] + _dot(u['kg_t'], vn)


def _gdn_body(xf_ref, bgf_ref, xb_ref, bgb_ref, of_ref, ob_ref, s_scr):
    @pl.when(pl.program_id(0) == 0)
    def _():
        s_scr[...] = jnp.zeros_like(s_scr)

    def per_pair(j, carry):
        batches = [j * GDN_BATCHES_PER_ITER + p for p in range(GDN_BATCHES_PER_ITER)]
        _gdn_batches(batches, xf_ref, bgf_ref, xb_ref, bgb_ref, of_ref, ob_ref, s_scr)
        return carry

    lax.fori_loop(0, BATCH // GDN_BATCHES_PER_ITER, per_pair, 0)


def _gdn(qkv, bg, n_ctx, n_all):
    rows = n_all * BATCH
    ctx_chunks, all_chunks = n_ctx // GDN_CHUNK, n_all // GDN_CHUNK
    slabs = GDN_QKV // LANES

    def seq_specs(d):
        ch = lambda i: _scan_chunk(d, i, ctx_chunks, all_chunks)
        return [pl.BlockSpec((slabs, GDN_ROWS, LANES), lambda i: (0, ch(i), 0)),
                pl.BlockSpec((GDN_ROWS, LANES), lambda i: (ch(i), 0))]

    out_spec = lambda d: pl.BlockSpec((GDN_HEADS, GDN_ROWS, LANES),
                                      lambda i: (0, _scan_chunk(d, i, ctx_chunks, all_chunks), 0))
    o_shape = jax.ShapeDtypeStruct((GDN_HEADS, rows, LANES), F32)
    return pl.pallas_call(
        _gdn_body,
        grid=(all_chunks,),
        in_specs=seq_specs(0) + seq_specs(1),
        out_specs=[out_spec(0), out_spec(1)],
        out_shape=[o_shape, o_shape],
        scratch_shapes=[pltpu.VMEM((BATCH * 2 * GDN_HEADS, GDN_DK, GDN_DV), F32)],
        compiler_params=_params("arbitrary"),
        name="gdn_chunks",
    )(qkv, bg, qkv, bg)


ADALN_COLS = 512
ADALN_ROWS = 2 * BATCH


def _adaln_body(c_ref, w_ref, b_ref, o_ref):
    c = c_ref[...]
    s = (c * jax.nn.sigmoid(c)).astype(BF16)
    o_ref[0] = _dot(s, w_ref[0].astype(BF16)) + b_ref[0]


def _adaln_all(cond16, mod_w, mod_b):
    n_out = mod_w.shape[-1]
    return pl.pallas_call(
        _adaln_body,
        grid=(DEPTH, n_out // ADALN_COLS),
        in_specs=[pl.BlockSpec((ADALN_ROWS, D_MODEL), lambda l, j: (0, 0)),
                  pl.BlockSpec((1, D_MODEL, ADALN_COLS), lambda l, j: (l, 0, j)),
                  pl.BlockSpec((1, 1, ADALN_COLS), lambda l, j: (l, 0, j))],
        out_specs=pl.BlockSpec((1, ADALN_ROWS, ADALN_COLS), lambda l, j: (l, 0, j)),
        out_shape=jax.ShapeDtypeStruct((DEPTH, ADALN_ROWS, n_out), F32),
        compiler_params=_params("arbitrary", "arbitrary"),
        name="adaln",
    )(cond16, mod_w, mod_b.reshape(DEPTH, 1, n_out))


def _col_chunks(n, width=512):
    return [(c0, min(width, n - c0)) for c0 in range(0, n, width)]


def _norm_modulate(x3, nw, shift, scale):
    y = x3 * lax.rsqrt(jnp.mean(x3 * x3, -1, keepdims=True) + NORM_EPS) * nw
    return y * (1.0 + scale) + shift


def _project(x, nw_ref, sh_ref, sc_ref, w_ref, o_ref):
    tm = x.shape[0]
    x3 = x.reshape(tm // BATCH, BATCH, D_MODEL)
    h = _norm_modulate(x3, nw_ref[...], sh_ref[0], sc_ref[0]).reshape(tm, D_MODEL).astype(BF16)
    for c0, cw in _col_chunks(w_ref.shape[1]):
        res = _dot(h, w_ref[:, c0:c0 + cw])
        for j in range(cw // LANES):
            o_ref[c0 // LANES + j] = res[:, j * LANES:(j + 1) * LANES]


def _in_proj_body(x_ref, nw_ref, sh_ref, sc_ref, w_ref, o_ref):
    _project(x_ref[...], nw_ref, sh_ref, sc_ref, w_ref, o_ref)


def _in_proj_first_body(ctx_tiles, ctx_ref, lat_ref, nw_ref, sh_ref, sc_ref, w_ref, o_ref, xs_ref, slab):
    steps = ROW_TILE // BATCH

    def load(src_ref):
        for b in range(BATCH):
            for s in range(ROW_CHUNKS):
                slab[s, _batch_rows(b, steps), :] = src_ref[b, :, s * LANES:(s + 1) * LANES]

    @pl.when(pl.program_id(0) < ctx_tiles)
    def _():
        load(ctx_ref)

    @pl.when(pl.program_id(0) >= ctx_tiles)
    def _():
        load(lat_ref)

    x = jnp.concatenate([slab[s] for s in range(ROW_CHUNKS)], 1)
    xs_ref[...] = x
    _project(x, nw_ref, sh_ref, sc_ref, w_ref, o_ref)


def _in_proj_first(ctx, x, norm_w, shift2, scale2, w_bf16, ctx_tiles):
    rows = (ctx.shape[1] + x.shape[1]) * BATCH
    n = w_bf16.shape[1]
    steps = ROW_TILE // BATCH
    kind = lambda r: (jnp.where(r < ctx_tiles, 0, 1), 0, 0)
    return pl.pallas_call(
        functools.partial(_in_proj_first_body, ctx_tiles),
        grid=(rows // ROW_TILE,),
        in_specs=[pl.BlockSpec((BATCH, steps, D_MODEL), lambda r: (0, jnp.minimum(r, ctx_tiles - 1), 0)),
                  pl.BlockSpec((BATCH, steps, D_MODEL), lambda r: (0, jnp.maximum(r - ctx_tiles, 0), 0)),
                  pl.BlockSpec((1, D_MODEL), lambda r: (0, 0)),
                  pl.BlockSpec((1, BATCH, D_MODEL), kind),
                  pl.BlockSpec((1, BATCH, D_MODEL), kind),
                  pl.BlockSpec((D_MODEL, n), lambda r: (0, 0))],
        out_specs=[pl.BlockSpec((n // LANES, ROW_TILE, LANES), lambda r: (0, r, 0)),
                   pl.BlockSpec((ROW_TILE, D_MODEL), lambda r: (r, 0))],
        out_shape=[jax.ShapeDtypeStruct((n // LANES, rows, LANES), F32),
                   jax.ShapeDtypeStruct((rows, D_MODEL), F32)],
        scratch_shapes=[pltpu.VMEM((ROW_CHUNKS, ROW_TILE, LANES), F32)],
        compiler_params=_params("arbitrary"),
        name="in_proj_first",
    )(ctx, x, norm_w.reshape(1, D_MODEL), shift2, scale2, w_bf16)


def _in_proj(xs, norm_w, shift2, scale2, w_bf16, ctx_tiles):
    rows = xs.shape[0]
    n = w_bf16.shape[1]
    kind = lambda r: (jnp.where(r < ctx_tiles, 0, 1), 0, 0)
    return pl.pallas_call(
        _in_proj_body,
        grid=(rows // ROW_TILE,),
        in_specs=[pl.BlockSpec((ROW_TILE, D_MODEL), lambda r: (r, 0)),
                  pl.BlockSpec((1, D_MODEL), lambda r: (0, 0)),
                  pl.BlockSpec((1, BATCH, D_MODEL), kind),
                  pl.BlockSpec((1, BATCH, D_MODEL), kind),
                  pl.BlockSpec((D_MODEL, n), lambda r: (0, 0))],
        out_specs=pl.BlockSpec((n // LANES, ROW_TILE, LANES), lambda r: (0, r, 0)),
        out_shape=jax.ShapeDtypeStruct((n // LANES, rows, LANES), F32),
        compiler_params=_params("arbitrary"),
        name="in_proj",
    )(xs, norm_w.reshape(1, D_MODEL), shift2, scale2, w_bf16)


def _s5_discretize(lam_re, lam_im, log_step, b_re, b_im, c_re, c_im):
    lr, li = lam_re, lam_im
    dt = jnp.exp(log_step)[:, None]
    mag = jnp.exp(lr * dt)
    ar, ai = mag * jnp.cos(li * dt), mag * jnp.sin(li * dt)
    pr, pi = ar - 1.0, ai
    den = lr * lr + li * li
    zr, zi = (pr * lr + pi * li) / den, (pi * lr - pr * li) / den
    bbr = zr[..., None] * b_re - zi[..., None] * b_im
    bbi = zr[..., None] * b_im + zi[..., None] * b_re
    gps = S5_GROUPS // S5_SLABS
    eye = jnp.eye(gps, dtype=F32)

    def in_slab(m):
        m = m.reshape(S5_SLABS, gps, S5_STATE, S5_GROUP)
        return jnp.einsum('sgnc,gh->sgchn', m, eye).reshape(S5_SLABS, S5_SLAB_CH, S5_SLAB_STATE)

    def out_slab(m):
        m = m.reshape(S5_SLABS, gps, S5_GROUP, S5_STATE)
        return jnp.einsum('sgcn,gh->sgnhc', m, eye).reshape(S5_SLABS, S5_SLAB_STATE, S5_SLAB_CH)

    bb = jnp.concatenate([in_slab(bbr), in_slab(bbi)], -1)
    cc = jnp.concatenate([out_slab(c_re), out_slab(-c_im)], 1)
    a = jnp.concatenate([ar.reshape(S5_SLABS, S5_SLAB_STATE), ai.reshape(S5_SLABS, S5_SLAB_STATE)], -1)
    return bb.astype(BF16), cc.astype(BF16), a.reshape(1, S5_SLABS * 2 * S5_SLAB_STATE)


def _s5_body(u_ref, bb_ref, cc_ref, a_ref, y_ref, x_scr, h_scr):
    d = pl.program_id(0)
    rows = u_ref.shape[1]
    steps = rows // BATCH
    sw = 2 * S5_SLAB_STATE

    @pl.when(pl.program_id(1) == 0)
    def _():
        h_scr[...] = jnp.zeros_like(h_scr)

    for s in range(S5_SLABS):
        x_scr[:, s * sw:(s + 1) * sw] = _dot(u_ref[s].astype(BF16), bb_ref[0, s])

    for s in range(S5_SLABS):
        re0, im0 = s * sw, s * sw + S5_SLAB_STATE
        ar = jnp.broadcast_to(a_ref[0, :, re0:im0], (BATCH, S5_SLAB_STATE))
        ai = jnp.broadcast_to(a_ref[0, :, im0:im0 + S5_SLAB_STATE], (BATCH, S5_SLAB_STATE))

        def step(i, carry):
            hr, hi = carry
            t = jnp.where(d == 0, i, steps - 1 - i)
            r0 = pl.multiple_of(t * BATCH, BATCH)
            xr = x_scr[pl.ds(r0, BATCH), re0:im0]
            xi = x_scr[pl.ds(r0, BATCH), im0:im0 + S5_SLAB_STATE]
            nr = ar * hr - ai * hi + xr
            ni = ar * hi + ai * hr + xi
            x_scr[pl.ds(r0, BATCH), re0:im0] = nr
            x_scr[pl.ds(r0, BATCH), im0:im0 + S5_SLAB_STATE] = ni
            return nr, ni

        hr, hi = lax.fori_loop(0, steps, step,
                               (h_scr[:, re0:im0], h_scr[:, im0:im0 + S5_SLAB_STATE]), unroll=4)
        h_scr[:, re0:im0] = hr
        h_scr[:, im0:im0 + S5_SLAB_STATE] = hi

    for s in range(S5_SLABS):
        y_ref[0, :, s * S5_SLAB_CH:(s + 1) * S5_SLAB_CH] = _dot(
            x_scr[:, s * sw:(s + 1) * sw].astype(BF16), cc_ref[0, s])


def _s5_scan(z, bb, cc, a, n_ctx, n_all):
    rows = n_all * BATCH
    blk = S5_STEPS * BATCH
    ctx_chunks, all_chunks = n_ctx // S5_STEPS, n_all // S5_STEPS
    chunk = lambda d, i: _scan_chunk(d, i, ctx_chunks, all_chunks)
    state_w = S5_SLABS * 2 * S5_SLAB_STATE
    return pl.pallas_call(
        _s5_body,
        grid=(2, all_chunks),
        in_specs=[pl.BlockSpec((S5_SLABS, blk, LANES), lambda d, i: (0, chunk(d, i), 0)),
                  pl.BlockSpec((1, S5_SLABS, S5_SLAB_CH, 2 * S5_SLAB_STATE), lambda d, i: (d, 0, 0, 0)),
                  pl.BlockSpec((1, S5_SLABS, 2 * S5_SLAB_STATE, S5_SLAB_CH), lambda d, i: (d, 0, 0, 0)),
                  pl.BlockSpec((1, 1, state_w), lambda d, i: (d, 0, 0))],
        out_specs=pl.BlockSpec((1, blk, HALF_W), lambda d, i: (d, chunk(d, i), 0)),
        out_shape=jax.ShapeDtypeStruct((2, rows, HALF_W), F32),
        scratch_shapes=[pltpu.VMEM((blk, state_w), F32), pltpu.VMEM((BATCH, state_w), F32)],
        compiler_params=_params("arbitrary", "arbitrary"),
        name="s5_scan",
    )(z, bb, cc, a)


def _rope_tables(n_ctx, n_lat):
    half = RET_DK // 2
    nf = half // 2
    grid_rows = n_lat // GRID_W
    row = jnp.repeat(jnp.arange(grid_rows, dtype=F32), GRID_W)
    col = jnp.tile(jnp.arange(GRID_W, dtype=F32), grid_rows)
    inv = ROPE_BASE ** (-jnp.arange(nf, dtype=F32) / nf)
    ang = jnp.concatenate([row[:, None] * inv, col[:, None] * inv], -1)
    cos = jnp.tile(jnp.cos(ang), (1, 2 * RET_HEADS))
    sin = jnp.tile(jnp.concatenate([-jnp.sin(ang), jnp.sin(ang)], -1), (1, RET_HEADS))
    cos = jnp.concatenate([jnp.ones((n_ctx, RET_QK), F32), cos], 0)
    sin = jnp.concatenate([jnp.zeros((n_ctx, RET_QK), F32), sin], 0)
    return cos, sin


def _ret_tables(ret_decay):
    lg = jax.nn.log_sigmoid(ret_decay)
    pos = jnp.arange(RET_CHUNK, dtype=F32)
    diff = pos[:, None] - pos[None, :]
    f_mask, b_mask = diff >= 0, diff < 0
    dm_f = jnp.where(f_mask, jnp.exp(lg[0][:, None, None] * jnp.where(f_mask, diff, 0.0)), 0.0)
    dm_b = jnp.where(b_mask, jnp.exp(lg[1][:, None, None] * jnp.where(b_mask, -diff, 0.0)), 0.0)
    dmat = jnp.stack([dm_f, dm_b])
    heads = lambda t: jnp.repeat(t, RET_DK, axis=-1)
    kdec = jnp.stack([heads(jnp.exp(lg[0][None] * (RET_CHUNK - 1 - pos)[:, None])),
                      heads(jnp.exp(lg[1][None] * pos[:, None]))])
    qdec = jnp.stack([heads(jnp.exp(lg[0][None] * (pos + 1)[:, None])),
                      heads(jnp.exp(lg[1][None] * (RET_CHUNK - pos)[:, None]))])
    blk = jnp.kron(jnp.eye(RET_HEADS, dtype=F32), jnp.ones((RET_DK, RET_DV), F32))
    sdec = jnp.repeat(jnp.exp(lg * RET_CHUNK), RET_DK, axis=-1)[:, :, None] * blk[None]
    return dmat, kdec, qdec, sdec, blk


def _rope(t, cos, sin):
    lane = lax.broadcasted_iota(jnp.int32, t.shape, 1)
    first = (lane % RET_DK) < (RET_DK // 2)
    partner = jnp.where(first, pltpu.roll(t, RET_QK - RET_DK // 2, 1), pltpu.roll(t, RET_DK // 2, 1))
    return t * cos + partner * sin


def _batch_rows(b, steps):
    return pl.ds(b, steps, stride=BATCH)


def _ret_direction(d, b, q_ref, k_ref, v_ref, cos_ref, sin_ref, dm_ref, kd_ref, qd_ref, sd_ref, blk_ref,
                   o_ref, s_scr):
    rows = _batch_rows(b, RET_CHUNK)
    cos, sin = cos_ref[...], sin_ref[...]
    q = _rope(jnp.concatenate([q_ref[0, rows, :], q_ref[1, rows, :]], 1), cos, sin)
    k = _rope(jnp.concatenate([k_ref[0, rows, :], k_ref[1, rows, :]], 1), cos, sin) * (RET_DK ** -0.5)
    v = [v_ref[h, rows, :].astype(BF16) for h in range(RET_HEADS)]
    kb = k.astype(BF16)
    lane = lax.broadcasted_iota(jnp.int32, q.shape, 1)
    si = d * BATCH + b
    state = s_scr[si]
    o_inter = _dot((q * qd_ref[d]).astype(BF16), state.astype(BF16))
    for h in range(RET_HEADS):
        qh = jnp.where(lane // RET_DK == h, q, 0.0).astype(BF16)
        scores = _dot_nt(qh, kb) * dm_ref[d, h]
        o_ref[h, rows, :] = _dot(scores.astype(BF16), v[h]) + o_inter[:, h * RET_DV:(h + 1) * RET_DV]
    kv = _dot((k * kd_ref[d]).T.astype(BF16), jnp.concatenate(v, 1))
    s_scr[si] = sd_ref[d] * state + blk_ref[...] * kv


def _ret_body(qf, kf, vf, cf, sf, qb, kb, vb, cb, sb, dm_ref, kd_ref, qd_ref, sd_ref, blk_ref,
              of_ref, ob_ref, s_scr):
    @pl.when(pl.program_id(0) == 0)
    def _():
        s_scr[...] = jnp.zeros_like(s_scr)

    def per_batch(b, carry):
        _ret_direction(0, b, qf, kf, vf, cf, sf, dm_ref, kd_ref, qd_ref, sd_ref, blk_ref, of_ref, s_scr)
        _ret_direction(1, b, qb, kb, vb, cb, sb, dm_ref, kd_ref, qd_ref, sd_ref, blk_ref, ob_ref, s_scr)
        return carry

    lax.fori_loop(0, BATCH, per_batch, 0)


def _retention(z, cos, sin, tables, n_ctx, n_all):
    dmat, kdec, qdec, sdec, blk = tables
    rows = n_all * BATCH
    blk_rows = RET_CHUNK * BATCH
    ctx_chunks, all_chunks = n_ctx // RET_CHUNK, n_all // RET_CHUNK
    qk_slabs, v_slabs = RET_QK // LANES, HALF_W // LANES
    q0, k0, v0 = EVEN_SPLITS[0] // RET_QK, EVEN_SPLITS[1] // RET_QK, EVEN_SPLITS[2] // HALF_W

    def seq_specs(d):
        ch = lambda i: _scan_chunk(d, i, ctx_chunks, all_chunks)
        return [pl.BlockSpec((qk_slabs, blk_rows, LANES), lambda i: (q0, ch(i), 0)),
                pl.BlockSpec((qk_slabs, blk_rows, LANES), lambda i: (k0, ch(i), 0)),
                pl.BlockSpec((v_slabs, blk_rows, LANES), lambda i: (v0, ch(i), 0)),
                pl.BlockSpec((RET_CHUNK, RET_QK), lambda i: (ch(i), 0)),
                pl.BlockSpec((RET_CHUNK, RET_QK), lambda i: (ch(i), 0))]

    def out_spec(d):
        return pl.BlockSpec((RET_HEADS, blk_rows, LANES),
                            lambda i: (0, _scan_chunk(d, i, ctx_chunks, all_chunks), 0))

    const = lambda nd: (lambda i: (0,) * nd)
    o_shape = jax.ShapeDtypeStruct((RET_HEADS, rows, LANES), F32)
    return pl.pallas_call(
        _ret_body,
        grid=(all_chunks,),
        in_specs=seq_specs(0) + seq_specs(1) + [
            pl.BlockSpec(dmat.shape, const(4)), pl.BlockSpec(kdec.shape, const(3)),
            pl.BlockSpec(qdec.shape, const(3)), pl.BlockSpec(sdec.shape, const(3)),
            pl.BlockSpec(blk.shape, const(2))],
        out_specs=[out_spec(0), out_spec(1)],
        out_shape=[o_shape, o_shape],
        scratch_shapes=[pltpu.VMEM((2 * BATCH, RET_QK, HALF_W), F32)],
        compiler_params=_params("arbitrary"),
        name="retention",
    )(z, z, z, cos, sin, z, z, z, cos, sin, dmat, kdec, qdec, sdec, blk)


def _lru_body(ctx_chunks, all_chunks, prev_ref, x_ref, next_ref, cw_ref, cb_ref, wg_ref, bg_ref, sp_ref,
              h_ref, a_scr, b_scr, h_scr):
    d, i = pl.program_id(0), pl.program_id(1)
    rows = x_ref.shape[1]
    steps = rows // BATCH
    chunk = _scan_chunk(d, i, ctx_chunks, all_chunks)

    @pl.when(i == 0)
    def _():
        h_scr[...] = jnp.zeros_like(h_scr)

    seg_first, seg_last = _segment_edges(chunk, ctx_chunks, all_chunks)
    prev = jnp.where(seg_first, 0.0, prev_ref[...])
    nxt = jnp.where(seg_last, 0.0, next_ref[...])
    slabs = []
    for s in range(LRU_W // LANES):
        sl = slice(s * LANES, (s + 1) * LANES)
        ext = jnp.concatenate([prev[s], x_ref[s], nxt[s]], 0)
        slabs.append(cb_ref[:, sl] + sum(cw_ref[k:k + 1, sl] * ext[k * BATCH:k * BATCH + rows]
                                         for k in range(CONV_K)))
    xs = jnp.concatenate(slabs, 1)
    gates = _dot(xs.astype(BF16), wg_ref[0]) + bg_ref[0]
    r = jax.nn.sigmoid(gates[:, :LRU_W])
    ig = jax.nn.sigmoid(gates[:, LRU_W:])
    log_a = -r * sp_ref[0]
    a = jnp.exp(log_a)
    a_scr[...] = a
    b_scr[...] = jnp.sqrt(1.0 - jnp.exp(2.0 * log_a)) * (ig * xs)

    def step(j, h):
        t = jnp.where(d == 0, j, steps - 1 - j)
        r0 = pl.multiple_of(t * BATCH, BATCH)
        h = a_scr[pl.ds(r0, BATCH), :] * h + b_scr[pl.ds(r0, BATCH), :]
        h_ref[0, pl.ds(r0, BATCH), :] = h
        return h

    h_scr[...] = lax.fori_loop(0, steps, step, h_scr[...], unroll=8)


def _lru(z, conv_w, conv_b, w_a, b_a, w_x, b_x, lam, n_ctx, n_all):
    rows = n_all * BATCH
    blk = LRU_STEPS * BATCH
    ctx_chunks, all_chunks = n_ctx // LRU_STEPS, n_all // LRU_STEPS
    chunk = lambda d, i: _scan_chunk(d, i, ctx_chunks, all_chunks)
    col = ODD_LX // LRU_W
    slabs = LRU_W // LANES
    eye = jnp.eye(LRU_BLOCKS, dtype=F32)
    dense = lambda w: jnp.einsum('dkij,kl->dkilj', w, eye).reshape(2, LRU_W, LRU_W)
    wg = jnp.concatenate([dense(w_a), dense(w_x)], -1).astype(BF16)
    bg = jnp.concatenate([b_a, b_x], -1).reshape(2, 1, 2 * LRU_W)
    sp = (LRU_C * jax.nn.softplus(-lam)).reshape(2, 1, LRU_W)
    halo_prev, halo_next = 2 * BATCH, BATCH
    return pl.pallas_call(
        functools.partial(_lru_body, ctx_chunks, all_chunks),
        grid=(2, all_chunks),
        in_specs=[pl.BlockSpec((slabs, halo_prev, LANES),
                               lambda d, i: (col, jnp.maximum(chunk(d, i) * (blk // halo_prev) - 1, 0), 0)),
                  pl.BlockSpec((slabs, blk, LANES), lambda d, i: (col, chunk(d, i), 0)),
                  pl.BlockSpec((slabs, halo_next, LANES),
                               lambda d, i: (col, jnp.minimum((chunk(d, i) + 1) * (blk // halo_next),
                                                              rows // halo_next - 1), 0)),
                  pl.BlockSpec((CONV_K, LRU_W), lambda d, i: (0, 0)),
                  pl.BlockSpec((1, LRU_W), lambda d, i: (0, 0)),
                  pl.BlockSpec((1, LRU_W, 2 * LRU_W), lambda d, i: (d, 0, 0)),
                  pl.BlockSpec((1, 1, 2 * LRU_W), lambda d, i: (d, 0, 0)),
                  pl.BlockSpec((1, 1, LRU_W), lambda d, i: (d, 0, 0))],
        out_specs=pl.BlockSpec((1, blk, LRU_W), lambda d, i: (d, chunk(d, i), 0)),
        out_shape=jax.ShapeDtypeStruct((2, rows, LRU_W), F32),
        scratch_shapes=[pltpu.VMEM((blk, LRU_W), F32), pltpu.VMEM((blk, LRU_W), F32),
                        pltpu.VMEM((BATCH, LRU_W), F32)],
        compiler_params=_params("arbitrary", "arbitrary"),
        name="rglru",
    )(z, z, z, conv_w, conv_b.reshape(1, LRU_W), wg, bg, sp)


def _residual_norm_route(upd, x_ref, g2_ref, nw_ref, sh_ref, sc_ref, wrh_ref, wrl_ref, rb_ref,
                         xo_ref, hf_ref, lg_ref):
    tm = x_ref.shape[0]
    x3 = x_ref[...].reshape(tm // BATCH, BATCH, D_MODEL) + g2_ref[0] * upd.reshape(tm // BATCH, BATCH, D_MODEL)
    xo_ref[...] = x3.reshape(tm, D_MODEL)
    hf = _norm_modulate(x3, nw_ref[...], sh_ref[0], sc_ref[0]).reshape(tm, D_MODEL)
    _store_rows(hf_ref, hf)
    hi = hf.astype(BF16)
    lo = (hf - hi.astype(F32)).astype(BF16)
    lg_ref[...] = _dot(hi, wrh_ref[...]) + _dot(hi, wrl_ref[...]) + _dot(lo, wrh_ref[...]) + rb_ref[...]


def _gelu_tanh(x):
    return 0.5 * x * (1.0 + jnp.tanh(0.7978845608028654 * (x + 0.044715 * x * x * x)))


def _out_body_even(yf_ref, yb_ref, u_ref, d_ref, gw_ref, gb_ref, of_ref, ob_ref, gate_ref, wo_ref, *rest):
    u = jnp.concatenate([u_ref[s] for s in range(HALF_W // LANES)], 1)
    ys = yf_ref[0] + yb_ref[0] + d_ref[...] * u
    y = _gelu_tanh(ys)
    s5o = y * jax.nn.sigmoid(_dot(y.astype(BF16), gw_ref[...]) + gb_ref[...])
    upd = _dot(s5o.astype(BF16), wo_ref[0:HALF_W, :])
    for h in range(RET_HEADS):
        oh = of_ref[h] + ob_ref[h]
        mu = jnp.mean(oh, -1, keepdims=True)
        var = jnp.mean(jnp.square(oh - mu), -1, keepdims=True)
        rh = _silu(gate_ref[h]) * ((oh - mu) * lax.rsqrt(var + NORM_EPS))
        upd = upd + _dot(rh.astype(BF16), wo_ref[HALF_W + h * RET_DV:HALF_W + (h + 1) * RET_DV, :])
    _residual_norm_route(upd, *rest)


def _out_body_odd(gf_ref, gb_ref, zg_ref, gnw_ref, hf_ref, hb_ref, lg_ref, wo_ref, *rest):
    lg = jnp.concatenate([lg_ref[s] for s in range(LRU_W // LANES)], 1)
    lru = (hf_ref[0] + hb_ref[0]) * _gelu_tanh(lg)
    upd = _dot(lru.astype(BF16), wo_ref[HALF_W:, :])
    for h in range(GDN_HEADS):
        oh = gf_ref[h] + gb_ref[h]
        oh = oh * lax.rsqrt(jnp.mean(oh * oh, -1, keepdims=True) + NORM_EPS) * gnw_ref[...] * _silu(zg_ref[h])
        upd = upd + _dot(oh.astype(BF16), wo_ref[h * GDN_DV:(h + 1) * GDN_DV, :])
    _residual_norm_route(upd, *rest)


def _out_proj(body, mix_args, mix_specs, xs, w_out, gate2, norm_w, shift2, scale2, wr_hi, wr_lo, rb,
              row0_tiles, n_tiles, ctx_tiles):
    kind = lambda r: (jnp.where(r + row0_tiles < ctx_tiles, 0, 1), 0, 0)
    const2 = lambda r: (0, 0)
    rows = n_tiles * ROW_TILE
    return pl.pallas_call(
        body,
        grid=(n_tiles,),
        in_specs=mix_specs + [
            pl.BlockSpec((D_MODEL, D_MODEL), const2),
            pl.BlockSpec((ROW_TILE, D_MODEL), lambda r: (r + row0_tiles, 0)),
            pl.BlockSpec((1, BATCH, D_MODEL), kind),
            pl.BlockSpec((1, D_MODEL), const2),
            pl.BlockSpec((1, BATCH, D_MODEL), kind),
            pl.BlockSpec((1, BATCH, D_MODEL), kind),
            pl.BlockSpec((D_MODEL, ROUTER_PAD), const2),
            pl.BlockSpec((D_MODEL, ROUTER_PAD), const2),
            pl.BlockSpec((1, ROUTER_PAD), const2)],
        out_specs=[pl.BlockSpec((ROW_TILE, D_MODEL), lambda r: (r, 0)),
                   pl.BlockSpec((ROW_TILE * ROW_CHUNKS, LANES), lambda r: (r, 0)),
                   pl.BlockSpec((ROW_TILE, ROUTER_PAD), lambda r: (r, 0))],
        out_shape=[jax.ShapeDtypeStruct((rows, D_MODEL), F32),
                   jax.ShapeDtypeStruct((rows * ROW_CHUNKS, LANES), F32),
                   jax.ShapeDtypeStruct((rows, ROUTER_PAD), F32)],
        compiler_params=_params("arbitrary"),
        name="out_proj",
    )(*mix_args, w_out, xs, gate2, norm_w.reshape(1, D_MODEL), shift2, scale2, wr_hi, wr_lo, rb)


def _route(logits):
    n_tok = logits.shape[0]
    g_logits = logits[:, :N_GROUPS]
    g_idx = jnp.argmax(g_logits, -1)
    g_w = jnp.max(jax.nn.softmax(g_logits, -1), -1, keepdims=True)
    e_logits = logits[:, N_GROUPS:N_GROUPS + N_EXPERTS].reshape(n_tok, N_GROUPS, EXPERTS_PER_GROUP)
    chosen = g_idx[:, None, None] == jnp.arange(N_GROUPS)[None, :, None]
    e_in = jnp.sum(jnp.where(chosen, e_logits, 0.0), 1)
    top_v, top_i = lax.top_k(e_in, TOP_K)
    top_w = jax.nn.softmax(top_v, -1) * g_w
    return (g_idx[:, None] * EXPERTS_PER_GROUP + top_i).astype(jnp.int32), top_w


def _dispatch_plan(expert_ids, top_w, n_tiles):
    n_tok = expert_ids.shape[0]
    n_asg = n_tok * TOP_K
    flat_e = expert_ids.T.reshape(n_asg)
    flat_w = top_w.T.reshape(n_asg)
    order = jnp.argsort(flat_e, stable=True).astype(jnp.int32)
    inv = jnp.argsort(order).astype(jnp.int32)
    onehot = flat_e[:, None] == jnp.arange(N_EXPERTS, dtype=jnp.int32)[None]
    counts = jnp.sum(onehot, 0, dtype=jnp.int32)
    padded = ((counts + MOE_TILE - 1) // MOE_TILE) * MOE_TILE
    pend = jnp.cumsum(padded)
    pstart = pend - padded
    start = jnp.cumsum(counts) - counts
    shift = pstart - start
    pos = inv + jnp.sum(jnp.where(onehot, shift[None], 0), 1)
    tile_start = jnp.arange(n_tiles, dtype=jnp.int32) * MOE_TILE
    tile_e = jnp.minimum(jnp.sum(tile_start[:, None] >= pend[None], 1), N_EXPERTS - 1).astype(jnp.int32)
    tile_valid = (tile_start < pend[-1]).astype(jnp.int32)
    tile_hot = tile_e[:, None] == jnp.arange(N_EXPERTS, dtype=jnp.int32)[None]
    per_tile = lambda tbl: jnp.sum(jnp.where(tile_hot, tbl[None], 0), 1)
    tile_rank0 = tile_start - per_tile(shift)
    tile_end = per_tile(start + counts)
    lane = jnp.arange(MOE_TILE, dtype=jnp.int32)[None]
    rank = (tile_rank0[:, None] + lane).reshape(n_tiles * MOE_TILE)
    row_ok = (tile_rank0[:, None] + lane < tile_end[:, None]).reshape(n_tiles * MOE_TILE)
    src_asg = order[jnp.clip(rank, 0, n_asg - 1)]
    src_tok = jnp.where(row_ok, src_asg % n_tok, 0)
    row_w = jnp.where(row_ok, flat_w[src_asg], 0.0)
    return src_tok, row_w, pos.reshape(TOP_K, n_tok), tile_e, tile_valid


def _gather_rows(idx_ref, src_hbm, dst_vmem, sem, n_rows):
    def issue(j, c):
        for p in range(DMA_PRIORITIES):
            r = j * DMA_PRIORITIES + p
            src = pl.multiple_of(idx_ref[0, 0, r] * ROW_CHUNKS, ROW_CHUNKS)
            dst = pl.multiple_of(r * ROW_CHUNKS, ROW_CHUNKS)
            pltpu.make_async_copy(src_hbm.at[pl.ds(src, ROW_CHUNKS)], dst_vmem.at[pl.ds(dst, ROW_CHUNKS)],
                                  sem).start(priority=p)
        return c

    lax.fori_loop(0, n_rows // DMA_PRIORITIES, issue, 0, unroll=8)


def _wait_rows(src_hbm, dst_vmem, sem, n_rows):
    pltpu.make_async_copy(src_hbm.at[pl.ds(0, n_rows * ROW_CHUNKS)], dst_vmem, sem).wait()


def _rows_2d(ref, n_rows):
    return jnp.concatenate([ref[pl.ds(s, n_rows, stride=ROW_CHUNKS), :] for s in range(ROW_CHUNKS)], 1)


def _store_rows(ref, val):
    for s in range(ROW_CHUNKS):
        ref[pl.ds(s, val.shape[0], stride=ROW_CHUNKS), :] = val[:, s * LANES:(s + 1) * LANES]


def _expert_body(te_ref, tv_ref, idx_ref, idx_next_ref, hf_hbm, cw_ref, wg32_ref, wu32_ref, wd32_ref, y_ref,
                 xbuf, wg_ref, wu_ref, wd_ref, sem):
    i = pl.program_id(0)
    n = pl.num_programs(0)
    slot = i % 2
    nxt = jnp.minimum(i + 1, n - 1)

    @pl.when(jnp.logical_or(i == 0, te_ref[i] != te_ref[jnp.maximum(i - 1, 0)]))
    def _():
        wg_ref[0] = wg32_ref[0, 0].astype(BF16)
        wu_ref[0] = wu32_ref[0, 0].astype(BF16)
        wd_ref[0] = wd32_ref[0, 0].astype(BF16)

    @pl.when(jnp.logical_and(i == 0, tv_ref[0] == 1))
    def _():
        _gather_rows(idx_ref, hf_hbm, xbuf.at[0], sem.at[0], MOE_TILE)

    @pl.when(jnp.logical_and(i + 1 < n, tv_ref[nxt] == 1))
    def _():
        _gather_rows(idx_next_ref, hf_hbm, xbuf.at[1 - slot], sem.at[1 - slot], MOE_TILE)

    @pl.when(tv_ref[i] == 1)
    def _():
        _wait_rows(hf_hbm, xbuf.at[slot], sem.at[slot], MOE_TILE)
        x = _rows_2d(xbuf.at[slot], MOE_TILE).astype(BF16)
        g = _dot(x, wg_ref[0])
        u = _dot(x, wu_ref[0])
        act = _silu(g) * u * cw_ref[:, 0:1]
        _store_rows(y_ref, _dot(act.astype(BF16), wd_ref[0]))

    @pl.when(tv_ref[i] == 0)
    def _():
        y_ref[...] = jnp.zeros_like(y_ref)


def _experts(layer, tile_e, tile_valid, src_tok, hf, cw_rows, wg, wu, wd):
    n_tiles = tile_e.shape[0]
    tile_spec = pl.BlockSpec((MOE_TILE * ROW_CHUNKS, LANES), lambda i, te, tv: (i, 0))
    idx3 = src_tok.reshape(n_tiles, 1, MOE_TILE)
    grid_spec = pltpu.PrefetchScalarGridSpec(
        num_scalar_prefetch=2,
        grid=(n_tiles,),
        in_specs=[pl.BlockSpec((1, 1, MOE_TILE), lambda i, te, tv: (i, 0, 0), memory_space=pltpu.SMEM),
                  pl.BlockSpec((1, 1, MOE_TILE), lambda i, te, tv: (jnp.minimum(i + 1, n_tiles - 1), 0, 0),
                               memory_space=pltpu.SMEM),
                  pl.BlockSpec(memory_space=pl.ANY),
                  pl.BlockSpec((MOE_TILE, LANES), lambda i, te, tv: (i, 0)),
                  pl.BlockSpec((1, 1, D_MODEL, EXPERT_FF), lambda i, te, tv: (layer, te[i], 0, 0)),
                  pl.BlockSpec((1, 1, D_MODEL, EXPERT_FF), lambda i, te, tv: (layer, te[i], 0, 0)),
                  pl.BlockSpec((1, 1, EXPERT_FF, D_MODEL), lambda i, te, tv: (layer, te[i], 0, 0))],
        out_specs=tile_spec,
        scratch_shapes=[pltpu.VMEM((2, MOE_TILE * ROW_CHUNKS, LANES), F32),
                        pltpu.VMEM((1, D_MODEL, EXPERT_FF), BF16), pltpu.VMEM((1, D_MODEL, EXPERT_FF), BF16),
                        pltpu.VMEM((1, EXPERT_FF, D_MODEL), BF16), pltpu.SemaphoreType.DMA((2,))])
    return pl.pallas_call(
        _expert_body,
        grid_spec=grid_spec,
        out_shape=jax.ShapeDtypeStruct((n_tiles * MOE_TILE * ROW_CHUNKS, LANES), F32),
        compiler_params=_params("arbitrary"),
        name="moe_experts",
    )(tile_e, tile_valid, idx3, idx3, hf, cw_rows, wg, wu, wd)


def _combine_rows(p0_ref, p1_ref, p0n_ref, p1n_ref, y_hbm, x_ref, g5_ref, buf, sem):
    r = pl.program_id(0)
    n = pl.num_programs(0)
    tm = x_ref.shape[0]
    slot = r % 2

    def fetch(p0, p1, s):
        _gather_rows(p0, y_hbm, buf.at[s, 0], sem.at[s, 0], tm)
        _gather_rows(p1, y_hbm, buf.at[s, 1], sem.at[s, 1], tm)

    @pl.when(r == 0)
    def _():
        fetch(p0_ref, p1_ref, 0)

    @pl.when(r + 1 < n)
    def _():
        fetch(p0n_ref, p1n_ref, 1 - slot)

    _wait_rows(y_hbm, buf.at[slot, 0], sem.at[slot, 0], tm)
    _wait_rows(y_hbm, buf.at[slot, 1], sem.at[slot, 1], tm)
    f3 = (_rows_2d(buf.at[slot, 0], tm) + _rows_2d(buf.at[slot, 1], tm)).reshape(tm // BATCH, BATCH, D_MODEL)
    return (x_ref[...].reshape(tm // BATCH, BATCH, D_MODEL) + g5_ref[0] * f3).reshape(tm, D_MODEL)


def _combine_body(p0_ref, p1_ref, p0n_ref, p1n_ref, y_hbm, x_ref, g5_ref, o_ref, buf, sem):
    o_ref[...] = _combine_rows(p0_ref, p1_ref, p0n_ref, p1n_ref, y_hbm, x_ref, g5_ref, buf, sem)


def _combine_final_body(p0_ref, p1_ref, p0n_ref, p1n_ref, y_hbm, x_ref, g5_ref, fw_ref, o_ref, buf, sem, slab):
    x = _combine_rows(p0_ref, p1_ref, p0n_ref, p1n_ref, y_hbm, x_ref, g5_ref, buf, sem)
    steps = x.shape[0] // BATCH
    xn = x * lax.rsqrt(jnp.mean(x * x, -1, keepdims=True) + NORM_EPS) * fw_ref[...]
    for s in range(ROW_CHUNKS):
        slab[s] = xn[:, s * LANES:(s + 1) * LANES]
    for b in range(BATCH):
        for s in range(ROW_CHUNKS):
            o_ref[b, :, s * LANES:(s + 1) * LANES] = slab[s, _batch_rows(b, steps), :]


def _combine(pos, y_sorted, xs, gate2, row0_tiles, ctx_tiles, final_w=None):
    n_tok = pos.shape[1]
    n_tiles = n_tok // MOE_TILE
    kind = lambda r: (jnp.where(r + row0_tiles < ctx_tiles, 0, 1), 0, 0)
    idx_spec = lambda k, ahead: pl.BlockSpec(
        (1, 1, MOE_TILE), lambda r: (k * n_tiles + jnp.minimum(r + ahead, n_tiles - 1), 0, 0),
        memory_space=pltpu.SMEM)
    p2 = pos.reshape(TOP_K * n_tiles, 1, MOE_TILE)
    in_specs = [idx_spec(0, 0), idx_spec(1, 0), idx_spec(0, 1), idx_spec(1, 1),
                pl.BlockSpec(memory_space=pl.ANY),
                pl.BlockSpec((MOE_TILE, D_MODEL), lambda r: (r, 0)),
                pl.BlockSpec((1, BATCH, D_MODEL), kind)]
    scratch = [pltpu.VMEM((2, TOP_K, MOE_TILE * ROW_CHUNKS, LANES), F32), pltpu.SemaphoreType.DMA((2, TOP_K))]
    if final_w is None:
        return pl.pallas_call(
            _combine_body,
            grid=(n_tiles,),
            in_specs=in_specs,
            out_specs=pl.BlockSpec((MOE_TILE, D_MODEL), lambda r: (r, 0)),
            out_shape=jax.ShapeDtypeStruct((n_tok, D_MODEL), F32),
            scratch_shapes=scratch,
            compiler_params=_params("arbitrary"),
            name="moe_combine",
        )(p2, p2, p2, p2, y_sorted, xs, gate2)
    steps = MOE_TILE // BATCH
    return pl.pallas_call(
        _combine_final_body,
        grid=(n_tiles,),
        in_specs=in_specs + [pl.BlockSpec((1, D_MODEL), lambda r: (0, 0))],
        out_specs=pl.BlockSpec((BATCH, steps, D_MODEL), lambda r: (0, r, 0)),
        out_shape=jax.ShapeDtypeStruct((BATCH, n_tok // BATCH, D_MODEL), F32),
        scratch_shapes=scratch + [pltpu.VMEM((ROW_CHUNKS, MOE_TILE, LANES), F32)],
        compiler_params=_params("arbitrary"),
        name="moe_combine_final",
    )(p2, p2, p2, p2, y_sorted, xs, gate2, final_w.reshape(1, D_MODEL))


def _moe(layer, hf, logits, xs, gate2, wg, wu, wd, row0_tiles, ctx_tiles, final_w=None):
    n_tok = hf.shape[0] // ROW_CHUNKS
    n_tiles = n_tok * TOP_K // MOE_TILE + N_EXPERTS
    expert_ids, top_w = _route(logits)
    src_tok, row_w, pos, tile_e, tile_valid = _dispatch_plan(expert_ids, top_w, n_tiles)
    cw_rows = jnp.broadcast_to(row_w[:, None], (n_tiles * MOE_TILE, LANES))
    y_sorted = _experts(layer, tile_e, tile_valid, src_tok, hf, cw_rows, wg, wu, wd)
    return _combine(pos, y_sorted, xs, gate2, row0_tiles, ctx_tiles, final_w)


def _kinds(mod_l, k):
    return jnp.stack([mod_l[BATCH:, k], mod_l[:BATCH, k]])


def kernel(x, c, ctx, c_ctx, mod_w, mod_b, norm_mix, norm_ffn, router_group_w, router_group_b,
           router_expert_w, router_expert_b, expert_w_gate, expert_w_up, expert_w_down,
           even_w_in, even_w_out, s5_lam_re, s5_lam_im, s5_log_step, s5_b_re, s5_b_im, s5_c_re,
           s5_c_im, s5_d, s5_glu_w, s5_glu_b, ret_decay, odd_w_in, odd_w_out, gdn_conv_w, gdn_a_log,
           gdn_dt_bias, gdn_norm_w, lru_conv_w, lru_conv_b, lru_w_a, lru_b_a, lru_w_x, lru_b_x,
           lru_lam, final_norm):
    bsz, n_lat, _ = x.shape
    n_ctx = ctx.shape[1]
    n_all = n_ctx + n_lat
    assert bsz == BATCH and n_ctx % RET_CHUNK == 0 and n_lat % RET_CHUNK == 0
    assert (n_ctx * BATCH) % ROW_TILE == 0 and (n_lat * BATCH) % ROW_TILE == 0
    ctx_tiles = n_ctx * BATCH // ROW_TILE
    all_tiles = n_all * BATCH // ROW_TILE
    ctx_moe_tiles = n_ctx * BATCH // MOE_TILE

    xs = None
    cond16 = jnp.concatenate([c, jnp.broadcast_to(c_ctx[None], (BATCH, D_MODEL))], 0)
    mod = _adaln_all(cond16, mod_w, mod_b).reshape(DEPTH, ADALN_ROWS, N_MOD, D_MODEL)

    for layer in range(DEPTH):
        last = layer == DEPTH - 1
        i = layer // 2
        m = [_kinds(mod[layer], k) for k in range(N_MOD)]
        wr = jnp.zeros((D_MODEL, ROUTER_PAD), F32)
        wr = wr.at[:, :N_GROUPS].set(router_group_w[layer]).at[:, N_GROUPS:N_GROUPS + N_EXPERTS].set(
            router_expert_w[layer])
        wr_hi = wr.astype(BF16)
        wr_lo = (wr - wr_hi.astype(F32)).astype(BF16)
        rb = jnp.zeros((1, ROUTER_PAD), F32)
        rb = rb.at[0, :N_GROUPS].set(router_group_b[layer]).at[0, N_GROUPS:N_GROUPS + N_EXPERTS].set(
            router_expert_b[layer])
        row0 = ctx_tiles if last else 0
        n_tiles = all_tiles - row0
        slab_spec = lambda sb: pl.BlockSpec((HALF_W // LANES, ROW_TILE, LANES), lambda r: (sb, r + row0, 0))
        dir_spec = lambda d: pl.BlockSpec((1, ROW_TILE, HALF_W), lambda r: (d, r + row0, 0))
        const2 = lambda r: (0, 0)

        if layer % 2 == 0:
            w_in = even_w_in[i].astype(BF16)
            if layer == 0:
                z, xs = _in_proj_first(ctx, x, norm_mix[layer], m[0], m[1], w_in, ctx_tiles)
            else:
                z = _in_proj(xs, norm_mix[layer], m[0], m[1], w_in, ctx_tiles)
            disc = [_s5_discretize(s5_lam_re[i, d], s5_lam_im[i, d], s5_log_step[i, d], s5_b_re[i, d],
                                   s5_b_im[i, d], s5_c_re[i, d], s5_c_im[i, d]) for d in range(2)]
            bb, cc, a = (jnp.stack(t) for t in zip(*disc))
            ys5 = _s5_scan(z, bb, cc, a, n_ctx, n_all)
            cos, sin = _rope_tables(n_ctx, n_lat)
            r_f, r_b = _retention(z, cos, sin, _ret_tables(ret_decay[i]), n_ctx, n_all)
            mix_args = [ys5, ys5, z, s5_d[i].reshape(1, HALF_W), s5_glu_w[i].astype(BF16),
                        s5_glu_b[i].reshape(1, HALF_W), r_f, r_b, z]
            mix_specs = [dir_spec(0), dir_spec(1), slab_spec(0),
                         pl.BlockSpec((1, HALF_W), const2),
                         pl.BlockSpec((HALF_W, HALF_W), const2),
                         pl.BlockSpec((1, HALF_W), const2),
                         slab_spec(0), slab_spec(0), slab_spec(EVEN_SPLITS[3] // HALF_W)]
            body, w_out = _out_body_even, even_w_out[i]
        else:
            w_in = odd_w_in[i]
            w_pad = jnp.concatenate([
                w_in[:, :ODD_SPLITS[1]],
                w_in[:, ODD_SPLITS[3]:],
                jnp.pad(w_in[:, ODD_SPLITS[1]:ODD_SPLITS[3]], ((0, 0), (0, ODD_PAD - 4 * GDN_HEADS)))], 1)
            z = _in_proj(xs, norm_mix[layer], m[0], m[1], w_pad.astype(BF16), ctx_tiles)
            qkv_act, bg = _gdn_prep(z, gdn_conv_w[i], gdn_a_log[i], gdn_dt_bias[i], n_ctx, n_all)
            g_f, g_b = _gdn(qkv_act, bg, n_ctx, n_all)
            hl = _lru(z, lru_conv_w[i], lru_conv_b[i], lru_w_a[i], lru_b_a[i], lru_w_x[i], lru_b_x[i],
                      lru_lam[i], n_ctx, n_all)
            mix_args = [g_f, g_b, z, gdn_norm_w[i].reshape(1, GDN_DV), hl, hl, z]
            mix_specs = [slab_spec(0), slab_spec(0), slab_spec(ODD_SPLITS[0] // HALF_W),
                         pl.BlockSpec((1, GDN_DV), const2), dir_spec(0), dir_spec(1),
                         slab_spec((ODD_LX + LRU_W) // LRU_W)]
            body, w_out = _out_body_odd, odd_w_out[i]

        xs_mid, hf, logits = _out_proj(body, mix_args, mix_specs, xs, w_out.astype(BF16), m[2],
                                       norm_ffn[layer], m[3], m[4], wr_hi, wr_lo, rb, row0, n_tiles, ctx_tiles)
        moe_row0 = ctx_moe_tiles if last else 0
        xs = _moe(layer, hf, logits, xs_mid, m[5], expert_w_gate, expert_w_up, expert_w_down,
                  moe_row0, ctx_moe_tiles, final_norm if last else None)

    return xs
```

```python
import functools

import jax
import jax.numpy as jnp
from jax import lax
from jax.experimental import pallas as pl
from jax.experimental.pallas import tpu as pltpu

D_MODEL = 1024
DEPTH = 2
GRID_W = 64
HALF_W = D_MODEL // 2
S5_GROUP = 16
S5_GROUPS = HALF_W // S5_GROUP
S5_STATE = 64
RET_HEADS = 4
RET_DV = HALF_W // RET_HEADS
RET_DK = RET_DV // 2
RET_QK = RET_HEADS * RET_DK
RET_CHUNK = 128
ROPE_BASE = 10000.0
GDN_HEADS = 4
GDN_DK = HALF_W // GDN_HEADS
GDN_DV = GDN_DK
GDN_CHUNK = 64
LRU_W = HALF_W
LRU_BLOCKS = 8
LRU_BLOCK = LRU_W // LRU_BLOCKS
LRU_C = 8.0
CONV_K = 4
CONV_PAD_LEFT = 2
N_GROUPS = 4
EXPERTS_PER_GROUP = 8
N_EXPERTS = N_GROUPS * EXPERTS_PER_GROUP
TOP_K = 2
EXPERT_FF = D_MODEL // 2
N_MOD = 6
NORM_EPS = 1e-6
EVEN_SPLITS = [HALF_W, HALF_W + RET_QK, HALF_W + 2 * RET_QK, 2 * HALF_W + 2 * RET_QK]
EVEN_IN = 3 * HALF_W + 2 * RET_QK
ODD_SPLITS = [3 * HALF_W, 4 * HALF_W, 4 * HALF_W + 2 * GDN_HEADS, 4 * HALF_W + 4 * GDN_HEADS,
              4 * HALF_W + 4 * GDN_HEADS + LRU_W]

F32 = jnp.float32
BF16 = jnp.bfloat16

SUBLANES = 8
LANES = 128
BATCH = SUBLANES
ROW_CHUNKS = D_MODEL // LANES
VMEM_LIMIT = 48 * 1024 * 1024

ROW_TILE = 512
S5_STEPS = 128
S5_SLABS = 4
S5_SLAB_CH = HALF_W // S5_SLABS
S5_SLAB_STATE = (S5_GROUPS // S5_SLABS) * S5_STATE
LRU_STEPS = 128
MOE_TILE = 256
DMA_PRIORITIES = 2
ODD_PAD = 512
ODD_LX = 4 * HALF_W
ODD_BD = ODD_LX + 2 * LRU_W
ODD_IN_PADDED = ODD_BD + ODD_PAD
GDN_QKV = 3 * HALF_W
GDN_ROWS = GDN_CHUNK * BATCH
GDN_BG_G = 2 * GDN_HEADS
GDN_NEUMANN_LEVELS = 5
GDN_BATCHES_PER_ITER = 4
ROUTER_PAD = LANES


def _dot(a, b):
    return jnp.dot(a, b, preferred_element_type=F32)


def _dot_nt(a, b):
    return lax.dot_general(a, b, (((1,), (1,)), ((), ())), preferred_element_type=F32)


def _params(*sem):
    return pltpu.CompilerParams(dimension_semantics=sem, vmem_limit_bytes=VMEM_LIMIT)


def _scan_chunk(d, i, ctx_chunks, all_chunks):
    back = jnp.where(i < ctx_chunks, ctx_chunks - 1 - i, all_chunks + ctx_chunks - 1 - i)
    return jnp.where(d == 0, i, back)


def _silu(x):
    return x * jax.nn.sigmoid(x)


def _softplus(x):
    return jnp.maximum(x, 0.0) + jnp.log(1.0 + jnp.exp(-jnp.abs(x)))


def _segment_edges(chunk, ctx_chunks, all_chunks):
    first = jnp.logical_or(chunk == 0, chunk == ctx_chunks)
    last = jnp.logical_or(chunk == ctx_chunks - 1, chunk == all_chunks - 1)
    return first, last


def _gdn_prep_body(ctx_chunks, all_chunks, prev_ref, x_ref, next_ref, bd_ref, cw_ref, al_ref, dtb_ref,
                   qkv_ref, bg_ref, ext_scr):
    rows = x_ref.shape[1]
    steps = rows // BATCH
    halo = CONV_PAD_LEFT * BATCH
    seg_first, seg_last = _segment_edges(pl.program_id(0), ctx_chunks, all_chunks)
    ext_scr[:, 0:halo, :] = jnp.where(seg_first, 0.0, prev_ref[...])
    ext_scr[:, halo:halo + rows, :] = x_ref[...]
    ext_scr[:, halo + rows:, :] = jnp.where(seg_last, 0.0, next_ref[...])
    for j in range(GDN_QKV // LANES):
        sl = slice(j * LANES, (j + 1) * LANES)
        c = sum(cw_ref[k:k + 1, sl] * ext_scr[j, k * BATCH:k * BATCH + rows, :] for k in range(CONV_K))
        a = _silu(c)
        if j < 2 * GDN_HEADS:
            a = a * lax.rsqrt(jnp.sum(a * a, -1, keepdims=True) + NORM_EPS)
        if j < GDN_HEADS:
            a = a * (GDN_DK ** -0.5)
        qkv_ref[j] = a

    bd = bd_ref[0]
    lane = lax.broadcasted_iota(jnp.int32, bd.shape, 1)
    val = jnp.where(lane < GDN_BG_G, jax.nn.sigmoid(bd), al_ref[...] * _softplus(bd + dtb_ref[...]))
    prefix = val
    for k in range((steps - 1).bit_length()):
        sh = BATCH << k
        prefix = prefix + jnp.concatenate([jnp.zeros((sh, LANES), F32), prefix[:rows - sh]], 0)
    p3 = prefix.reshape(steps, BATCH, LANES)
    suffix = (p3[steps - 1][None] - p3 + val.reshape(steps, BATCH, LANES)).reshape(rows, LANES)
    fwd_g = jnp.logical_and(lane >= GDN_BG_G, lane < GDN_BG_G + GDN_HEADS)
    bwd_g = jnp.logical_and(lane >= GDN_BG_G + GDN_HEADS, lane < GDN_BG_G + 2 * GDN_HEADS)
    bg_ref[...] = jnp.where(fwd_g, prefix, jnp.where(bwd_g, suffix, val))


def _gdn_prep(z, conv_w, a_log, dt_bias, n_ctx, n_all):
    rows = n_all * BATCH
    ctx_chunks, all_chunks = n_ctx // GDN_CHUNK, n_all // GDN_CHUNK
    halo_prev, halo_next = CONV_PAD_LEFT * BATCH, (CONV_K - 1 - CONV_PAD_LEFT) * BATCH
    slabs = GDN_QKV // LANES
    lanes16 = lambda t: jnp.zeros((1, LANES), F32).at[0, GDN_BG_G:GDN_BG_G + 2 * GDN_HEADS].set(t.reshape(-1))
    return pl.pallas_call(
        functools.partial(_gdn_prep_body, ctx_chunks, all_chunks),
        grid=(all_chunks,),
        in_specs=[pl.BlockSpec((slabs, halo_prev, LANES),
                               lambda i: (0, jnp.maximum(i * (GDN_ROWS // halo_prev) - 1, 0), 0)),
                  pl.BlockSpec((slabs, GDN_ROWS, LANES), lambda i: (0, i, 0)),
                  pl.BlockSpec((slabs, halo_next, LANES),
                               lambda i: (0, jnp.minimum((i + 1) * (GDN_ROWS // halo_next), rows // halo_next - 1), 0)),
                  pl.BlockSpec((1, GDN_ROWS, LANES), lambda i: (ODD_BD // LANES, i, 0)),
                  pl.BlockSpec((CONV_K, GDN_QKV), lambda i: (0, 0)),
                  pl.BlockSpec((1, LANES), lambda i: (0, 0)),
                  pl.BlockSpec((1, LANES), lambda i: (0, 0))],
        out_specs=[pl.BlockSpec((slabs, GDN_ROWS, LANES), lambda i: (0, i, 0)),
                   pl.BlockSpec((GDN_ROWS, LANES), lambda i: (i, 0))],
        out_shape=[jax.ShapeDtypeStruct((slabs, rows, LANES), F32), jax.ShapeDtypeStruct((rows, LANES), F32)],
        scratch_shapes=[pltpu.VMEM((slabs, GDN_ROWS + halo_prev + halo_next, LANES), F32)],
        compiler_params=_params("arbitrary"),
        name="gdn_prep",
    )(z, z, z, z, conv_w, lanes16(-jnp.exp(a_log)), lanes16(dt_bias))


def _dot_split(a, b):
    ah, bh = a.astype(BF16), b.astype(BF16)
    al, bl = (a - ah.astype(F32)).astype(BF16), (b - bh.astype(F32)).astype(BF16)
    return _dot(ah, bh) + _dot(ah, bl) + _dot(al, bh)


def _gdn_batches(batches, xf_ref, bgf_ref, xb_ref, bgb_ref, of_ref, ob_ref, s_scr):
    ii = lax.broadcasted_iota(jnp.int32, (GDN_CHUNK, GDN_CHUNK), 0)
    jj = lax.broadcasted_iota(jnp.int32, (GDN_CHUNK, GDN_CHUNK), 1)
    eye = jnp.where(ii == jj, 1.0, 0.0)
    units = []
    for b in batches:
        rows = _batch_rows(b, GDN_CHUNK)
        for d, (x_ref, bg_ref, o_ref) in enumerate(((xf_ref, bgf_ref, of_ref), (xb_ref, bgb_ref, ob_ref))):
            bg = bg_ref[rows, :]
            bg_t = bg.T
            incl = (jj <= ii) if d == 0 else (jj >= ii)
            strict = (jj < ii) if d == 0 else (jj > ii)
            last = GDN_CHUNK - 1 if d == 0 else 0
            for h in range(GDN_HEADS):
                q, k, v = x_ref[h, rows, :], x_ref[GDN_HEADS + h, rows, :], x_ref[2 * GDN_HEADS + h, rows, :]
                cb = d * GDN_HEADS + h
                cg = GDN_BG_G + cb
                beta, g_col = bg[:, cb:cb + 1], bg[:, cg:cg + 1]
                g_row, g_last = bg_t[cg:cg + 1, :], bg_t[cg:cg + 1, last:last + 1]
                decay = jnp.where(incl, jnp.exp(jnp.where(incl, g_col - g_row, 0.0)), 0.0)
                kb16 = k.astype(BF16)
                eg = jnp.exp(g_col)
                units.append(dict(
                    si=b * (2 * GDN_HEADS) + cb, h=h, o_ref=o_ref, rows=rows,
                    qk=(_dot_nt(q.astype(BF16), kb16) * decay).astype(BF16),
                    n=jnp.where(strict, -(_dot_nt(kb16, kb16) * decay * beta), 0.0),
                    rhs=jnp.concatenate([v * beta, k * (beta * eg)], 1),
                    qg=(q * eg).astype(BF16),
                    kg_t=(k * jnp.exp(g_last - g_col)).T.astype(BF16),
                    d_last=jnp.exp(g_last)))

    ts = [eye + u['n'] for u in units]
    ps = [u['n'] for u in units]
    for _ in range(GDN_NEUMANN_LEVELS):
        ps = [_dot(p.astype(BF16), p.astype(BF16)) for p in ps]
        ts = [t + _dot(t.astype(BF16), p.astype(BF16)) for t, p in zip(ts, ps)]
    sols = [_dot_split(t, u['rhs']) for t, u in zip(ts, units)]
    states = [s_scr[u['si']] for u in units]
    sbs = [s.astype(BF16) for s in states]
    v_news = [(sol[:, :GDN_DV] - _dot(sol[:, GDN_DV:].astype(BF16), sb)).astype(BF16)
              for sol, sb in zip(sols, sbs)]
    for u, s, sb, vn in zip(units, states, sbs, v_news):
        u['o_ref'][u['h'], u['rows'], :] = _dot(u['qg'], sb) + _dot(u['qk'], vn)
        s_scr[u['si']] = s * u['d_last'] + _dot(u['kg_t'], vn)


def _gdn_body(xf_ref, bgf_ref, xb_ref, bgb_ref, of_ref, ob_ref, s_scr):
    @pl.when(pl.program_id(0) == 0)
    def _():
        s_scr[...] = jnp.zeros_like(s_scr)

    def per_pair(j, carry):
        batches = [j * GDN_BATCHES_PER_ITER + p for p in range(GDN_BATCHES_PER_ITER)]
        _gdn_batches(batches, xf_ref, bgf_ref, xb_ref, bgb_ref, of_ref, ob_ref, s_scr)
        return carry

    lax.fori_loop(0, BATCH // GDN_BATCHES_PER_ITER, per_pair, 0)


def _gdn(qkv, bg, n_ctx, n_all):
    rows = n_all * BATCH
    ctx_chunks, all_chunks = n_ctx // GDN_CHUNK, n_all // GDN_CHUNK
    slabs = GDN_QKV // LANES

    def seq_specs(d):
        ch = lambda i: _scan_chunk(d, i, ctx_chunks, all_chunks)
        return [pl.BlockSpec((slabs, GDN_ROWS, LANES), lambda i: (0, ch(i), 0)),
                pl.BlockSpec((GDN_ROWS, LANES), lambda i: (ch(i), 0))]

    out_spec = lambda d: pl.BlockSpec((GDN_HEADS, GDN_ROWS, LANES),
                                      lambda i: (0, _scan_chunk(d, i, ctx_chunks, all_chunks), 0))
    o_shape = jax.ShapeDtypeStruct((GDN_HEADS, rows, LANES), F32)
    return pl.pallas_call(
        _gdn_body,
        grid=(all_chunks,),
        in_specs=seq_specs(0) + seq_specs(1),
        out_specs=[out_spec(0), out_spec(1)],
        out_shape=[o_shape, o_shape],
        scratch_shapes=[pltpu.VMEM((BATCH * 2 * GDN_HEADS, GDN_DK, GDN_DV), F32)],
        compiler_params=_params("arbitrary"),
        name="gdn_chunks",
    )(qkv, bg, qkv, bg)


ADALN_COLS = 512
ADALN_ROWS = 2 * BATCH


def _adaln_body(c_ref, w_ref, b_ref, o_ref):
    c = c_ref[...]
    s = (c * jax.nn.sigmoid(c)).astype(BF16)
    o_ref[0] = _dot(s, w_ref[0].astype(BF16)) + b_ref[0]


def _adaln_all(cond16, mod_w, mod_b):
    n_out = mod_w.shape[-1]
    return pl.pallas_call(
        _adaln_body,
        grid=(DEPTH, n_out // ADALN_COLS),
        in_specs=[pl.BlockSpec((ADALN_ROWS, D_MODEL), lambda l, j: (0, 0)),
                  pl.BlockSpec((1, D_MODEL, ADALN_COLS), lambda l, j: (l, 0, j)),
                  pl.BlockSpec((1, 1, ADALN_COLS), lambda l, j: (l, 0, j))],
        out_specs=pl.BlockSpec((1, ADALN_ROWS, ADALN_COLS), lambda l, j: (l, 0, j)),
        out_shape=jax.ShapeDtypeStruct((DEPTH, ADALN_ROWS, n_out), F32),
        compiler_params=_params("arbitrary", "arbitrary"),
        name="adaln",
    )(cond16, mod_w, mod_b.reshape(DEPTH, 1, n_out))


def _col_chunks(n, width=512):
    return [(c0, min(width, n - c0)) for c0 in range(0, n, width)]


def _norm_modulate(x3, nw, shift, scale):
    y = x3 * lax.rsqrt(jnp.mean(x3 * x3, -1, keepdims=True) + NORM_EPS) * nw
    return y * (1.0 + scale) + shift


def _project(x, nw_ref, sh_ref, sc_ref, w_ref, o_ref):
    tm = x.shape[0]
    x3 = x.reshape(tm // BATCH, BATCH, D_MODEL)
    h = _norm_modulate(x3, nw_ref[...], sh_ref[0], sc_ref[0]).reshape(tm, D_MODEL).astype(BF16)
    for c0, cw in _col_chunks(w_ref.shape[1]):
        res = _dot(h, w_ref[:, c0:c0 + cw])
        for j in range(cw // LANES):
            o_ref[c0 // LANES + j] = res[:, j * LANES:(j + 1) * LANES]


def _in_proj_body(x_ref, nw_ref, sh_ref, sc_ref, w_ref, o_ref):
    _project(x_ref[...], nw_ref, sh_ref, sc_ref, w_ref, o_ref)


def _in_proj_first_body(ctx_tiles, ctx_ref, lat_ref, nw_ref, sh_ref, sc_ref, w_ref, o_ref, xs_ref, slab):
    steps = ROW_TILE // BATCH

    def load(src_ref):
        for b in range(BATCH):
            for s in range(ROW_CHUNKS):
                slab[s, _batch_rows(b, steps), :] = src_ref[b, :, s * LANES:(s + 1) * LANES]

    @pl.when(pl.program_id(0) < ctx_tiles)
    def _():
        load(ctx_ref)

    @pl.when(pl.program_id(0) >= ctx_tiles)
    def _():
        load(lat_ref)

    x = jnp.concatenate([slab[s] for s in range(ROW_CHUNKS)], 1)
    xs_ref[...] = x
    _project(x, nw_ref, sh_ref, sc_ref, w_ref, o_ref)


def _in_proj_first(ctx, x, norm_w, shift2, scale2, w_bf16, ctx_tiles):
    rows = (ctx.shape[1] + x.shape[1]) * BATCH
    n = w_bf16.shape[1]
    steps = ROW_TILE // BATCH
    kind = lambda r: (jnp.where(r < ctx_tiles, 0, 1), 0, 0)
    return pl.pallas_call(
        functools.partial(_in_proj_first_body, ctx_tiles),
        grid=(rows // ROW_TILE,),
        in_specs=[pl.BlockSpec((BATCH, steps, D_MODEL), lambda r: (0, jnp.minimum(r, ctx_tiles - 1), 0)),
                  pl.BlockSpec((BATCH, steps, D_MODEL), lambda r: (0, jnp.maximum(r - ctx_tiles, 0), 0)),
                  pl.BlockSpec((1, D_MODEL), lambda r: (0, 0)),
                  pl.BlockSpec((1, BATCH, D_MODEL), kind),
                  pl.BlockSpec((1, BATCH, D_MODEL), kind),
                  pl.BlockSpec((D_MODEL, n), lambda r: (0, 0))],
        out_specs=[pl.BlockSpec((n // LANES, ROW_TILE, LANES), lambda r: (0, r, 0)),
                   pl.BlockSpec((ROW_TILE, D_MODEL), lambda r: (r, 0))],
        out_shape=[jax.ShapeDtypeStruct((n // LANES, rows, LANES), F32),
                   jax.ShapeDtypeStruct((rows, D_MODEL), F32)],
        scratch_shapes=[pltpu.VMEM((ROW_CHUNKS, ROW_TILE, LANES), F32)],
        compiler_params=_params("arbitrary"),
        name="in_proj_first",
    )(ctx, x, norm_w.reshape(1, D_MODEL), shift2, scale2, w_bf16)


def _in_proj(xs, norm_w, shift2, scale2, w_bf16, ctx_tiles):
    rows = xs.shape[0]
    n = w_bf16.shape[1]
    kind = lambda r: (jnp.where(r < ctx_tiles, 0, 1), 0, 0)
    return pl.pallas_call(
        _in_proj_body,
        grid=(rows // ROW_TILE,),
        in_specs=[pl.BlockSpec((ROW_TILE, D_MODEL), lambda r: (r, 0)),
                  pl.BlockSpec((1, D_MODEL), lambda r: (0, 0)),
                  pl.BlockSpec((1, BATCH, D_MODEL), kind),
                  pl.BlockSpec((1, BATCH, D_MODEL), kind),
                  pl.BlockSpec((D_MODEL, n), lambda r: (0, 0))],
        out_specs=pl.BlockSpec((n // LANES, ROW_TILE, LANES), lambda r: (0, r, 0)),
        out_shape=jax.ShapeDtypeStruct((n // LANES, rows, LANES), F32),
        compiler_params=_params("arbitrary"),
        name="in_proj",
    )(xs, norm_w.reshape(1, D_MODEL), shift2, scale2, w_bf16)


def _s5_discretize(lam_re, lam_im, log_step, b_re, b_im, c_re, c_im):
    lr, li = lam_re, lam_im
    dt = jnp.exp(log_step)[:, None]
    mag = jnp.exp(lr * dt)
    ar, ai = mag * jnp.cos(li * dt), mag * jnp.sin(li * dt)
    pr, pi = ar - 1.0, ai
    den = lr * lr + li * li
    zr, zi = (pr * lr + pi * li) / den, (pi * lr - pr * li) / den
    bbr = zr[..., None] * b_re - zi[..., None] * b_im
    bbi = zr[..., None] * b_im + zi[..., None] * b_re
    gps = S5_GROUPS // S5_SLABS
    eye = jnp.eye(gps, dtype=F32)

    def in_slab(m):
        m = m.reshape(S5_SLABS, gps, S5_STATE, S5_GROUP)
        return jnp.einsum('sgnc,gh->sgchn', m, eye).reshape(S5_SLABS, S5_SLAB_CH, S5_SLAB_STATE)

    def out_slab(m):
        m = m.reshape(S5_SLABS, gps, S5_GROUP, S5_STATE)
        return jnp.einsum('sgcn,gh->sgnhc', m, eye).reshape(S5_SLABS, S5_SLAB_STATE, S5_SLAB_CH)

    bb = jnp.concatenate([in_slab(bbr), in_slab(bbi)], -1)
    cc = jnp.concatenate([out_slab(c_re), out_slab(-c_im)], 1)
    a = jnp.concatenate([ar.reshape(S5_SLABS, S5_SLAB_STATE), ai.reshape(S5_SLABS, S5_SLAB_STATE)], -1)
    return bb.astype(BF16), cc.astype(BF16), a.reshape(1, S5_SLABS * 2 * S5_SLAB_STATE)


def _s5_body(u_ref, bb_ref, cc_ref, a_ref, y_ref, x_scr, h_scr):
    d = pl.program_id(0)
    rows = u_ref.shape[1]
    steps = rows // BATCH
    sw = 2 * S5_SLAB_STATE

    @pl.when(pl.program_id(1) == 0)
    def _():
        h_scr[...] = jnp.zeros_like(h_scr)

    for s in range(S5_SLABS):
        x_scr[:, s * sw:(s + 1) * sw] = _dot(u_ref[s].astype(BF16), bb_ref[0, s])

    for s in range(S5_SLABS):
        re0, im0 = s * sw, s * sw + S5_SLAB_STATE
        ar = jnp.broadcast_to(a_ref[0, :, re0:im0], (BATCH, S5_SLAB_STATE))
        ai = jnp.broadcast_to(a_ref[0, :, im0:im0 + S5_SLAB_STATE], (BATCH, S5_SLAB_STATE))

        def step(i, carry):
            hr, hi = carry
            t = jnp.where(d == 0, i, steps - 1 - i)
            r0 = pl.multiple_of(t * BATCH, BATCH)
            xr = x_scr[pl.ds(r0, BATCH), re0:im0]
            xi = x_scr[pl.ds(r0, BATCH), im0:im0 + S5_SLAB_STATE]
            nr = ar * hr - ai * hi + xr
            ni = ar * hi + ai * hr + xi
            x_scr[pl.ds(r0, BATCH), re0:im0] = nr
            x_scr[pl.ds(r0, BATCH), im0:im0 + S5_SLAB_STATE] = ni
            return nr, ni

        hr, hi = lax.fori_loop(0, steps, step,
                               (h_scr[:, re0:im0], h_scr[:, im0:im0 + S5_SLAB_STATE]), unroll=4)
        h_scr[:, re0:im0] = hr
        h_scr[:, im0:im0 + S5_SLAB_STATE] = hi

    for s in range(S5_SLABS):
        y_ref[0, :, s * S5_SLAB_CH:(s + 1) * S5_SLAB_CH] = _dot(
            x_scr[:, s * sw:(s + 1) * sw].astype(BF16), cc_ref[0, s])


def _s5_scan(z, bb, cc, a, n_ctx, n_all):
    rows = n_all * BATCH
    blk = S5_STEPS * BATCH
    ctx_chunks, all_chunks = n_ctx // S5_STEPS, n_all // S5_STEPS
    chunk = lambda d, i: _scan_chunk(d, i, ctx_chunks, all_chunks)
    state_w = S5_SLABS * 2 * S5_SLAB_STATE
    return pl.pallas_call(
        _s5_body,
        grid=(2, all_chunks),
        in_specs=[pl.BlockSpec((S5_SLABS, blk, LANES), lambda d, i: (0, chunk(d, i), 0)),
                  pl.BlockSpec((1, S5_SLABS, S5_SLAB_CH, 2 * S5_SLAB_STATE), lambda d, i: (d, 0, 0, 0)),
                  pl.BlockSpec((1, S5_SLABS, 2 * S5_SLAB_STATE, S5_SLAB_CH), lambda d, i: (d, 0, 0, 0)),
                  pl.BlockSpec((1, 1, state_w), lambda d, i: (d, 0, 0))],
        out_specs=pl.BlockSpec((1, blk, HALF_W), lambda d, i: (d, chunk(d, i), 0)),
        out_shape=jax.ShapeDtypeStruct((2, rows, HALF_W), F32),
        scratch_shapes=[pltpu.VMEM((blk, state_w), F32), pltpu.VMEM((BATCH, state_w), F32)],
        compiler_params=_params("arbitrary", "arbitrary"),
        name="s5_scan",
    )(z, bb, cc, a)


def _rope_tables(n_ctx, n_lat):
    half = RET_DK // 2
    nf = half // 2
    grid_rows = n_lat // GRID_W
    row = jnp.repeat(jnp.arange(grid_rows, dtype=F32), GRID_W)
    col = jnp.tile(jnp.arange(GRID_W, dtype=F32), grid_rows)
    inv = ROPE_BASE ** (-jnp.arange(nf, dtype=F32) / nf)
    ang = jnp.concatenate([row[:, None] * inv, col[:, None] * inv], -1)
    cos = jnp.tile(jnp.cos(ang), (1, 2 * RET_HEADS))
    sin = jnp.tile(jnp.concatenate([-jnp.sin(ang), jnp.sin(ang)], -1), (1, RET_HEADS))
    cos = jnp.concatenate([jnp.ones((n_ctx, RET_QK), F32), cos], 0)
    sin = jnp.concatenate([jnp.zeros((n_ctx, RET_QK), F32), sin], 0)
    return cos, sin


def _ret_tables(ret_decay):
    lg = jax.nn.log_sigmoid(ret_decay)
    pos = jnp.arange(RET_CHUNK, dtype=F32)
    diff = pos[:, None] - pos[None, :]
    f_mask, b_mask = diff >= 0, diff < 0
    dm_f = jnp.where(f_mask, jnp.exp(lg[0][:, None, None] * jnp.where(f_mask, diff, 0.0)), 0.0)
    dm_b = jnp.where(b_mask, jnp.exp(lg[1][:, None, None] * jnp.where(b_mask, -diff, 0.0)), 0.0)
    dmat = jnp.stack([dm_f, dm_b])
    heads = lambda t: jnp.repeat(t, RET_DK, axis=-1)
    kdec = jnp.stack([heads(jnp.exp(lg[0][None] * (RET_CHUNK - 1 - pos)[:, None])),
                      heads(jnp.exp(lg[1][None] * pos[:, None]))])
    qdec = jnp.stack([heads(jnp.exp(lg[0][None] * (pos + 1)[:, None])),
                      heads(jnp.exp(lg[1][None] * (RET_CHUNK - pos)[:, None]))])
    blk = jnp.kron(jnp.eye(RET_HEADS, dtype=F32), jnp.ones((RET_DK, RET_DV), F32))
    sdec = jnp.repeat(jnp.exp(lg * RET_CHUNK), RET_DK, axis=-1)[:, :, None] * blk[None]
    return dmat, kdec, qdec, sdec, blk


def _rope(t, cos, sin):
    lane = lax.broadcasted_iota(jnp.int32, t.shape, 1)
    first = (lane % RET_DK) < (RET_DK // 2)
    partner = jnp.where(first, pltpu.roll(t, RET_QK - RET_DK // 2, 1), pltpu.roll(t, RET_DK // 2, 1))
    return t * cos + partner * sin


def _batch_rows(b, steps):
    return pl.ds(b, steps, stride=BATCH)


def _ret_direction(d, b, q_ref, k_ref, v_ref, cos_ref, sin_ref, dm_ref, kd_ref, qd_ref, sd_ref, blk_ref,
                   o_ref, s_scr):
    rows = _batch_rows(b, RET_CHUNK)
    cos, sin = cos_ref[...], sin_ref[...]
    q = _rope(jnp.concatenate([q_ref[0, rows, :], q_ref[1, rows, :]], 1), cos, sin)
    k = _rope(jnp.concatenate([k_ref[0, rows, :], k_ref[1, rows, :]], 1), cos, sin) * (RET_DK ** -0.5)
    v = [v_ref[h, rows, :].astype(BF16) for h in range(RET_HEADS)]
    kb = k.astype(BF16)
    lane = lax.broadcasted_iota(jnp.int32, q.shape, 1)
    si = d * BATCH + b
    state = s_scr[si]
    o_inter = _dot((q * qd_ref[d]).astype(BF16), state.astype(BF16))
    for h in range(RET_HEADS):
        qh = jnp.where(lane // RET_DK == h, q, 0.0).astype(BF16)
        scores = _dot_nt(qh, kb) * dm_ref[d, h]
        o_ref[h, rows, :] = _dot(scores.astype(BF16), v[h]) + o_inter[:, h * RET_DV:(h + 1) * RET_DV]
    kv = _dot((k * kd_ref[d]).T.astype(BF16), jnp.concatenate(v, 1))
    s_scr[si] = sd_ref[d] * state + blk_ref[...] * kv


def _ret_body(qf, kf, vf, cf, sf, qb, kb, vb, cb, sb, dm_ref, kd_ref, qd_ref, sd_ref, blk_ref,
              of_ref, ob_ref, s_scr):
    @pl.when(pl.program_id(0) == 0)
    def _():
        s_scr[...] = jnp.zeros_like(s_scr)

    def per_batch(b, carry):
        _ret_direction(0, b, qf, kf, vf, cf, sf, dm_ref, kd_ref, qd_ref, sd_ref, blk_ref, of_ref, s_scr)
        _ret_direction(1, b, qb, kb, vb, cb, sb, dm_ref, kd_ref, qd_ref, sd_ref, blk_ref, ob_ref, s_scr)
        return carry

    lax.fori_loop(0, BATCH, per_batch, 0)


def _retention(z, cos, sin, tables, n_ctx, n_all):
    dmat, kdec, qdec, sdec, blk = tables
    rows = n_all * BATCH
    blk_rows = RET_CHUNK * BATCH
    ctx_chunks, all_chunks = n_ctx // RET_CHUNK, n_all // RET_CHUNK
    qk_slabs, v_slabs = RET_QK // LANES, HALF_W // LANES
    q0, k0, v0 = EVEN_SPLITS[0] // RET_QK, EVEN_SPLITS[1] // RET_QK, EVEN_SPLITS[2] // HALF_W

    def seq_specs(d):
        ch = lambda i: _scan_chunk(d, i, ctx_chunks, all_chunks)
        return [pl.BlockSpec((qk_slabs, blk_rows, LANES), lambda i: (q0, ch(i), 0)),
                pl.BlockSpec((qk_slabs, blk_rows, LANES), lambda i: (k0, ch(i), 0)),
                pl.BlockSpec((v_slabs, blk_rows, LANES), lambda i: (v0, ch(i), 0)),
                pl.BlockSpec((RET_CHUNK, RET_QK), lambda i: (ch(i), 0)),
                pl.BlockSpec((RET_CHUNK, RET_QK), lambda i: (ch(i), 0))]

    def out_spec(d):
        return pl.BlockSpec((RET_HEADS, blk_rows, LANES),
                            lambda i: (0, _scan_chunk(d, i, ctx_chunks, all_chunks), 0))

    const = lambda nd: (lambda i: (0,) * nd)
    o_shape = jax.ShapeDtypeStruct((RET_HEADS, rows, LANES), F32)
    return pl.pallas_call(
        _ret_body,
        grid=(all_chunks,),
        in_specs=seq_specs(0) + seq_specs(1) + [
            pl.BlockSpec(dmat.shape, const(4)), pl.BlockSpec(kdec.shape, const(3)),
            pl.BlockSpec(qdec.shape, const(3)), pl.BlockSpec(sdec.shape, const(3)),
            pl.BlockSpec(blk.shape, const(2))],
        out_specs=[out_spec(0), out_spec(1)],
        out_shape=[o_shape, o_shape],
        scratch_shapes=[pltpu.VMEM((2 * BATCH, RET_QK, HALF_W), F32)],
        compiler_params=_params("arbitrary"),
        name="retention",
    )(z, z, z, cos, sin, z, z, z, cos, sin, dmat, kdec, qdec, sdec, blk)


def _lru_body(ctx_chunks, all_chunks, prev_ref, x_ref, next_ref, cw_ref, cb_ref, wg_ref, bg_ref, sp_ref,
              h_ref, a_scr, b_scr, h_scr):
    d, i = pl.program_id(0), pl.program_id(1)
    rows = x_ref.shape[1]
    steps = rows // BATCH
    chunk = _scan_chunk(d, i, ctx_chunks, all_chunks)

    @pl.when(i == 0)
    def _():
        h_scr[...] = jnp.zeros_like(h_scr)

    seg_first, seg_last = _segment_edges(chunk, ctx_chunks, all_chunks)
    prev = jnp.where(seg_first, 0.0, prev_ref[...])
    nxt = jnp.where(seg_last, 0.0, next_ref[...])
    slabs = []
    for s in range(LRU_W // LANES):
        sl = slice(s * LANES, (s + 1) * LANES)
        ext = jnp.concatenate([prev[s], x_ref[s], nxt[s]], 0)
        slabs.append(cb_ref[:, sl] + sum(cw_ref[k:k + 1, sl] * ext[k * BATCH:k * BATCH + rows]
                                         for k in range(CONV_K)))
    xs = jnp.concatenate(slabs, 1)
    gates = _dot(xs.astype(BF16), wg_ref[0]) + bg_ref[0]
    r = jax.nn.sigmoid(gates[:, :LRU_W])
    ig = jax.nn.sigmoid(gates[:, LRU_W:])
    log_a = -r * sp_ref[0]
    a = jnp.exp(log_a)
    a_scr[...] = a
    b_scr[...] = jnp.sqrt(1.0 - jnp.exp(2.0 * log_a)) * (ig * xs)

    def step(j, h):
        t = jnp.where(d == 0, j, steps - 1 - j)
        r0 = pl.multiple_of(t * BATCH, BATCH)
        h = a_scr[pl.ds(r0, BATCH), :] * h + b_scr[pl.ds(r0, BATCH), :]
        h_ref[0, pl.ds(r0, BATCH), :] = h
        return h

    h_scr[...] = lax.fori_loop(0, steps, step, h_scr[...], unroll=8)


def _lru(z, conv_w, conv_b, w_a, b_a, w_x, b_x, lam, n_ctx, n_all):
    rows = n_all * BATCH
    blk = LRU_STEPS * BATCH
    ctx_chunks, all_chunks = n_ctx // LRU_STEPS, n_all // LRU_STEPS
    chunk = lambda d, i: _scan_chunk(d, i, ctx_chunks, all_chunks)
    col = ODD_LX // LRU_W
    slabs = LRU_W // LANES
    eye = jnp.eye(LRU_BLOCKS, dtype=F32)
    dense = lambda w: jnp.einsum('dkij,kl->dkilj', w, eye).reshape(2, LRU_W, LRU_W)
    wg = jnp.concatenate([dense(w_a), dense(w_x)], -1).astype(BF16)
    bg = jnp.concatenate([b_a, b_x], -1).reshape(2, 1, 2 * LRU_W)
    sp = (LRU_C * jax.nn.softplus(-lam)).reshape(2, 1, LRU_W)
    halo_prev, halo_next = 2 * BATCH, BATCH
    return pl.pallas_call(
        functools.partial(_lru_body, ctx_chunks, all_chunks),
        grid=(2, all_chunks),
        in_specs=[pl.BlockSpec((slabs, halo_prev, LANES),
                               lambda d, i: (col, jnp.maximum(chunk(d, i) * (blk // halo_prev) - 1, 0), 0)),
                  pl.BlockSpec((slabs, blk, LANES), lambda d, i: (col, chunk(d, i), 0)),
                  pl.BlockSpec((slabs, halo_next, LANES),
                               lambda d, i: (col, jnp.minimum((chunk(d, i) + 1) * (blk // halo_next),
                                                              rows // halo_next - 1), 0)),
                  pl.BlockSpec((CONV_K, LRU_W), lambda d, i: (0, 0)),
                  pl.BlockSpec((1, LRU_W), lambda d, i: (0, 0)),
                  pl.BlockSpec((1, LRU_W, 2 * LRU_W), lambda d, i: (d, 0, 0)),
                  pl.BlockSpec((1, 1, 2 * LRU_W), lambda d, i: (d, 0, 0)),
                  pl.BlockSpec((1, 1, LRU_W), lambda d, i: (d, 0, 0))],
        out_specs=pl.BlockSpec((1, blk, LRU_W), lambda d, i: (d, chunk(d, i), 0)),
        out_shape=jax.ShapeDtypeStruct((2, rows, LRU_W), F32),
        scratch_shapes=[pltpu.VMEM((blk, LRU_W), F32), pltpu.VMEM((blk, LRU_W), F32),
                        pltpu.VMEM((BATCH, LRU_W), F32)],
        compiler_params=_params("arbitrary", "arbitrary"),
        name="rglru",
    )(z, z, z, conv_w, conv_b.reshape(1, LRU_W), wg, bg, sp)


def _residual_norm_route(upd, x_ref, g2_ref, nw_ref, sh_ref, sc_ref, wrh_ref, wrl_ref, rb_ref,
                         xo_ref, hf_ref, lg_ref):
    tm = x_ref.shape[0]
    x3 = x_ref[...].reshape(tm // BATCH, BATCH, D_MODEL) + g2_ref[0] * upd.reshape(tm // BATCH, BATCH, D_MODEL)
    xo_ref[...] = x3.reshape(tm, D_MODEL)
    hf = _norm_modulate(x3, nw_ref[...], sh_ref[0], sc_ref[0]).reshape(tm, D_MODEL)
    _store_rows(hf_ref, hf)
    hi = hf.astype(BF16)
    lo = (hf - hi.astype(F32)).astype(BF16)
    lg_ref[...] = _dot(hi, wrh_ref[...]) + _dot(hi, wrl_ref[...]) + _dot(lo, wrh_ref[...]) + rb_ref[...]


def _gelu_tanh(x):
    return 0.5 * x * (1.0 + jnp.tanh(0.7978845608028654 * (x + 0.044715 * x * x * x)))


def _out_body_even(yf_ref, yb_ref, u_ref, d_ref, gw_ref, gb_ref, of_ref, ob_ref, gate_ref, wo_ref, *rest):
    u = jnp.concatenate([u_ref[s] for s in range(HALF_W // LANES)], 1)
    ys = yf_ref[0] + yb_ref[0] + d_ref[...] * u
    y = _gelu_tanh(ys)
    s5o = y * jax.nn.sigmoid(_dot(y.astype(BF16), gw_ref[...]) + gb_ref[...])
    upd = _dot(s5o.astype(BF16), wo_ref[0:HALF_W, :])
    for h in range(RET_HEADS):
        oh = of_ref[h] + ob_ref[h]
        mu = jnp.mean(oh, -1, keepdims=True)
        var = jnp.mean(jnp.square(oh - mu), -1, keepdims=True)
        rh = _silu(gate_ref[h]) * ((oh - mu) * lax.rsqrt(var + NORM_EPS))
        upd = upd + _dot(rh.astype(BF16), wo_ref[HALF_W + h * RET_DV:HALF_W + (h + 1) * RET_DV, :])
    _residual_norm_route(upd, *rest)


def _out_body_odd(gf_ref, gb_ref, zg_ref, gnw_ref, hf_ref, hb_ref, lg_ref, wo_ref, *rest):
    lg = jnp.concatenate([lg_ref[s] for s in range(LRU_W // LANES)], 1)
    lru = (hf_ref[0] + hb_ref[0]) * _gelu_tanh(lg)
    upd = _dot(lru.astype(BF16), wo_ref[HALF_W:, :])
    for h in range(GDN_HEADS):
        oh = gf_ref[h] + gb_ref[h]
        oh = oh * lax.rsqrt(jnp.mean(oh * oh, -1, keepdims=True) + NORM_EPS) * gnw_ref[...] * _silu(zg_ref[h])
        upd = upd + _dot(oh.astype(BF16), wo_ref[h * GDN_DV:(h + 1) * GDN_DV, :])
    _residual_norm_route(upd, *rest)


def _out_proj(body, mix_args, mix_specs, xs, w_out, gate2, norm_w, shift2, scale2, wr_hi, wr_lo, rb,
              row0_tiles, n_tiles, ctx_tiles):
    kind = lambda r: (jnp.where(r + row0_tiles < ctx_tiles, 0, 1), 0, 0)
    const2 = lambda r: (0, 0)
    rows = n_tiles * ROW_TILE
    return pl.pallas_call(
        body,
        grid=(n_tiles,),
        in_specs=mix_specs + [
            pl.BlockSpec((D_MODEL, D_MODEL), const2),
            pl.BlockSpec((ROW_TILE, D_MODEL), lambda r: (r + row0_tiles, 0)),
            pl.BlockSpec((1, BATCH, D_MODEL), kind),
            pl.BlockSpec((1, D_MODEL), const2),
            pl.BlockSpec((1, BATCH, D_MODEL), kind),
            pl.BlockSpec((1, BATCH, D_MODEL), kind),
            pl.BlockSpec((D_MODEL, ROUTER_PAD), const2),
            pl.BlockSpec((D_MODEL, ROUTER_PAD), const2),
            pl.BlockSpec((1, ROUTER_PAD), const2)],
        out_specs=[pl.BlockSpec((ROW_TILE, D_MODEL), lambda r: (r, 0)),
                   pl.BlockSpec((ROW_TILE * ROW_CHUNKS, LANES), lambda r: (r, 0)),
                   pl.BlockSpec((ROW_TILE, ROUTER_PAD), lambda r: (r, 0))],
        out_shape=[jax.ShapeDtypeStruct((rows, D_MODEL), F32),
                   jax.ShapeDtypeStruct((rows * ROW_CHUNKS, LANES), F32),
                   jax.ShapeDtypeStruct((rows, ROUTER_PAD), F32)],
        compiler_params=_params("arbitrary"),
        name="out_proj",
    )(*mix_args, w_out, xs, gate2, norm_w.reshape(1, D_MODEL), shift2, scale2, wr_hi, wr_lo, rb)


def _route(logits):
    n_tok = logits.shape[0]
    g_logits = logits[:, :N_GROUPS]
    g_idx = jnp.argmax(g_logits, -1)
    g_w = jnp.max(jax.nn.softmax(g_logits, -1), -1, keepdims=True)
    e_logits = logits[:, N_GROUPS:N_GROUPS + N_EXPERTS].reshape(n_tok, N_GROUPS, EXPERTS_PER_GROUP)
    chosen = g_idx[:, None, None] == jnp.arange(N_GROUPS)[None, :, None]
    e_in = jnp.sum(jnp.where(chosen, e_logits, 0.0), 1)
    top_v, top_i = lax.top_k(e_in, TOP_K)
    top_w = jax.nn.softmax(top_v, -1) * g_w
    return (g_idx[:, None] * EXPERTS_PER_GROUP + top_i).astype(jnp.int32), top_w


def _dispatch_plan(expert_ids, top_w, n_tiles):
    n_tok = expert_ids.shape[0]
    n_asg = n_tok * TOP_K
    flat_e = expert_ids.T.reshape(n_asg)
    flat_w = top_w.T.reshape(n_asg)
    order = jnp.argsort(flat_e, stable=True).astype(jnp.int32)
    inv = jnp.argsort(order).astype(jnp.int32)
    onehot = flat_e[:, None] == jnp.arange(N_EXPERTS, dtype=jnp.int32)[None]
    counts = jnp.sum(onehot, 0, dtype=jnp.int32)
    padded = ((counts + MOE_TILE - 1) // MOE_TILE) * MOE_TILE
    pend = jnp.cumsum(padded)
    pstart = pend - padded
    start = jnp.cumsum(counts) - counts
    shift = pstart - start
    pos = inv + jnp.sum(jnp.where(onehot, shift[None], 0), 1)
    tile_start = jnp.arange(n_tiles, dtype=jnp.int32) * MOE_TILE
    tile_e = jnp.minimum(jnp.sum(tile_start[:, None] >= pend[None], 1), N_EXPERTS - 1).astype(jnp.int32)
    tile_valid = (tile_start < pend[-1]).astype(jnp.int32)
    tile_hot = tile_e[:, None] == jnp.arange(N_EXPERTS, dtype=jnp.int32)[None]
    per_tile = lambda tbl: jnp.sum(jnp.where(tile_hot, tbl[None], 0), 1)
    tile_rank0 = tile_start - per_tile(shift)
    tile_end = per_tile(start + counts)
    lane = jnp.arange(MOE_TILE, dtype=jnp.int32)[None]
    rank = (tile_rank0[:, None] + lane).reshape(n_tiles * MOE_TILE)
    row_ok = (tile_rank0[:, None] + lane < tile_end[:, None]).reshape(n_tiles * MOE_TILE)
    src_asg = order[jnp.clip(rank, 0, n_asg - 1)]
    src_tok = jnp.where(row_ok, src_asg % n_tok, 0)
    row_w = jnp.where(row_ok, flat_w[src_asg], 0.0)
    return src_tok, row_w, pos.reshape(TOP_K, n_tok), tile_e, tile_valid


def _gather_rows(idx_ref, src_hbm, dst_vmem, sem, n_rows):
    def issue(j, c):
        for p in range(DMA_PRIORITIES):
            r = j * DMA_PRIORITIES + p
            src = pl.multiple_of(idx_ref[0, 0, r] * ROW_CHUNKS, ROW_CHUNKS)
            dst = pl.multiple_of(r * ROW_CHUNKS, ROW_CHUNKS)
            pltpu.make_async_copy(src_hbm.at[pl.ds(src, ROW_CHUNKS)], dst_vmem.at[pl.ds(dst, ROW_CHUNKS)],
                                  sem).start(priority=p)
        return c

    lax.fori_loop(0, n_rows // DMA_PRIORITIES, issue, 0, unroll=8)


def _wait_rows(src_hbm, dst_vmem, sem, n_rows):
    pltpu.make_async_copy(src_hbm.at[pl.ds(0, n_rows * ROW_CHUNKS)], dst_vmem, sem).wait()


def _rows_2d(ref, n_rows):
    return jnp.concatenate([ref[pl.ds(s, n_rows, stride=ROW_CHUNKS), :] for s in range(ROW_CHUNKS)], 1)


def _store_rows(ref, val):
    for s in range(ROW_CHUNKS):
        ref[pl.ds(s, val.shape[0], stride=ROW_CHUNKS), :] = val[:, s * LANES:(s + 1) * LANES]


def _expert_body(te_ref, tv_ref, idx_ref, idx_next_ref, hf_hbm, cw_ref, wg32_ref, wu32_ref, wd32_ref, y_ref,
                 xbuf, wg_ref, wu_ref, wd_ref, sem):
    i = pl.program_id(0)
    n = pl.num_programs(0)
    slot = i % 2
    nxt = jnp.minimum(i + 1, n - 1)

    @pl.when(jnp.logical_or(i == 0, te_ref[i] != te_ref[jnp.maximum(i - 1, 0)]))
    def _():
        wg_ref[0] = wg32_ref[0, 0].astype(BF16)
        wu_ref[0] = wu32_ref[0, 0].astype(BF16)
        wd_ref[0] = wd32_ref[0, 0].astype(BF16)

    @pl.when(jnp.logical_and(i == 0, tv_ref[0] == 1))
    def _():
        _gather_rows(idx_ref, hf_hbm, xbuf.at[0], sem.at[0], MOE_TILE)

    @pl.when(jnp.logical_and(i + 1 < n, tv_ref[nxt] == 1))
    def _():
        _gather_rows(idx_next_ref, hf_hbm, xbuf.at[1 - slot], sem.at[1 - slot], MOE_TILE)

    @pl.when(tv_ref[i] == 1)
    def _():
        _wait_rows(hf_hbm, xbuf.at[slot], sem.at[slot], MOE_TILE)
        x = _rows_2d(xbuf.at[slot], MOE_TILE).astype(BF16)
        g = _dot(x, wg_ref[0])
        u = _dot(x, wu_ref[0])
        act = _silu(g) * u * cw_ref[:, 0:1]
        _store_rows(y_ref, _dot(act.astype(BF16), wd_ref[0]))

    @pl.when(tv_ref[i] == 0)
    def _():
        y_ref[...] = jnp.zeros_like(y_ref)


def _experts(layer, tile_e, tile_valid, src_tok, hf, cw_rows, wg, wu, wd):
    n_tiles = tile_e.shape[0]
    tile_spec = pl.BlockSpec((MOE_TILE * ROW_CHUNKS, LANES), lambda i, te, tv: (i, 0))
    idx3 = src_tok.reshape(n_tiles, 1, MOE_TILE)
    grid_spec = pltpu.PrefetchScalarGridSpec(
        num_scalar_prefetch=2,
        grid=(n_tiles,),
        in_specs=[pl.BlockSpec((1, 1, MOE_TILE), lambda i, te, tv: (i, 0, 0), memory_space=pltpu.SMEM),
                  pl.BlockSpec((1, 1, MOE_TILE), lambda i, te, tv: (jnp.minimum(i + 1, n_tiles - 1), 0, 0),
                               memory_space=pltpu.SMEM),
                  pl.BlockSpec(memory_space=pl.ANY),
                  pl.BlockSpec((MOE_TILE, LANES), lambda i, te, tv: (i, 0)),
                  pl.BlockSpec((1, 1, D_MODEL, EXPERT_FF), lambda i, te, tv: (layer, te[i], 0, 0)),
                  pl.BlockSpec((1, 1, D_MODEL, EXPERT_FF), lambda i, te, tv: (layer, te[i], 0, 0)),
                  pl.BlockSpec((1, 1, EXPERT_FF, D_MODEL), lambda i, te, tv: (layer, te[i], 0, 0))],
        out_specs=tile_spec,
        scratch_shapes=[pltpu.VMEM((2, MOE_TILE * ROW_CHUNKS, LANES), F32),
                        pltpu.VMEM((1, D_MODEL, EXPERT_FF), BF16), pltpu.VMEM((1, D_MODEL, EXPERT_FF), BF16),
                        pltpu.VMEM((1, EXPERT_FF, D_MODEL), BF16), pltpu.SemaphoreType.DMA((2,))])
    return pl.pallas_call(
        _expert_body,
        grid_spec=grid_spec,
        out_shape=jax.ShapeDtypeStruct((n_tiles * MOE_TILE * ROW_CHUNKS, LANES), F32),
        compiler_params=_params("arbitrary"),
        name="moe_experts",
    )(tile_e, tile_valid, idx3, idx3, hf, cw_rows, wg, wu, wd)


def _combine_rows(p0_ref, p1_ref, p0n_ref, p1n_ref, y_hbm, x_ref, g5_ref, buf, sem):
    r = pl.program_id(0)
    n = pl.num_programs(0)
    tm = x_ref.shape[0]
    slot = r % 2

    def fetch(p0, p1, s):
        _gather_rows(p0, y_hbm, buf.at[s, 0], sem.at[s, 0], tm)
        _gather_rows(p1, y_hbm, buf.at[s, 1], sem.at[s, 1], tm)

    @pl.when(r == 0)
    def _():
        fetch(p0_ref, p1_ref, 0)

    @pl.when(r + 1 < n)
    def _():
        fetch(p0n_ref, p1n_ref, 1 - slot)

    _wait_rows(y_hbm, buf.at[slot, 0], sem.at[slot, 0], tm)
    _wait_rows(y_hbm, buf.at[slot, 1], sem.at[slot, 1], tm)
    f3 = (_rows_2d(buf.at[slot, 0], tm) + _rows_2d(buf.at[slot, 1], tm)).reshape(tm // BATCH, BATCH, D_MODEL)
    return (x_ref[...].reshape(tm // BATCH, BATCH, D_MODEL) + g5_ref[0] * f3).reshape(tm, D_MODEL)


def _combine_body(p0_ref, p1_ref, p0n_ref, p1n_ref, y_hbm, x_ref, g5_ref, o_ref, buf, sem):
    o_ref[...] = _combine_rows(p0_ref, p1_ref, p0n_ref, p1n_ref, y_hbm, x_ref, g5_ref, buf, sem)


def _combine_final_body(p0_ref, p1_ref, p0n_ref, p1n_ref, y_hbm, x_ref, g5_ref, fw_ref, o_ref, buf, sem, slab):
    x = _combine_rows(p0_ref, p1_ref, p0n_ref, p1n_ref, y_hbm, x_ref, g5_ref, buf, sem)
    steps = x.shape[0] // BATCH
    xn = x * lax.rsqrt(jnp.mean(x * x, -1, keepdims=True) + NORM_EPS) * fw_ref[...]
    for s in range(ROW_CHUNKS):
        slab[s] = xn[:, s * LANES:(s + 1) * LANES]
    for b in range(BATCH):
        for s in range(ROW_CHUNKS):
            o_ref[b, :, s * LANES:(s + 1) * LANES] = slab[s, _batch_rows(b, steps), :]


def _combine(pos, y_sorted, xs, gate2, row0_tiles, ctx_tiles, final_w=None):
    n_tok = pos.shape[1]
    n_tiles = n_tok // MOE_TILE
    kind = lambda r: (jnp.where(r + row0_tiles < ctx_tiles, 0, 1), 0, 0)
    idx_spec = lambda k, ahead: pl.BlockSpec(
        (1, 1, MOE_TILE), lambda r: (k * n_tiles + jnp.minimum(r + ahead, n_tiles - 1), 0, 0),
        memory_space=pltpu.SMEM)
    p2 = pos.reshape(TOP_K * n_tiles, 1, MOE_TILE)
    in_specs = [idx_spec(0, 0), idx_spec(1, 0), idx_spec(0, 1), idx_spec(1, 1),
                pl.BlockSpec(memory_space=pl.ANY),
                pl.BlockSpec((MOE_TILE, D_MODEL), lambda r: (r, 0)),
                pl.BlockSpec((1, BATCH, D_MODEL), kind)]
    scratch = [pltpu.VMEM((2, TOP_K, MOE_TILE * ROW_CHUNKS, LANES), F32), pltpu.SemaphoreType.DMA((2, TOP_K))]
    if final_w is None:
        return pl.pallas_call(
            _combine_body,
            grid=(n_tiles,),
            in_specs=in_specs,
            out_specs=pl.BlockSpec((MOE_TILE, D_MODEL), lambda r: (r, 0)),
            out_shape=jax.ShapeDtypeStruct((n_tok, D_MODEL), F32),
            scratch_shapes=scratch,
            compiler_params=_params("arbitrary"),
            name="moe_combine",
        )(p2, p2, p2, p2, y_sorted, xs, gate2)
    steps = MOE_TILE // BATCH
    return pl.pallas_call(
        _combine_final_body,
        grid=(n_tiles,),
        in_specs=in_specs + [pl.BlockSpec((1, D_MODEL), lambda r: (0, 0))],
        out_specs=pl.BlockSpec((BATCH, steps, D_MODEL), lambda r: (0, r, 0)),
        out_shape=jax.ShapeDtypeStruct((BATCH, n_tok // BATCH, D_MODEL), F32),
        scratch_shapes=scratch + [pltpu.VMEM((ROW_CHUNKS, MOE_TILE, LANES), F32)],
        compiler_params=_params("arbitrary"),
        name="moe_combine_final",
    )(p2, p2, p2, p2, y_sorted, xs, gate2, final_w.reshape(1, D_MODEL))


def _moe(layer, hf, logits, xs, gate2, wg, wu, wd, row0_tiles, ctx_tiles, final_w=None):
    n_tok = hf.shape[0] // ROW_CHUNKS
    n_tiles = n_tok * TOP_K // MOE_TILE + N_EXPERTS
    expert_ids, top_w = _route(logits)
    src_tok, row_w, pos, tile_e, tile_valid = _dispatch_plan(expert_ids, top_w, n_tiles)
    cw_rows = jnp.broadcast_to(row_w[:, None], (n_tiles * MOE_TILE, LANES))
    y_sorted = _experts(layer, tile_e, tile_valid, src_tok, hf, cw_rows, wg, wu, wd)
    return _combine(pos, y_sorted, xs, gate2, row0_tiles, ctx_tiles, final_w)


def _kinds(mod_l, k):
    return jnp.stack([mod_l[BATCH:, k], mod_l[:BATCH, k]])


def kernel(x, c, ctx, c_ctx, mod_w, mod_b, norm_mix, norm_ffn, router_group_w, router_group_b,
           router_expert_w, router_expert_b, expert_w_gate, expert_w_up, expert_w_down,
           even_w_in, even_w_out, s5_lam_re, s5_lam_im, s5_log_step, s5_b_re, s5_b_im, s5_c_re,
           s5_c_im, s5_d, s5_glu_w, s5_glu_b, ret_decay, odd_w_in, odd_w_out, gdn_conv_w, gdn_a_log,
           gdn_dt_bias, gdn_norm_w, lru_conv_w, lru_conv_b, lru_w_a, lru_b_a, lru_w_x, lru_b_x,
           lru_lam, final_norm):
    bsz, n_lat, _ = x.shape
    n_ctx = ctx.shape[1]
    n_all = n_ctx + n_lat
    assert bsz == BATCH and n_ctx % RET_CHUNK == 0 and n_lat % RET_CHUNK == 0
    assert (n_ctx * BATCH) % ROW_TILE == 0 and (n_lat * BATCH) % ROW_TILE == 0
    ctx_tiles = n_ctx * BATCH // ROW_TILE
    all_tiles = n_all * BATCH // ROW_TILE
    ctx_moe_tiles = n_ctx * BATCH // MOE_TILE

    xs = None
    cond16 = jnp.concatenate([c, jnp.broadcast_to(c_ctx[None], (BATCH, D_MODEL))], 0)
    mod = _adaln_all(cond16, mod_w, mod_b).reshape(DEPTH, ADALN_ROWS, N_MOD, D_MODEL)

    for layer in range(DEPTH):
        last = layer == DEPTH - 1
        i = layer // 2
        m = [_kinds(mod[layer], k) for k in range(N_MOD)]
        wr = jnp.zeros((D_MODEL, ROUTER_PAD), F32)
        wr = wr.at[:, :N_GROUPS].set(router_group_w[layer]).at[:, N_GROUPS:N_GROUPS + N_EXPERTS].set(
            router_expert_w[layer])
        wr_hi = wr.astype(BF16)
        wr_lo = (wr - wr_hi.astype(F32)).astype(BF16)
        rb = jnp.zeros((1, ROUTER_PAD), F32)
        rb = rb.at[0, :N_GROUPS].set(router_group_b[layer]).at[0, N_GROUPS:N_GROUPS + N_EXPERTS].set(
            router_expert_b[layer])
        row0 = ctx_tiles if last else 0
        n_tiles = all_tiles - row0
        slab_spec = lambda sb: pl.BlockSpec((HALF_W // LANES, ROW_TILE, LANES), lambda r: (sb, r + row0, 0))
        dir_spec = lambda d: pl.BlockSpec((1, ROW_TILE, HALF_W), lambda r: (d, r + row0, 0))
        const2 = lambda r: (0, 0)

        if layer % 2 == 0:
            w_in = even_w_in[i].astype(BF16)
            if layer == 0:
                z, xs = _in_proj_first(ctx, x, norm_mix[layer], m[0], m[1], w_in, ctx_tiles)
            else:
                z = _in_proj(xs, norm_mix[layer], m[0], m[1], w_in, ctx_tiles)
            disc = [_s5_discretize(s5_lam_re[i, d], s5_lam_im[i, d], s5_log_step[i, d], s5_b_re[i, d],
                                   s5_b_im[i, d], s5_c_re[i, d], s5_c_im[i, d]) for d in range(2)]
            bb, cc, a = (jnp.stack(t) for t in zip(*disc))
            ys5 = _s5_scan(z, bb, cc, a, n_ctx, n_all)
            cos, sin = _rope_tables(n_ctx, n_lat)
            r_f, r_b = _retention(z, cos, sin, _ret_tables(ret_decay[i]), n_ctx, n_all)
            mix_args = [ys5, ys5, z, s5_d[i].reshape(1, HALF_W), s5_glu_w[i].astype(BF16),
                        s5_glu_b[i].reshape(1, HALF_W), r_f, r_b, z]
            mix_specs = [dir_spec(0), dir_spec(1), slab_spec(0),
                         pl.BlockSpec((1, HALF_W), const2),
                         pl.BlockSpec((HALF_W, HALF_W), const2),
                         pl.BlockSpec((1, HALF_W), const2),
                         slab_spec(0), slab_spec(0), slab_spec(EVEN_SPLITS[3] // HALF_W)]
            body, w_out = _out_body_even, even_w_out[i]
        else:
            w_in = odd_w_in[i]
            w_pad = jnp.concatenate([
                w_in[:, :ODD_SPLITS[1]],
                w_in[:, ODD_SPLITS[3]:],
                jnp.pad(w_in[:, ODD_SPLITS[1]:ODD_SPLITS[3]], ((0, 0), (0, ODD_PAD - 4 * GDN_HEADS)))], 1)
            z = _in_proj(xs, norm_mix[layer], m[0], m[1], w_pad.astype(BF16), ctx_tiles)
            qkv_act, bg = _gdn_prep(z, gdn_conv_w[i], gdn_a_log[i], gdn_dt_bias[i], n_ctx, n_all)
            g_f, g_b = _gdn(qkv_act, bg, n_ctx, n_all)
            hl = _lru(z, lru_conv_w[i], lru_conv_b[i], lru_w_a[i], lru_b_a[i], lru_w_x[i], lru_b_x[i],
                      lru_lam[i], n_ctx, n_all)
            mix_args = [g_f, g_b, z, gdn_norm_w[i].reshape(1, GDN_DV), hl, hl, z]
            mix_specs = [slab_spec(0), slab_spec(0), slab_spec(ODD_SPLITS[0] // HALF_W),
                         pl.BlockSpec((1, GDN_DV), const2), dir_spec(0), dir_spec(1),
                         slab_spec((ODD_LX + LRU_W) // LRU_W)]
            body, w_out = _out_body_odd, odd_w_out[i]

        xs_mid, hf, logits = _out_proj(body, mix_args, mix_specs, xs, w_out.astype(BF16), m[2],
                                       norm_ffn[layer], m[3], m[4], wr_hi, wr_lo, rb, row0, n_tiles, ctx_tiles)
        moe_row0 = ctx_moe_tiles if last else 0
        xs = _moe(layer, hf, logits, xs_mid, m[5], expert_w_gate, expert_w_up, expert_w_down,
                  moe_row0, ctx_moe_tiles, final_norm if last else None)

    return xs
```

```python
import functools

import jax
import jax.numpy as jnp
from jax import lax
from jax.experimental import pallas as pl
from jax.experimental.pallas import tpu as pltpu

D_MODEL = 1024
DEPTH = 2
GRID_W = 64
HALF_W = D_MODEL // 2
S5_GROUP = 16
S5_GROUPS = HALF_W // S5_GROUP
S5_STATE = 64
RET_HEADS = 4
RET_DV = HALF_W // RET_HEADS
RET_DK = RET_DV // 2
RET_QK = RET_HEADS * RET_DK
RET_CHUNK = 128
ROPE_BASE = 10000.0
GDN_HEADS = 4
GDN_DK = HALF_W // GDN_HEADS
GDN_DV = GDN_DK
GDN_CHUNK = 64
LRU_W = HALF_W
LRU_BLOCKS = 8
LRU_BLOCK = LRU_W // LRU_BLOCKS
LRU_C = 8.0
CONV_K = 4
CONV_PAD_LEFT = 2
N_GROUPS = 4
EXPERTS_PER_GROUP = 8
N_EXPERTS = N_GROUPS * EXPERTS_PER_GROUP
TOP_K = 2
EXPERT_FF = D_MODEL // 2
N_MOD = 6
NORM_EPS = 1e-6
EVEN_SPLITS = [HALF_W, HALF_W + RET_QK, HALF_W + 2 * RET_QK, 2 * HALF_W + 2 * RET_QK]
EVEN_IN = 3 * HALF_W + 2 * RET_QK
ODD_SPLITS = [3 * HALF_W, 4 * HALF_W, 4 * HALF_W + 2 * GDN_HEADS, 4 * HALF_W + 4 * GDN_HEADS,
              4 * HALF_W + 4 * GDN_HEADS + LRU_W]

F32 = jnp.float32
BF16 = jnp.bfloat16

SUBLANES = 8
LANES = 128
BATCH = SUBLANES
ROW_CHUNKS = D_MODEL // LANES
VMEM_LIMIT = 48 * 1024 * 1024

ROW_TILE = 512
S5_STEPS = 128
S5_SLABS = 4
S5_SLAB_CH = HALF_W // S5_SLABS
S5_SLAB_STATE = (S5_GROUPS // S5_SLABS) * S5_STATE
LRU_STEPS = 128
MOE_TILE = 256
DMA_PRIORITIES = 2
ODD_PAD = 512
ODD_LX = 4 * HALF_W
ODD_BD = ODD_LX + 2 * LRU_W
ODD_IN_PADDED = ODD_BD + ODD_PAD
GDN_QKV = 3 * HALF_W
GDN_ROWS = GDN_CHUNK * BATCH
GDN_BG_G = 2 * GDN_HEADS
GDN_NEUMANN_LEVELS = 5
GDN_BATCHES_PER_ITER = 4
ROUTER_PAD = LANES


def _dot(a, b):
    return jnp.dot(a, b, preferred_element_type=F32)


def _dot_nt(a, b):
    return lax.dot_general(a, b, (((1,), (1,)), ((), ())), preferred_element_type=F32)


def _params(*sem):
    return pltpu.CompilerParams(dimension_semantics=sem, vmem_limit_bytes=VMEM_LIMIT)


def _scan_chunk(d, i, ctx_chunks, all_chunks):
    back = jnp.where(i < ctx_chunks, ctx_chunks - 1 - i, all_chunks + ctx_chunks - 1 - i)
    return jnp.where(d == 0, i, back)


def _silu(x):
    return x * jax.nn.sigmoid(x)


def _softplus(x):
    return jnp.maximum(x, 0.0) + jnp.log(1.0 + jnp.exp(-jnp.abs(x)))


def _segment_edges(chunk, ctx_chunks, all_chunks):
    first = jnp.logical_or(chunk == 0, chunk == ctx_chunks)
    last = jnp.logical_or(chunk == ctx_chunks - 1, chunk == all_chunks - 1)
    return first, last


def _gdn_prep_body(ctx_chunks, all_chunks, prev_ref, x_ref, next_ref, bd_ref, cw_ref, al_ref, dtb_ref,
                   qkv_ref, bg_ref, ext_scr):
    rows = x_ref.shape[1]
    steps = rows // BATCH
    halo = CONV_PAD_LEFT * BATCH
    seg_first, seg_last = _segment_edges(pl.program_id(0), ctx_chunks, all_chunks)
    ext_scr[:, 0:halo, :] = jnp.where(seg_first, 0.0, prev_ref[...])
    ext_scr[:, halo:halo + rows, :] = x_ref[...]
    ext_scr[:, halo + rows:, :] = jnp.where(seg_last, 0.0, next_ref[...])
    for j in range(GDN_QKV // LANES):
        sl = slice(j * LANES, (j + 1) * LANES)
        c = sum(cw_ref[k:k + 1, sl] * ext_scr[j, k * BATCH:k * BATCH + rows, :] for k in range(CONV_K))
        a = _silu(c)
        if j < 2 * GDN_HEADS:
            a = a * lax.rsqrt(jnp.sum(a * a, -1, keepdims=True) + NORM_EPS)
        if j < GDN_HEADS:
            a = a * (GDN_DK ** -0.5)
        qkv_ref[j] = a

    bd = bd_ref[0]
    lane = lax.broadcasted_iota(jnp.int32, bd.shape, 1)
    val = jnp.where(lane < GDN_BG_G, jax.nn.sigmoid(bd), al_ref[...] * _softplus(bd + dtb_ref[...]))
    prefix = val
    for k in range((steps - 1).bit_length()):
        sh = BATCH << k
        prefix = prefix + jnp.concatenate([jnp.zeros((sh, LANES), F32), prefix[:rows - sh]], 0)
    p3 = prefix.reshape(steps, BATCH, LANES)
    suffix = (p3[steps - 1][None] - p3 + val.reshape(steps, BATCH, LANES)).reshape(rows, LANES)
    fwd_g = jnp.logical_and(lane >= GDN_BG_G, lane < GDN_BG_G + GDN_HEADS)
    bwd_g = jnp.logical_and(lane >= GDN_BG_G + GDN_HEADS, lane < GDN_BG_G + 2 * GDN_HEADS)
    bg_ref[...] = jnp.where(fwd_g, prefix, jnp.where(bwd_g, suffix, val))


def _gdn_prep(z, conv_w, a_log, dt_bias, n_ctx, n_all):
    rows = n_all * BATCH
    ctx_chunks, all_chunks = n_ctx // GDN_CHUNK, n_all // GDN_CHUNK
    halo_prev, halo_next = CONV_PAD_LEFT * BATCH, (CONV_K - 1 - CONV_PAD_LEFT) * BATCH
    slabs = GDN_QKV // LANES
    lanes16 = lambda t: jnp.zeros((1, LANES), F32).at[0, GDN_BG_G:GDN_BG_G + 2 * GDN_HEADS].set(t.reshape(-1))
    return pl.pallas_call(
        functools.partial(_gdn_prep_body, ctx_chunks, all_chunks),
        grid=(all_chunks,),
        in_specs=[pl.BlockSpec((slabs, halo_prev, LANES),
                               lambda i: (0, jnp.maximum(i * (GDN_ROWS // halo_prev) - 1, 0), 0)),
                  pl.BlockSpec((slabs, GDN_ROWS, LANES), lambda i: (0, i, 0)),
                  pl.BlockSpec((slabs, halo_next, LANES),
                               lambda i: (0, jnp.minimum((i + 1) * (GDN_ROWS // halo_next), rows // halo_next - 1), 0)),
                  pl.BlockSpec((1, GDN_ROWS, LANES), lambda i: (ODD_BD // LANES, i, 0)),
                  pl.BlockSpec((CONV_K, GDN_QKV), lambda i: (0, 0)),
                  pl.BlockSpec((1, LANES), lambda i: (0, 0)),
                  pl.BlockSpec((1, LANES), lambda i: (0, 0))],
        out_specs=[pl.BlockSpec((slabs, GDN_ROWS, LANES), lambda i: (0, i, 0)),
                   pl.BlockSpec((GDN_ROWS, LANES), lambda i: (i, 0))],
        out_shape=[jax.ShapeDtypeStruct((slabs, rows, LANES), F32), jax.ShapeDtypeStruct((rows, LANES), F32)],
        scratch_shapes=[pltpu.VMEM((slabs, GDN_ROWS + halo_prev + halo_next, LANES), F32)],
        compiler_params=_params("arbitrary"),
        name="gdn_prep",
    )(z, z, z, z, conv_w, lanes16(-jnp.exp(a_log)), lanes16(dt_bias))


def _dot_split(a, b):
    ah, bh = a.astype(BF16), b.astype(BF16)
    al, bl = (a - ah.astype(F32)).astype(BF16), (b - bh.astype(F32)).astype(BF16)
    return _dot(ah, bh) + _dot(ah, bl) + _dot(al, bh)


def _gdn_batches(batches, xf_ref, bgf_ref, xb_ref, bgb_ref, of_ref, ob_ref, s_scr):
    ii = lax.broadcasted_iota(jnp.int32, (GDN_CHUNK, GDN_CHUNK), 0)
    jj = lax.broadcasted_iota(jnp.int32, (GDN_CHUNK, GDN_CHUNK), 1)
    eye = jnp.where(ii == jj, 1.0, 0.0)
    units = []
    for b in batches:
        rows = _batch_rows(b, GDN_CHUNK)
        for d, (x_ref, bg_ref, o_ref) in enumerate(((xf_ref, bgf_ref, of_ref), (xb_ref, bgb_ref, ob_ref))):
            bg = bg_ref[rows, :]
            bg_t = bg.T
            incl = (jj <= ii) if d == 0 else (jj >= ii)
            strict = (jj < ii) if d == 0 else (jj > ii)
            last = GDN_CHUNK - 1 if d == 0 else 0
            for h in range(GDN_HEADS):
                q, k, v = x_ref[h, rows, :], x_ref[GDN_HEADS + h, rows, :], x_ref[2 * GDN_HEADS + h, rows, :]
                cb = d * GDN_HEADS + h
                cg = GDN_BG_G + cb
                beta, g_col = bg[:, cb:cb + 1], bg[:, cg:cg + 1]
                g_row, g_last = bg_t[cg:cg + 1, :], bg_t[cg:cg + 1, last:last + 1]
                decay = jnp.where(incl, jnp.exp(jnp.where(incl, g_col - g_row, 0.0)), 0.0)
                kb16 = k.astype(BF16)
                eg = jnp.exp(g_col)
                units.append(dict(
                    si=b * (2 * GDN_HEADS) + cb, h=h, o_ref=o_ref, rows=rows,
                    qk=(_dot_nt(q.astype(BF16), kb16) * decay).astype(BF16),
                    n=jnp.where(strict, -(_dot_nt(kb16, kb16) * decay * beta), 0.0),
                    rhs=jnp.concatenate([v * beta, k * (beta * eg)], 1),
                    qg=(q * eg).astype(BF16),
                    kg_t=(k * jnp.exp(g_last - g_col)).T.astype(BF16),
                    d_last=jnp.exp(g_last)))

    ts = [eye + u['n'] for u in units]
    ps = [u['n'] for u in units]
    for _ in range(GDN_NEUMANN_LEVELS):
        ps = [_dot(p.astype(BF16), p.astype(BF16)) for p in ps]
        ts = [t + _dot(t.astype(BF16), p.astype(BF16)) for t, p in zip(ts, ps)]
    sols = [_dot_split(t, u['rhs']) for t, u in zip(ts, units)]
    states = [s_scr[u['si']] for u in units]
    sbs = [s.astype(BF16) for s in states]
    v_news = [(sol[:, :GDN_DV] - _dot(sol[:, GDN_DV:].astype(BF16), sb)).astype(BF16)
              for sol, sb in zip(sols, sbs)]
    for u, s, sb, vn in zip(units, states, sbs, v_news):
        u['o_ref'][u['h'], u['rows'], :] = _dot(u['qg'], sb) + _dot(u['qk'], vn)
        s_scr[u['si']] = s * u['d_last'] + _dot(u['kg_t'], vn)


def _gdn_body(xf_ref, bgf_ref, xb_ref, bgb_ref, of_ref, ob_ref, s_scr):
    @pl.when(pl.program_id(0) == 0)
    def _():
        s_scr[...] = jnp.zeros_like(s_scr)

    def per_pair(j, carry):
        batches = [j * GDN_BATCHES_PER_ITER + p for p in range(GDN_BATCHES_PER_ITER)]
        _gdn_batches(batches, xf_ref, bgf_ref, xb_ref, bgb_ref, of_ref, ob_ref, s_scr)
        return carry

    lax.fori_loop(0, BATCH // GDN_BATCHES_PER_ITER, per_pair, 0)


def _gdn(qkv, bg, n_ctx, n_all):
    rows = n_all * BATCH
    ctx_chunks, all_chunks = n_ctx // GDN_CHUNK, n_all // GDN_CHUNK
    slabs = GDN_QKV // LANES

    def seq_specs(d):
        ch = lambda i: _scan_chunk(d, i, ctx_chunks, all_chunks)
        return [pl.BlockSpec((slabs, GDN_ROWS, LANES), lambda i: (0, ch(i), 0)),
                pl.BlockSpec((GDN_ROWS, LANES), lambda i: (ch(i), 0))]

    out_spec = lambda d: pl.BlockSpec((GDN_HEADS, GDN_ROWS, LANES),
                                      lambda i: (0, _scan_chunk(d, i, ctx_chunks, all_chunks), 0))
    o_shape = jax.ShapeDtypeStruct((GDN_HEADS, rows, LANES), F32)
    return pl.pallas_call(
        _gdn_body,
        grid=(all_chunks,),
        in_specs=seq_specs(0) + seq_specs(1),
        out_specs=[out_spec(0), out_spec(1)],
        out_shape=[o_shape, o_shape],
        scratch_shapes=[pltpu.VMEM((BATCH * 2 * GDN_HEADS, GDN_DK, GDN_DV), F32)],
        compiler_params=_params("arbitrary"),
        name="gdn_chunks",
    )(qkv, bg, qkv, bg)


ADALN_COLS = 512
ADALN_ROWS = 2 * BATCH


def _adaln_body(c_ref, w_ref, b_ref, o_ref):
    c = c_ref[...]
    s = (c * jax.nn.sigmoid(c)).astype(BF16)
    o_ref[0] = _dot(s, w_ref[0].astype(BF16)) + b_ref[0]


def _adaln_all(cond16, mod_w, mod_b):
    n_out = mod_w.shape[-1]
    return pl.pallas_call(
        _adaln_body,
        grid=(DEPTH, n_out // ADALN_COLS),
        in_specs=[pl.BlockSpec((ADALN_ROWS, D_MODEL), lambda l, j: (0, 0)),
                  pl.BlockSpec((1, D_MODEL, ADALN_COLS), lambda l, j: (l, 0, j)),
                  pl.BlockSpec((1, 1, ADALN_COLS), lambda l, j: (l, 0, j))],
        out_specs=pl.BlockSpec((1, ADALN_ROWS, ADALN_COLS), lambda l, j: (l, 0, j)),
        out_shape=jax.ShapeDtypeStruct((DEPTH, ADALN_ROWS, n_out), F32),
        compiler_params=_params("arbitrary", "arbitrary"),
        name="adaln",
    )(cond16, mod_w, mod_b.reshape(DEPTH, 1, n_out))


def _col_chunks(n, width=512):
    return [(c0, min(width, n - c0)) for c0 in range(0, n, width)]


def _norm_modulate(x3, nw, shift, scale):
    y = x3 * lax.rsqrt(jnp.mean(x3 * x3, -1, keepdims=True) + NORM_EPS) * nw
    return y * (1.0 + scale) + shift


def _project(x, nw_ref, sh_ref, sc_ref, w_ref, o_ref):
    tm = x.shape[0]
    x3 = x.reshape(tm // BATCH, BATCH, D_MODEL)
    h = _norm_modulate(x3, nw_ref[...], sh_ref[0], sc_ref[0]).reshape(tm, D_MODEL).astype(BF16)
    for c0, cw in _col_chunks(w_ref.shape[1]):
        res = _dot(h, w_ref[:, c0:c0 + cw])
        for j in range(cw // LANES):
            o_ref[c0 // LANES + j] = res[:, j * LANES:(j + 1) * LANES]


def _in_proj_body(x_ref, nw_ref, sh_ref, sc_ref, w_ref, o_ref):
    _project(x_ref[...], nw_ref, sh_ref, sc_ref, w_ref, o_ref)


def _in_proj_first_body(ctx_tiles, ctx_ref, lat_ref, nw_ref, sh_ref, sc_ref, w_ref, o_ref, xs_ref, slab):
    steps = ROW_TILE // BATCH

    def load(src_ref):
        for b in range(BATCH):
            for s in range(ROW_CHUNKS):
                slab[s, _batch_rows(b, steps), :] = src_ref[b, :, s * LANES:(s + 1) * LANES]

    @pl.when(pl.program_id(0) < ctx_tiles)
    def _():
        load(ctx_ref)

    @pl.when(pl.program_id(0) >= ctx_tiles)
    def _():
        load(lat_ref)

    x = jnp.concatenate([slab[s] for s in range(ROW_CHUNKS)], 1)
    xs_ref[...] = x
    _project(x, nw_ref, sh_ref, sc_ref, w_ref, o_ref)


def _in_proj_first(ctx, x, norm_w, shift2, scale2, w_bf16, ctx_tiles):
    rows = (ctx.shape[1] + x.shape[1]) * BATCH
    n = w_bf16.shape[1]
    steps = ROW_TILE // BATCH
    kind = lambda r: (jnp.where(r < ctx_tiles, 0, 1), 0, 0)
    return pl.pallas_call(
        functools.partial(_in_proj_first_body, ctx_tiles),
        grid=(rows // ROW_TILE,),
        in_specs=[pl.BlockSpec((BATCH, steps, D_MODEL), lambda r: (0, jnp.minimum(r, ctx_tiles - 1), 0)),
                  pl.BlockSpec((BATCH, steps, D_MODEL), lambda r: (0, jnp.maximum(r - ctx_tiles, 0), 0)),
                  pl.BlockSpec((1, D_MODEL), lambda r: (0, 0)),
                  pl.BlockSpec((1, BATCH, D_MODEL), kind),
                  pl.BlockSpec((1, BATCH, D_MODEL), kind),
                  pl.BlockSpec((D_MODEL, n), lambda r: (0, 0))],
        out_specs=[pl.BlockSpec((n // LANES, ROW_TILE, LANES), lambda r: (0, r, 0)),
                   pl.BlockSpec((ROW_TILE, D_MODEL), lambda r: (r, 0))],
        out_shape=[jax.ShapeDtypeStruct((n // LANES, rows, LANES), F32),
                   jax.ShapeDtypeStruct((rows, D_MODEL), F32)],
        scratch_shapes=[pltpu.VMEM((ROW_CHUNKS, ROW_TILE, LANES), F32)],
        compiler_params=_params("arbitrary"),
        name="in_proj_first",
    )(ctx, x, norm_w.reshape(1, D_MODEL), shift2, scale2, w_bf16)


def _in_proj(xs, norm_w, shift2, scale2, w_bf16, ctx_tiles):
    rows = xs.shape[0]
    n = w_bf16.shape[1]
    kind = lambda r: (jnp.where(r < ctx_tiles, 0, 1), 0, 0)
    return pl.pallas_call(
        _in_proj_body,
        grid=(rows // ROW_TILE,),
        in_specs=[pl.BlockSpec((ROW_TILE, D_MODEL), lambda r: (r, 0)),
                  pl.BlockSpec((1, D_MODEL), lambda r: (0, 0)),
                  pl.BlockSpec((1, BATCH, D_MODEL), kind),
                  pl.BlockSpec((1, BATCH, D_MODEL), kind),
                  pl.BlockSpec((D_MODEL, n), lambda r: (0, 0))],
        out_specs=pl.BlockSpec((n // LANES, ROW_TILE, LANES), lambda r: (0, r, 0)),
        out_shape=jax.ShapeDtypeStruct((n // LANES, rows, LANES), F32),
        compiler_params=_params("arbitrary"),
        name="in_proj",
    )(xs, norm_w.reshape(1, D_MODEL), shift2, scale2, w_bf16)


def _s5_discretize(lam_re, lam_im, log_step, b_re, b_im, c_re, c_im):
    lr, li = lam_re, lam_im
    dt = jnp.exp(log_step)[:, None]
    mag = jnp.exp(lr * dt)
    ar, ai = mag * jnp.cos(li * dt), mag * jnp.sin(li * dt)
    pr, pi = ar - 1.0, ai
    den = lr * lr + li * li
    zr, zi = (pr * lr + pi * li) / den, (pi * lr - pr * li) / den
    bbr = zr[..., None] * b_re - zi[..., None] * b_im
    bbi = zr[..., None] * b_im + zi[..., None] * b_re
    gps = S5_GROUPS // S5_SLABS
    eye = jnp.eye(gps, dtype=F32)

    def in_slab(m):
        m = m.reshape(S5_SLABS, gps, S5_STATE, S5_GROUP)
        return jnp.einsum('sgnc,gh->sgchn', m, eye).reshape(S5_SLABS, S5_SLAB_CH, S5_SLAB_STATE)

    def out_slab(m):
        m = m.reshape(S5_SLABS, gps, S5_GROUP, S5_STATE)
        return jnp.einsum('sgcn,gh->sgnhc', m, eye).reshape(S5_SLABS, S5_SLAB_STATE, S5_SLAB_CH)

    bb = jnp.concatenate([in_slab(bbr), in_slab(bbi)], -1)
    cc = jnp.concatenate([out_slab(c_re), out_slab(-c_im)], 1)
    a = jnp.concatenate([ar.reshape(S5_SLABS, S5_SLAB_STATE), ai.reshape(S5_SLABS, S5_SLAB_STATE)], -1)
    return bb.astype(BF16), cc.astype(BF16), a.reshape(1, S5_SLABS * 2 * S5_SLAB_STATE)


def _s5_body(u_ref, bb_ref, cc_ref, a_ref, y_ref, x_scr, h_scr):
    d = pl.program_id(0)
    rows = u_ref.shape[1]
    steps = rows // BATCH
    sw = 2 * S5_SLAB_STATE

    @pl.when(pl.program_id(1) == 0)
    def _():
        h_scr[...] = jnp.zeros_like(h_scr)

    for s in range(S5_SLABS):
        x_scr[:, s * sw:(s + 1) * sw] = _dot(u_ref[s].astype(BF16), bb_ref[0, s])

    for s in range(S5_SLABS):
        re0, im0 = s * sw, s * sw + S5_SLAB_STATE
        ar = jnp.broadcast_to(a_ref[0, :, re0:im0], (BATCH, S5_SLAB_STATE))
        ai = jnp.broadcast_to(a_ref[0, :, im0:im0 + S5_SLAB_STATE], (BATCH, S5_SLAB_STATE))

        def step(i, carry):
            hr, hi = carry
            t = jnp.where(d == 0, i, steps - 1 - i)
            r0 = pl.multiple_of(t * BATCH, BATCH)
            xr = x_scr[pl.ds(r0, BATCH), re0:im0]
            xi = x_scr[pl.ds(r0, BATCH), im0:im0 + S5_SLAB_STATE]
            nr = ar * hr - ai * hi + xr
            ni = ar * hi + ai * hr + xi
            x_scr[pl.ds(r0, BATCH), re0:im0] = nr
            x_scr[pl.ds(r0, BATCH), im0:im0 + S5_SLAB_STATE] = ni
            return nr, ni

        hr, hi = lax.fori_loop(0, steps, step,
                               (h_scr[:, re0:im0], h_scr[:, im0:im0 + S5_SLAB_STATE]), unroll=8)
        h_scr[:, re0:im0] = hr
        h_scr[:, im0:im0 + S5_SLAB_STATE] = hi

    for s in range(S5_SLABS):
        y_ref[0, :, s * S5_SLAB_CH:(s + 1) * S5_SLAB_CH] = _dot(
            x_scr[:, s * sw:(s + 1) * sw].astype(BF16), cc_ref[0, s])


def _s5_scan(z, bb, cc, a, n_ctx, n_all):
    rows = n_all * BATCH
    blk = S5_STEPS * BATCH
    ctx_chunks, all_chunks = n_ctx // S5_STEPS, n_all // S5_STEPS
    chunk = lambda d, i: _scan_chunk(d, i, ctx_chunks, all_chunks)
    state_w = S5_SLABS * 2 * S5_SLAB_STATE
    return pl.pallas_call(
        _s5_body,
        grid=(2, all_chunks),
        in_specs=[pl.BlockSpec((S5_SLABS, blk, LANES), lambda d, i: (0, chunk(d, i), 0)),
                  pl.BlockSpec((1, S5_SLABS, S5_SLAB_CH, 2 * S5_SLAB_STATE), lambda d, i: (d, 0, 0, 0)),
                  pl.BlockSpec((1, S5_SLABS, 2 * S5_SLAB_STATE, S5_SLAB_CH), lambda d, i: (d, 0, 0, 0)),
                  pl.BlockSpec((1, 1, state_w), lambda d, i: (d, 0, 0))],
        out_specs=pl.BlockSpec((1, blk, HALF_W), lambda d, i: (d, chunk(d, i), 0)),
        out_shape=jax.ShapeDtypeStruct((2, rows, HALF_W), F32),
        scratch_shapes=[pltpu.VMEM((blk, state_w), F32), pltpu.VMEM((BATCH, state_w), F32)],
        compiler_params=_params("arbitrary", "arbitrary"),
        name="s5_scan",
    )(z, bb, cc, a)


def _rope_tables(n_ctx, n_lat):
    half = RET_DK // 2
    nf = half // 2
    grid_rows = n_lat // GRID_W
    row = jnp.repeat(jnp.arange(grid_rows, dtype=F32), GRID_W)
    col = jnp.tile(jnp.arange(GRID_W, dtype=F32), grid_rows)
    inv = ROPE_BASE ** (-jnp.arange(nf, dtype=F32) / nf)
    ang = jnp.concatenate([row[:, None] * inv, col[:, None] * inv], -1)
    cos = jnp.tile(jnp.cos(ang), (1, 2 * RET_HEADS))
    sin = jnp.tile(jnp.concatenate([-jnp.sin(ang), jnp.sin(ang)], -1), (1, RET_HEADS))
    cos = jnp.concatenate([jnp.ones((n_ctx, RET_QK), F32), cos], 0)
    sin = jnp.concatenate([jnp.zeros((n_ctx, RET_QK), F32), sin], 0)
    return cos, sin


def _ret_tables(ret_decay):
    lg = jax.nn.log_sigmoid(ret_decay)
    pos = jnp.arange(RET_CHUNK, dtype=F32)
    diff = pos[:, None] - pos[None, :]
    f_mask, b_mask = diff >= 0, diff < 0
    dm_f = jnp.where(f_mask, jnp.exp(lg[0][:, None, None] * jnp.where(f_mask, diff, 0.0)), 0.0)
    dm_b = jnp.where(b_mask, jnp.exp(lg[1][:, None, None] * jnp.where(b_mask, -diff, 0.0)), 0.0)
    dmat = jnp.stack([dm_f, dm_b])
    heads = lambda t: jnp.repeat(t, RET_DK, axis=-1)
    kdec = jnp.stack([heads(jnp.exp(lg[0][None] * (RET_CHUNK - 1 - pos)[:, None])),
                      heads(jnp.exp(lg[1][None] * pos[:, None]))])
    qdec = jnp.stack([heads(jnp.exp(lg[0][None] * (pos + 1)[:, None])),
                      heads(jnp.exp(lg[1][None] * (RET_CHUNK - pos)[:, None]))])
    blk = jnp.kron(jnp.eye(RET_HEADS, dtype=F32), jnp.ones((RET_DK, RET_DV), F32))
    sdec = jnp.repeat(jnp.exp(lg * RET_CHUNK), RET_DK, axis=-1)[:, :, None] * blk[None]
    return dmat, kdec, qdec, sdec, blk


def _rope(t, cos, sin):
    lane = lax.broadcasted_iota(jnp.int32, t.shape, 1)
    first = (lane % RET_DK) < (RET_DK // 2)
    partner = jnp.where(first, pltpu.roll(t, RET_QK - RET_DK // 2, 1), pltpu.roll(t, RET_DK // 2, 1))
    return t * cos + partner * sin


def _batch_rows(b, steps):
    return pl.ds(b, steps, stride=BATCH)


def _ret_direction(d, b, q_ref, k_ref, v_ref, cos_ref, sin_ref, dm_ref, kd_ref, qd_ref, sd_ref, blk_ref,
                   o_ref, s_scr):
    rows = _batch_rows(b, RET_CHUNK)
    cos, sin = cos_ref[...], sin_ref[...]
    q = _rope(jnp.concatenate([q_ref[0, rows, :], q_ref[1, rows, :]], 1), cos, sin)
    k = _rope(jnp.concatenate([k_ref[0, rows, :], k_ref[1, rows, :]], 1), cos, sin) * (RET_DK ** -0.5)
    v = [v_ref[h, rows, :].astype(BF16) for h in range(RET_HEADS)]
    kb = k.astype(BF16)
    lane = lax.broadcasted_iota(jnp.int32, q.shape, 1)
    si = d * BATCH + b
    state = s_scr[si]
    o_inter = _dot((q * qd_ref[d]).astype(BF16), state.astype(BF16))
    for h in range(RET_HEADS):
        qh = jnp.where(lane // RET_DK == h, q, 0.0).astype(BF16)
        scores = _dot_nt(qh, kb) * dm_ref[d, h]
        o_ref[h, rows, :] = _dot(scores.astype(BF16), v[h]) + o_inter[:, h * RET_DV:(h + 1) * RET_DV]
    kv = _dot((k * kd_ref[d]).T.astype(BF16), jnp.concatenate(v, 1))
    s_scr[si] = sd_ref[d] * state + blk_ref[...] * kv


def _ret_body(qf, kf, vf, cf, sf, qb, kb, vb, cb, sb, dm_ref, kd_ref, qd_ref, sd_ref, blk_ref,
              of_ref, ob_ref, s_scr):
    @pl.when(pl.program_id(0) == 0)
    def _():
        s_scr[...] = jnp.zeros_like(s_scr)

    def per_batch(b, carry):
        _ret_direction(0, b, qf, kf, vf, cf, sf, dm_ref, kd_ref, qd_ref, sd_ref, blk_ref, of_ref, s_scr)
        _ret_direction(1, b, qb, kb, vb, cb, sb, dm_ref, kd_ref, qd_ref, sd_ref, blk_ref, ob_ref, s_scr)
        return carry

    lax.fori_loop(0, BATCH, per_batch, 0)


def _retention(z, cos, sin, tables, n_ctx, n_all):
    dmat, kdec, qdec, sdec, blk = tables
    rows = n_all * BATCH
    blk_rows = RET_CHUNK * BATCH
    ctx_chunks, all_chunks = n_ctx // RET_CHUNK, n_all // RET_CHUNK
    qk_slabs, v_slabs = RET_QK // LANES, HALF_W // LANES
    q0, k0, v0 = EVEN_SPLITS[0] // RET_QK, EVEN_SPLITS[1] // RET_QK, EVEN_SPLITS[2] // HALF_W

    def seq_specs(d):
        ch = lambda i: _scan_chunk(d, i, ctx_chunks, all_chunks)
        return [pl.BlockSpec((qk_slabs, blk_rows, LANES), lambda i: (q0, ch(i), 0)),
                pl.BlockSpec((qk_slabs, blk_rows, LANES), lambda i: (k0, ch(i), 0)),
                pl.BlockSpec((v_slabs, blk_rows, LANES), lambda i: (v0, ch(i), 0)),
                pl.BlockSpec((RET_CHUNK, RET_QK), lambda i: (ch(i), 0)),
                pl.BlockSpec((RET_CHUNK, RET_QK), lambda i: (ch(i), 0))]

    def out_spec(d):
        return pl.BlockSpec((RET_HEADS, blk_rows, LANES),
                            lambda i: (0, _scan_chunk(d, i, ctx_chunks, all_chunks), 0))

    const = lambda nd: (lambda i: (0,) * nd)
    o_shape = jax.ShapeDtypeStruct((RET_HEADS, rows, LANES), F32)
    return pl.pallas_call(
        _ret_body,
        grid=(all_chunks,),
        in_specs=seq_specs(0) + seq_specs(1) + [
            pl.BlockSpec(dmat.shape, const(4)), pl.BlockSpec(kdec.shape, const(3)),
            pl.BlockSpec(qdec.shape, const(3)), pl.BlockSpec(sdec.shape, const(3)),
            pl.BlockSpec(blk.shape, const(2))],
        out_specs=[out_spec(0), out_spec(1)],
        out_shape=[o_shape, o_shape],
        scratch_shapes=[pltpu.VMEM((2 * BATCH, RET_QK, HALF_W), F32)],
        compiler_params=_params("arbitrary"),
        name="retention",
    )(z, z, z, cos, sin, z, z, z, cos, sin, dmat, kdec, qdec, sdec, blk)


def _lru_body(ctx_chunks, all_chunks, prev_ref, x_ref, next_ref, cw_ref, cb_ref, wg_ref, bg_ref, sp_ref,
              h_ref, a_scr, b_scr, h_scr):
    d, i = pl.program_id(0), pl.program_id(1)
    rows = x_ref.shape[1]
    steps = rows // BATCH
    chunk = _scan_chunk(d, i, ctx_chunks, all_chunks)

    @pl.when(i == 0)
    def _():
        h_scr[...] = jnp.zeros_like(h_scr)

    seg_first, seg_last = _segment_edges(chunk, ctx_chunks, all_chunks)
    prev = jnp.where(seg_first, 0.0, prev_ref[...])
    nxt = jnp.where(seg_last, 0.0, next_ref[...])
    slabs = []
    for s in range(LRU_W // LANES):
        sl = slice(s * LANES, (s + 1) * LANES)
        ext = jnp.concatenate([prev[s], x_ref[s], nxt[s]], 0)
        slabs.append(cb_ref[:, sl] + sum(cw_ref[k:k + 1, sl] * ext[k * BATCH:k * BATCH + rows]
                                         for k in range(CONV_K)))
    xs = jnp.concatenate(slabs, 1)
    gates = _dot(xs.astype(BF16), wg_ref[0]) + bg_ref[0]
    r = jax.nn.sigmoid(gates[:, :LRU_W])
    ig = jax.nn.sigmoid(gates[:, LRU_W:])
    log_a = -r * sp_ref[0]
    a = jnp.exp(log_a)
    a_scr[...] = a
    b_scr[...] = jnp.sqrt(1.0 - jnp.exp(2.0 * log_a)) * (ig * xs)

    def step(j, h):
        t = jnp.where(d == 0, j, steps - 1 - j)
        r0 = pl.multiple_of(t * BATCH, BATCH)
        h = a_scr[pl.ds(r0, BATCH), :] * h + b_scr[pl.ds(r0, BATCH), :]
        h_ref[0, pl.ds(r0, BATCH), :] = h
        return h

    h_scr[...] = lax.fori_loop(0, steps, step, h_scr[...], unroll=8)


def _lru(z, conv_w, conv_b, w_a, b_a, w_x, b_x, lam, n_ctx, n_all):
    rows = n_all * BATCH
    blk = LRU_STEPS * BATCH
    ctx_chunks, all_chunks = n_ctx // LRU_STEPS, n_all // LRU_STEPS
    chunk = lambda d, i: _scan_chunk(d, i, ctx_chunks, all_chunks)
    col = ODD_LX // LRU_W
    slabs = LRU_W // LANES
    eye = jnp.eye(LRU_BLOCKS, dtype=F32)
    dense = lambda w: jnp.einsum('dkij,kl->dkilj', w, eye).reshape(2, LRU_W, LRU_W)
    wg = jnp.concatenate([dense(w_a), dense(w_x)], -1).astype(BF16)
    bg = jnp.concatenate([b_a, b_x], -1).reshape(2, 1, 2 * LRU_W)
    sp = (LRU_C * jax.nn.softplus(-lam)).reshape(2, 1, LRU_W)
    halo_prev, halo_next = 2 * BATCH, BATCH
    return pl.pallas_call(
        functools.partial(_lru_body, ctx_chunks, all_chunks),
        grid=(2, all_chunks),
        in_specs=[pl.BlockSpec((slabs, halo_prev, LANES),
                               lambda d, i: (col, jnp.maximum(chunk(d, i) * (blk // halo_prev) - 1, 0), 0)),
                  pl.BlockSpec((slabs, blk, LANES), lambda d, i: (col, chunk(d, i), 0)),
                  pl.BlockSpec((slabs, halo_next, LANES),
                               lambda d, i: (col, jnp.minimum((chunk(d, i) + 1) * (blk // halo_next),
                                                              rows // halo_next - 1), 0)),
                  pl.BlockSpec((CONV_K, LRU_W), lambda d, i: (0, 0)),
                  pl.BlockSpec((1, LRU_W), lambda d, i: (0, 0)),
                  pl.BlockSpec((1, LRU_W, 2 * LRU_W), lambda d, i: (d, 0, 0)),
                  pl.BlockSpec((1, 1, 2 * LRU_W), lambda d, i: (d, 0, 0)),
                  pl.BlockSpec((1, 1, LRU_W), lambda d, i: (d, 0, 0))],
        out_specs=pl.BlockSpec((1, blk, LRU_W), lambda d, i: (d, chunk(d, i), 0)),
        out_shape=jax.ShapeDtypeStruct((2, rows, LRU_W), F32),
        scratch_shapes=[pltpu.VMEM((blk, LRU_W), F32), pltpu.VMEM((blk, LRU_W), F32),
                        pltpu.VMEM((BATCH, LRU_W), F32)],
        compiler_params=_params("arbitrary", "arbitrary"),
        name="rglru",
    )(z, z, z, conv_w, conv_b.reshape(1, LRU_W), wg, bg, sp)


def _residual_norm_route(upd, x_ref, g2_ref, nw_ref, sh_ref, sc_ref, wrh_ref, wrl_ref, rb_ref,
                         xo_ref, hf_ref, lg_ref):
    tm = x_ref.shape[0]
    x3 = x_ref[...].reshape(tm // BATCH, BATCH, D_MODEL) + g2_ref[0] * upd.reshape(tm // BATCH, BATCH, D_MODEL)
    xo_ref[...] = x3.reshape(tm, D_MODEL)
    hf = _norm_modulate(x3, nw_ref[...], sh_ref[0], sc_ref[0]).reshape(tm, D_MODEL)
    _store_rows(hf_ref, hf)
    hi = hf.astype(BF16)
    lo = (hf - hi.astype(F32)).astype(BF16)
    lg_ref[...] = _dot(hi, wrh_ref[...]) + _dot(hi, wrl_ref[...]) + _dot(lo, wrh_ref[...]) + rb_ref[...]


def _gelu_tanh(x):
    return 0.5 * x * (1.0 + jnp.tanh(0.7978845608028654 * (x + 0.044715 * x * x * x)))


def _out_body_even(yf_ref, yb_ref, u_ref, d_ref, gw_ref, gb_ref, of_ref, ob_ref, gate_ref, wo_ref, *rest):
    u = jnp.concatenate([u_ref[s] for s in range(HALF_W // LANES)], 1)
    ys = yf_ref[0] + yb_ref[0] + d_ref[...] * u
    y = _gelu_tanh(ys)
    s5o = y * jax.nn.sigmoid(_dot(y.astype(BF16), gw_ref[...]) + gb_ref[...])
    upd = _dot(s5o.astype(BF16), wo_ref[0:HALF_W, :])
    for h in range(RET_HEADS):
        oh = of_ref[h] + ob_ref[h]
        mu = jnp.mean(oh, -1, keepdims=True)
        var = jnp.mean(jnp.square(oh - mu), -1, keepdims=True)
        rh = _silu(gate_ref[h]) * ((oh - mu) * lax.rsqrt(var + NORM_EPS))
        upd = upd + _dot(rh.astype(BF16), wo_ref[HALF_W + h * RET_DV:HALF_W + (h + 1) * RET_DV, :])
    _residual_norm_route(upd, *rest)


def _out_body_odd(gf_ref, gb_ref, zg_ref, gnw_ref, hf_ref, hb_ref, lg_ref, wo_ref, *rest):
    lg = jnp.concatenate([lg_ref[s] for s in range(LRU_W // LANES)], 1)
    lru = (hf_ref[0] + hb_ref[0]) * _gelu_tanh(lg)
    upd = _dot(lru.astype(BF16), wo_ref[HALF_W:, :])
    for h in range(GDN_HEADS):
        oh = gf_ref[h] + gb_ref[h]
        oh = oh * lax.rsqrt(jnp.mean(oh * oh, -1, keepdims=True) + NORM_EPS) * gnw_ref[...] * _silu(zg_ref[h])
        upd = upd + _dot(oh.astype(BF16), wo_ref[h * GDN_DV:(h + 1) * GDN_DV, :])
    _residual_norm_route(upd, *rest)


def _out_proj(body, mix_args, mix_specs, xs, w_out, gate2, norm_w, shift2, scale2, wr_hi, wr_lo, rb,
              row0_tiles, n_tiles, ctx_tiles):
    kind = lambda r: (jnp.where(r + row0_tiles < ctx_tiles, 0, 1), 0, 0)
    const2 = lambda r: (0, 0)
    rows = n_tiles * ROW_TILE
    return pl.pallas_call(
        body,
        grid=(n_tiles,),
        in_specs=mix_specs + [
            pl.BlockSpec((D_MODEL, D_MODEL), const2),
            pl.BlockSpec((ROW_TILE, D_MODEL), lambda r: (r + row0_tiles, 0)),
            pl.BlockSpec((1, BATCH, D_MODEL), kind),
            pl.BlockSpec((1, D_MODEL), const2),
            pl.BlockSpec((1, BATCH, D_MODEL), kind),
            pl.BlockSpec((1, BATCH, D_MODEL), kind),
            pl.BlockSpec((D_MODEL, ROUTER_PAD), const2),
            pl.BlockSpec((D_MODEL, ROUTER_PAD), const2),
            pl.BlockSpec((1, ROUTER_PAD), const2)],
        out_specs=[pl.BlockSpec((ROW_TILE, D_MODEL), lambda r: (r, 0)),
                   pl.BlockSpec((ROW_TILE * ROW_CHUNKS, LANES), lambda r: (r, 0)),
                   pl.BlockSpec((ROW_TILE, ROUTER_PAD), lambda r: (r, 0))],
        out_shape=[jax.ShapeDtypeStruct((rows, D_MODEL), F32),
                   jax.ShapeDtypeStruct((rows * ROW_CHUNKS, LANES), F32),
                   jax.ShapeDtypeStruct((rows, ROUTER_PAD), F32)],
        compiler_params=_params("arbitrary"),
        name="out_proj",
    )(*mix_args, w_out, xs, gate2, norm_w.reshape(1, D_MODEL), shift2, scale2, wr_hi, wr_lo, rb)


def _route(logits):
    n_tok = logits.shape[0]
    g_logits = logits[:, :N_GROUPS]
    g_idx = jnp.argmax(g_logits, -1)
    g_w = jnp.max(jax.nn.softmax(g_logits, -1), -1, keepdims=True)
    e_logits = logits[:, N_GROUPS:N_GROUPS + N_EXPERTS].reshape(n_tok, N_GROUPS, EXPERTS_PER_GROUP)
    chosen = g_idx[:, None, None] == jnp.arange(N_GROUPS)[None, :, None]
    e_in = jnp.sum(jnp.where(chosen, e_logits, 0.0), 1)
    top_v, top_i = lax.top_k(e_in, TOP_K)
    top_w = jax.nn.softmax(top_v, -1) * g_w
    return (g_idx[:, None] * EXPERTS_PER_GROUP + top_i).astype(jnp.int32), top_w


def _dispatch_plan(expert_ids, top_w, n_tiles):
    n_tok = expert_ids.shape[0]
    n_asg = n_tok * TOP_K
    flat_e = expert_ids.T.reshape(n_asg)
    flat_w = top_w.T.reshape(n_asg)
    order = jnp.argsort(flat_e, stable=True).astype(jnp.int32)
    inv = jnp.argsort(order).astype(jnp.int32)
    onehot = flat_e[:, None] == jnp.arange(N_EXPERTS, dtype=jnp.int32)[None]
    counts = jnp.sum(onehot, 0, dtype=jnp.int32)
    padded = ((counts + MOE_TILE - 1) // MOE_TILE) * MOE_TILE
    pend = jnp.cumsum(padded)
    pstart = pend - padded
    start = jnp.cumsum(counts) - counts
    shift = pstart - start
    pos = inv + jnp.sum(jnp.where(onehot, shift[None], 0), 1)
    tile_start = jnp.arange(n_tiles, dtype=jnp.int32) * MOE_TILE
    tile_e = jnp.minimum(jnp.sum(tile_start[:, None] >= pend[None], 1), N_EXPERTS - 1).astype(jnp.int32)
    tile_valid = (tile_start < pend[-1]).astype(jnp.int32)
    tile_hot = tile_e[:, None] == jnp.arange(N_EXPERTS, dtype=jnp.int32)[None]
    per_tile = lambda tbl: jnp.sum(jnp.where(tile_hot, tbl[None], 0), 1)
    tile_rank0 = tile_start - per_tile(shift)
    tile_end = per_tile(start + counts)
    lane = jnp.arange(MOE_TILE, dtype=jnp.int32)[None]
    rank = (tile_rank0[:, None] + lane).reshape(n_tiles * MOE_TILE)
    row_ok = (tile_rank0[:, None] + lane < tile_end[:, None]).reshape(n_tiles * MOE_TILE)
    src_asg = order[jnp.clip(rank, 0, n_asg - 1)]
    src_tok = jnp.where(row_ok, src_asg % n_tok, 0)
    row_w = jnp.where(row_ok, flat_w[src_asg], 0.0)
    return src_tok, row_w, pos.reshape(TOP_K, n_tok), tile_e, tile_valid


def _gather_rows(idx_ref, src_hbm, dst_vmem, sem, n_rows):
    def issue(j, c):
        for p in range(DMA_PRIORITIES):
            r = j * DMA_PRIORITIES + p
            src = pl.multiple_of(idx_ref[0, 0, r] * ROW_CHUNKS, ROW_CHUNKS)
            dst = pl.multiple_of(r * ROW_CHUNKS, ROW_CHUNKS)
            pltpu.make_async_copy(src_hbm.at[pl.ds(src, ROW_CHUNKS)], dst_vmem.at[pl.ds(dst, ROW_CHUNKS)],
                                  sem).start(priority=p)
        return c

    lax.fori_loop(0, n_rows // DMA_PRIORITIES, issue, 0, unroll=16)


def _wait_rows(src_hbm, dst_vmem, sem, n_rows):
    pltpu.make_async_copy(src_hbm.at[pl.ds(0, n_rows * ROW_CHUNKS)], dst_vmem, sem).wait()


def _rows_2d(ref, n_rows):
    return jnp.concatenate([ref[pl.ds(s, n_rows, stride=ROW_CHUNKS), :] for s in range(ROW_CHUNKS)], 1)


def _store_rows(ref, val):
    for s in range(ROW_CHUNKS):
        ref[pl.ds(s, val.shape[0], stride=ROW_CHUNKS), :] = val[:, s * LANES:(s + 1) * LANES]


def _expert_body(te_ref, tv_ref, idx_ref, idx_next_ref, hf_hbm, cw_ref, wg32_ref, wu32_ref, wd32_ref, y_ref,
                 xbuf, wg_ref, wu_ref, wd_ref, sem):
    i = pl.program_id(0)
    n = pl.num_programs(0)
    slot = i % 2
    nxt = jnp.minimum(i + 1, n - 1)

    @pl.when(jnp.logical_or(i == 0, te_ref[i] != te_ref[jnp.maximum(i - 1, 0)]))
    def _():
        wg_ref[0] = wg32_ref[0, 0].astype(BF16)
        wu_ref[0] = wu32_ref[0, 0].astype(BF16)
        wd_ref[0] = wd32_ref[0, 0].astype(BF16)

    @pl.when(jnp.logical_and(i == 0, tv_ref[0] == 1))
    def _():
        _gather_rows(idx_ref, hf_hbm, xbuf.at[0], sem.at[0], MOE_TILE)

    @pl.when(jnp.logical_and(i + 1 < n, tv_ref[nxt] == 1))
    def _():
        _gather_rows(idx_next_ref, hf_hbm, xbuf.at[1 - slot], sem.at[1 - slot], MOE_TILE)

    @pl.when(tv_ref[i] == 1)
    def _():
        _wait_rows(hf_hbm, xbuf.at[slot], sem.at[slot], MOE_TILE)
        x = _rows_2d(xbuf.at[slot], MOE_TILE).astype(BF16)
        g = _dot(x, wg_ref[0])
        u = _dot(x, wu_ref[0])
        act = _silu(g) * u * cw_ref[:, 0:1]
        _store_rows(y_ref, _dot(act.astype(BF16), wd_ref[0]))

    @pl.when(tv_ref[i] == 0)
    def _():
        y_ref[...] = jnp.zeros_like(y_ref)


def _experts(layer, tile_e, tile_valid, src_tok, hf, cw_rows, wg, wu, wd):
    n_tiles = tile_e.shape[0]
    tile_spec = pl.BlockSpec((MOE_TILE * ROW_CHUNKS, LANES), lambda i, te, tv: (i, 0))
    idx3 = src_tok.reshape(n_tiles, 1, MOE_TILE)
    grid_spec = pltpu.PrefetchScalarGridSpec(
        num_scalar_prefetch=2,
        grid=(n_tiles,),
        in_specs=[pl.BlockSpec((1, 1, MOE_TILE), lambda i, te, tv: (i, 0, 0), memory_space=pltpu.SMEM),
                  pl.BlockSpec((1, 1, MOE_TILE), lambda i, te, tv: (jnp.minimum(i + 1, n_tiles - 1), 0, 0),
                               memory_space=pltpu.SMEM),
                  pl.BlockSpec(memory_space=pl.ANY),
                  pl.BlockSpec((MOE_TILE, LANES), lambda i, te, tv: (i, 0)),
                  pl.BlockSpec((1, 1, D_MODEL, EXPERT_FF), lambda i, te, tv: (layer, te[i], 0, 0)),
                  pl.BlockSpec((1, 1, D_MODEL, EXPERT_FF), lambda i, te, tv: (layer, te[i], 0, 0)),
                  pl.BlockSpec((1, 1, EXPERT_FF, D_MODEL), lambda i, te, tv: (layer, te[i], 0, 0))],
        out_specs=tile_spec,
        scratch_shapes=[pltpu.VMEM((2, MOE_TILE * ROW_CHUNKS, LANES), F32),
                        pltpu.VMEM((1, D_MODEL, EXPERT_FF), BF16), pltpu.VMEM((1, D_MODEL, EXPERT_FF), BF16),
                        pltpu.VMEM((1, EXPERT_FF, D_MODEL), BF16), pltpu.SemaphoreType.DMA((2,))])
    return pl.pallas_call(
        _expert_body,
        grid_spec=grid_spec,
        out_shape=jax.ShapeDtypeStruct((n_tiles * MOE_TILE * ROW_CHUNKS, LANES), F32),
        compiler_params=_params("arbitrary"),
        name="moe_experts",
    )(tile_e, tile_valid, idx3, idx3, hf, cw_rows, wg, wu, wd)


def _combine_rows(p0_ref, p1_ref, p0n_ref, p1n_ref, y_hbm, x_ref, g5_ref, buf, sem):
    r = pl.program_id(0)
    n = pl.num_programs(0)
    tm = x_ref.shape[0]
    slot = r % 2

    def fetch(p0, p1, s):
        _gather_rows(p0, y_hbm, buf.at[s, 0], sem.at[s, 0], tm)
        _gather_rows(p1, y_hbm, buf.at[s, 1], sem.at[s, 1], tm)

    @pl.when(r == 0)
    def _():
        fetch(p0_ref, p1_ref, 0)

    @pl.when(r + 1 < n)
    def _():
        fetch(p0n_ref, p1n_ref, 1 - slot)

    _wait_rows(y_hbm, buf.at[slot, 0], sem.at[slot, 0], tm)
    _wait_rows(y_hbm, buf.at[slot, 1], sem.at[slot, 1], tm)
    f3 = (_rows_2d(buf.at[slot, 0], tm) + _rows_2d(buf.at[slot, 1], tm)).reshape(tm // BATCH, BATCH, D_MODEL)
    return (x_ref[...].reshape(tm // BATCH, BATCH, D_MODEL) + g5_ref[0] * f3).reshape(tm, D_MODEL)


def _combine_body(p0_ref, p1_ref, p0n_ref, p1n_ref, y_hbm, x_ref, g5_ref, o_ref, buf, sem):
    o_ref[...] = _combine_rows(p0_ref, p1_ref, p0n_ref, p1n_ref, y_hbm, x_ref, g5_ref, buf, sem)


def _combine_final_body(p0_ref, p1_ref, p0n_ref, p1n_ref, y_hbm, x_ref, g5_ref, fw_ref, o_ref, buf, sem, slab):
    x = _combine_rows(p0_ref, p1_ref, p0n_ref, p1n_ref, y_hbm, x_ref, g5_ref, buf, sem)
    steps = x.shape[0] // BATCH
    xn = x * lax.rsqrt(jnp.mean(x * x, -1, keepdims=True) + NORM_EPS) * fw_ref[...]
    for s in range(ROW_CHUNKS):
        slab[s] = xn[:, s * LANES:(s + 1) * LANES]
    for b in range(BATCH):
        for s in range(ROW_CHUNKS):
            o_ref[b, :, s * LANES:(s + 1) * LANES] = slab[s, _batch_rows(b, steps), :]


def _combine(pos, y_sorted, xs, gate2, row0_tiles, ctx_tiles, final_w=None):
    n_tok = pos.shape[1]
    n_tiles = n_tok // MOE_TILE
    kind = lambda r: (jnp.where(r + row0_tiles < ctx_tiles, 0, 1), 0, 0)
    idx_spec = lambda k, ahead: pl.BlockSpec(
        (1, 1, MOE_TILE), lambda r: (k * n_tiles + jnp.minimum(r + ahead, n_tiles - 1), 0, 0),
        memory_space=pltpu.SMEM)
    p2 = pos.reshape(TOP_K * n_tiles, 1, MOE_TILE)
    in_specs = [idx_spec(0, 0), idx_spec(1, 0), idx_spec(0, 1), idx_spec(1, 1),
                pl.BlockSpec(memory_space=pl.ANY),
                pl.BlockSpec((MOE_TILE, D_MODEL), lambda r: (r, 0)),
                pl.BlockSpec((1, BATCH, D_MODEL), kind)]
    scratch = [pltpu.VMEM((2, TOP_K, MOE_TILE * ROW_CHUNKS, LANES), F32), pltpu.SemaphoreType.DMA((2, TOP_K))]
    if final_w is None:
        return pl.pallas_call(
            _combine_body,
            grid=(n_tiles,),
            in_specs=in_specs,
            out_specs=pl.BlockSpec((MOE_TILE, D_MODEL), lambda r: (r, 0)),
            out_shape=jax.ShapeDtypeStruct((n_tok, D_MODEL), F32),
            scratch_shapes=scratch,
            compiler_params=_params("arbitrary"),
            name="moe_combine",
        )(p2, p2, p2, p2, y_sorted, xs, gate2)
    steps = MOE_TILE // BATCH
    return pl.pallas_call(
        _combine_final_body,
        grid=(n_tiles,),
        in_specs=in_specs + [pl.BlockSpec((1, D_MODEL), lambda r: (0, 0))],
        out_specs=pl.BlockSpec((BATCH, steps, D_MODEL), lambda r: (0, r, 0)),
        out_shape=jax.ShapeDtypeStruct((BATCH, n_tok // BATCH, D_MODEL), F32),
        scratch_shapes=scratch + [pltpu.VMEM((ROW_CHUNKS, MOE_TILE, LANES), F32)],
        compiler_params=_params("arbitrary"),
        name="moe_combine_final",
    )(p2, p2, p2, p2, y_sorted, xs, gate2, final_w.reshape(1, D_MODEL))


def _moe(layer, hf, logits, xs, gate2, wg, wu, wd, row0_tiles, ctx_tiles, final_w=None):
    n_tok = hf.shape[0] // ROW_CHUNKS
    n_tiles = n_tok * TOP_K // MOE_TILE + N_EXPERTS
    expert_ids, top_w = _route(logits)
    src_tok, row_w, pos, tile_e, tile_valid = _dispatch_plan(expert_ids, top_w, n_tiles)
    cw_rows = jnp.broadcast_to(row_w[:, None], (n_tiles * MOE_TILE, LANES))
    y_sorted = _experts(layer, tile_e, tile_valid, src_tok, hf, cw_rows, wg, wu, wd)
    return _combine(pos, y_sorted, xs, gate2, row0_tiles, ctx_tiles, final_w)


def _kinds(mod_l, k):
    return jnp.stack([mod_l[BATCH:, k], mod_l[:BATCH, k]])


def kernel(x, c, ctx, c_ctx, mod_w, mod_b, norm_mix, norm_ffn, router_group_w, router_group_b,
           router_expert_w, router_expert_b, expert_w_gate, expert_w_up, expert_w_down,
           even_w_in, even_w_out, s5_lam_re, s5_lam_im, s5_log_step, s5_b_re, s5_b_im, s5_c_re,
           s5_c_im, s5_d, s5_glu_w, s5_glu_b, ret_decay, odd_w_in, odd_w_out, gdn_conv_w, gdn_a_log,
           gdn_dt_bias, gdn_norm_w, lru_conv_w, lru_conv_b, lru_w_a, lru_b_a, lru_w_x, lru_b_x,
           lru_lam, final_norm):
    bsz, n_lat, _ = x.shape
    n_ctx = ctx.shape[1]
    n_all = n_ctx + n_lat
    assert bsz == BATCH and n_ctx % RET_CHUNK == 0 and n_lat % RET_CHUNK == 0
    assert (n_ctx * BATCH) % ROW_TILE == 0 and (n_lat * BATCH) % ROW_TILE == 0
    ctx_tiles = n_ctx * BATCH // ROW_TILE
    all_tiles = n_all * BATCH // ROW_TILE
    ctx_moe_tiles = n_ctx * BATCH // MOE_TILE

    xs = None
    cond16 = jnp.concatenate([c, jnp.broadcast_to(c_ctx[None], (BATCH, D_MODEL))], 0)
    mod = _adaln_all(cond16, mod_w, mod_b).reshape(DEPTH, ADALN_ROWS, N_MOD, D_MODEL)

    for layer in range(DEPTH):
        last = layer == DEPTH - 1
        i = layer // 2
        m = [_kinds(mod[layer], k) for k in range(N_MOD)]
        wr = jnp.zeros((D_MODEL, ROUTER_PAD), F32)
        wr = wr.at[:, :N_GROUPS].set(router_group_w[layer]).at[:, N_GROUPS:N_GROUPS + N_EXPERTS].set(
            router_expert_w[layer])
        wr_hi = wr.astype(BF16)
        wr_lo = (wr - wr_hi.astype(F32)).astype(BF16)
        rb = jnp.zeros((1, ROUTER_PAD), F32)
        rb = rb.at[0, :N_GROUPS].set(router_group_b[layer]).at[0, N_GROUPS:N_GROUPS + N_EXPERTS].set(
            router_expert_b[layer])
        row0 = ctx_tiles if last else 0
        n_tiles = all_tiles - row0
        slab_spec = lambda sb: pl.BlockSpec((HALF_W // LANES, ROW_TILE, LANES), lambda r: (sb, r + row0, 0))
        dir_spec = lambda d: pl.BlockSpec((1, ROW_TILE, HALF_W), lambda r: (d, r + row0, 0))
        const2 = lambda r: (0, 0)

        if layer % 2 == 0:
            w_in = even_w_in[i].astype(BF16)
            if layer == 0:
                z, xs = _in_proj_first(ctx, x, norm_mix[layer], m[0], m[1], w_in, ctx_tiles)
            else:
                z = _in_proj(xs, norm_mix[layer], m[0], m[1], w_in, ctx_tiles)
            disc = [_s5_discretize(s5_lam_re[i, d], s5_lam_im[i, d], s5_log_step[i, d], s5_b_re[i, d],
                                   s5_b_im[i, d], s5_c_re[i, d], s5_c_im[i, d]) for d in range(2)]
            bb, cc, a = (jnp.stack(t) for t in zip(*disc))
            ys5 = _s5_scan(z, bb, cc, a, n_ctx, n_all)
            cos, sin = _rope_tables(n_ctx, n_lat)
            r_f, r_b = _retention(z, cos, sin, _ret_tables(ret_decay[i]), n_ctx, n_all)
            mix_args = [ys5, ys5, z, s5_d[i].reshape(1, HALF_W), s5_glu_w[i].astype(BF16),
                        s5_glu_b[i].reshape(1, HALF_W), r_f, r_b, z]
            mix_specs = [dir_spec(0), dir_spec(1), slab_spec(0),
                         pl.BlockSpec((1, HALF_W), const2),
                         pl.BlockSpec((HALF_W, HALF_W), const2),
                         pl.BlockSpec((1, HALF_W), const2),
                         slab_spec(0), slab_spec(0), slab_spec(EVEN_SPLITS[3] // HALF_W)]
            body, w_out = _out_body_even, even_w_out[i]
        else:
            w_in = odd_w_in[i]
            w_pad = jnp.concatenate([
                w_in[:, :ODD_SPLITS[1]],
                w_in[:, ODD_SPLITS[3]:],
                jnp.pad(w_in[:, ODD_SPLITS[1]:ODD_SPLITS[3]], ((0, 0), (0, ODD_PAD - 4 * GDN_HEADS)))], 1)
            z = _in_proj(xs, norm_mix[layer], m[0], m[1], w_pad.astype(BF16), ctx_tiles)
            qkv_act, bg = _gdn_prep(z, gdn_conv_w[i], gdn_a_log[i], gdn_dt_bias[i], n_ctx, n_all)
            g_f, g_b = _gdn(qkv_act, bg, n_ctx, n_all)
            hl = _lru(z, lru_conv_w[i], lru_conv_b[i], lru_w_a[i], lru_b_a[i], lru_w_x[i], lru_b_x[i],
                      lru_lam[i], n_ctx, n_all)
            mix_args = [g_f, g_b, z, gdn_norm_w[i].reshape(1, GDN_DV), hl, hl, z]
            mix_specs = [slab_spec(0), slab_spec(0), slab_spec(ODD_SPLITS[0] // HALF_W),
                         pl.BlockSpec((1, GDN_DV), const2), dir_spec(0), dir_spec(1),
                         slab_spec((ODD_LX + LRU_W) // LRU_W)]
            body, w_out = _out_body_odd, odd_w_out[i]

        xs_mid, hf, logits = _out_proj(body, mix_args, mix_specs, xs, w_out.astype(BF16), m[2],
                                       norm_ffn[layer], m[3], m[4], wr_hi, wr_lo, rb, row0, n_tiles, ctx_tiles)
        moe_row0 = ctx_moe_tiles if last else 0
        xs = _moe(layer, hf, logits, xs_mid, m[5], expert_w_gate, expert_w_up, expert_w_down,
                  moe_row0, ctx_moe_tiles, final_norm if last else None)

    return xs
```

```python
import functools

import jax
import jax.numpy as jnp
from jax import lax
from jax.experimental import pallas as pl
from jax.experimental.pallas import tpu as pltpu

D_MODEL = 1024
DEPTH = 2
GRID_W = 64
HALF_W = D_MODEL // 2
S5_GROUP = 16
S5_GROUPS = HALF_W // S5_GROUP
S5_STATE = 64
RET_HEADS = 4
RET_DV = HALF_W // RET_HEADS
RET_DK = RET_DV // 2
RET_QK = RET_HEADS * RET_DK
RET_CHUNK = 128
ROPE_BASE = 10000.0
GDN_HEADS = 4
GDN_DK = HALF_W // GDN_HEADS
GDN_DV = GDN_DK
GDN_CHUNK = 64
LRU_W = HALF_W
LRU_BLOCKS = 8
LRU_BLOCK = LRU_W // LRU_BLOCKS
LRU_C = 8.0
CONV_K = 4
CONV_PAD_LEFT = 2
N_GROUPS = 4
EXPERTS_PER_GROUP = 8
N_EXPERTS = N_GROUPS * EXPERTS_PER_GROUP
TOP_K = 2
EXPERT_FF = D_MODEL // 2
N_MOD = 6
NORM_EPS = 1e-6
EVEN_SPLITS = [HALF_W, HALF_W + RET_QK, HALF_W + 2 * RET_QK, 2 * HALF_W + 2 * RET_QK]
EVEN_IN = 3 * HALF_W + 2 * RET_QK
ODD_SPLITS = [3 * HALF_W, 4 * HALF_W, 4 * HALF_W + 2 * GDN_HEADS, 4 * HALF_W + 4 * GDN_HEADS,
              4 * HALF_W + 4 * GDN_HEADS + LRU_W]

F32 = jnp.float32
BF16 = jnp.bfloat16

SUBLANES = 8
LANES = 128
BATCH = SUBLANES
ROW_CHUNKS = D_MODEL // LANES
VMEM_LIMIT = 48 * 1024 * 1024

ROW_TILE = 512
S5_STEPS = 128
S5_SLABS = 4
S5_SLAB_CH = HALF_W // S5_SLABS
S5_SLAB_STATE = (S5_GROUPS // S5_SLABS) * S5_STATE
LRU_STEPS = 128
MOE_TILE = 256
DMA_PRIORITIES = 2
ODD_PAD = 512
ODD_LX = 4 * HALF_W
ODD_BD = ODD_LX + 2 * LRU_W
ODD_IN_PADDED = ODD_BD + ODD_PAD
GDN_QKV = 3 * HALF_W
GDN_ROWS = GDN_CHUNK * BATCH
GDN_BG_G = 2 * GDN_HEADS
GDN_NEUMANN_LEVELS = 5
GDN_BATCHES_PER_ITER = 4
ROUTER_PAD = LANES


def _dot(a, b):
    return jnp.dot(a, b, preferred_element_type=F32)


def _dot_nt(a, b):
    return lax.dot_general(a, b, (((1,), (1,)), ((), ())), preferred_element_type=F32)


def _params(*sem):
    return pltpu.CompilerParams(dimension_semantics=sem, vmem_limit_bytes=VMEM_LIMIT)


def _scan_chunk(d, i, ctx_chunks, all_chunks):
    back = jnp.where(i < ctx_chunks, ctx_chunks - 1 - i, all_chunks + ctx_chunks - 1 - i)
    return jnp.where(d == 0, i, back)


def _silu(x):
    return x * jax.nn.sigmoid(x)


def _softplus(x):
    return jnp.maximum(x, 0.0) + jnp.log(1.0 + jnp.exp(-jnp.abs(x)))


def _segment_edges(chunk, ctx_chunks, all_chunks):
    first = jnp.logical_or(chunk == 0, chunk == ctx_chunks)
    last = jnp.logical_or(chunk == ctx_chunks - 1, chunk == all_chunks - 1)
    return first, last


def _gdn_prep_body(ctx_chunks, all_chunks, prev_ref, x_ref, next_ref, bd_ref, cw_ref, al_ref, dtb_ref,
                   qkv_ref, bg_ref, ext_scr):
    rows = x_ref.shape[1]
    steps = rows // BATCH
    halo = CONV_PAD_LEFT * BATCH
    seg_first, seg_last = _segment_edges(pl.program_id(0), ctx_chunks, all_chunks)
    ext_scr[:, 0:halo, :] = jnp.where(seg_first, 0.0, prev_ref[...])
    ext_scr[:, halo:halo + rows, :] = x_ref[...]
    ext_scr[:, halo + rows:, :] = jnp.where(seg_last, 0.0, next_ref[...])
    for j in range(GDN_QKV // LANES):
        sl = slice(j * LANES, (j + 1) * LANES)
        c = sum(cw_ref[k:k + 1, sl] * ext_scr[j, k * BATCH:k * BATCH + rows, :] for k in range(CONV_K))
        a = _silu(c)
        if j < 2 * GDN_HEADS:
            a = a * lax.rsqrt(jnp.sum(a * a, -1, keepdims=True) + NORM_EPS)
        if j < GDN_HEADS:
            a = a * (GDN_DK ** -0.5)
        qkv_ref[j] = a

    bd = bd_ref[0]
    lane = lax.broadcasted_iota(jnp.int32, bd.shape, 1)
    val = jnp.where(lane < GDN_BG_G, jax.nn.sigmoid(bd), al_ref[...] * _softplus(bd + dtb_ref[...]))
    prefix = val
    for k in range((steps - 1).bit_length()):
        sh = BATCH << k
        prefix = prefix + jnp.concatenate([jnp.zeros((sh, LANES), F32), prefix[:rows - sh]], 0)
    p3 = prefix.reshape(steps, BATCH, LANES)
    suffix = (p3[steps - 1][None] - p3 + val.reshape(steps, BATCH, LANES)).reshape(rows, LANES)
    fwd_g = jnp.logical_and(lane >= GDN_BG_G, lane < GDN_BG_G + GDN_HEADS)
    bwd_g = jnp.logical_and(lane >= GDN_BG_G + GDN_HEADS, lane < GDN_BG_G + 2 * GDN_HEADS)
    bg_ref[...] = jnp.where(fwd_g, prefix, jnp.where(bwd_g, suffix, val))


def _gdn_prep(z, conv_w, a_log, dt_bias, n_ctx, n_all):
    rows = n_all * BATCH
    ctx_chunks, all_chunks = n_ctx // GDN_CHUNK, n_all // GDN_CHUNK
    halo_prev, halo_next = CONV_PAD_LEFT * BATCH, (CONV_K - 1 - CONV_PAD_LEFT) * BATCH
    slabs = GDN_QKV // LANES
    lanes16 = lambda t: jnp.zeros((1, LANES), F32).at[0, GDN_BG_G:GDN_BG_G + 2 * GDN_HEADS].set(t.reshape(-1))
    return pl.pallas_call(
        functools.partial(_gdn_prep_body, ctx_chunks, all_chunks),
        grid=(all_chunks,),
        in_specs=[pl.BlockSpec((slabs, halo_prev, LANES),
                               lambda i: (0, jnp.maximum(i * (GDN_ROWS // halo_prev) - 1, 0), 0)),
                  pl.BlockSpec((slabs, GDN_ROWS, LANES), lambda i: (0, i, 0)),
                  pl.BlockSpec((slabs, halo_next, LANES),
                               lambda i: (0, jnp.minimum((i + 1) * (GDN_ROWS // halo_next), rows // halo_next - 1), 0)),
                  pl.BlockSpec((1, GDN_ROWS, LANES), lambda i: (ODD_BD // LANES, i, 0)),
                  pl.BlockSpec((CONV_K, GDN_QKV), lambda i: (0, 0)),
                  pl.BlockSpec((1, LANES), lambda i: (0, 0)),
                  pl.BlockSpec((1, LANES), lambda i: (0, 0))],
        out_specs=[pl.BlockSpec((slabs, GDN_ROWS, LANES), lambda i: (0, i, 0)),
                   pl.BlockSpec((GDN_ROWS, LANES), lambda i: (i, 0))],
        out_shape=[jax.ShapeDtypeStruct((slabs, rows, LANES), F32), jax.ShapeDtypeStruct((rows, LANES), F32)],
        scratch_shapes=[pltpu.VMEM((slabs, GDN_ROWS + halo_prev + halo_next, LANES), F32)],
        compiler_params=_params("arbitrary"),
        name="gdn_prep",
    )(z, z, z, z, conv_w, lanes16(-jnp.exp(a_log)), lanes16(dt_bias))


def _dot_split(a, b):
    ah, bh = a.astype(BF16), b.astype(BF16)
    al, bl = (a - ah.astype(F32)).astype(BF16), (b - bh.astype(F32)).astype(BF16)
    return _dot(ah, bh) + _dot(ah, bl) + _dot(al, bh)


def _gdn_batches(batches, xf_ref, bgf_ref, xb_ref, bgb_ref, of_ref, ob_ref, s_scr):
    ii = lax.broadcasted_iota(jnp.int32, (GDN_CHUNK, GDN_CHUNK), 0)
    jj = lax.broadcasted_iota(jnp.int32, (GDN_CHUNK, GDN_CHUNK), 1)
    eye = jnp.where(ii == jj, 1.0, 0.0)
    units = []
    for b in batches:
        rows = _batch_rows(b, GDN_CHUNK)
        for d, (x_ref, bg_ref, o_ref) in enumerate(((xf_ref, bgf_ref, of_ref), (xb_ref, bgb_ref, ob_ref))):
            bg = bg_ref[rows, :]
            bg_t = bg.T
            incl = (jj <= ii) if d == 0 else (jj >= ii)
            strict = (jj < ii) if d == 0 else (jj > ii)
            last = GDN_CHUNK - 1 if d == 0 else 0
            for h in range(GDN_HEADS):
                q, k, v = x_ref[h, rows, :], x_ref[GDN_HEADS + h, rows, :], x_ref[2 * GDN_HEADS + h, rows, :]
                cb = d * GDN_HEADS + h
                cg = GDN_BG_G + cb
                beta, g_col = bg[:, cb:cb + 1], bg[:, cg:cg + 1]
                g_row, g_last = bg_t[cg:cg + 1, :], bg_t[cg:cg + 1, last:last + 1]
                decay = jnp.where(incl, jnp.exp(jnp.where(incl, g_col - g_row, 0.0)), 0.0)
                kb16 = k.astype(BF16)
                eg = jnp.exp(g_col)
                units.append(dict(
                    si=b * (2 * GDN_HEADS) + cb, h=h, o_ref=o_ref, rows=rows,
                    qk=(_dot_nt(q.astype(BF16), kb16) * decay).astype(BF16),
                    n=jnp.where(strict, -(_dot_nt(kb16, kb16) * decay * beta), 0.0),
                    rhs=jnp.concatenate([v * beta, k * (beta * eg)], 1),
                    qg=(q * eg).astype(BF16),
                    kg_t=(k * jnp.exp(g_last - g_col)).T.astype(BF16),
                    d_last=jnp.exp(g_last)))

    ts = [eye + u['n'] for u in units]
    ps = [u['n'] for u in units]
    for _ in range(GDN_NEUMANN_LEVELS):
        ps = [_dot(p.astype(BF16), p.astype(BF16)) for p in ps]
        ts = [t + _dot(t.astype(BF16), p.astype(BF16)) for t, p in zip(ts, ps)]
    sols = [_dot_split(t, u['rhs']) for t, u in zip(ts, units)]
    states = [s_scr[u['si']] for u in units]
    sbs = [s.astype(BF16) for s in states]
    v_news = [(sol[:, :GDN_DV] - _dot(sol[:, GDN_DV:].astype(BF16), sb)).astype(BF16)
              for sol, sb in zip(sols, sbs)]
    for u, s, sb, vn in zip(units, states, sbs, v_news):
        u['o_ref'][u['h'], u['rows'], :] = _dot(u['qg'], sb) + _dot(u['qk'], vn)
        s_scr[u['si']] = s * u['d_last'] + _dot(u['kg_t'], vn)


def _gdn_body(xf_ref, bgf_ref, xb_ref, bgb_ref, of_ref, ob_ref, s_scr):
    @pl.when(pl.program_id(0) == 0)
    def _():
        s_scr[...] = jnp.zeros_like(s_scr)

    def per_pair(j, carry):
        batches = [j * GDN_BATCHES_PER_ITER + p for p in range(GDN_BATCHES_PER_ITER)]
        _gdn_batches(batches, xf_ref, bgf_ref, xb_ref, bgb_ref, of_ref, ob_ref, s_scr)
        return carry

    lax.fori_loop(0, BATCH // GDN_BATCHES_PER_ITER, per_pair, 0)


def _gdn(qkv, bg, n_ctx, n_all):
    rows = n_all * BATCH
    ctx_chunks, all_chunks = n_ctx // GDN_CHUNK, n_all // GDN_CHUNK
    slabs = GDN_QKV // LANES

    def seq_specs(d):
        ch = lambda i: _scan_chunk(d, i, ctx_chunks, all_chunks)
        return [pl.BlockSpec((slabs, GDN_ROWS, LANES), lambda i: (0, ch(i), 0)),
                pl.BlockSpec((GDN_ROWS, LANES), lambda i: (ch(i), 0))]

    out_spec = lambda d: pl.BlockSpec((GDN_HEADS, GDN_ROWS, LANES),
                                      lambda i: (0, _scan_chunk(d, i, ctx_chunks, all_chunks), 0))
    o_shape = jax.ShapeDtypeStruct((GDN_HEADS, rows, LANES), F32)
    return pl.pallas_call(
        _gdn_body,
        grid=(all_chunks,),
        in_specs=seq_specs(0) + seq_specs(1),
        out_specs=[out_spec(0), out_spec(1)],
        out_shape=[o_shape, o_shape],
        scratch_shapes=[pltpu.VMEM((BATCH * 2 * GDN_HEADS, GDN_DK, GDN_DV), F32)],
        compiler_params=_params("arbitrary"),
        name="gdn_chunks",
    )(qkv, bg, qkv, bg)


ADALN_COLS = 512
ADALN_ROWS = 2 * BATCH


def _adaln_body(c_ref, w_ref, b_ref, o_ref):
    c = c_ref[...]
    s = (c * jax.nn.sigmoid(c)).astype(BF16)
    o_ref[0] = _dot(s, w_ref[0].astype(BF16)) + b_ref[0]


def _adaln_all(cond16, mod_w, mod_b):
    n_out = mod_w.shape[-1]
    return pl.pallas_call(
        _adaln_body,
        grid=(DEPTH, n_out // ADALN_COLS),
        in_specs=[pl.BlockSpec((ADALN_ROWS, D_MODEL), lambda l, j: (0, 0)),
                  pl.BlockSpec((1, D_MODEL, ADALN_COLS), lambda l, j: (l, 0, j)),
                  pl.BlockSpec((1, 1, ADALN_COLS), lambda l, j: (l, 0, j))],
        out_specs=pl.BlockSpec((1, ADALN_ROWS, ADALN_COLS), lambda l, j: (l, 0, j)),
        out_shape=jax.ShapeDtypeStruct((DEPTH, ADALN_ROWS, n_out), F32),
        compiler_params=_params("arbitrary", "arbitrary"),
        name="adaln",
    )(cond16, mod_w, mod_b.reshape(DEPTH, 1, n_out))


def _col_chunks(n, width=512):
    return [(c0, min(width, n - c0)) for c0 in range(0, n, width)]


def _norm_modulate(x3, nw, shift, scale):
    y = x3 * lax.rsqrt(jnp.mean(x3 * x3, -1, keepdims=True) + NORM_EPS) * nw
    return y * (1.0 + scale) + shift


def _project(x, nw_ref, sh_ref, sc_ref, w_ref, o_ref):
    tm = x.shape[0]
    x3 = x.reshape(tm // BATCH, BATCH, D_MODEL)
    h = _norm_modulate(x3, nw_ref[...], sh_ref[0], sc_ref[0]).reshape(tm, D_MODEL).astype(BF16)
    for c0, cw in _col_chunks(w_ref.shape[1]):
        res = _dot(h, w_ref[:, c0:c0 + cw])
        for j in range(cw // LANES):
            o_ref[c0 // LANES + j] = res[:, j * LANES:(j + 1) * LANES]


def _in_proj_body(x_ref, nw_ref, sh_ref, sc_ref, w_ref, o_ref):
    _project(x_ref[...], nw_ref, sh_ref, sc_ref, w_ref, o_ref)


def _in_proj_first_body(ctx_tiles, ctx_ref, lat_ref, nw_ref, sh_ref, sc_ref, w_ref, o_ref, xs_ref, slab):
    steps = ROW_TILE // BATCH

    def load(src_ref):
        for b in range(BATCH):
            for s in range(ROW_CHUNKS):
                slab[s, _batch_rows(b, steps), :] = src_ref[b, :, s * LANES:(s + 1) * LANES]

    @pl.when(pl.program_id(0) < ctx_tiles)
    def _():
        load(ctx_ref)

    @pl.when(pl.program_id(0) >= ctx_tiles)
    def _():
        load(lat_ref)

    x = jnp.concatenate([slab[s] for s in range(ROW_CHUNKS)], 1)
    xs_ref[...] = x
    _project(x, nw_ref, sh_ref, sc_ref, w_ref, o_ref)


def _in_proj_first(ctx, x, norm_w, shift2, scale2, w_bf16, ctx_tiles):
    rows = (ctx.shape[1] + x.shape[1]) * BATCH
    n = w_bf16.shape[1]
    steps = ROW_TILE // BATCH
    kind = lambda r: (jnp.where(r < ctx_tiles, 0, 1), 0, 0)
    return pl.pallas_call(
        functools.partial(_in_proj_first_body, ctx_tiles),
        grid=(rows // ROW_TILE,),
        in_specs=[pl.BlockSpec((BATCH, steps, D_MODEL), lambda r: (0, jnp.minimum(r, ctx_tiles - 1), 0)),
                  pl.BlockSpec((BATCH, steps, D_MODEL), lambda r: (0, jnp.maximum(r - ctx_tiles, 0), 0)),
                  pl.BlockSpec((1, D_MODEL), lambda r: (0, 0)),
                  pl.BlockSpec((1, BATCH, D_MODEL), kind),
                  pl.BlockSpec((1, BATCH, D_MODEL), kind),
                  pl.BlockSpec((D_MODEL, n), lambda r: (0, 0))],
        out_specs=[pl.BlockSpec((n // LANES, ROW_TILE, LANES), lambda r: (0, r, 0)),
                   pl.BlockSpec((ROW_TILE, D_MODEL), lambda r: (r, 0))],
        out_shape=[jax.ShapeDtypeStruct((n // LANES, rows, LANES), F32),
                   jax.ShapeDtypeStruct((rows, D_MODEL), F32)],
        scratch_shapes=[pltpu.VMEM((ROW_CHUNKS, ROW_TILE, LANES), F32)],
        compiler_params=_params("arbitrary"),
        name="in_proj_first",
    )(ctx, x, norm_w.reshape(1, D_MODEL), shift2, scale2, w_bf16)


def _in_proj(xs, norm_w, shift2, scale2, w_bf16, ctx_tiles):
    rows = xs.shape[0]
    n = w_bf16.shape[1]
    kind = lambda r: (jnp.where(r < ctx_tiles, 0, 1), 0, 0)
    return pl.pallas_call(
        _in_proj_body,
        grid=(rows // ROW_TILE,),
        in_specs=[pl.BlockSpec((ROW_TILE, D_MODEL), lambda r: (r, 0)),
                  pl.BlockSpec((1, D_MODEL), lambda r: (0, 0)),
                  pl.BlockSpec((1, BATCH, D_MODEL), kind),
                  pl.BlockSpec((1, BATCH, D_MODEL), kind),
                  pl.BlockSpec((D_MODEL, n), lambda r: (0, 0))],
        out_specs=pl.BlockSpec((n // LANES, ROW_TILE, LANES), lambda r: (0, r, 0)),
        out_shape=jax.ShapeDtypeStruct((n // LANES, rows, LANES), F32),
        compiler_params=_params("arbitrary"),
        name="in_proj",
    )(xs, norm_w.reshape(1, D_MODEL), shift2, scale2, w_bf16)


def _s5_discretize(lam_re, lam_im, log_step, b_re, b_im, c_re, c_im):
    lr, li = lam_re, lam_im
    dt = jnp.exp(log_step)[:, None]
    mag = jnp.exp(lr * dt)
    ar, ai = mag * jnp.cos(li * dt), mag * jnp.sin(li * dt)
    pr, pi = ar - 1.0, ai
    den = lr * lr + li * li
    zr, zi = (pr * lr + pi * li) / den, (pi * lr - pr * li) / den
    bbr = zr[..., None] * b_re - zi[..., None] * b_im
    bbi = zr[..., None] * b_im + zi[..., None] * b_re
    gps = S5_GROUPS // S5_SLABS
    eye = jnp.eye(gps, dtype=F32)

    def in_slab(m):
        m = m.reshape(S5_SLABS, gps, S5_STATE, S5_GROUP)
        return jnp.einsum('sgnc,gh->sgchn', m, eye).reshape(S5_SLABS, S5_SLAB_CH, S5_SLAB_STATE)

    def out_slab(m):
        m = m.reshape(S5_SLABS, gps, S5_GROUP, S5_STATE)
        return jnp.einsum('sgcn,gh->sgnhc', m, eye).reshape(S5_SLABS, S5_SLAB_STATE, S5_SLAB_CH)

    bb = jnp.concatenate([in_slab(bbr), in_slab(bbi)], -1)
    cc = jnp.concatenate([out_slab(c_re), out_slab(-c_im)], 1)
    a = jnp.concatenate([ar.reshape(S5_SLABS, S5_SLAB_STATE), ai.reshape(S5_SLABS, S5_SLAB_STATE)], -1)
    return bb.astype(BF16), cc.astype(BF16), a.reshape(1, S5_SLABS * 2 * S5_SLAB_STATE)


def _s5_body(u_ref, bb_ref, cc_ref, a_ref, y_ref, x_scr, h_scr):
    d = pl.program_id(0)
    rows = u_ref.shape[1]
    steps = rows // BATCH
    sw = 2 * S5_SLAB_STATE

    @pl.when(pl.program_id(1) == 0)
    def _():
        h_scr[...] = jnp.zeros_like(h_scr)

    for s in range(S5_SLABS):
        x_scr[:, s * sw:(s + 1) * sw] = _dot(u_ref[s].astype(BF16), bb_ref[0, s])

    for s in range(S5_SLABS):
        re0, im0 = s * sw, s * sw + S5_SLAB_STATE
        ar = jnp.broadcast_to(a_ref[0, :, re0:im0], (BATCH, S5_SLAB_STATE))
        ai = jnp.broadcast_to(a_ref[0, :, im0:im0 + S5_SLAB_STATE], (BATCH, S5_SLAB_STATE))

        def step(i, carry):
            hr, hi = carry
            t = jnp.where(d == 0, i, steps - 1 - i)
            r0 = pl.multiple_of(t * BATCH, BATCH)
            xr = x_scr[pl.ds(r0, BATCH), re0:im0]
            xi = x_scr[pl.ds(r0, BATCH), im0:im0 + S5_SLAB_STATE]
            nr = ar * hr - ai * hi + xr
            ni = ar * hi + ai * hr + xi
            x_scr[pl.ds(r0, BATCH), re0:im0] = nr
            x_scr[pl.ds(r0, BATCH), im0:im0 + S5_SLAB_STATE] = ni
            return nr, ni

        hr, hi = lax.fori_loop(0, steps, step,
                               (h_scr[:, re0:im0], h_scr[:, im0:im0 + S5_SLAB_STATE]), unroll=8)
        h_scr[:, re0:im0] = hr
        h_scr[:, im0:im0 + S5_SLAB_STATE] = hi

    for s in range(S5_SLABS):
        y_ref[0, :, s * S5_SLAB_CH:(s + 1) * S5_SLAB_CH] = _dot(
            x_scr[:, s * sw:(s + 1) * sw].astype(BF16), cc_ref[0, s])


def _s5_scan(z, bb, cc, a, n_ctx, n_all):
    rows = n_all * BATCH
    blk = S5_STEPS * BATCH
    ctx_chunks, all_chunks = n_ctx // S5_STEPS, n_all // S5_STEPS
    chunk = lambda d, i: _scan_chunk(d, i, ctx_chunks, all_chunks)
    state_w = S5_SLABS * 2 * S5_SLAB_STATE
    return pl.pallas_call(
        _s5_body,
        grid=(2, all_chunks),
        in_specs=[pl.BlockSpec((S5_SLABS, blk, LANES), lambda d, i: (0, chunk(d, i), 0)),
                  pl.BlockSpec((1, S5_SLABS, S5_SLAB_CH, 2 * S5_SLAB_STATE), lambda d, i: (d, 0, 0, 0)),
                  pl.BlockSpec((1, S5_SLABS, 2 * S5_SLAB_STATE, S5_SLAB_CH), lambda d, i: (d, 0, 0, 0)),
                  pl.BlockSpec((1, 1, state_w), lambda d, i: (d, 0, 0))],
        out_specs=pl.BlockSpec((1, blk, HALF_W), lambda d, i: (d, chunk(d, i), 0)),
        out_shape=jax.ShapeDtypeStruct((2, rows, HALF_W), F32),
        scratch_shapes=[pltpu.VMEM((blk, state_w), F32), pltpu.VMEM((BATCH, state_w), F32)],
        compiler_params=_params("arbitrary", "arbitrary"),
        name="s5_scan",
    )(z, bb, cc, a)


def _rope_tables(n_ctx, n_lat):
    half = RET_DK // 2
    nf = half // 2
    grid_rows = n_lat // GRID_W
    row = jnp.repeat(jnp.arange(grid_rows, dtype=F32), GRID_W)
    col = jnp.tile(jnp.arange(GRID_W, dtype=F32), grid_rows)
    inv = ROPE_BASE ** (-jnp.arange(nf, dtype=F32) / nf)
    ang = jnp.concatenate([row[:, None] * inv, col[:, None] * inv], -1)
    cos = jnp.tile(jnp.cos(ang), (1, 2 * RET_HEADS))
    sin = jnp.tile(jnp.concatenate([-jnp.sin(ang), jnp.sin(ang)], -1), (1, RET_HEADS))
    cos = jnp.concatenate([jnp.ones((n_ctx, RET_QK), F32), cos], 0)
    sin = jnp.concatenate([jnp.zeros((n_ctx, RET_QK), F32), sin], 0)
    return cos, sin


def _ret_tables(ret_decay):
    lg = jax.nn.log_sigmoid(ret_decay)
    pos = jnp.arange(RET_CHUNK, dtype=F32)
    diff = pos[:, None] - pos[None, :]
    f_mask, b_mask = diff >= 0, diff < 0
    dm_f = jnp.where(f_mask, jnp.exp(lg[0][:, None, None] * jnp.where(f_mask, diff, 0.0)), 0.0)
    dm_b = jnp.where(b_mask, jnp.exp(lg[1][:, None, None] * jnp.where(b_mask, -diff, 0.0)), 0.0)
    dmat = jnp.stack([dm_f, dm_b])
    heads = lambda t: jnp.repeat(t, RET_DK, axis=-1)
    kdec = jnp.stack([heads(jnp.exp(lg[0][None] * (RET_CHUNK - 1 - pos)[:, None])),
                      heads(jnp.exp(lg[1][None] * pos[:, None]))])
    qdec = jnp.stack([heads(jnp.exp(lg[0][None] * (pos + 1)[:, None])),
                      heads(jnp.exp(lg[1][None] * (RET_CHUNK - pos)[:, None]))])
    blk = jnp.kron(jnp.eye(RET_HEADS, dtype=F32), jnp.ones((RET_DK, RET_DV), F32))
    sdec = jnp.repeat(jnp.exp(lg * RET_CHUNK), RET_DK, axis=-1)[:, :, None] * blk[None]
    return dmat, kdec, qdec, sdec, blk


def _rope(t, cos, sin):
    lane = lax.broadcasted_iota(jnp.int32, t.shape, 1)
    first = (lane % RET_DK) < (RET_DK // 2)
    partner = jnp.where(first, pltpu.roll(t, RET_QK - RET_DK // 2, 1), pltpu.roll(t, RET_DK // 2, 1))
    return t * cos + partner * sin


def _batch_rows(b, steps):
    return pl.ds(b, steps, stride=BATCH)


def _ret_direction(d, b, q_ref, k_ref, v_ref, cos_ref, sin_ref, dm_ref, kd_ref, qd_ref, sd_ref, blk_ref,
                   o_ref, s_scr):
    rows = _batch_rows(b, RET_CHUNK)
    cos, sin = cos_ref[...], sin_ref[...]
    q = _rope(jnp.concatenate([q_ref[0, rows, :], q_ref[1, rows, :]], 1), cos, sin)
    k = _rope(jnp.concatenate([k_ref[0, rows, :], k_ref[1, rows, :]], 1), cos, sin) * (RET_DK ** -0.5)
    v = [v_ref[h, rows, :].astype(BF16) for h in range(RET_HEADS)]
    kb = k.astype(BF16)
    lane = lax.broadcasted_iota(jnp.int32, q.shape, 1)
    si = d * BATCH + b
    state = s_scr[si]
    o_inter = _dot((q * qd_ref[d]).astype(BF16), state.astype(BF16))
    for h in range(RET_HEADS):
        qh = jnp.where(lane // RET_DK == h, q, 0.0).astype(BF16)
        scores = _dot_nt(qh, kb) * dm_ref[d, h]
        o_ref[h, rows, :] = _dot(scores.astype(BF16), v[h]) + o_inter[:, h * RET_DV:(h + 1) * RET_DV]
    kv = _dot((k * kd_ref[d]).T.astype(BF16), jnp.concatenate(v, 1))
    s_scr[si] = sd_ref[d] * state + blk_ref[...] * kv


def _ret_body(qf, kf, vf, cf, sf, qb, kb, vb, cb, sb, dm_ref, kd_ref, qd_ref, sd_ref, blk_ref,
              of_ref, ob_ref, s_scr):
    @pl.when(pl.program_id(0) == 0)
    def _():
        s_scr[...] = jnp.zeros_like(s_scr)

    def per_batch(b, carry):
        _ret_direction(0, b, qf, kf, vf, cf, sf, dm_ref, kd_ref, qd_ref, sd_ref, blk_ref, of_ref, s_scr)
        _ret_direction(1, b, qb, kb, vb, cb, sb, dm_ref, kd_ref, qd_ref, sd_ref, blk_ref, ob_ref, s_scr)
        return carry

    lax.fori_loop(0, BATCH, per_batch, 0, unroll=2)


def _retention(z, cos, sin, tables, n_ctx, n_all):
    dmat, kdec, qdec, sdec, blk = tables
    rows = n_all * BATCH
    blk_rows = RET_CHUNK * BATCH
    ctx_chunks, all_chunks = n_ctx // RET_CHUNK, n_all // RET_CHUNK
    qk_slabs, v_slabs = RET_QK // LANES, HALF_W // LANES
    q0, k0, v0 = EVEN_SPLITS[0] // RET_QK, EVEN_SPLITS[1] // RET_QK, EVEN_SPLITS[2] // HALF_W

    def seq_specs(d):
        ch = lambda i: _scan_chunk(d, i, ctx_chunks, all_chunks)
        return [pl.BlockSpec((qk_slabs, blk_rows, LANES), lambda i: (q0, ch(i), 0)),
                pl.BlockSpec((qk_slabs, blk_rows, LANES), lambda i: (k0, ch(i), 0)),
                pl.BlockSpec((v_slabs, blk_rows, LANES), lambda i: (v0, ch(i), 0)),
                pl.BlockSpec((RET_CHUNK, RET_QK), lambda i: (ch(i), 0)),
                pl.BlockSpec((RET_CHUNK, RET_QK), lambda i: (ch(i), 0))]

    def out_spec(d):
        return pl.BlockSpec((RET_HEADS, blk_rows, LANES),
                            lambda i: (0, _scan_chunk(d, i, ctx_chunks, all_chunks), 0))

    const = lambda nd: (lambda i: (0,) * nd)
    o_shape = jax.ShapeDtypeStruct((RET_HEADS, rows, LANES), F32)
    return pl.pallas_call(
        _ret_body,
        grid=(all_chunks,),
        in_specs=seq_specs(0) + seq_specs(1) + [
            pl.BlockSpec(dmat.shape, const(4)), pl.BlockSpec(kdec.shape, const(3)),
            pl.BlockSpec(qdec.shape, const(3)), pl.BlockSpec(sdec.shape, const(3)),
            pl.BlockSpec(blk.shape, const(2))],
        out_specs=[out_spec(0), out_spec(1)],
        out_shape=[o_shape, o_shape],
        scratch_shapes=[pltpu.VMEM((2 * BATCH, RET_QK, HALF_W), F32)],
        compiler_params=_params("arbitrary"),
        name="retention",
    )(z, z, z, cos, sin, z, z, z, cos, sin, dmat, kdec, qdec, sdec, blk)


def _lru_body(ctx_chunks, all_chunks, prev_ref, x_ref, next_ref, cw_ref, cb_ref, wg_ref, bg_ref, sp_ref,
              h_ref, a_scr, b_scr, h_scr):
    d, i = pl.program_id(0), pl.program_id(1)
    rows = x_ref.shape[1]
    steps = rows // BATCH
    chunk = _scan_chunk(d, i, ctx_chunks, all_chunks)

    @pl.when(i == 0)
    def _():
        h_scr[...] = jnp.zeros_like(h_scr)

    seg_first, seg_last = _segment_edges(chunk, ctx_chunks, all_chunks)
    prev = jnp.where(seg_first, 0.0, prev_ref[...])
    nxt = jnp.where(seg_last, 0.0, next_ref[...])
    slabs = []
    for s in range(LRU_W // LANES):
        sl = slice(s * LANES, (s + 1) * LANES)
        ext = jnp.concatenate([prev[s], x_ref[s], nxt[s]], 0)
        slabs.append(cb_ref[:, sl] + sum(cw_ref[k:k + 1, sl] * ext[k * BATCH:k * BATCH + rows]
                                         for k in range(CONV_K)))
    xs = jnp.concatenate(slabs, 1)
    gates = _dot(xs.astype(BF16), wg_ref[0]) + bg_ref[0]
    r = jax.nn.sigmoid(gates[:, :LRU_W])
    ig = jax.nn.sigmoid(gates[:, LRU_W:])
    log_a = -r * sp_ref[0]
    a = jnp.exp(log_a)
    a_scr[...] = a
    b_scr[...] = jnp.sqrt(1.0 - jnp.exp(2.0 * log_a)) * (ig * xs)

    def step(j, h):
        t = jnp.where(d == 0, j, steps - 1 - j)
        r0 = pl.multiple_of(t * BATCH, BATCH)
        h = a_scr[pl.ds(r0, BATCH), :] * h + b_scr[pl.ds(r0, BATCH), :]
        h_ref[0, pl.ds(r0, BATCH), :] = h
        return h

    h_scr[...] = lax.fori_loop(0, steps, step, h_scr[...], unroll=8)


def _lru(z, conv_w, conv_b, w_a, b_a, w_x, b_x, lam, n_ctx, n_all):
    rows = n_all * BATCH
    blk = LRU_STEPS * BATCH
    ctx_chunks, all_chunks = n_ctx // LRU_STEPS, n_all // LRU_STEPS
    chunk = lambda d, i: _scan_chunk(d, i, ctx_chunks, all_chunks)
    col = ODD_LX // LRU_W
    slabs = LRU_W // LANES
    eye = jnp.eye(LRU_BLOCKS, dtype=F32)
    dense = lambda w: jnp.einsum('dkij,kl->dkilj', w, eye).reshape(2, LRU_W, LRU_W)
    wg = jnp.concatenate([dense(w_a), dense(w_x)], -1).astype(BF16)
    bg = jnp.concatenate([b_a, b_x], -1).reshape(2, 1, 2 * LRU_W)
    sp = (LRU_C * jax.nn.softplus(-lam)).reshape(2, 1, LRU_W)
    halo_prev, halo_next = 2 * BATCH, BATCH
    return pl.pallas_call(
        functools.partial(_lru_body, ctx_chunks, all_chunks),
        grid=(2, all_chunks),
        in_specs=[pl.BlockSpec((slabs, halo_prev, LANES),
                               lambda d, i: (col, jnp.maximum(chunk(d, i) * (blk // halo_prev) - 1, 0), 0)),
                  pl.BlockSpec((slabs, blk, LANES), lambda d, i: (col, chunk(d, i), 0)),
                  pl.BlockSpec((slabs, halo_next, LANES),
                               lambda d, i: (col, jnp.minimum((chunk(d, i) + 1) * (blk // halo_next),
                                                              rows // halo_next - 1), 0)),
                  pl.BlockSpec((CONV_K, LRU_W), lambda d, i: (0, 0)),
                  pl.BlockSpec((1, LRU_W), lambda d, i: (0, 0)),
                  pl.BlockSpec((1, LRU_W, 2 * LRU_W), lambda d, i: (d, 0, 0)),
                  pl.BlockSpec((1, 1, 2 * LRU_W), lambda d, i: (d, 0, 0)),
                  pl.BlockSpec((1, 1, LRU_W), lambda d, i: (d, 0, 0))],
        out_specs=pl.BlockSpec((1, blk, LRU_W), lambda d, i: (d, chunk(d, i), 0)),
        out_shape=jax.ShapeDtypeStruct((2, rows, LRU_W), F32),
        scratch_shapes=[pltpu.VMEM((blk, LRU_W), F32), pltpu.VMEM((blk, LRU_W), F32),
                        pltpu.VMEM((BATCH, LRU_W), F32)],
        compiler_params=_params("arbitrary", "arbitrary"),
        name="rglru",
    )(z, z, z, conv_w, conv_b.reshape(1, LRU_W), wg, bg, sp)


def _residual_norm_route(upd, x_ref, g2_ref, nw_ref, sh_ref, sc_ref, wrh_ref, wrl_ref, rb_ref,
                         xo_ref, hf_ref, lg_ref):
    tm = x_ref.shape[0]
    x3 = x_ref[...].reshape(tm // BATCH, BATCH, D_MODEL) + g2_ref[0] * upd.reshape(tm // BATCH, BATCH, D_MODEL)
    xo_ref[...] = x3.reshape(tm, D_MODEL)
    hf = _norm_modulate(x3, nw_ref[...], sh_ref[0], sc_ref[0]).reshape(tm, D_MODEL)
    _store_rows(hf_ref, hf)
    hi = hf.astype(BF16)
    lo = (hf - hi.astype(F32)).astype(BF16)
    lg_ref[...] = _dot(hi, wrh_ref[...]) + _dot(hi, wrl_ref[...]) + _dot(lo, wrh_ref[...]) + rb_ref[...]


def _gelu_tanh(x):
    return 0.5 * x * (1.0 + jnp.tanh(0.7978845608028654 * (x + 0.044715 * x * x * x)))


def _out_body_even(yf_ref, yb_ref, u_ref, d_ref, gw_ref, gb_ref, of_ref, ob_ref, gate_ref, wo_ref, *rest):
    u = jnp.concatenate([u_ref[s] for s in range(HALF_W // LANES)], 1)
    ys = yf_ref[0] + yb_ref[0] + d_ref[...] * u
    y = _gelu_tanh(ys)
    s5o = y * jax.nn.sigmoid(_dot(y.astype(BF16), gw_ref[...]) + gb_ref[...])
    upd = _dot(s5o.astype(BF16), wo_ref[0:HALF_W, :])
    for h in range(RET_HEADS):
        oh = of_ref[h] + ob_ref[h]
        mu = jnp.mean(oh, -1, keepdims=True)
        var = jnp.mean(jnp.square(oh - mu), -1, keepdims=True)
        rh = _silu(gate_ref[h]) * ((oh - mu) * lax.rsqrt(var + NORM_EPS))
        upd = upd + _dot(rh.astype(BF16), wo_ref[HALF_W + h * RET_DV:HALF_W + (h + 1) * RET_DV, :])
    _residual_norm_route(upd, *rest)


def _out_body_odd(gf_ref, gb_ref, zg_ref, gnw_ref, hf_ref, hb_ref, lg_ref, wo_ref, *rest):
    lg = jnp.concatenate([lg_ref[s] for s in range(LRU_W // LANES)], 1)
    lru = (hf_ref[0] + hb_ref[0]) * _gelu_tanh(lg)
    upd = _dot(lru.astype(BF16), wo_ref[HALF_W:, :])
    for h in range(GDN_HEADS):
        oh = gf_ref[h] + gb_ref[h]
        oh = oh * lax.rsqrt(jnp.mean(oh * oh, -1, keepdims=True) + NORM_EPS) * gnw_ref[...] * _silu(zg_ref[h])
        upd = upd + _dot(oh.astype(BF16), wo_ref[h * GDN_DV:(h + 1) * GDN_DV, :])
    _residual_norm_route(upd, *rest)


def _out_proj(body, mix_args, mix_specs, xs, w_out, gate2, norm_w, shift2, scale2, wr_hi, wr_lo, rb,
              row0_tiles, n_tiles, ctx_tiles):
    kind = lambda r: (jnp.where(r + row0_tiles < ctx_tiles, 0, 1), 0, 0)
    const2 = lambda r: (0, 0)
    rows = n_tiles * ROW_TILE
    return pl.pallas_call(
        body,
        grid=(n_tiles,),
        in_specs=mix_specs + [
            pl.BlockSpec((D_MODEL, D_MODEL), const2),
            pl.BlockSpec((ROW_TILE, D_MODEL), lambda r: (r + row0_tiles, 0)),
            pl.BlockSpec((1, BATCH, D_MODEL), kind),
            pl.BlockSpec((1, D_MODEL), const2),
            pl.BlockSpec((1, BATCH, D_MODEL), kind),
            pl.BlockSpec((1, BATCH, D_MODEL), kind),
            pl.BlockSpec((D_MODEL, ROUTER_PAD), const2),
            pl.BlockSpec((D_MODEL, ROUTER_PAD), const2),
            pl.BlockSpec((1, ROUTER_PAD), const2)],
        out_specs=[pl.BlockSpec((ROW_TILE, D_MODEL), lambda r: (r, 0)),
                   pl.BlockSpec((ROW_TILE * ROW_CHUNKS, LANES), lambda r: (r, 0)),
                   pl.BlockSpec((ROW_TILE, ROUTER_PAD), lambda r: (r, 0))],
        out_shape=[jax.ShapeDtypeStruct((rows, D_MODEL), F32),
                   jax.ShapeDtypeStruct((rows * ROW_CHUNKS, LANES), F32),
                   jax.ShapeDtypeStruct((rows, ROUTER_PAD), F32)],
        compiler_params=_params("arbitrary"),
        name="out_proj",
    )(*mix_args, w_out, xs, gate2, norm_w.reshape(1, D_MODEL), shift2, scale2, wr_hi, wr_lo, rb)


def _route(logits):
    n_tok = logits.shape[0]
    g_logits = logits[:, :N_GROUPS]
    g_idx = jnp.argmax(g_logits, -1)
    g_w = jnp.max(jax.nn.softmax(g_logits, -1), -1, keepdims=True)
    e_logits = logits[:, N_GROUPS:N_GROUPS + N_EXPERTS].reshape(n_tok, N_GROUPS, EXPERTS_PER_GROUP)
    chosen = g_idx[:, None, None] == jnp.arange(N_GROUPS)[None, :, None]
    e_in = jnp.sum(jnp.where(chosen, e_logits, 0.0), 1)
    top_v, top_i = lax.top_k(e_in, TOP_K)
    top_w = jax.nn.softmax(top_v, -1) * g_w
    return (g_idx[:, None] * EXPERTS_PER_GROUP + top_i).astype(jnp.int32), top_w


def _dispatch_plan(expert_ids, top_w, n_tiles):
    n_tok = expert_ids.shape[0]
    n_asg = n_tok * TOP_K
    flat_e = expert_ids.T.reshape(n_asg)
    flat_w = top_w.T.reshape(n_asg)
    order = jnp.argsort(flat_e, stable=True).astype(jnp.int32)
    inv = jnp.argsort(order).astype(jnp.int32)
    onehot = flat_e[:, None] == jnp.arange(N_EXPERTS, dtype=jnp.int32)[None]
    counts = jnp.sum(onehot, 0, dtype=jnp.int32)
    padded = ((counts + MOE_TILE - 1) // MOE_TILE) * MOE_TILE
    pend = jnp.cumsum(padded)
    pstart = pend - padded
    start = jnp.cumsum(counts) - counts
    shift = pstart - start
    pos = inv + jnp.sum(jnp.where(onehot, shift[None], 0), 1)
    tile_start = jnp.arange(n_tiles, dtype=jnp.int32) * MOE_TILE
    tile_e = jnp.minimum(jnp.sum(tile_start[:, None] >= pend[None], 1), N_EXPERTS - 1).astype(jnp.int32)
    tile_valid = (tile_start < pend[-1]).astype(jnp.int32)
    tile_hot = tile_e[:, None] == jnp.arange(N_EXPERTS, dtype=jnp.int32)[None]
    per_tile = lambda tbl: jnp.sum(jnp.where(tile_hot, tbl[None], 0), 1)
    tile_rank0 = tile_start - per_tile(shift)
    tile_end = per_tile(start + counts)
    lane = jnp.arange(MOE_TILE, dtype=jnp.int32)[None]
    rank = (tile_rank0[:, None] + lane).reshape(n_tiles * MOE_TILE)
    row_ok = (tile_rank0[:, None] + lane < tile_end[:, None]).reshape(n_tiles * MOE_TILE)
    src_asg = order[jnp.clip(rank, 0, n_asg - 1)]
    src_tok = jnp.where(row_ok, src_asg % n_tok, 0)
    row_w = jnp.where(row_ok, flat_w[src_asg], 0.0)
    return src_tok, row_w, pos.reshape(TOP_K, n_tok), tile_e, tile_valid


def _gather_rows(idx_ref, src_hbm, dst_vmem, sem, n_rows):
    def issue(j, c):
        for p in range(DMA_PRIORITIES):
            r = j * DMA_PRIORITIES + p
            src = pl.multiple_of(idx_ref[0, 0, r] * ROW_CHUNKS, ROW_CHUNKS)
            dst = pl.multiple_of(r * ROW_CHUNKS, ROW_CHUNKS)
            pltpu.make_async_copy(src_hbm.at[pl.ds(src, ROW_CHUNKS)], dst_vmem.at[pl.ds(dst, ROW_CHUNKS)],
                                  sem).start(priority=p)
        return c

    lax.fori_loop(0, n_rows // DMA_PRIORITIES, issue, 0, unroll=16)


def _wait_rows(src_hbm, dst_vmem, sem, n_rows):
    pltpu.make_async_copy(src_hbm.at[pl.ds(0, n_rows * ROW_CHUNKS)], dst_vmem, sem).wait()


def _rows_2d(ref, n_rows):
    return jnp.concatenate([ref[pl.ds(s, n_rows, stride=ROW_CHUNKS), :] for s in range(ROW_CHUNKS)], 1)


def _store_rows(ref, val):
    for s in range(ROW_CHUNKS):
        ref[pl.ds(s, val.shape[0], stride=ROW_CHUNKS), :] = val[:, s * LANES:(s + 1) * LANES]


def _expert_body(te_ref, tv_ref, idx_ref, idx_next_ref, hf_hbm, cw_ref, wg32_ref, wu32_ref, wd32_ref, y_ref,
                 xbuf, wg_ref, wu_ref, wd_ref, sem):
    i = pl.program_id(0)
    n = pl.num_programs(0)
    slot = i % 2
    nxt = jnp.minimum(i + 1, n - 1)

    @pl.when(jnp.logical_or(i == 0, te_ref[i] != te_ref[jnp.maximum(i - 1, 0)]))
    def _():
        wg_ref[0] = wg32_ref[0, 0].astype(BF16)
        wu_ref[0] = wu32_ref[0, 0].astype(BF16)
        wd_ref[0] = wd32_ref[0, 0].astype(BF16)

    @pl.when(jnp.logical_and(i == 0, tv_ref[0] == 1))
    def _():
        _gather_rows(idx_ref, hf_hbm, xbuf.at[0], sem.at[0], MOE_TILE)

    @pl.when(jnp.logical_and(i + 1 < n, tv_ref[nxt] == 1))
    def _():
        _gather_rows(idx_next_ref, hf_hbm, xbuf.at[1 - slot], sem.at[1 - slot], MOE_TILE)

    @pl.when(tv_ref[i] == 1)
    def _():
        _wait_rows(hf_hbm, xbuf.at[slot], sem.at[slot], MOE_TILE)
        x = _rows_2d(xbuf.at[slot], MOE_TILE).astype(BF16)
        g = _dot(x, wg_ref[0])
        u = _dot(x, wu_ref[0])
        act = _silu(g) * u * cw_ref[:, 0:1]
        _store_rows(y_ref, _dot(act.astype(BF16), wd_ref[0]))

    @pl.when(tv_ref[i] == 0)
    def _():
        y_ref[...] = jnp.zeros_like(y_ref)


def _experts(layer, tile_e, tile_valid, src_tok, hf, cw_rows, wg, wu, wd):
    n_tiles = tile_e.shape[0]
    tile_spec = pl.BlockSpec((MOE_TILE * ROW_CHUNKS, LANES), lambda i, te, tv: (i, 0))
    idx3 = src_tok.reshape(n_tiles, 1, MOE_TILE)
    grid_spec = pltpu.PrefetchScalarGridSpec(
        num_scalar_prefetch=2,
        grid=(n_tiles,),
        in_specs=[pl.BlockSpec((1, 1, MOE_TILE), lambda i, te, tv: (i, 0, 0), memory_space=pltpu.SMEM),
                  pl.BlockSpec((1, 1, MOE_TILE), lambda i, te, tv: (jnp.minimum(i + 1, n_tiles - 1), 0, 0),
                               memory_space=pltpu.SMEM),
                  pl.BlockSpec(memory_space=pl.ANY),
                  pl.BlockSpec((MOE_TILE, LANES), lambda i, te, tv: (i, 0)),
                  pl.BlockSpec((1, 1, D_MODEL, EXPERT_FF), lambda i, te, tv: (layer, te[i], 0, 0)),
                  pl.BlockSpec((1, 1, D_MODEL, EXPERT_FF), lambda i, te, tv: (layer, te[i], 0, 0)),
                  pl.BlockSpec((1, 1, EXPERT_FF, D_MODEL), lambda i, te, tv: (layer, te[i], 0, 0))],
        out_specs=tile_spec,
        scratch_shapes=[pltpu.VMEM((2, MOE_TILE * ROW_CHUNKS, LANES), F32),
                        pltpu.VMEM((1, D_MODEL, EXPERT_FF), BF16), pltpu.VMEM((1, D_MODEL, EXPERT_FF), BF16),
                        pltpu.VMEM((1, EXPERT_FF, D_MODEL), BF16), pltpu.SemaphoreType.DMA((2,))])
    return pl.pallas_call(
        _expert_body,
        grid_spec=grid_spec,
        out_shape=jax.ShapeDtypeStruct((n_tiles * MOE_TILE * ROW_CHUNKS, LANES), F32),
        compiler_params=_params("arbitrary"),
        name="moe_experts",
    )(tile_e, tile_valid, idx3, idx3, hf, cw_rows, wg, wu, wd)


def _combine_rows(p0_ref, p1_ref, p0n_ref, p1n_ref, y_hbm, x_ref, g5_ref, buf, sem):
    r = pl.program_id(0)
    n = pl.num_programs(0)
    tm = x_ref.shape[0]
    slot = r % 2

    def fetch(p0, p1, s):
        _gather_rows(p0, y_hbm, buf.at[s, 0], sem.at[s, 0], tm)
        _gather_rows(p1, y_hbm, buf.at[s, 1], sem.at[s, 1], tm)

    @pl.when(r == 0)
    def _():
        fetch(p0_ref, p1_ref, 0)

    @pl.when(r + 1 < n)
    def _():
        fetch(p0n_ref, p1n_ref, 1 - slot)

    _wait_rows(y_hbm, buf.at[slot, 0], sem.at[slot, 0], tm)
    _wait_rows(y_hbm, buf.at[slot, 1], sem.at[slot, 1], tm)
    f3 = (_rows_2d(buf.at[slot, 0], tm) + _rows_2d(buf.at[slot, 1], tm)).reshape(tm // BATCH, BATCH, D_MODEL)
    return (x_ref[...].reshape(tm // BATCH, BATCH, D_MODEL) + g5_ref[0] * f3).reshape(tm, D_MODEL)


def _combine_body(p0_ref, p1_ref, p0n_ref, p1n_ref, y_hbm, x_ref, g5_ref, o_ref, buf, sem):
    o_ref[...] = _combine_rows(p0_ref, p1_ref, p0n_ref, p1n_ref, y_hbm, x_ref, g5_ref, buf, sem)


def _combine_final_body(p0_ref, p1_ref, p0n_ref, p1n_ref, y_hbm, x_ref, g5_ref, fw_ref, o_ref, buf, sem, slab):
    x = _combine_rows(p0_ref, p1_ref, p0n_ref, p1n_ref, y_hbm, x_ref, g5_ref, buf, sem)
    steps = x.shape[0] // BATCH
    xn = x * lax.rsqrt(jnp.mean(x * x, -1, keepdims=True) + NORM_EPS) * fw_ref[...]
    for s in range(ROW_CHUNKS):
        slab[s] = xn[:, s * LANES:(s + 1) * LANES]
    for b in range(BATCH):
        for s in range(ROW_CHUNKS):
            o_ref[b, :, s * LANES:(s + 1) * LANES] = slab[s, _batch_rows(b, steps), :]


def _combine(pos, y_sorted, xs, gate2, row0_tiles, ctx_tiles, final_w=None):
    n_tok = pos.shape[1]
    n_tiles = n_tok // MOE_TILE
    kind = lambda r: (jnp.where(r + row0_tiles < ctx_tiles, 0, 1), 0, 0)
    idx_spec = lambda k, ahead: pl.BlockSpec(
        (1, 1, MOE_TILE), lambda r: (k * n_tiles + jnp.minimum(r + ahead, n_tiles - 1), 0, 0),
        memory_space=pltpu.SMEM)
    p2 = pos.reshape(TOP_K * n_tiles, 1, MOE_TILE)
    in_specs = [idx_spec(0, 0), idx_spec(1, 0), idx_spec(0, 1), idx_spec(1, 1),
                pl.BlockSpec(memory_space=pl.ANY),
                pl.BlockSpec((MOE_TILE, D_MODEL), lambda r: (r, 0)),
                pl.BlockSpec((1, BATCH, D_MODEL), kind)]
    scratch = [pltpu.VMEM((2, TOP_K, MOE_TILE * ROW_CHUNKS, LANES), F32), pltpu.SemaphoreType.DMA((2, TOP_K))]
    if final_w is None:
        return pl.pallas_call(
            _combine_body,
            grid=(n_tiles,),
            in_specs=in_specs,
            out_specs=pl.BlockSpec((MOE_TILE, D_MODEL), lambda r: (r, 0)),
            out_shape=jax.ShapeDtypeStruct((n_tok, D_MODEL), F32),
            scratch_shapes=scratch,
            compiler_params=_params("arbitrary"),
            name="moe_combine",
        )(p2, p2, p2, p2, y_sorted, xs, gate2)
    steps = MOE_TILE // BATCH
    return pl.pallas_call(
        _combine_final_body,
        grid=(n_tiles,),
        in_specs=in_specs + [pl.BlockSpec((1, D_MODEL), lambda r: (0, 0))],
        out_specs=pl.BlockSpec((BATCH, steps, D_MODEL), lambda r: (0, r, 0)),
        out_shape=jax.ShapeDtypeStruct((BATCH, n_tok // BATCH, D_MODEL), F32),
        scratch_shapes=scratch + [pltpu.VMEM((ROW_CHUNKS, MOE_TILE, LANES), F32)],
        compiler_params=_params("arbitrary"),
        name="moe_combine_final",
    )(p2, p2, p2, p2, y_sorted, xs, gate2, final_w.reshape(1, D_MODEL))


def _moe(layer, hf, logits, xs, gate2, wg, wu, wd, row0_tiles, ctx_tiles, final_w=None):
    n_tok = hf.shape[0] // ROW_CHUNKS
    n_tiles = n_tok * TOP_K // MOE_TILE + N_EXPERTS
    expert_ids, top_w = _route(logits)
    src_tok, row_w, pos, tile_e, tile_valid = _dispatch_plan(expert_ids, top_w, n_tiles)
    cw_rows = jnp.broadcast_to(row_w[:, None], (n_tiles * MOE_TILE, LANES))
    y_sorted = _experts(layer, tile_e, tile_valid, src_tok, hf, cw_rows, wg, wu, wd)
    return _combine(pos, y_sorted, xs, gate2, row0_tiles, ctx_tiles, final_w)


def _kinds(mod_l, k):
    return jnp.stack([mod_l[BATCH:, k], mod_l[:BATCH, k]])


def kernel(x, c, ctx, c_ctx, mod_w, mod_b, norm_mix, norm_ffn, router_group_w, router_group_b,
           router_expert_w, router_expert_b, expert_w_gate, expert_w_up, expert_w_down,
           even_w_in, even_w_out, s5_lam_re, s5_lam_im, s5_log_step, s5_b_re, s5_b_im, s5_c_re,
           s5_c_im, s5_d, s5_glu_w, s5_glu_b, ret_decay, odd_w_in, odd_w_out, gdn_conv_w, gdn_a_log,
           gdn_dt_bias, gdn_norm_w, lru_conv_w, lru_conv_b, lru_w_a, lru_b_a, lru_w_x, lru_b_x,
           lru_lam, final_norm):
    bsz, n_lat, _ = x.shape
    n_ctx = ctx.shape[1]
    n_all = n_ctx + n_lat
    assert bsz == BATCH and n_ctx % RET_CHUNK == 0 and n_lat % RET_CHUNK == 0
    assert (n_ctx * BATCH) % ROW_TILE == 0 and (n_lat * BATCH) % ROW_TILE == 0
    ctx_tiles = n_ctx * BATCH // ROW_TILE
    all_tiles = n_all * BATCH // ROW_TILE
    ctx_moe_tiles = n_ctx * BATCH // MOE_TILE

    xs = None
    cond16 = jnp.concatenate([c, jnp.broadcast_to(c_ctx[None], (BATCH, D_MODEL))], 0)
    mod = _adaln_all(cond16, mod_w, mod_b).reshape(DEPTH, ADALN_ROWS, N_MOD, D_MODEL)

    for layer in range(DEPTH):
        last = layer == DEPTH - 1
        i = layer // 2
        m = [_kinds(mod[layer], k) for k in range(N_MOD)]
        wr = jnp.zeros((D_MODEL, ROUTER_PAD), F32)
        wr = wr.at[:, :N_GROUPS].set(router_group_w[layer]).at[:, N_GROUPS:N_GROUPS + N_EXPERTS].set(
            router_expert_w[layer])
        wr_hi = wr.astype(BF16)
        wr_lo = (wr - wr_hi.astype(F32)).astype(BF16)
        rb = jnp.zeros((1, ROUTER_PAD), F32)
        rb = rb.at[0, :N_GROUPS].set(router_group_b[layer]).at[0, N_GROUPS:N_GROUPS + N_EXPERTS].set(
            router_expert_b[layer])
        row0 = ctx_tiles if last else 0
        n_tiles = all_tiles - row0
        slab_spec = lambda sb: pl.BlockSpec((HALF_W // LANES, ROW_TILE, LANES), lambda r: (sb, r + row0, 0))
        dir_spec = lambda d: pl.BlockSpec((1, ROW_TILE, HALF_W), lambda r: (d, r + row0, 0))
        const2 = lambda r: (0, 0)

        if layer % 2 == 0:
            w_in = even_w_in[i].astype(BF16)
            if layer == 0:
                z, xs = _in_proj_first(ctx, x, norm_mix[layer], m[0], m[1], w_in, ctx_tiles)
            else:
                z = _in_proj(xs, norm_mix[layer], m[0], m[1], w_in, ctx_tiles)
            disc = [_s5_discretize(s5_lam_re[i, d], s5_lam_im[i, d], s5_log_step[i, d], s5_b_re[i, d],
                                   s5_b_im[i, d], s5_c_re[i, d], s5_c_im[i, d]) for d in range(2)]
            bb, cc, a = (jnp.stack(t) for t in zip(*disc))
            ys5 = _s5_scan(z, bb, cc, a, n_ctx, n_all)
            cos, sin = _rope_tables(n_ctx, n_lat)
            r_f, r_b = _retention(z, cos, sin, _ret_tables(ret_decay[i]), n_ctx, n_all)
            mix_args = [ys5, ys5, z, s5_d[i].reshape(1, HALF_W), s5_glu_w[i].astype(BF16),
                        s5_glu_b[i].reshape(1, HALF_W), r_f, r_b, z]
            mix_specs = [dir_spec(0), dir_spec(1), slab_spec(0),
                         pl.BlockSpec((1, HALF_W), const2),
                         pl.BlockSpec((HALF_W, HALF_W), const2),
                         pl.BlockSpec((1, HALF_W), const2),
                         slab_spec(0), slab_spec(0), slab_spec(EVEN_SPLITS[3] // HALF_W)]
            body, w_out = _out_body_even, even_w_out[i]
        else:
            w_in = odd_w_in[i]
            w_pad = jnp.concatenate([
                w_in[:, :ODD_SPLITS[1]],
                w_in[:, ODD_SPLITS[3]:],
                jnp.pad(w_in[:, ODD_SPLITS[1]:ODD_SPLITS[3]], ((0, 0), (0, ODD_PAD - 4 * GDN_HEADS)))], 1)
            z = _in_proj(xs, norm_mix[layer], m[0], m[1], w_pad.astype(BF16), ctx_tiles)
            qkv_act, bg = _gdn_prep(z, gdn_conv_w[i], gdn_a_log[i], gdn_dt_bias[i], n_ctx, n_all)
            g_f, g_b = _gdn(qkv_act, bg, n_ctx, n_all)
            hl = _lru(z, lru_conv_w[i], lru_conv_b[i], lru_w_a[i], lru_b_a[i], lru_w_x[i], lru_b_x[i],
                      lru_lam[i], n_ctx, n_all)
            mix_args = [g_f, g_b, z, gdn_norm_w[i].reshape(1, GDN_DV), hl, hl, z]
            mix_specs = [slab_spec(0), slab_spec(0), slab_spec(ODD_SPLITS[0] // HALF_W),
                         pl.BlockSpec((1, GDN_DV), const2), dir_spec(0), dir_spec(1),
                         slab_spec((ODD_LX + LRU_W) // LRU_W)]
            body, w_out = _out_body_odd, odd_w_out[i]

        xs_mid, hf, logits = _out_proj(body, mix_args, mix_specs, xs, w_out.astype(BF16), m[2],
                                       norm_ffn[layer], m[3], m[4], wr_hi, wr_lo, rb, row0, n_tiles, ctx_tiles)
        moe_row0 = ctx_moe_tiles if last else 0
        xs = _moe(layer, hf, logits, xs_mid, m[5], expert_w_gate, expert_w_up, expert_w_down,
                  moe_row0, ctx_moe_tiles, final_norm if last else None)

    return xs
```
